```python
import jax, jax.numpy as jnp
from jax import lax
import numpy as np

D_MODEL = 1024
BATCH = 32
SEQ = 256
DEPTH = 4
DEC_BATCH = 4
DEC_SEQ = 1024
PAST_LEN = 512

GRID_W = 64
HEAD_DIM = 64
N_Q_HEADS = 8
N_KV_HEADS = 2
GQA_GROUP = N_Q_HEADS // N_KV_HEADS
ATTN_WIDTH = N_Q_HEADS * HEAD_DIM
KV_WIDTH = N_KV_HEADS * HEAD_DIM
CONV_WIDTH = D_MODEL // 2
CONV_K = 3
POOL_WIDTH = D_MODEL // 2
POOL_WINDOWS = (2, 4, 8, 16)
POOL_GROUP = POOL_WIDTH // len(POOL_WINDOWS)
MIX_WIDTH = CONV_WIDTH + ATTN_WIDTH
EVEN_IN = 3 * CONV_WIDTH + ATTN_WIDTH + 2 * KV_WIDTH
ODD_IN = ATTN_WIDTH + 2 * KV_WIDTH + POOL_WIDTH
EVEN_SPLITS = (CONV_WIDTH, 2 * CONV_WIDTH, 3 * CONV_WIDTH, 3 * CONV_WIDTH + ATTN_WIDTH, 3 * CONV_WIDTH + ATTN_WIDTH + KV_WIDTH)
ODD_SPLITS = (ATTN_WIDTH, ATTN_WIDTH + KV_WIDTH, ATTN_WIDTH + 2 * KV_WIDTH)
Q_BLOCK = 128
WINDOW = 128
D_FF = 2816
N_EXPERTS = 8
TOP_K = 2
D_FF_EXPERT = 1024
ROPE_THETA = 10000.0
EPS = 1e-6
N_EVEN = (DEPTH + 1) // 2
N_ODD = DEPTH // 2

kernel_name = 'hybrid_flow_prefix_trunk'


def rms_norm(x, g):
    xf = x.astype(jnp.float32)
    y = xf * lax.rsqrt(jnp.mean(xf * xf, axis=-1, keepdims=True) + EPS)
    return (y * g.astype(jnp.float32)).astype(x.dtype)


def modulation(cond, w_mod, b_mod):
    m = jax.nn.silu(cond) @ w_mod + b_mod
    return jnp.split(m[:, None, :], 6, axis=-1)


def axial_rope_tables(n_tokens):
    n_rows = n_tokens // GRID_W
    rows = jnp.repeat(jnp.arange(n_rows, dtype=jnp.float32), GRID_W)
    cols = jnp.tile(jnp.arange(GRID_W, dtype=jnp.float32), n_rows)
    quarter = HEAD_DIM // 4
    inv = ROPE_THETA ** (-jnp.arange(quarter, dtype=jnp.float32) / quarter)
    ang = jnp.stack([rows[:, None] * inv, cols[:, None] * inv], axis=1)
    return jnp.cos(ang), jnp.sin(ang)


def apply_axial_rope(x, cos, sin):
    shp = x.shape
    xs = x.reshape(shp[:-1] + (2, 2, HEAD_DIM // 4)).astype(jnp.float32)
    x1, x2 = xs[..., 0, :], xs[..., 1, :]
    out = jnp.stack([x1 * cos - x2 * sin, x2 * cos + x1 * sin], axis=-2)
    return out.reshape(shp).astype(x.dtype)


def split_q(t):
    b, s, _ = t.shape
    return t.reshape(b, s, N_KV_HEADS, GQA_GROUP, HEAD_DIM).transpose(0, 2, 3, 1, 4)


def split_kv(t):
    b, s, _ = t.shape
    return t.reshape(b, s, N_KV_HEADS, HEAD_DIM).transpose(0, 2, 1, 3)


def merge_heads(o):
    b, kh, g, s, hd = o.shape
    return o.transpose(0, 3, 1, 2, 4).reshape(b, s, kh * g * hd)


def qkv_heads(q, k, v, q_gain, k_gain, rope):
    q = rms_norm(split_q(q), q_gain)
    k = rms_norm(split_kv(k), k_gain)
    v = split_kv(v)
    if rope is not None:
        cos, sin = rope
        q = apply_axial_rope(q, cos, sin)
        k = apply_axial_rope(k, cos, sin)
    return q, k, v


def attn_probs(s, sink):
    if sink is None:
        return jax.nn.softmax(s, axis=-1)
    sk = sink.astype(jnp.float32).reshape((1, N_KV_HEADS, GQA_GROUP) + (1,) * (s.ndim - 3))
    m = jnp.maximum(jnp.max(s, axis=-1, keepdims=True), sk)
    e = jnp.exp(s - m)
    return e / (jnp.sum(e, axis=-1, keepdims=True) + jnp.exp(sk - m))


def blocked_attention(q, k, v, sink=None):
    b, kh, g, sq, hd = q.shape
    nb = sq // Q_BLOCK
    qb = jnp.moveaxis(q.reshape(b, kh, g, nb, Q_BLOCK, hd), 3, 0)
    scale = HEAD_DIM ** -0.5

    def one_block(qi):
        s = jnp.einsum('bkgqd,bksd->bkgqs', qi, k, preferred_element_type=jnp.float32) * scale
        p = attn_probs(s, sink)
        return jnp.einsum('bkgqs,bksd->bkgqd', p.astype(v.dtype), v)

    out = lax.map(one_block, qb)
    return jnp.moveaxis(out, 0, 3).reshape(b, kh, g, sq, hd)


def banded_window_attention(q, k, v, k_ctx, v_ctx, sink):
    b, kh, g, s, hd = q.shape
    nb = s // Q_BLOCK
    n_ctx = k_ctx.shape[2]
    pad = ((0, 0), (0, 0), (Q_BLOCK, Q_BLOCK), (0, 0))
    kp = jnp.pad(k, pad).reshape(b, kh, nb + 2, Q_BLOCK, hd)
    vp = jnp.pad(v, pad).reshape(b, kh, nb + 2, Q_BLOCK, hd)
    kw = jnp.concatenate([kp[:, :, :-2], kp[:, :, 1:-1], kp[:, :, 2:]], axis=3)
    vw = jnp.concatenate([vp[:, :, :-2], vp[:, :, 1:-1], vp[:, :, 2:]], axis=3)
    qb = q.reshape(b, kh, g, nb, Q_BLOCK, hd)
    scale = HEAD_DIM ** -0.5
    s_loc = jnp.einsum('bkgnqd,bknwd->bkgnqw', qb, kw, preferred_element_type=jnp.float32) * scale
    qi = jnp.arange(nb)[:, None] * Q_BLOCK + jnp.arange(Q_BLOCK)[None, :]
    kj = jnp.arange(nb)[:, None] * Q_BLOCK - Q_BLOCK + jnp.arange(3 * Q_BLOCK)[None, :]
    valid = (jnp.abs(qi[:, :, None] - kj[:, None, :]) <= WINDOW) & (kj[:, None, :] >= 0) & (kj[:, None, :] < s)
    s_loc = jnp.where(valid, s_loc, -jnp.inf)
    s_ctx = jnp.einsum('bkgnqd,bkld->bkgnql', qb, k_ctx, preferred_element_type=jnp.float32) * scale
    p = attn_probs(jnp.concatenate([s_ctx, s_loc], axis=-1), sink).astype(v.dtype)
    out = (jnp.einsum('bkgnql,bkld->bkgnqd', p[..., :n_ctx], v_ctx)
           + jnp.einsum('bkgnqw,bknwd->bkgnqd', p[..., n_ctx:], vw))
    return out.reshape(b, kh, g, s, hd)


def multiscale_pool(x, pool_w, pool_scale):
    b, s, _ = x.shape
    xf = x.astype(jnp.float32)
    cs = jnp.pad(jnp.cumsum(xf, axis=1), ((0, 0), (1, 0), (0, 0)))
    t = jnp.arange(s)
    outs = []
    for gi, w in enumerate(POOL_WINDOWS):
        lo = jnp.clip(t - w // 2, 0, s)
        hi = jnp.clip(t - w // 2 + w, 0, s)
        sl = slice(gi * POOL_GROUP, (gi + 1) * POOL_GROUP)
        seg = cs[:, :, sl]
        cnt = (hi - lo).astype(jnp.float32)[None, :, None]
        outs.append((seg[:, hi] - seg[:, lo]) / cnt - xf[:, :, sl])
    d = jnp.stack(outs, axis=2)
    y = jnp.einsum('bsgc,gcd->bsgd', d, pool_w.astype(jnp.float32)).reshape(b, s, POOL_WIDTH)
    return (y * pool_scale.astype(jnp.float32)).astype(x.dtype)


def even_mixer(h, w_in, conv_w, q_gain, k_gain, w_out, ctx_kv, rope):
    z = h @ w_in
    bg, cg, xa, q, k, v = jnp.split(z, EVEN_SPLITS, axis=-1)
    u = jnp.pad(cg * xa, ((0, 0), (1, 1), (0, 0)))
    ya = bg * (u[:, :-2] * conv_w[0] + u[:, 1:-1] * conv_w[1] + u[:, 2:] * conv_w[2])
    qh, kh, vh = qkv_heads(q, k, v, q_gain, k_gain, rope)
    if ctx_kv is None:
        o = blocked_attention(qh, kh, vh)
        new_kv = (kh, vh)
    else:
        kc, vc = ctx_kv
        o = blocked_attention(qh, jnp.concatenate([kc, kh], axis=2), jnp.concatenate([vc, vh], axis=2))
        new_kv = None
    y = jnp.concatenate([ya, merge_heads(o)], axis=-1) @ w_out
    return y, new_kv


def odd_mixer(h, w_in, q_gain, k_gain, sink, pool_w, pool_scale, w_out, ctx_kv, rope):
    z = h @ w_in
    q, k, v, xd = jnp.split(z, ODD_SPLITS, axis=-1)
    qh, kh, vh = qkv_heads(q, k, v, q_gain, k_gain, rope)
    if ctx_kv is None:
        o = blocked_attention(qh, kh, vh, sink)
        new_kv = (kh, vh)
    else:
        kc, vc = ctx_kv
        o = banded_window_attention(qh, kh, vh, kc, vc, sink)
        new_kv = None
    yd = multiscale_pool(xd, pool_w, pool_scale)
    y = jnp.concatenate([merge_heads(o), yd], axis=-1) @ w_out
    return y, new_kv


def swiglu(h, w1, w3, w2):
    return (jax.nn.silu(h @ w1) * (h @ w3)) @ w2


def moe_swiglu(h, router_w, router_b, w1, w3, w2):
    b, s, d = h.shape
    t = h.reshape(b * s, d)
    logits = (t @ router_w).astype(jnp.float32) + router_b.astype(jnp.float32)
    top_v, top_i = lax.top_k(logits, TOP_K)
    gates = jax.nn.softmax(top_v, axis=-1)
    dense_g = jnp.sum(jax.nn.one_hot(top_i, N_EXPERTS, dtype=jnp.float32) * gates[..., None], axis=1)
    hid = jax.nn.silu(jnp.einsum('nd,edf->enf', t, w1)) * jnp.einsum('nd,edf->enf', t, w3)
    hid = hid * dense_g.T.astype(hid.dtype)[:, :, None]
    return jnp.einsum('enf,efd->nd', hid, w2).reshape(b, s, d)


def run_trunk(x, cond, cache_k, cache_v, rope, norm1, norm2, w_mod, b_mod,
              ev_w_in, ev_conv, ev_q_norm, ev_k_norm, ev_w_out,
              od_w_in, od_q_norm, od_k_norm, od_sink, od_pool_w, od_pool_scale, od_w_out,
              ffn_w1, ffn_w3, ffn_w2, moe_router, moe_router_b, moe_w1, moe_w3, moe_w2):
    ks, vs = [], []
    for l in range(DEPTH):
        i = l // 2
        sh1, sc1, g1, sh2, sc2, g2 = modulation(cond, w_mod[l], b_mod[l])
        h = rms_norm(x, norm1[l]) * (1.0 + sc1) + sh1
        ctx_kv = None if cache_k is None else (cache_k[:, l], cache_v[:, l])
        if l % 2 == 0:
            y, kv = even_mixer(h, ev_w_in[i], ev_conv[i], ev_q_norm[i], ev_k_norm[i], ev_w_out[i], ctx_kv, rope)
        else:
            y, kv = odd_mixer(h, od_w_in[i], od_q_norm[i], od_k_norm[i], od_sink[i],
                              od_pool_w[i], od_pool_scale[i], od_w_out[i], ctx_kv, rope)
        x = x + g1 * y
        h = rms_norm(x, norm2[l]) * (1.0 + sc2) + sh2
        if l % 2 == 0:
            f = swiglu(h, ffn_w1[i], ffn_w3[i], ffn_w2[i])
        else:
            f = moe_swiglu(h, moe_router[i], moe_router_b[i], moe_w1[i], moe_w3[i], moe_w2[i])
        x = x + g2 * f
        if kv is not None:
            ks.append(kv[0])
            vs.append(kv[1])
    return x, ks, vs


def setup_inputs(seed: int = 0) -> dict:
    key = jax.random.key(seed)
    keys = iter(jax.random.split(key, 40))

    def nrm(shape, scale):
        return jax.random.normal(next(keys), shape, jnp.float32) * scale

    def gain(shape):
        return 1.0 + nrm(shape, 0.05)

    kv_cache_shape = (DEC_BATCH, DEPTH, N_KV_HEADS, PAST_LEN, HEAD_DIM)
    return {
        'x_prompt': nrm((BATCH, SEQ, D_MODEL), 1.0),
        'x_sample': nrm((DEC_BATCH, DEC_SEQ, D_MODEL), 1.0),
        'cache_k': nrm(kv_cache_shape, 1.0),
        'cache_v': nrm(kv_cache_shape, 1.0),
        'c': nrm((DEC_BATCH, D_MODEL), 1.0),
        'c_ctx': nrm((D_MODEL,), 1.0),
        'norm1': gain((DEPTH, D_MODEL)),
        'norm2': gain((DEPTH, D_MODEL)),
        'w_mod': nrm((DEPTH, D_MODEL, 6 * D_MODEL), 0.5 * D_MODEL ** -0.5),
        'b_mod': nrm((DEPTH, 6 * D_MODEL), 0.02),
        'ev_w_in': nrm((N_EVEN, D_MODEL, EVEN_IN), D_MODEL ** -0.5),
        'ev_conv': nrm((N_EVEN, CONV_K, CONV_WIDTH), CONV_K ** -0.5),
        'ev_q_norm': gain((N_EVEN, HEAD_DIM)),
        'ev_k_norm': gain((N_EVEN, HEAD_DIM)),
        'ev_w_out': nrm((N_EVEN, MIX_WIDTH, D_MODEL), MIX_WIDTH ** -0.5),
        'od_w_in': nrm((N_ODD, D_MODEL, ODD_IN), D_MODEL ** -0.5),
        'od_q_norm': gain((N_ODD, HEAD_DIM)),
        'od_k_norm': gain((N_ODD, HEAD_DIM)),
        'od_sink': nrm((N_ODD, N_Q_HEADS), 0.5),
        'od_pool_w': nrm((N_ODD, len(POOL_WINDOWS), POOL_GROUP, POOL_GROUP), POOL_GROUP ** -0.5),
        'od_pool_scale': gain((N_ODD, POOL_WIDTH)),
        'od_w_out': nrm((N_ODD, MIX_WIDTH, D_MODEL), MIX_WIDTH ** -0.5),
        'ffn_w1': nrm((N_EVEN, D_MODEL, D_FF), D_MODEL ** -0.5),
        'ffn_w3': nrm((N_EVEN, D_MODEL, D_FF), D_MODEL ** -0.5),
        'ffn_w2': nrm((N_EVEN, D_FF, D_MODEL), D_FF ** -0.5),
        'moe_router': nrm((N_ODD, D_MODEL, N_EXPERTS), D_MODEL ** -0.5),
        'moe_router_b': nrm((N_ODD, N_EXPERTS), 0.01),
        'moe_w1': nrm((N_ODD, N_EXPERTS, D_MODEL, D_FF_EXPERT), D_MODEL ** -0.5),
        'moe_w3': nrm((N_ODD, N_EXPERTS, D_MODEL, D_FF_EXPERT), D_MODEL ** -0.5),
        'moe_w2': nrm((N_ODD, N_EXPERTS, D_FF_EXPERT, D_MODEL), D_FF_EXPERT ** -0.5),
    }


def reference(x_prompt, x_sample, cache_k, cache_v, c, c_ctx, norm1, norm2, w_mod, b_mod,
              ev_w_in, ev_conv, ev_q_norm, ev_k_norm, ev_w_out,
              od_w_in, od_q_norm, od_k_norm, od_sink, od_pool_w, od_pool_scale, od_w_out,
              ffn_w1, ffn_w3, ffn_w2, moe_router, moe_router_b, moe_w1, moe_w3, moe_w2):
    weights = (norm1, norm2, w_mod, b_mod,
               ev_w_in, ev_conv, ev_q_norm, ev_k_norm, ev_w_out,
               od_w_in, od_q_norm, od_k_norm, od_sink, od_pool_w, od_pool_scale, od_w_out,
               ffn_w1, ffn_w3, ffn_w2, moe_router, moe_router_b, moe_w1, moe_w3, moe_w2)
    y_prompt, ks, vs = run_trunk(x_prompt, c_ctx[None, :], None, None, None, *weights)
    new_k = jnp.stack(ks, axis=1)
    new_v = jnp.stack(vs, axis=1)
    rope = axial_rope_tables(x_sample.shape[1])
    y_sample, _, _ = run_trunk(x_sample, c, cache_k, cache_v, rope, *weights)
    return (y_prompt, y_sample, new_k, new_v)
```

```python
import functools

import jax
import jax.numpy as jnp
from jax import lax
from jax.experimental import pallas as pl
from jax.experimental.pallas import tpu as pltpu

F32 = jnp.float32
BF16 = jnp.bfloat16

D_MODEL = 1024
N_SEQ_CTX = 32
L_CTX = 256
N_SEQ_LAT = 4
L_LAT = 1024
DEPTH = 4
PAST = 512
GRID_W = 64
HD = 64
N_Q = 8
N_KV = 2
GROUP = N_Q // N_KV
Q_W = N_Q * HD
KV_W = N_KV * HD
CONV_W = 512
POOL_W = 512
POOL_WINDOWS = (2, 4, 8, 16)
POOL_G = 128
EVEN_IN = 3 * CONV_W + Q_W + 2 * KV_W
ODD_IN = Q_W + 2 * KV_W + POOL_W
WINDOW = 128
D_FF = 2816
N_EXP = 8
D_FF_E = 1024
ROPE_THETA = 10000.0
EPS = 1e-6

N_CTX_TOK = N_SEQ_CTX * L_CTX
N_LAT_TOK = N_SEQ_LAT * L_LAT
N_TOK = N_CTX_TOK + N_LAT_TOK
MOD_ROWS = 16

LANES = 128
VMEM_LIMIT = 56 * 1024 * 1024

TM_IN = 512
TM = 1024
TQ = 256
TF_FFN = 256
TF_MOE = 512
NEG_INF = float("-inf")


def _cparams(sem):
    return pltpu.CompilerParams(dimension_semantics=sem, vmem_limit_bytes=VMEM_LIMIT)


def _mod_row(i, tm):
    n_ctx = N_CTX_TOK // tm
    return jnp.where(i < n_ctx, 0, 1 + (i - n_ctx) // (L_LAT // tm))


def _normmod(x, g, scale, shift):
    ms = jnp.mean(x * x, axis=-1, keepdims=True)
    y = x * lax.rsqrt(ms + EPS) * g
    return y * (1.0 + scale) + shift


def _silu(x):
    return x * jax.nn.sigmoid(x)


def _dot(a, b):
    return jnp.dot(a, b, preferred_element_type=F32)


def _dot_nt(a, b):
    return lax.dot_general(a, b, (((1,), (1,)), ((), ())), preferred_element_type=F32)


def _cast_rows(src_ref, dst_ref, rows, chunk=256):
    for r in range(0, rows, chunk):
        dst_ref[r:r + chunk, :] = src_ref[r:r + chunk, :].astype(dst_ref.dtype)


def _mod_kernel(c_ref, w_ref, b_ref, o_ref):
    s = _silu(c_ref[...]).astype(BF16)
    o_ref[0] = _dot(s, w_ref[0].astype(BF16)) + b_ref[0]


def _modulation(cond, w_mod, b_mod):
    tn = 1536
    return pl.pallas_call(
        _mod_kernel,
        grid=(DEPTH, 6 * D_MODEL // tn),
        in_specs=[
            pl.BlockSpec((MOD_ROWS, D_MODEL), lambda l, j: (0, 0)),
            pl.BlockSpec((1, D_MODEL, tn), lambda l, j: (l, 0, j)),
            pl.BlockSpec((1, 1, tn), lambda l, j: (l, 0, j)),
        ],
        out_specs=pl.BlockSpec((1, MOD_ROWS, tn), lambda l, j: (l, 0, j)),
        out_shape=jax.ShapeDtypeStruct((DEPTH, MOD_ROWS, 6 * D_MODEL), F32),
        compiler_params=_cparams(("arbitrary", "arbitrary")),
        name="modulation",
    )(cond, w_mod, b_mod.reshape(DEPTH, 1, 6 * D_MODEL))


def _head_rms(t, ones_bd, gain):
    sq = t * t
    hi = sq.astype(BF16)
    lo = (sq - hi.astype(F32)).astype(BF16)
    ssq = _dot(hi, ones_bd) + _dot(lo, ones_bd)
    return t * lax.rsqrt(ssq * (1.0 / HD) + EPS) * gain


def _rope(t, cos, sin_signed):
    lane = lax.broadcasted_iota(jnp.int32, (t.shape[0], LANES), 1)
    first = (lane & 31) < 16
    outs = []
    for c in range(t.shape[1] // LANES):
        tc = t[:, c * LANES:(c + 1) * LANES]
        nxt = pltpu.roll(tc, LANES - 16, axis=1)
        prv = pltpu.roll(tc, 16, axis=1)
        outs.append(tc * cos + jnp.where(first, nxt, prv) * sin_signed)
    return outs[0] if len(outs) == 1 else jnp.concatenate(outs, axis=1)


def _in_proj_kernel(even, x_ref, mod_ref, g_ref, w_ref, onesq_ref, onesk_ref, qg_ref, kg_ref,
                    cos_ref, sin_ref, *rest):
    if even:
        zc_ref, q_ref, k_ref, v_ref, wbf = rest
        q0 = 3 * CONV_W
    else:
        q_ref, k_ref, v_ref, xd_ref, wbf = rest
        q0 = 0
    k0 = q0 + Q_W
    v0 = k0 + KV_W

    @pl.when(pl.program_id(0) == 0)
    def _():
        _cast_rows(w_ref, wbf, D_MODEL)

    h = _normmod(x_ref[...], g_ref[...], mod_ref[0, 1:2, :], mod_ref[0, 0:1, :]).astype(BF16)
    cos = cos_ref[...]
    sin = sin_ref[...]

    q = _dot(h, wbf[:, q0:q0 + Q_W])
    q = _rope(_head_rms(q, onesq_ref[...], qg_ref[...]), cos, sin) * (HD ** -0.5)
    q_ref[...] = q.astype(BF16)

    k = _dot(h, wbf[:, k0:k0 + KV_W])
    k = _rope(_head_rms(k, onesk_ref[...], kg_ref[...]), cos, sin)
    k_ref[...] = k.astype(BF16)

    v_ref[...] = _dot(h, wbf[:, v0:v0 + KV_W]).astype(BF16)

    if even:
        zc_ref[...] = _dot(h, wbf[:, 0:3 * CONV_W]).astype(BF16)
    else:
        xd_ref[...] = _dot(h, wbf[:, v0 + KV_W:v0 + KV_W + POOL_W])


def _in_proj(even, x, mod_l, g, w, ones_q, ones_k, q_gain, k_gain, cos_tab, sin_tab):
    tm = TM_IN
    n_in = EVEN_IN if even else ODD_IN
    n_ctx = N_CTX_TOK // tm
    per_seq = L_LAT // tm

    def rope_idx(i):
        return (jnp.where(i < n_ctx, 0, per_seq + (i - n_ctx) % per_seq), 0)

    row = lambda i: (i, 0)
    const = lambda i: (0, 0)
    in_specs = [
        pl.BlockSpec((tm, D_MODEL), row),
        pl.BlockSpec((1, 6, D_MODEL), lambda i: (_mod_row(i, tm), 0, 0)),
        pl.BlockSpec((1, D_MODEL), const),
        pl.BlockSpec((D_MODEL, n_in), const),
        pl.BlockSpec((Q_W, Q_W), const),
        pl.BlockSpec((KV_W, KV_W), const),
        pl.BlockSpec((1, Q_W), const),
        pl.BlockSpec((1, KV_W), const),
        pl.BlockSpec((tm, LANES), rope_idx),
        pl.BlockSpec((tm, LANES), rope_idx),
    ]
    qkv_specs = [pl.BlockSpec((tm, Q_W), row), pl.BlockSpec((tm, KV_W), row), pl.BlockSpec((tm, KV_W), row)]
    qkv_shapes = [jax.ShapeDtypeStruct((N_TOK, Q_W), BF16), jax.ShapeDtypeStruct((N_TOK, KV_W), BF16),
                  jax.ShapeDtypeStruct((N_TOK, KV_W), BF16)]
    if even:
        out_specs = [pl.BlockSpec((tm, 3 * CONV_W), row)] + qkv_specs
        out_shape = [jax.ShapeDtypeStruct((N_TOK, 3 * CONV_W), BF16)] + qkv_shapes
    else:
        out_specs = qkv_specs + [pl.BlockSpec((tm, POOL_W), row)]
        out_shape = qkv_shapes + [jax.ShapeDtypeStruct((N_TOK, POOL_W), F32)]
    return pl.pallas_call(
        functools.partial(_in_proj_kernel, even),
        grid=(N_TOK // tm,),
        in_specs=in_specs,
        out_specs=out_specs,
        out_shape=out_shape,
        scratch_shapes=[pltpu.VMEM((D_MODEL, n_in), BF16)],
        compiler_params=_cparams(("arbitrary",)),
        name="in_proj_even" if even else "in_proj_odd",
    )(x, mod_l, g, w, ones_q, ones_k, q_gain, k_gain, cos_tab, sin_tab)


def _dup_heads(t):
    lane = lax.broadcasted_iota(jnp.int32, t.shape, 1)
    swapped = pltpu.roll(t, HD, axis=1)
    low = lane < HD
    return jnp.where(low, t, swapped), jnp.where(low, swapped, t)


def _softmax_pv(scores, values, sink):
    m = scores[0].max(axis=-1, keepdims=True)
    for s in scores[1:]:
        m = jnp.maximum(m, s.max(axis=-1, keepdims=True))
    if sink is not None:
        m = jnp.maximum(m, sink)
    den = None
    acc = None
    for s, v in zip(scores, values):
        e = jnp.exp(s - m)
        d = e.sum(axis=-1, keepdims=True)
        a = _dot(e.astype(BF16), v)
        den = d if den is None else den + d
        acc = a if acc is None else acc + a
    if sink is not None:
        den = den + jnp.exp(sink - m)
    return acc / den


def _ctx_attn_kernel(has_sink, *refs):
    if has_sink:
        sink_ref, q_ref, k_ref, v_ref, o_ref, nk_ref, nv_ref = refs
    else:
        q_ref, k_ref, v_ref, o_ref, nk_ref, nv_ref = refs
        sink_ref = None
    k = k_ref[...].astype(F32)
    v = v_ref[...].astype(F32)
    k_sw = pltpu.roll(k, HD, axis=1)
    v_sw = pltpu.roll(v, HD, axis=1)
    nk_ref[0, 0] = k[:, 0:HD]
    nk_ref[0, 1] = k_sw[:, 0:HD]
    nv_ref[0, 0] = v[:, 0:HD]
    nv_ref[0, 1] = v_sw[:, 0:HD]
    lane = lax.broadcasted_iota(jnp.int32, (L_CTX, LANES), 1)
    low = lane < HD
    k2 = (jnp.where(low, k, k_sw).astype(BF16), jnp.where(low, k_sw, k).astype(BF16))
    v2 = (jnp.where(low, v, v_sw).astype(BF16), jnp.where(low, v_sw, v).astype(BF16))
    zero = jnp.zeros((L_CTX, LANES), BF16)
    for pair in range(N_Q // 2):
        kv = pair // (GROUP // 2)
        qp = q_ref[:, pair * LANES:(pair + 1) * LANES]
        outs = []
        for half in range(2):
            qm = jnp.where(low if half == 0 else jnp.logical_not(low), qp, zero)
            s = _dot_nt(qm, k2[kv])
            sink = sink_ref[2 * pair + half] if has_sink else None
            outs.append(_softmax_pv([s], [v2[kv]], sink))
        o_ref[:, pair * LANES:(pair + 1) * LANES] = jnp.where(low, outs[0], outs[1]).astype(BF16)


def _ctx_attn(q, k, v, sink):
    has_sink = sink is not None
    row = lambda b: (b, 0)
    in_specs = [pl.BlockSpec((L_CTX, Q_W), row), pl.BlockSpec((L_CTX, KV_W), row), pl.BlockSpec((L_CTX, KV_W), row)]
    args = [q, k, v]
    if has_sink:
        in_specs = [pl.BlockSpec(memory_space=pltpu.SMEM)] + in_specs
        args = [sink] + args
    kv_spec = pl.BlockSpec((1, N_KV, L_CTX, HD), lambda b: (b, 0, 0, 0))
    kv_shape = jax.ShapeDtypeStruct((N_SEQ_CTX, N_KV, L_CTX, HD), F32)
    return pl.pallas_call(
        functools.partial(_ctx_attn_kernel, has_sink),
        grid=(N_SEQ_CTX,),
        in_specs=in_specs,
        out_specs=[pl.BlockSpec((L_CTX, Q_W), row), kv_spec, kv_spec],
        out_shape=[jax.ShapeDtypeStruct((N_CTX_TOK, Q_W), BF16), kv_shape, kv_shape],
        compiler_params=_cparams(("arbitrary",)),
        name="ctx_attn_sink" if has_sink else "ctx_attn",
    )(*args)


def _lat_attn_kernel(windowed, *refs):
    if windowed:
        sink_ref, q_ref, k_ref, v_ref, ck_ref, cv_ref, dup_ref, o_ref, k2s, v2s, ck2s, cv2s = refs
    else:
        q_ref, k_ref, v_ref, ck_ref, cv_ref, dup_ref, o_ref, k2s, v2s, ck2s, cv2s = refs
        sink_ref = None
    j = pl.program_id(1)

    @pl.when(j == 0)
    def _():
        ka, kb = _dup_heads(k_ref[...].astype(F32))
        va, vb = _dup_heads(v_ref[...].astype(F32))
        k2s[0] = ka.astype(BF16)
        k2s[1] = kb.astype(BF16)
        v2s[0] = va.astype(BF16)
        v2s[1] = vb.astype(BF16)
        dup = dup_ref[...]
        for kv in range(N_KV):
            ck2s[kv] = _dot(ck_ref[0, 0, kv].astype(BF16), dup).astype(BF16)
            cv2s[kv] = _dot(cv_ref[0, 0, kv].astype(BF16), dup).astype(BF16)

    lane = lax.broadcasted_iota(jnp.int32, (TQ, LANES), 1)
    low = lane < HD
    zero = jnp.zeros((TQ, LANES), BF16)
    if windowed:
        n_loc = 2 * TQ
        start = pl.multiple_of(jnp.clip(j * TQ - WINDOW, 0, L_LAT - n_loc), WINDOW)
        qpos = j * TQ + lax.broadcasted_iota(jnp.int32, (TQ, n_loc), 0)
        kpos = start + lax.broadcasted_iota(jnp.int32, (TQ, n_loc), 1)
        valid = jnp.abs(qpos - kpos) <= WINDOW
    for pair in range(N_Q // 2):
        kv = pair // (GROUP // 2)
        qp = q_ref[:, pair * LANES:(pair + 1) * LANES]
        if windowed:
            k_own = k2s[kv, pl.ds(start, n_loc), :]
            v_own = v2s[kv, pl.ds(start, n_loc), :]
        else:
            k_own = k2s[kv]
            v_own = v2s[kv]
        outs = []
        for half in range(2):
            qm = jnp.where(low if half == 0 else jnp.logical_not(low), qp, zero)
            s_ctx = _dot_nt(qm, ck2s[kv])
            s_own = _dot_nt(qm, k_own)
            if windowed:
                s_own = jnp.where(valid, s_own, NEG_INF)
            sink = sink_ref[2 * pair + half] if windowed else None
            outs.append(_softmax_pv([s_ctx, s_own], [cv2s[kv], v_own], sink))
        o_ref[:, pair * LANES:(pair + 1) * LANES] = jnp.where(low, outs[0], outs[1]).astype(BF16)


def _lat_attn(layer, q, k, v, cache_k, cache_v, dup, sink):
    windowed = sink is not None
    n_qt = L_LAT // TQ
    ctx_tiles = N_CTX_TOK // TQ
    ctx_seqs = N_CTX_TOK // L_LAT
    cache_spec = pl.BlockSpec((1, 1, N_KV, PAST, HD), lambda b, j: (b, layer, 0, 0, 0))
    in_specs = [
        pl.BlockSpec((TQ, Q_W), lambda b, j: (ctx_tiles + b * n_qt + j, 0)),
        pl.BlockSpec((L_LAT, KV_W), lambda b, j: (ctx_seqs + b, 0)),
        pl.BlockSpec((L_LAT, KV_W), lambda b, j: (ctx_seqs + b, 0)),
        cache_spec,
        cache_spec,
        pl.BlockSpec((HD, LANES), lambda b, j: (0, 0)),
    ]
    args = [q, k, v, cache_k, cache_v, dup]
    if windowed:
        in_specs = [pl.BlockSpec(memory_space=pltpu.SMEM)] + in_specs
        args = [sink] + args
    return pl.pallas_call(
        functools.partial(_lat_attn_kernel, windowed),
        grid=(N_SEQ_LAT, n_qt),
        in_specs=in_specs,
        out_specs=pl.BlockSpec((TQ, Q_W), lambda b, j: (b * n_qt + j, 0)),
        out_shape=jax.ShapeDtypeStruct((N_LAT_TOK, Q_W), BF16),
        scratch_shapes=[
            pltpu.VMEM((N_KV, L_LAT, LANES), BF16),
            pltpu.VMEM((N_KV, L_LAT, LANES), BF16),
            pltpu.VMEM((N_KV, PAST, LANES), BF16),
            pltpu.VMEM((N_KV, PAST, LANES), BF16),
        ],
        compiler_params=_cparams(("arbitrary", "arbitrary")),
        name="lat_attn_window" if windowed else "lat_attn",
    )(*args)


def _seq_pos(i, width):
    r = lax.broadcasted_iota(jnp.int32, (TM, width), 0)
    is_ctx = i < N_CTX_TOK // TM
    seq_len = jnp.where(is_ctx, L_CTX, L_LAT)
    return r & (seq_len - 1), seq_len


def _shift_rows(t, j, pos, seq_len):
    if j == 0:
        return t
    moved = pltpu.roll(t, (-j) % TM, axis=0)
    ok = (pos + j >= 0) & (pos + j < seq_len)
    return jnp.where(ok, moved, 0.0)


def _conv_mixer(zc_ref, cw_ref, pos, seq_len):
    bg = zc_ref[:, 0:CONV_W].astype(F32)
    u = zc_ref[:, CONV_W:2 * CONV_W].astype(F32) * zc_ref[:, 2 * CONV_W:3 * CONV_W].astype(F32)
    y = (_shift_rows(u, -1, pos, seq_len) * cw_ref[0:1, :] + u * cw_ref[1:2, :]
         + _shift_rows(u, 1, pos, seq_len) * cw_ref[2:3, :])
    return (bg * y).astype(BF16)


def _pool_mixer(xd_ref, pw_ref, ps_ref, pos, seq_len):
    outs = []
    for gi, w in enumerate(POOL_WINDOWS):
        xg = xd_ref[:, gi * POOL_G:(gi + 1) * POOL_G]
        tot = None
        for j in range(-(w // 2), w - w // 2):
            sh = _shift_rows(xg, j, pos, seq_len)
            tot = sh if tot is None else tot + sh
        lo = jnp.clip(pos - w // 2, 0, seq_len)
        hi = jnp.clip(pos - w // 2 + w, 0, seq_len)
        cnt = (hi - lo).astype(F32)
        d = tot / cnt - xg
        outs.append(_dot(d.astype(BF16), pw_ref[gi].astype(BF16)))
    return (jnp.concatenate(outs, axis=1) * ps_ref[...]).astype(BF16)


def _mix_out_kernel(even, x_ref, mod_ref, oc_ref, ol_ref, w_ref, *rest):
    if even:
        zc_ref, cw_ref, out_ref, wbf = rest
    else:
        xd_ref, pw_ref, ps_ref, out_ref, wbf = rest
    i = pl.program_id(0)

    @pl.when(i == 0)
    def _():
        _cast_rows(w_ref, wbf, D_MODEL)

    pos, seq_len = _seq_pos(i, CONV_W if even else POOL_G)
    o = jnp.where(i < N_CTX_TOK // TM, oc_ref[...], ol_ref[...])
    if even:
        ya = _conv_mixer(zc_ref, cw_ref, pos, seq_len)
        y = _dot(ya, wbf[0:CONV_W, :]) + _dot(o, wbf[CONV_W:, :])
    else:
        yd = _pool_mixer(xd_ref, pw_ref, ps_ref, pos, seq_len)
        y = _dot(o, wbf[0:Q_W, :]) + _dot(yd, wbf[Q_W:, :])
    out_ref[...] = x_ref[...] + mod_ref[0, 2:3, :] * y


def _mix_out(even, x, mod_l, o_ctx, o_lat, w_out, *extra):
    n_ctx = N_CTX_TOK // TM
    n_lat = N_LAT_TOK // TM
    row = lambda i: (i, 0)
    const = lambda i: (0, 0)
    in_specs = [
        pl.BlockSpec((TM, D_MODEL), row),
        pl.BlockSpec((1, 6, D_MODEL), lambda i: (_mod_row(i, TM), 0, 0)),
        pl.BlockSpec((TM, Q_W), lambda i: (jnp.minimum(i, n_ctx - 1), 0)),
        pl.BlockSpec((TM, Q_W), lambda i: (jnp.clip(i - n_ctx, 0, n_lat - 1), 0)),
        pl.BlockSpec((D_MODEL, D_MODEL), const),
    ]
    if even:
        in_specs += [pl.BlockSpec((TM, 3 * CONV_W), row), pl.BlockSpec((3, CONV_W), const)]
    else:
        in_specs += [pl.BlockSpec((TM, POOL_W), row),
                     pl.BlockSpec((len(POOL_WINDOWS), POOL_G, POOL_G), lambda i: (0, 0, 0)),
                     pl.BlockSpec((1, POOL_W), const)]
    return pl.pallas_call(
        functools.partial(_mix_out_kernel, even),
        grid=(N_TOK // TM,),
        in_specs=in_specs,
        out_specs=pl.BlockSpec((TM, D_MODEL), row),
        out_shape=jax.ShapeDtypeStruct((N_TOK, D_MODEL), F32),
        scratch_shapes=[pltpu.VMEM((D_MODEL, D_MODEL), BF16)],
        compiler_params=_cparams(("arbitrary",)),
        name="mix_out_even" if even else "mix_out_odd",
    )(x, mod_l, o_ctx, o_lat, w_out, *extra)


def _ffn_kernel(x_ref, mod_ref, g_ref, w1_ref, w3_ref, w2_ref, out_ref, w1s, w3s, w2s, hs, acc):
    i = pl.program_id(0)
    f = pl.program_id(1)

    @pl.when(i == 0)
    def _():
        w1s[f] = w1_ref[0].astype(BF16)
        w3s[f] = w3_ref[0].astype(BF16)
        w2s[f] = w2_ref[0].astype(BF16)

    @pl.when(f == 0)
    def _():
        hs[...] = _normmod(x_ref[...], g_ref[...], mod_ref[0, 4:5, :], mod_ref[0, 3:4, :]).astype(BF16)
        acc[...] = jnp.zeros_like(acc)

    h = hs[...]
    hid = (_silu(_dot(h, w1s[f])) * _dot(h, w3s[f])).astype(BF16)
    acc[...] += _dot(hid, w2s[f])

    @pl.when(f == pl.num_programs(1) - 1)
    def _():
        out_ref[...] = x_ref[...] + mod_ref[0, 5:6, :] * acc[...]


def _ffn(layer_i, x, mod_l, g, w1, w3, w2):
    nf = D_FF // TF_FFN
    wf = lambda i, f: jnp.where(i == 0, f, nf - 1)
    row = lambda i, f: (i, 0)
    return pl.pallas_call(
        _ffn_kernel,
        grid=(N_TOK // TM, nf),
        in_specs=[
            pl.BlockSpec((TM, D_MODEL), row),
            pl.BlockSpec((1, 6, D_MODEL), lambda i, f: (_mod_row(i, TM), 0, 0)),
            pl.BlockSpec((1, D_MODEL), lambda i, f: (0, 0)),
            pl.BlockSpec((1, D_MODEL, TF_FFN), lambda i, f: (layer_i, 0, wf(i, f))),
            pl.BlockSpec((1, D_MODEL, TF_FFN), lambda i, f: (layer_i, 0, wf(i, f))),
            pl.BlockSpec((1, TF_FFN, D_MODEL), lambda i, f: (layer_i, wf(i, f), 0)),
        ],
        out_specs=pl.BlockSpec((TM, D_MODEL), row),
        out_shape=jax.ShapeDtypeStruct((N_TOK, D_MODEL), F32),
        scratch_shapes=[
            pltpu.VMEM((nf, D_MODEL, TF_FFN), BF16),
            pltpu.VMEM((nf, D_MODEL, TF_FFN), BF16),
            pltpu.VMEM((nf, TF_FFN, D_MODEL), BF16),
            pltpu.VMEM((TM, D_MODEL), BF16),
            pltpu.VMEM((TM, D_MODEL), F32),
        ],
        compiler_params=_cparams(("arbitrary", "arbitrary")),
        name="ffn",
    )(x, mod_l, g, w1, w3, w2)


def _split_bf16(t):
    hi = t.astype(BF16)
    return hi, (t - hi.astype(F32)).astype(BF16)


def _router_gates(h, rw_ref, rb_ref, gates):
    h_hi, h_lo = _split_bf16(h)
    w_hi, w_lo = _split_bf16(rw_ref[...])
    logits = _dot(h_hi, w_hi) + _dot(h_hi, w_lo) + _dot(h_lo, w_hi) + rb_ref[...]
    lane = lax.broadcasted_iota(jnp.int32, logits.shape, 1).astype(F32)
    logits = jnp.where(lane < N_EXP, logits, NEG_INF)
    m1 = logits.max(axis=-1, keepdims=True)
    i1 = jnp.where(logits == m1, lane, float(LANES)).min(axis=-1, keepdims=True)
    rest = jnp.where(lane == i1, NEG_INF, logits)
    m2 = rest.max(axis=-1, keepdims=True)
    i2 = jnp.where(rest == m2, lane, float(LANES)).min(axis=-1, keepdims=True)
    e2 = jnp.exp(m2 - m1)
    den = 1.0 + e2
    g1 = 1.0 / den
    g2 = e2 / den
    for e in range(N_EXP):
        ge = jnp.where(i1 == e, g1, 0.0) + jnp.where(i2 == e, g2, 0.0)
        gates[e] = jnp.broadcast_to(ge, (TM, LANES))


def _moe_kernel(x_ref, mod_ref, g_ref, rw_ref, rb_ref, w1_ref, w3_ref, w2_ref, out_ref, hs, acc, gates):
    e = pl.program_id(1)
    f = pl.program_id(2)
    first = (e == 0) & (f == 0)
    last = (e == pl.num_programs(1) - 1) & (f == pl.num_programs(2) - 1)

    @pl.when(first)
    def _():
        h = _normmod(x_ref[...], g_ref[...], mod_ref[0, 4:5, :], mod_ref[0, 3:4, :])
        hs[...] = h.astype(BF16)
        acc[...] = jnp.zeros_like(acc)
        _router_gates(h, rw_ref, rb_ref, gates)

    h = hs[...]
    hid = _silu(_dot(h, w1_ref[0, 0].astype(BF16))) * _dot(h, w3_ref[0, 0].astype(BF16))
    ge = gates[e]
    hid = hid * jnp.concatenate([ge] * (TF_MOE // LANES), axis=1)
    acc[...] += _dot(hid.astype(BF16), w2_ref[0, 0].astype(BF16))

    @pl.when(last)
    def _():
        out_ref[...] = x_ref[...] + mod_ref[0, 5:6, :] * acc[...]


def _moe(layer_i, x, mod_l, g, router_w, router_b, w1, w3, w2):
    nf = D_FF_E // TF_MOE
    row = lambda i, e, f: (i, 0)
    const = lambda i, e, f: (0, 0)
    return pl.pallas_call(
        _moe_kernel,
        grid=(N_TOK // TM, N_EXP, nf),
        in_specs=[
            pl.BlockSpec((TM, D_MODEL), row),
            pl.BlockSpec((1, 6, D_MODEL), lambda i, e, f: (_mod_row(i, TM), 0, 0)),
            pl.BlockSpec((1, D_MODEL), const),
            pl.BlockSpec((D_MODEL, LANES), const),
            pl.BlockSpec((1, LANES), const),
            pl.BlockSpec((1, 1, D_MODEL, TF_MOE), lambda i, e, f: (layer_i, e, 0, f)),
            pl.BlockSpec((1, 1, D_MODEL, TF_MOE), lambda i, e, f: (layer_i, e, 0, f)),
            pl.BlockSpec((1, 1, TF_MOE, D_MODEL), lambda i, e, f: (layer_i, e, f, 0)),
        ],
        out_specs=pl.BlockSpec((TM, D_MODEL), row),
        out_shape=jax.ShapeDtypeStruct((N_TOK, D_MODEL), F32),
        scratch_shapes=[
            pltpu.VMEM((TM, D_MODEL), BF16),
            pltpu.VMEM((TM, D_MODEL), F32),
            pltpu.VMEM((N_EXP, TM, LANES), F32),
        ],
        compiler_params=_cparams(("arbitrary", "arbitrary", "arbitrary")),
        name="moe",
    )(x, mod_l, g, router_w, router_b, w1, w3, w2)


def _rope_tables():
    n_rows = L_LAT // GRID_W
    rows = jnp.repeat(jnp.arange(n_rows, dtype=F32), GRID_W)
    cols = jnp.tile(jnp.arange(GRID_W, dtype=F32), n_rows)
    quarter = HD // 4
    inv = ROPE_THETA ** (-jnp.arange(quarter, dtype=F32) / quarter)
    ang_r = rows[:, None] * inv
    ang_c = cols[:, None] * inv
    cos = jnp.concatenate([jnp.cos(ang_r)] * 2 + [jnp.cos(ang_c)] * 2, axis=1)
    sin = jnp.concatenate([-jnp.sin(ang_r), jnp.sin(ang_r), -jnp.sin(ang_c), jnp.sin(ang_c)], axis=1)
    cos = jnp.concatenate([cos, cos], axis=1)
    sin = jnp.concatenate([sin, sin], axis=1)
    cos = jnp.concatenate([jnp.ones((L_LAT, LANES), F32), cos], axis=0)
    sin = jnp.concatenate([jnp.zeros((L_LAT, LANES), F32), sin], axis=0)
    return cos, sin


def _block_ones(width):
    r = jnp.arange(width) // HD
    return (r[:, None] == r[None, :]).astype(BF16)


def kernel(x_prompt, x_sample, cache_k, cache_v, c, c_ctx, norm1, norm2, w_mod, b_mod, ev_w_in, ev_conv, ev_q_norm, ev_k_norm, ev_w_out, od_w_in, od_q_norm, od_k_norm, od_sink, od_pool_w, od_pool_scale, od_w_out, ffn_w1, ffn_w3, ffn_w2, moe_router, moe_router_b, moe_w1, moe_w3, moe_w2):
    x = jnp.concatenate([x_prompt.reshape(N_CTX_TOK, D_MODEL), x_sample.reshape(N_LAT_TOK, D_MODEL)], axis=0)
    cond = jnp.concatenate([c_ctx[None, :], c, jnp.zeros((MOD_ROWS - 1 - N_SEQ_LAT, D_MODEL), F32)], axis=0)
    mod = _modulation(cond, w_mod, b_mod).reshape(DEPTH, MOD_ROWS, 6, D_MODEL)

    cos_tab, sin_tab = _rope_tables()
    ones_q = _block_ones(Q_W)
    ones_k = _block_ones(KV_W)
    eye = jnp.eye(HD, dtype=BF16)
    dup = jnp.concatenate([eye, eye], axis=1)

    new_k, new_v = [], []
    for l in range(DEPTH):
        i = l // 2
        even = l % 2 == 0
        mod_l = mod[l]
        g1 = norm1[l][None, :]
        g2 = norm2[l][None, :]
        if even:
            q_gain, k_gain = ev_q_norm[i], ev_k_norm[i]
            w_in, w_out = ev_w_in[i], ev_w_out[i]
        else:
            q_gain, k_gain = od_q_norm[i], od_k_norm[i]
            w_in, w_out = od_w_in[i], od_w_out[i]
        q_gain = jnp.tile(q_gain, N_Q)[None, :]
        k_gain = jnp.tile(k_gain, N_KV)[None, :]
        outs = _in_proj(even, x, mod_l, g1, w_in, ones_q, ones_k, q_gain, k_gain, cos_tab, sin_tab)
        if even:
            zc, q, k, v = outs
            sink = None
        else:
            q, k, v, xd = outs
            sink = od_sink[i]
        o_ctx, nk, nv = _ctx_attn(q, k, v, sink)
        o_lat = _lat_attn(l, q, k, v, cache_k, cache_v, dup, sink)
        new_k.append(nk)
        new_v.append(nv)
        if even:
            x = _mix_out(True, x, mod_l, o_ctx, o_lat, w_out, zc, ev_conv[i])
            x = _ffn(i, x, mod_l, g2, ffn_w1, ffn_w3, ffn_w2)
        else:
            x = _mix_out(False, x, mod_l, o_ctx, o_lat, w_out, xd, od_pool_w[i], od_pool_scale[i][None, :])
            rw = jnp.pad(moe_router[i], ((0, 0), (0, LANES - N_EXP)))
            rb = jnp.pad(moe_router_b[i], (0, LANES - N_EXP))[None, :]
            x = _moe(i, x, mod_l, g2, rw, rb, moe_w1, moe_w3, moe_w2)

    y_prompt = x[:N_CTX_TOK].reshape(N_SEQ_CTX, L_CTX, D_MODEL)
    y_sample = x[N_CTX_TOK:].reshape(N_SEQ_LAT, L_LAT, D_MODEL)
    return (y_prompt, y_sample, jnp.stack(new_k, axis=1), jnp.stack(new_v, axis=1))
```

```python
import functools

import jax
import jax.numpy as jnp
from jax import lax
from jax.experimental import pallas as pl
from jax.experimental.pallas import tpu as pltpu

F32 = jnp.float32
BF16 = jnp.bfloat16

D_MODEL = 1024
N_SEQ_CTX = 32
L_CTX = 256
N_SEQ_LAT = 4
L_LAT = 1024
DEPTH = 4
PAST = 512
GRID_W = 64
HD = 64
N_Q = 8
N_KV = 2
GROUP = N_Q // N_KV
Q_W = N_Q * HD
KV_W = N_KV * HD
CONV_W = 512
POOL_W = 512
POOL_WINDOWS = (2, 4, 8, 16)
POOL_G = 128
EVEN_IN = 3 * CONV_W + Q_W + 2 * KV_W
ODD_IN = Q_W + 2 * KV_W + POOL_W
WINDOW = 128
D_FF = 2816
N_EXP = 8
TOP_K = 2
D_FF_E = 1024
ROPE_THETA = 10000.0
EPS = 1e-6

N_CTX_TOK = N_SEQ_CTX * L_CTX
N_LAT_TOK = N_SEQ_LAT * L_LAT
N_TOK = N_CTX_TOK + N_LAT_TOK
MOD_ROWS = 16

LANES = 128
VMEM_LIMIT = 56 * 1024 * 1024

TM_IN = 512
TM = 1024
TQ = 256
TF_FFN = 256
TMS = 512
TMC = 512
MOE_TILES = (TOP_K * N_TOK) // TMS + N_EXP
MOE_ROWS = MOE_TILES * TMS
NEG_INF = float("-inf")


def _cparams(sem):
    return pltpu.CompilerParams(dimension_semantics=sem, vmem_limit_bytes=VMEM_LIMIT)


def _mod_row(i, tm):
    n_ctx = N_CTX_TOK // tm
    return jnp.where(i < n_ctx, 0, 1 + (i - n_ctx) // (L_LAT // tm))


def _normmod(x, g, scale, shift):
    ms = jnp.mean(x * x, axis=-1, keepdims=True)
    y = x * lax.rsqrt(ms + EPS) * g
    return y * (1.0 + scale) + shift


def _silu(x):
    return x * jax.nn.sigmoid(x)


def _dot(a, b):
    return jnp.dot(a, b, preferred_element_type=F32)


def _dot_nt(a, b):
    return lax.dot_general(a, b, (((1,), (1,)), ((), ())), preferred_element_type=F32)


def _cast_rows(src_ref, dst_ref, rows, chunk=256):
    for r in range(0, rows, chunk):
        dst_ref[r:r + chunk, :] = src_ref[r:r + chunk, :].astype(dst_ref.dtype)


def _mod_kernel(c_ref, w_ref, b_ref, o_ref):
    s = _silu(c_ref[...]).astype(BF16)
    o_ref[0] = _dot(s, w_ref[0].astype(BF16)) + b_ref[0]


def _modulation(cond, w_mod, b_mod):
    tn = 1536
    return pl.pallas_call(
        _mod_kernel,
        grid=(DEPTH, 6 * D_MODEL // tn),
        in_specs=[
            pl.BlockSpec((MOD_ROWS, D_MODEL), lambda l, j: (0, 0)),
            pl.BlockSpec((1, D_MODEL, tn), lambda l, j: (l, 0, j)),
            pl.BlockSpec((1, 1, tn), lambda l, j: (l, 0, j)),
        ],
        out_specs=pl.BlockSpec((1, MOD_ROWS, tn), lambda l, j: (l, 0, j)),
        out_shape=jax.ShapeDtypeStruct((DEPTH, MOD_ROWS, 6 * D_MODEL), F32),
        compiler_params=_cparams(("arbitrary", "arbitrary")),
        name="modulation",
    )(cond, w_mod, b_mod.reshape(DEPTH, 1, 6 * D_MODEL))


def _head_rms(t, ones_bd, gain):
    sq = t * t
    hi = sq.astype(BF16)
    lo = (sq - hi.astype(F32)).astype(BF16)
    ssq = _dot(hi, ones_bd) + _dot(lo, ones_bd)
    return t * lax.rsqrt(ssq * (1.0 / HD) + EPS) * gain


def _rope(t, cos, sin_signed):
    lane = lax.broadcasted_iota(jnp.int32, (t.shape[0], LANES), 1)
    first = (lane & 31) < 16
    outs = []
    for c in range(t.shape[1] // LANES):
        tc = t[:, c * LANES:(c + 1) * LANES]
        nxt = pltpu.roll(tc, LANES - 16, axis=1)
        prv = pltpu.roll(tc, 16, axis=1)
        outs.append(tc * cos + jnp.where(first, nxt, prv) * sin_signed)
    return outs[0] if len(outs) == 1 else jnp.concatenate(outs, axis=1)


def _in_proj_kernel(even, x_ref, mod_ref, g_ref, w_ref, onesq_ref, onesk_ref, qg_ref, kg_ref,
                    cos_ref, sin_ref, *rest):
    if even:
        zc_ref, q_ref, k_ref, v_ref, wbf = rest
        q0 = 3 * CONV_W
    else:
        q_ref, k_ref, v_ref, xd_ref, wbf = rest
        q0 = 0
    k0 = q0 + Q_W
    v0 = k0 + KV_W

    @pl.when(pl.program_id(0) == 0)
    def _():
        _cast_rows(w_ref, wbf, D_MODEL)

    h = _normmod(x_ref[...], g_ref[...], mod_ref[0, 1:2, :], mod_ref[0, 0:1, :]).astype(BF16)
    cos = cos_ref[...]
    sin = sin_ref[...]

    q = _dot(h, wbf[:, q0:q0 + Q_W])
    q = _rope(_head_rms(q, onesq_ref[...], qg_ref[...]), cos, sin) * (HD ** -0.5)
    q_ref[...] = q.astype(BF16)

    k = _dot(h, wbf[:, k0:k0 + KV_W])
    k = _rope(_head_rms(k, onesk_ref[...], kg_ref[...]), cos, sin)
    k_ref[...] = k.astype(BF16)

    v_ref[...] = _dot(h, wbf[:, v0:v0 + KV_W]).astype(BF16)

    if even:
        zc_ref[...] = _dot(h, wbf[:, 0:3 * CONV_W]).astype(BF16)
    else:
        xd_ref[...] = _dot(h, wbf[:, v0 + KV_W:v0 + KV_W + POOL_W])


def _in_proj(even, x, mod_l, g, w, ones_q, ones_k, q_gain, k_gain, cos_tab, sin_tab):
    tm = TM_IN
    n_in = EVEN_IN if even else ODD_IN
    n_ctx = N_CTX_TOK // tm
    per_seq = L_LAT // tm

    def rope_idx(i):
        return (jnp.where(i < n_ctx, 0, per_seq + (i - n_ctx) % per_seq), 0)

    row = lambda i: (i, 0)
    const = lambda i: (0, 0)
    in_specs = [
        pl.BlockSpec((tm, D_MODEL), row),
        pl.BlockSpec((1, 6, D_MODEL), lambda i: (_mod_row(i, tm), 0, 0)),
        pl.BlockSpec((1, D_MODEL), const),
        pl.BlockSpec((D_MODEL, n_in), const),
        pl.BlockSpec((Q_W, Q_W), const),
        pl.BlockSpec((KV_W, KV_W), const),
        pl.BlockSpec((1, Q_W), const),
        pl.BlockSpec((1, KV_W), const),
        pl.BlockSpec((tm, LANES), rope_idx),
        pl.BlockSpec((tm, LANES), rope_idx),
    ]
    qkv_specs = [pl.BlockSpec((tm, Q_W), row), pl.BlockSpec((tm, KV_W), row), pl.BlockSpec((tm, KV_W), row)]
    qkv_shapes = [jax.ShapeDtypeStruct((N_TOK, Q_W), BF16), jax.ShapeDtypeStruct((N_TOK, KV_W), BF16),
                  jax.ShapeDtypeStruct((N_TOK, KV_W), BF16)]
    if even:
        out_specs = [pl.BlockSpec((tm, 3 * CONV_W), row)] + qkv_specs
        out_shape = [jax.ShapeDtypeStruct((N_TOK, 3 * CONV_W), BF16)] + qkv_shapes
    else:
        out_specs = qkv_specs + [pl.BlockSpec((tm, POOL_W), row)]
        out_shape = qkv_shapes + [jax.ShapeDtypeStruct((N_TOK, POOL_W), F32)]
    return pl.pallas_call(
        functools.partial(_in_proj_kernel, even),
        grid=(N_TOK // tm,),
        in_specs=in_specs,
        out_specs=out_specs,
        out_shape=out_shape,
        scratch_shapes=[pltpu.VMEM((D_MODEL, n_in), BF16)],
        compiler_params=_cparams(("arbitrary",)),
        name="in_proj_even" if even else "in_proj_odd",
    )(x, mod_l, g, w, ones_q, ones_k, q_gain, k_gain, cos_tab, sin_tab)


def _dup_heads(t):
    lane = lax.broadcasted_iota(jnp.int32, t.shape, 1)
    swapped = pltpu.roll(t, HD, axis=1)
    low = lane < HD
    return jnp.where(low, t, swapped), jnp.where(low, swapped, t)


def _softmax_pv(scores, values, sink):
    m = scores[0].max(axis=-1, keepdims=True)
    for s in scores[1:]:
        m = jnp.maximum(m, s.max(axis=-1, keepdims=True))
    if sink is not None:
        m = jnp.maximum(m, sink)
    den = None
    acc = None
    for s, v in zip(scores, values):
        e = jnp.exp(s - m)
        d = e.sum(axis=-1, keepdims=True)
        a = _dot(e.astype(BF16), v)
        den = d if den is None else den + d
        acc = a if acc is None else acc + a
    if sink is not None:
        den = den + jnp.exp(sink - m)
    return acc / den


def _ctx_attn_kernel(has_sink, *refs):
    if has_sink:
        sink_ref, q_ref, k_ref, v_ref, o_ref, nk_ref, nv_ref = refs
    else:
        q_ref, k_ref, v_ref, o_ref, nk_ref, nv_ref = refs
        sink_ref = None
    k = k_ref[...].astype(F32)
    v = v_ref[...].astype(F32)
    k_sw = pltpu.roll(k, HD, axis=1)
    v_sw = pltpu.roll(v, HD, axis=1)
    nk_ref[0, 0] = k[:, 0:HD]
    nk_ref[0, 1] = k_sw[:, 0:HD]
    nv_ref[0, 0] = v[:, 0:HD]
    nv_ref[0, 1] = v_sw[:, 0:HD]
    lane = lax.broadcasted_iota(jnp.int32, (L_CTX, LANES), 1)
    low = lane < HD
    k2 = (jnp.where(low, k, k_sw).astype(BF16), jnp.where(low, k_sw, k).astype(BF16))
    v2 = (jnp.where(low, v, v_sw).astype(BF16), jnp.where(low, v_sw, v).astype(BF16))
    zero = jnp.zeros((L_CTX, LANES), BF16)
    for pair in range(N_Q // 2):
        kv = pair // (GROUP // 2)
        qp = q_ref[:, pair * LANES:(pair + 1) * LANES]
        outs = []
        for half in range(2):
            qm = jnp.where(low if half == 0 else jnp.logical_not(low), qp, zero)
            s = _dot_nt(qm, k2[kv])
            sink = sink_ref[2 * pair + half] if has_sink else None
            outs.append(_softmax_pv([s], [v2[kv]], sink))
        o_ref[:, pair * LANES:(pair + 1) * LANES] = jnp.where(low, outs[0], outs[1]).astype(BF16)


def _ctx_attn(q, k, v, sink):
    has_sink = sink is not None
    row = lambda b: (b, 0)
    in_specs = [pl.BlockSpec((L_CTX, Q_W), row), pl.BlockSpec((L_CTX, KV_W), row), pl.BlockSpec((L_CTX, KV_W), row)]
    args = [q, k, v]
    if has_sink:
        in_specs = [pl.BlockSpec(memory_space=pltpu.SMEM)] + in_specs
        args = [sink] + args
    kv_spec = pl.BlockSpec((1, N_KV, L_CTX, HD), lambda b: (b, 0, 0, 0))
    kv_shape = jax.ShapeDtypeStruct((N_SEQ_CTX, N_KV, L_CTX, HD), F32)
    return pl.pallas_call(
        functools.partial(_ctx_attn_kernel, has_sink),
        grid=(N_SEQ_CTX,),
        in_specs=in_specs,
        out_specs=[pl.BlockSpec((L_CTX, Q_W), row), kv_spec, kv_spec],
        out_shape=[jax.ShapeDtypeStruct((N_CTX_TOK, Q_W), BF16), kv_shape, kv_shape],
        compiler_params=_cparams(("arbitrary",)),
        name="ctx_attn_sink" if has_sink else "ctx_attn",
    )(*args)


def _lat_attn_kernel(windowed, *refs):
    if windowed:
        sink_ref, q_ref, k_ref, v_ref, ck_ref, cv_ref, dup_ref, o_ref, k2s, v2s, ck2s, cv2s = refs
    else:
        q_ref, k_ref, v_ref, ck_ref, cv_ref, dup_ref, o_ref, k2s, v2s, ck2s, cv2s = refs
        sink_ref = None
    j = pl.program_id(1)

    @pl.when(j == 0)
    def _():
        ka, kb = _dup_heads(k_ref[...].astype(F32))
        va, vb = _dup_heads(v_ref[...].astype(F32))
        k2s[0] = ka.astype(BF16)
        k2s[1] = kb.astype(BF16)
        v2s[0] = va.astype(BF16)
        v2s[1] = vb.astype(BF16)
        dup = dup_ref[...]
        for kv in range(N_KV):
            ck2s[kv] = _dot(ck_ref[0, 0, kv].astype(BF16), dup).astype(BF16)
            cv2s[kv] = _dot(cv_ref[0, 0, kv].astype(BF16), dup).astype(BF16)

    lane = lax.broadcasted_iota(jnp.int32, (TQ, LANES), 1)
    low = lane < HD
    zero = jnp.zeros((TQ, LANES), BF16)
    if windowed:
        n_loc = 2 * TQ
        start = pl.multiple_of(jnp.clip(j * TQ - WINDOW, 0, L_LAT - n_loc), WINDOW)
        qpos = j * TQ + lax.broadcasted_iota(jnp.int32, (TQ, n_loc), 0)
        kpos = start + lax.broadcasted_iota(jnp.int32, (TQ, n_loc), 1)
        valid = jnp.abs(qpos - kpos) <= WINDOW
    for pair in range(N_Q // 2):
        kv = pair // (GROUP // 2)
        qp = q_ref[:, pair * LANES:(pair + 1) * LANES]
        if windowed:
            k_own = k2s[kv, pl.ds(start, n_loc), :]
            v_own = v2s[kv, pl.ds(start, n_loc), :]
        else:
            k_own = k2s[kv]
            v_own = v2s[kv]
        outs = []
        for half in range(2):
            qm = jnp.where(low if half == 0 else jnp.logical_not(low), qp, zero)
            s_ctx = _dot_nt(qm, ck2s[kv])
            s_own = _dot_nt(qm, k_own)
            if windowed:
                s_own = jnp.where(valid, s_own, NEG_INF)
            sink = sink_ref[2 * pair + half] if windowed else None
            outs.append(_softmax_pv([s_ctx, s_own], [cv2s[kv], v_own], sink))
        o_ref[:, pair * LANES:(pair + 1) * LANES] = jnp.where(low, outs[0], outs[1]).astype(BF16)


def _lat_attn(layer, q, k, v, cache_k, cache_v, dup, sink):
    windowed = sink is not None
    n_qt = L_LAT // TQ
    ctx_tiles = N_CTX_TOK // TQ
    ctx_seqs = N_CTX_TOK // L_LAT
    cache_spec = pl.BlockSpec((1, 1, N_KV, PAST, HD), lambda b, j: (b, layer, 0, 0, 0))
    in_specs = [
        pl.BlockSpec((TQ, Q_W), lambda b, j: (ctx_tiles + b * n_qt + j, 0)),
        pl.BlockSpec((L_LAT, KV_W), lambda b, j: (ctx_seqs + b, 0)),
        pl.BlockSpec((L_LAT, KV_W), lambda b, j: (ctx_seqs + b, 0)),
        cache_spec,
        cache_spec,
        pl.BlockSpec((HD, LANES), lambda b, j: (0, 0)),
    ]
    args = [q, k, v, cache_k, cache_v, dup]
    if windowed:
        in_specs = [pl.BlockSpec(memory_space=pltpu.SMEM)] + in_specs
        args = [sink] + args
    return pl.pallas_call(
        functools.partial(_lat_attn_kernel, windowed),
        grid=(N_SEQ_LAT, n_qt),
        in_specs=in_specs,
        out_specs=pl.BlockSpec((TQ, Q_W), lambda b, j: (b * n_qt + j, 0)),
        out_shape=jax.ShapeDtypeStruct((N_LAT_TOK, Q_W), BF16),
        scratch_shapes=[
            pltpu.VMEM((N_KV, L_LAT, LANES), BF16),
            pltpu.VMEM((N_KV, L_LAT, LANES), BF16),
            pltpu.VMEM((N_KV, PAST, LANES), BF16),
            pltpu.VMEM((N_KV, PAST, LANES), BF16),
        ],
        compiler_params=_cparams(("arbitrary", "arbitrary")),
        name="lat_attn_window" if windowed else "lat_attn",
    )(*args)


def _seq_pos(i, width):
    r = lax.broadcasted_iota(jnp.int32, (TM, width), 0)
    is_ctx = i < N_CTX_TOK // TM
    seq_len = jnp.where(is_ctx, L_CTX, L_LAT)
    return r & (seq_len - 1), seq_len


def _shift_rows(t, j, pos, seq_len):
    if j == 0:
        return t
    moved = pltpu.roll(t, (-j) % TM, axis=0)
    ok = (pos + j >= 0) & (pos + j < seq_len)
    return jnp.where(ok, moved, 0.0)


def _conv_mixer(zc_ref, cw_ref, pos, seq_len):
    bg = zc_ref[:, 0:CONV_W].astype(F32)
    u = zc_ref[:, CONV_W:2 * CONV_W].astype(F32) * zc_ref[:, 2 * CONV_W:3 * CONV_W].astype(F32)
    y = (_shift_rows(u, -1, pos, seq_len) * cw_ref[0:1, :] + u * cw_ref[1:2, :]
         + _shift_rows(u, 1, pos, seq_len) * cw_ref[2:3, :])
    return (bg * y).astype(BF16)


def _pool_mixer(xd_ref, pw_ref, ps_ref, pos, seq_len):
    outs = []
    for gi, w in enumerate(POOL_WINDOWS):
        xg = xd_ref[:, gi * POOL_G:(gi + 1) * POOL_G]
        tot = None
        for j in range(-(w // 2), w - w // 2):
            sh = _shift_rows(xg, j, pos, seq_len)
            tot = sh if tot is None else tot + sh
        lo = jnp.clip(pos - w // 2, 0, seq_len)
        hi = jnp.clip(pos - w // 2 + w, 0, seq_len)
        cnt = (hi - lo).astype(F32)
        d = tot / cnt - xg
        outs.append(_dot(d.astype(BF16), pw_ref[gi].astype(BF16)))
    return (jnp.concatenate(outs, axis=1) * ps_ref[...]).astype(BF16)


def _mix_out_kernel(even, x_ref, mod_ref, oc_ref, ol_ref, w_ref, *rest):
    if even:
        zc_ref, cw_ref, out_ref, wbf = rest
    else:
        xd_ref, pw_ref, ps_ref, out_ref, wbf = rest
    i = pl.program_id(0)

    @pl.when(i == 0)
    def _():
        _cast_rows(w_ref, wbf, D_MODEL)

    pos, seq_len = _seq_pos(i, CONV_W if even else POOL_G)
    o = jnp.where(i < N_CTX_TOK // TM, oc_ref[...], ol_ref[...])
    if even:
        ya = _conv_mixer(zc_ref, cw_ref, pos, seq_len)
        y = _dot(ya, wbf[0:CONV_W, :]) + _dot(o, wbf[CONV_W:, :])
    else:
        yd = _pool_mixer(xd_ref, pw_ref, ps_ref, pos, seq_len)
        y = _dot(o, wbf[0:Q_W, :]) + _dot(yd, wbf[Q_W:, :])
    out_ref[...] = x_ref[...] + mod_ref[0, 2:3, :] * y


def _mix_out(even, x, mod_l, o_ctx, o_lat, w_out, *extra):
    n_ctx = N_CTX_TOK // TM
    n_lat = N_LAT_TOK // TM
    row = lambda i: (i, 0)
    const = lambda i: (0, 0)
    in_specs = [
        pl.BlockSpec((TM, D_MODEL), row),
        pl.BlockSpec((1, 6, D_MODEL), lambda i: (_mod_row(i, TM), 0, 0)),
        pl.BlockSpec((TM, Q_W), lambda i: (jnp.minimum(i, n_ctx - 1), 0)),
        pl.BlockSpec((TM, Q_W), lambda i: (jnp.clip(i - n_ctx, 0, n_lat - 1), 0)),
        pl.BlockSpec((D_MODEL, D_MODEL), const),
    ]
    if even:
        in_specs += [pl.BlockSpec((TM, 3 * CONV_W), row), pl.BlockSpec((3, CONV_W), const)]
    else:
        in_specs += [pl.BlockSpec((TM, POOL_W), row),
                     pl.BlockSpec((len(POOL_WINDOWS), POOL_G, POOL_G), lambda i: (0, 0, 0)),
                     pl.BlockSpec((1, POOL_W), const)]
    return pl.pallas_call(
        functools.partial(_mix_out_kernel, even),
        grid=(N_TOK // TM,),
        in_specs=in_specs,
        out_specs=pl.BlockSpec((TM, D_MODEL), row),
        out_shape=jax.ShapeDtypeStruct((N_TOK, D_MODEL), F32),
        scratch_shapes=[pltpu.VMEM((D_MODEL, D_MODEL), BF16)],
        compiler_params=_cparams(("arbitrary",)),
        name="mix_out_even" if even else "mix_out_odd",
    )(x, mod_l, o_ctx, o_lat, w_out, *extra)


def _ffn_kernel(x_ref, mod_ref, g_ref, w1_ref, w3_ref, w2_ref, out_ref, w1s, w3s, w2s, hs, acc):
    i = pl.program_id(0)
    f = pl.program_id(1)

    @pl.when(i == 0)
    def _():
        w1s[f] = w1_ref[0].astype(BF16)
        w3s[f] = w3_ref[0].astype(BF16)
        w2s[f] = w2_ref[0].astype(BF16)

    @pl.when(f == 0)
    def _():
        hs[...] = _normmod(x_ref[...], g_ref[...], mod_ref[0, 4:5, :], mod_ref[0, 3:4, :]).astype(BF16)
        acc[...] = jnp.zeros_like(acc)

    h = hs[...]
    hid = (_silu(_dot(h, w1s[f])) * _dot(h, w3s[f])).astype(BF16)
    acc[...] += _dot(hid, w2s[f])

    @pl.when(f == pl.num_programs(1) - 1)
    def _():
        out_ref[...] = x_ref[...] + mod_ref[0, 5:6, :] * acc[...]


def _ffn(layer_i, x, mod_l, g, w1, w3, w2):
    nf = D_FF // TF_FFN
    wf = lambda i, f: jnp.where(i == 0, f, nf - 1)
    row = lambda i, f: (i, 0)
    return pl.pallas_call(
        _ffn_kernel,
        grid=(N_TOK // TM, nf),
        in_specs=[
            pl.BlockSpec((TM, D_MODEL), row),
            pl.BlockSpec((1, 6, D_MODEL), lambda i, f: (_mod_row(i, TM), 0, 0)),
            pl.BlockSpec((1, D_MODEL), lambda i, f: (0, 0)),
            pl.BlockSpec((1, D_MODEL, TF_FFN), lambda i, f: (layer_i, 0, wf(i, f))),
            pl.BlockSpec((1, D_MODEL, TF_FFN), lambda i, f: (layer_i, 0, wf(i, f))),
            pl.BlockSpec((1, TF_FFN, D_MODEL), lambda i, f: (layer_i, wf(i, f), 0)),
        ],
        out_specs=pl.BlockSpec((TM, D_MODEL), row),
        out_shape=jax.ShapeDtypeStruct((N_TOK, D_MODEL), F32),
        scratch_shapes=[
            pltpu.VMEM((nf, D_MODEL, TF_FFN), BF16),
            pltpu.VMEM((nf, D_MODEL, TF_FFN), BF16),
            pltpu.VMEM((nf, TF_FFN, D_MODEL), BF16),
            pltpu.VMEM((TM, D_MODEL), BF16),
            pltpu.VMEM((TM, D_MODEL), F32),
        ],
        compiler_params=_cparams(("arbitrary", "arbitrary")),
        name="ffn",
    )(x, mod_l, g, w1, w3, w2)


def _split_bf16(t):
    hi = t.astype(BF16)
    return hi, (t - hi.astype(F32)).astype(BF16)


def _route_kernel(x_ref, mod_ref, g_ref, rw_ref, rb_ref, tri_ref, h_ref, meta_ref, cnt_ref, carry):
    @pl.when(pl.program_id(0) == 0)
    def _():
        carry[...] = jnp.zeros_like(carry)

    h = _normmod(x_ref[...], g_ref[...], mod_ref[0, 4:5, :], mod_ref[0, 3:4, :])
    h_ref[...] = h
    h_hi, h_lo = _split_bf16(h)
    w_hi, w_lo = _split_bf16(rw_ref[...])
    logits = _dot(h_hi, w_hi) + _dot(h_hi, w_lo) + _dot(h_lo, w_hi) + rb_ref[...]
    lane = lax.broadcasted_iota(jnp.int32, logits.shape, 1).astype(F32)
    logits = jnp.where(lane < N_EXP, logits, NEG_INF)
    m1 = logits.max(axis=-1, keepdims=True)
    i1 = jnp.where(logits == m1, lane, float(LANES)).min(axis=-1, keepdims=True)
    rest = jnp.where(lane == i1, NEG_INF, logits)
    m2 = rest.max(axis=-1, keepdims=True)
    i2 = jnp.where(rest == m2, lane, float(LANES)).min(axis=-1, keepdims=True)
    e2 = jnp.exp(m2 - m1)
    den = 1.0 + e2
    g1 = 1.0 / den
    g2 = e2 / den

    oh_a = jnp.where(lane == i1, 1.0, 0.0)
    oh_b = jnp.where(lane == i2, 1.0, 0.0)
    tri = tri_ref[...]
    base = carry[...]
    tot_a = oh_a.sum(axis=0, keepdims=True)
    before_a = _dot(tri, oh_a.astype(BF16)) + base
    before_b = _dot(tri, oh_b.astype(BF16)) + (base + tot_a)
    rank_a = (oh_a * before_a).sum(axis=-1, keepdims=True)
    rank_b = (oh_b * before_b).sum(axis=-1, keepdims=True)
    total = base + tot_a + oh_b.sum(axis=0, keepdims=True)
    carry[...] = total
    cnt_ref[...] = total

    meta = jnp.zeros_like(logits)
    for k, col in enumerate((i1, i2, rank_a, rank_b, g1, g2)):
        meta = jnp.where(lane == k, col, meta)
    meta_ref[...] = meta


def _route(x, mod_l, g, router_w, router_b, tri):
    row = lambda i: (i, 0)
    const = lambda i: (0, 0)
    return pl.pallas_call(
        _route_kernel,
        grid=(N_TOK // TM,),
        in_specs=[
            pl.BlockSpec((TM, D_MODEL), row),
            pl.BlockSpec((1, 6, D_MODEL), lambda i: (_mod_row(i, TM), 0, 0)),
            pl.BlockSpec((1, D_MODEL), const),
            pl.BlockSpec((D_MODEL, LANES), const),
            pl.BlockSpec((1, LANES), const),
            pl.BlockSpec((TM, TM), const),
        ],
        out_specs=[pl.BlockSpec((TM, D_MODEL), row), pl.BlockSpec((TM, LANES), row), pl.BlockSpec((1, LANES), const)],
        out_shape=[jax.ShapeDtypeStruct((N_TOK, D_MODEL), F32), jax.ShapeDtypeStruct((N_TOK, LANES), F32),
                   jax.ShapeDtypeStruct((1, LANES), F32)],
        scratch_shapes=[pltpu.VMEM((1, LANES), F32)],
        compiler_params=_cparams(("arbitrary",)),
        name="moe_route",
    )(x, mod_l, g, router_w, router_b, tri)


def _moe_plan(meta, counts):
    cnt = counts[0, :N_EXP].astype(jnp.int32)
    padded = (cnt + (TMS - 1)) // TMS * TMS
    ends = jnp.cumsum(padded)
    offs = ends - padded
    experts = jnp.arange(N_EXP, dtype=jnp.int32)

    def position(e_col, rank_col):
        e = meta[:, e_col].astype(jnp.int32)
        start = jnp.sum(jnp.where(e[:, None] == experts[None, :], offs[None, :], 0), axis=1)
        return start + meta[:, rank_col].astype(jnp.int32)

    tile_start = jnp.arange(MOE_TILES, dtype=jnp.int32) * TMS
    tile_expert = jnp.sum((tile_start[:, None] >= ends[None, :]).astype(jnp.int32), axis=1)
    tile_expert = jnp.minimum(tile_expert, N_EXP - 1)
    n_used = (ends[N_EXP - 1] // TMS).reshape(1)
    return position(0, 2), position(1, 3), tile_expert, n_used


def _row_copy(src_ref, src_row, dst_ref, dst_row, sem):
    return pltpu.make_async_copy(src_ref.at[pl.ds(src_row, 1), :], dst_ref.at[pl.ds(dst_row, 1), :], sem)


def _dispatch_kernel(pa_ref, pb_ref, h_hbm, zero_hbm, xs_hbm, sem):
    del zero_hbm
    base = pl.program_id(0) * TM

    def issue(r, c):
        _row_copy(h_hbm, base + r, xs_hbm, pa_ref[r], sem).start()
        _row_copy(h_hbm, base + r, xs_hbm, pb_ref[r], sem).start()
        return c

    lax.fori_loop(0, TM, issue, 0, unroll=8)
    pltpu.make_async_copy(h_hbm.at[pl.ds(0, 2 * TM), :], xs_hbm.at[pl.ds(0, 2 * TM), :], sem).wait()


def _dispatch(pos_a, pos_b, h):
    smem_rows = pl.BlockSpec((TM,), lambda i: (i,), memory_space=pltpu.SMEM)
    any_spec = pl.BlockSpec(memory_space=pl.ANY)
    return pl.pallas_call(
        _dispatch_kernel,
        grid=(N_TOK // TM,),
        in_specs=[smem_rows, smem_rows, any_spec, any_spec],
        out_specs=any_spec,
        out_shape=jax.ShapeDtypeStruct((MOE_ROWS, D_MODEL), F32),
        scratch_shapes=[pltpu.SemaphoreType.DMA(())],
        input_output_aliases={3: 0},
        compiler_params=_cparams(("arbitrary",)),
        name="moe_dispatch",
    )(pos_a, pos_b, h, jnp.zeros((MOE_ROWS, D_MODEL), F32))


def _expert_kernel(te_ref, nu_ref, xs_ref, w1_ref, w3_ref, w2_ref, y_ref, w1s, w3s, w2s):
    t = pl.program_id(0)
    new_expert = (t == 0) | (te_ref[t] != te_ref[jnp.maximum(t - 1, 0)])

    @pl.when(new_expert)
    def _():
        _cast_rows(w1_ref.at[0, 0], w1s, D_MODEL)
        _cast_rows(w3_ref.at[0, 0], w3s, D_MODEL)
        _cast_rows(w2_ref.at[0, 0], w2s, D_FF_E)

    @pl.when(t < nu_ref[0])
    def _():
        x = xs_ref[...].astype(BF16)
        hid = (_silu(_dot(x, w1s[...])) * _dot(x, w3s[...])).astype(BF16)
        y_ref[...] = _dot(hid, w2s[...])

    @pl.when(t >= nu_ref[0])
    def _():
        y_ref[...] = jnp.zeros_like(y_ref)


def _experts(layer_i, tile_expert, n_used, xs, w1, w3, w2):
    rows = lambda t, te, nu: (jnp.minimum(t, nu[0] - 1), 0)
    wsel = lambda t, te, nu: (layer_i, te[t], 0, 0)
    return pl.pallas_call(
        _expert_kernel,
        grid_spec=pltpu.PrefetchScalarGridSpec(
            num_scalar_prefetch=2,
            grid=(MOE_TILES,),
            in_specs=[
                pl.BlockSpec((TMS, D_MODEL), rows),
                pl.BlockSpec((1, 1, D_MODEL, D_FF_E), wsel),
                pl.BlockSpec((1, 1, D_MODEL, D_FF_E), wsel),
                pl.BlockSpec((1, 1, D_FF_E, D_MODEL), wsel),
            ],
            out_specs=pl.BlockSpec((TMS, D_MODEL), lambda t, te, nu: (t, 0)),
            scratch_shapes=[
                pltpu.VMEM((D_MODEL, D_FF_E), BF16),
                pltpu.VMEM((D_MODEL, D_FF_E), BF16),
                pltpu.VMEM((D_FF_E, D_MODEL), BF16),
            ],
        ),
        out_shape=jax.ShapeDtypeStruct((MOE_ROWS, D_MODEL), F32),
        compiler_params=_cparams(("arbitrary",)),
        name="moe_experts",
    )(tile_expert, n_used, xs, w1, w3, w2)


def _combine_kernel(pa_ref, pb_ref, x_ref, mod_ref, meta_ref, y_hbm, out_ref, buf_a, buf_b, sem):
    def issue(r, c):
        _row_copy(y_hbm, pa_ref[r], buf_a, r, sem.at[0]).start()
        _row_copy(y_hbm, pb_ref[r], buf_b, r, sem.at[1]).start()
        return c

    lax.fori_loop(0, TMC, issue, 0, unroll=8)
    pltpu.make_async_copy(y_hbm.at[pl.ds(0, TMC), :], buf_a, sem.at[0]).wait()
    pltpu.make_async_copy(y_hbm.at[pl.ds(0, TMC), :], buf_b, sem.at[1]).wait()
    f = meta_ref[:, 4:5] * buf_a[...] + meta_ref[:, 5:6] * buf_b[...]
    out_ref[...] = x_ref[...] + mod_ref[0, 5:6, :] * f


def _combine(pos_a, pos_b, x, mod_l, meta, y):
    smem_rows = pl.BlockSpec((TMC,), lambda i: (i,), memory_space=pltpu.SMEM)
    row = lambda i: (i, 0)
    return pl.pallas_call(
        _combine_kernel,
        grid=(N_TOK // TMC,),
        in_specs=[
            smem_rows, smem_rows,
            pl.BlockSpec((TMC, D_MODEL), row),
            pl.BlockSpec((1, 6, D_MODEL), lambda i: (_mod_row(i, TMC), 0, 0)),
            pl.BlockSpec((TMC, LANES), row),
            pl.BlockSpec(memory_space=pl.ANY),
        ],
        out_specs=pl.BlockSpec((TMC, D_MODEL), row),
        out_shape=jax.ShapeDtypeStruct((N_TOK, D_MODEL), F32),
        scratch_shapes=[
            pltpu.VMEM((TMC, D_MODEL), F32),
            pltpu.VMEM((TMC, D_MODEL), F32),
            pltpu.SemaphoreType.DMA((2,)),
        ],
        compiler_params=_cparams(("arbitrary",)),
        name="moe_combine",
    )(pos_a, pos_b, x, mod_l, meta, y)


def _moe(layer_i, x, mod_l, g, router_w, router_b, tri, w1, w3, w2):
    h, meta, counts = _route(x, mod_l, g, router_w, router_b, tri)
    pos_a, pos_b, tile_expert, n_used = _moe_plan(meta, counts)
    xs = _dispatch(pos_a, pos_b, h)
    y = _experts(layer_i, tile_expert, n_used, xs, w1, w3, w2)
    return _combine(pos_a, pos_b, x, mod_l, meta, y)


def _rope_tables():
    n_rows = L_LAT // GRID_W
    rows = jnp.repeat(jnp.arange(n_rows, dtype=F32), GRID_W)
    cols = jnp.tile(jnp.arange(GRID_W, dtype=F32), n_rows)
    quarter = HD // 4
    inv = ROPE_THETA ** (-jnp.arange(quarter, dtype=F32) / quarter)
    ang_r = rows[:, None] * inv
    ang_c = cols[:, None] * inv
    cos = jnp.concatenate([jnp.cos(ang_r)] * 2 + [jnp.cos(ang_c)] * 2, axis=1)
    sin = jnp.concatenate([-jnp.sin(ang_r), jnp.sin(ang_r), -jnp.sin(ang_c), jnp.sin(ang_c)], axis=1)
    cos = jnp.concatenate([cos, cos], axis=1)
    sin = jnp.concatenate([sin, sin], axis=1)
    cos = jnp.concatenate([jnp.ones((L_LAT, LANES), F32), cos], axis=0)
    sin = jnp.concatenate([jnp.zeros((L_LAT, LANES), F32), sin], axis=0)
    return cos, sin


def _block_ones(width):
    r = jnp.arange(width) // HD
    return (r[:, None] == r[None, :]).astype(BF16)


def kernel(x_prompt, x_sample, cache_k, cache_v, c, c_ctx, norm1, norm2, w_mod, b_mod, ev_w_in, ev_conv, ev_q_norm, ev_k_norm, ev_w_out, od_w_in, od_q_norm, od_k_norm, od_sink, od_pool_w, od_pool_scale, od_w_out, ffn_w1, ffn_w3, ffn_w2, moe_router, moe_router_b, moe_w1, moe_w3, moe_w2):
    x = jnp.concatenate([x_prompt.reshape(N_CTX_TOK, D_MODEL), x_sample.reshape(N_LAT_TOK, D_MODEL)], axis=0)
    cond = jnp.concatenate([c_ctx[None, :], c, jnp.zeros((MOD_ROWS - 1 - N_SEQ_LAT, D_MODEL), F32)], axis=0)
    mod = _modulation(cond, w_mod, b_mod).reshape(DEPTH, MOD_ROWS, 6, D_MODEL)

    cos_tab, sin_tab = _rope_tables()
    ones_q = _block_ones(Q_W)
    ones_k = _block_ones(KV_W)
    eye = jnp.eye(HD, dtype=BF16)
    dup = jnp.concatenate([eye, eye], axis=1)
    tri = (jnp.arange(TM)[:, None] > jnp.arange(TM)[None, :]).astype(BF16)

    new_k, new_v = [], []
    for l in range(DEPTH):
        i = l // 2
        even = l % 2 == 0
        mod_l = mod[l]
        g1 = norm1[l][None, :]
        g2 = norm2[l][None, :]
        if even:
            q_gain, k_gain = ev_q_norm[i], ev_k_norm[i]
            w_in, w_out = ev_w_in[i], ev_w_out[i]
        else:
            q_gain, k_gain = od_q_norm[i], od_k_norm[i]
            w_in, w_out = od_w_in[i], od_w_out[i]
        q_gain = jnp.tile(q_gain, N_Q)[None, :]
        k_gain = jnp.tile(k_gain, N_KV)[None, :]
        outs = _in_proj(even, x, mod_l, g1, w_in, ones_q, ones_k, q_gain, k_gain, cos_tab, sin_tab)
        if even:
            zc, q, k, v = outs
            sink = None
        else:
            q, k, v, xd = outs
            sink = od_sink[i]
        o_ctx, nk, nv = _ctx_attn(q, k, v, sink)
        o_lat = _lat_attn(l, q, k, v, cache_k, cache_v, dup, sink)
        new_k.append(nk)
        new_v.append(nv)
        if even:
            x = _mix_out(True, x, mod_l, o_ctx, o_lat, w_out, zc, ev_conv[i])
            x = _ffn(i, x, mod_l, g2, ffn_w1, ffn_w3, ffn_w2)
        else:
            x = _mix_out(False, x, mod_l, o_ctx, o_lat, w_out, xd, od_pool_w[i], od_pool_scale[i][None, :])
            rw = jnp.pad(moe_router[i], ((0, 0), (0, LANES - N_EXP)))
            rb = jnp.pad(moe_router_b[i], (0, LANES - N_EXP))[None, :]
            x = _moe(i, x, mod_l, g2, rw, rb, tri, moe_w1, moe_w3, moe_w2)

    y_prompt = x[:N_CTX_TOK].reshape(N_SEQ_CTX, L_CTX, D_MODEL)
    y_sample = x[N_CTX_TOK:].reshape(N_SEQ_LAT, L_LAT, D_MODEL)
    return (y_prompt, y_sample, jnp.stack(new_k, axis=1), jnp.stack(new_v, axis=1))
```

```python
import functools

import jax
import jax.numpy as jnp
from jax import lax
from jax.experimental import pallas as pl
from jax.experimental.pallas import tpu as pltpu

F32 = jnp.float32
BF16 = jnp.bfloat16

D_MODEL = 1024
N_SEQ_CTX = 32
L_CTX = 256
N_SEQ_LAT = 4
L_LAT = 1024
DEPTH = 4
PAST = 512
GRID_W = 64
HD = 64
N_Q = 8
N_KV = 2
GROUP = N_Q // N_KV
Q_W = N_Q * HD
KV_W = N_KV * HD
CONV_W = 512
POOL_W = 512
POOL_WINDOWS = (2, 4, 8, 16)
POOL_G = 128
EVEN_IN = 3 * CONV_W + Q_W + 2 * KV_W
ODD_IN = Q_W + 2 * KV_W + POOL_W
WINDOW = 128
D_FF = 2816
N_EXP = 8
TOP_K = 2
D_FF_E = 1024
ROPE_THETA = 10000.0
EPS = 1e-6

N_CTX_TOK = N_SEQ_CTX * L_CTX
N_LAT_TOK = N_SEQ_LAT * L_LAT
N_TOK = N_CTX_TOK + N_LAT_TOK
MOD_ROWS = 16

LANES = 128
VMEM_LIMIT = 56 * 1024 * 1024

TM_IN = 512
TM = 1024
TQ = 256
TF_FFN = 256
TR = 512
RUN_ALIGN = 16
LOCAL_ROWS = TOP_K * TR + N_EXP * RUN_ALIGN
TMS = 512
MOE_TILES = -(-(TOP_K * N_TOK + (N_TOK // TR) * N_EXP * (RUN_ALIGN - 1) + N_EXP * (TMS - 1)) // TMS)
MOE_ROWS = MOE_TILES * TMS
NEG_INF = float("-inf")


def _cparams(sem):
    return pltpu.CompilerParams(dimension_semantics=sem, vmem_limit_bytes=VMEM_LIMIT)


def _mod_row(i, tm):
    n_ctx = N_CTX_TOK // tm
    return jnp.where(i < n_ctx, 0, 1 + (i - n_ctx) // (L_LAT // tm))


def _normmod(x, g, scale, shift):
    ms = jnp.mean(x * x, axis=-1, keepdims=True)
    y = x * lax.rsqrt(ms + EPS) * g
    return y * (1.0 + scale) + shift


def _silu(x):
    return x * jax.nn.sigmoid(x)


def _dot(a, b):
    return jnp.dot(a, b, preferred_element_type=F32)


def _dot_nt(a, b):
    return lax.dot_general(a, b, (((1,), (1,)), ((), ())), preferred_element_type=F32)


def _cast_rows(src_ref, dst_ref, rows, chunk=256):
    for r in range(0, rows, chunk):
        dst_ref[r:r + chunk, :] = src_ref[r:r + chunk, :].astype(dst_ref.dtype)


def _mod_kernel(c_ref, w_ref, b_ref, o_ref):
    s = _silu(c_ref[...]).astype(BF16)
    o_ref[0] = _dot(s, w_ref[0].astype(BF16)) + b_ref[0]


def _modulation(cond, w_mod, b_mod):
    tn = 1536
    return pl.pallas_call(
        _mod_kernel,
        grid=(DEPTH, 6 * D_MODEL // tn),
        in_specs=[
            pl.BlockSpec((MOD_ROWS, D_MODEL), lambda l, j: (0, 0)),
            pl.BlockSpec((1, D_MODEL, tn), lambda l, j: (l, 0, j)),
            pl.BlockSpec((1, 1, tn), lambda l, j: (l, 0, j)),
        ],
        out_specs=pl.BlockSpec((1, MOD_ROWS, tn), lambda l, j: (l, 0, j)),
        out_shape=jax.ShapeDtypeStruct((DEPTH, MOD_ROWS, 6 * D_MODEL), F32),
        compiler_params=_cparams(("arbitrary", "arbitrary")),
        name="modulation",
    )(cond, w_mod, b_mod.reshape(DEPTH, 1, 6 * D_MODEL))


def _head_rms(t, ones_bd, gain):
    sq = t * t
    hi = sq.astype(BF16)
    lo = (sq - hi.astype(F32)).astype(BF16)
    ssq = _dot(hi, ones_bd) + _dot(lo, ones_bd)
    return t * lax.rsqrt(ssq * (1.0 / HD) + EPS) * gain


def _rope(t, cos, sin_signed):
    lane = lax.broadcasted_iota(jnp.int32, (t.shape[0], LANES), 1)
    first = (lane & 31) < 16
    outs = []
    for c in range(t.shape[1] // LANES):
        tc = t[:, c * LANES:(c + 1) * LANES]
        nxt = pltpu.roll(tc, LANES - 16, axis=1)
        prv = pltpu.roll(tc, 16, axis=1)
        outs.append(tc * cos + jnp.where(first, nxt, prv) * sin_signed)
    return outs[0] if len(outs) == 1 else jnp.concatenate(outs, axis=1)


def _in_proj_kernel(even, x_ref, mod_ref, g_ref, w_ref, onesq_ref, onesk_ref, qg_ref, kg_ref,
                    cos_ref, sin_ref, *rest):
    if even:
        zc_ref, q_ref, k_ref, v_ref, wbf = rest
        q0 = 3 * CONV_W
    else:
        q_ref, k_ref, v_ref, xd_ref, wbf = rest
        q0 = 0
    k0 = q0 + Q_W
    v0 = k0 + KV_W

    @pl.when(pl.program_id(0) == 0)
    def _():
        _cast_rows(w_ref, wbf, D_MODEL)

    h = _normmod(x_ref[...], g_ref[...], mod_ref[0, 1:2, :], mod_ref[0, 0:1, :]).astype(BF16)
    cos = cos_ref[...]
    sin = sin_ref[...]

    q = _dot(h, wbf[:, q0:q0 + Q_W])
    q = _rope(_head_rms(q, onesq_ref[...], qg_ref[...]), cos, sin) * (HD ** -0.5)
    q_ref[...] = q.astype(BF16)

    k = _dot(h, wbf[:, k0:k0 + KV_W])
    k = _rope(_head_rms(k, onesk_ref[...], kg_ref[...]), cos, sin)
    k_ref[...] = k.astype(BF16)

    v_ref[...] = _dot(h, wbf[:, v0:v0 + KV_W]).astype(BF16)

    if even:
        zc_ref[...] = _dot(h, wbf[:, 0:3 * CONV_W]).astype(BF16)
    else:
        xd_ref[...] = _dot(h, wbf[:, v0 + KV_W:v0 + KV_W + POOL_W])


def _in_proj(even, x, mod_l, g, w, ones_q, ones_k, q_gain, k_gain, cos_tab, sin_tab):
    tm = TM_IN
    n_in = EVEN_IN if even else ODD_IN
    n_ctx = N_CTX_TOK // tm
    per_seq = L_LAT // tm

    def rope_idx(i):
        return (jnp.where(i < n_ctx, 0, per_seq + (i - n_ctx) % per_seq), 0)

    row = lambda i: (i, 0)
    const = lambda i: (0, 0)
    in_specs = [
        pl.BlockSpec((tm, D_MODEL), row),
        pl.BlockSpec((1, 6, D_MODEL), lambda i: (_mod_row(i, tm), 0, 0)),
        pl.BlockSpec((1, D_MODEL), const),
        pl.BlockSpec((D_MODEL, n_in), const),
        pl.BlockSpec((Q_W, Q_W), const),
        pl.BlockSpec((KV_W, KV_W), const),
        pl.BlockSpec((1, Q_W), const),
        pl.BlockSpec((1, KV_W), const),
        pl.BlockSpec((tm, LANES), rope_idx),
        pl.BlockSpec((tm, LANES), rope_idx),
    ]
    qkv_specs = [pl.BlockSpec((tm, Q_W), row), pl.BlockSpec((tm, KV_W), row), pl.BlockSpec((tm, KV_W), row)]
    qkv_shapes = [jax.ShapeDtypeStruct((N_TOK, Q_W), BF16), jax.ShapeDtypeStruct((N_TOK, KV_W), BF16),
                  jax.ShapeDtypeStruct((N_TOK, KV_W), BF16)]
    if even:
        out_specs = [pl.BlockSpec((tm, 3 * CONV_W), row)] + qkv_specs
        out_shape = [jax.ShapeDtypeStruct((N_TOK, 3 * CONV_W), BF16)] + qkv_shapes
    else:
        out_specs = qkv_specs + [pl.BlockSpec((tm, POOL_W), row)]
        out_shape = qkv_shapes + [jax.ShapeDtypeStruct((N_TOK, POOL_W), F32)]
    return pl.pallas_call(
        functools.partial(_in_proj_kernel, even),
        grid=(N_TOK // tm,),
        in_specs=in_specs,
        out_specs=out_specs,
        out_shape=out_shape,
        scratch_shapes=[pltpu.VMEM((D_MODEL, n_in), BF16)],
        compiler_params=_cparams(("arbitrary",)),
        name="in_proj_even" if even else "in_proj_odd",
    )(x, mod_l, g, w, ones_q, ones_k, q_gain, k_gain, cos_tab, sin_tab)


def _dup_heads(t):
    lane = lax.broadcasted_iota(jnp.int32, t.shape, 1)
    swapped = pltpu.roll(t, HD, axis=1)
    low = lane < HD
    return jnp.where(low, t, swapped), jnp.where(low, swapped, t)


def _softmax_pv(scores, values, sink):
    m = scores[0].max(axis=-1, keepdims=True)
    for s in scores[1:]:
        m = jnp.maximum(m, s.max(axis=-1, keepdims=True))
    if sink is not None:
        m = jnp.maximum(m, sink)
    den = None
    acc = None
    for s, v in zip(scores, values):
        e = jnp.exp(s - m)
        d = e.sum(axis=-1, keepdims=True)
        a = _dot(e.astype(BF16), v)
        den = d if den is None else den + d
        acc = a if acc is None else acc + a
    if sink is not None:
        den = den + jnp.exp(sink - m)
    return acc / den


def _ctx_attn_kernel(has_sink, *refs):
    if has_sink:
        sink_ref, q_ref, k_ref, v_ref, o_ref, nk_ref, nv_ref = refs
    else:
        q_ref, k_ref, v_ref, o_ref, nk_ref, nv_ref = refs
        sink_ref = None
    k = k_ref[...].astype(F32)
    v = v_ref[...].astype(F32)
    k_sw = pltpu.roll(k, HD, axis=1)
    v_sw = pltpu.roll(v, HD, axis=1)
    nk_ref[0, 0] = k[:, 0:HD]
    nk_ref[0, 1] = k_sw[:, 0:HD]
    nv_ref[0, 0] = v[:, 0:HD]
    nv_ref[0, 1] = v_sw[:, 0:HD]
    lane = lax.broadcasted_iota(jnp.int32, (L_CTX, LANES), 1)
    low = lane < HD
    k2 = (jnp.where(low, k, k_sw).astype(BF16), jnp.where(low, k_sw, k).astype(BF16))
    v2 = (jnp.where(low, v, v_sw).astype(BF16), jnp.where(low, v_sw, v).astype(BF16))
    zero = jnp.zeros((L_CTX, LANES), BF16)
    for pair in range(N_Q // 2):
        kv = pair // (GROUP // 2)
        qp = q_ref[:, pair * LANES:(pair + 1) * LANES]
        outs = []
        for half in range(2):
            qm = jnp.where(low if half == 0 else jnp.logical_not(low), qp, zero)
            s = _dot_nt(qm, k2[kv])
            sink = sink_ref[2 * pair + half] if has_sink else None
            outs.append(_softmax_pv([s], [v2[kv]], sink))
        o_ref[:, pair * LANES:(pair + 1) * LANES] = jnp.where(low, outs[0], outs[1]).astype(BF16)


def _ctx_attn(q, k, v, sink):
    has_sink = sink is not None
    row = lambda b: (b, 0)
    in_specs = [pl.BlockSpec((L_CTX, Q_W), row), pl.BlockSpec((L_CTX, KV_W), row), pl.BlockSpec((L_CTX, KV_W), row)]
    args = [q, k, v]
    if has_sink:
        in_specs = [pl.BlockSpec(memory_space=pltpu.SMEM)] + in_specs
        args = [sink] + args
    kv_spec = pl.BlockSpec((1, N_KV, L_CTX, HD), lambda b: (b, 0, 0, 0))
    kv_shape = jax.ShapeDtypeStruct((N_SEQ_CTX, N_KV, L_CTX, HD), F32)
    return pl.pallas_call(
        functools.partial(_ctx_attn_kernel, has_sink),
        grid=(N_SEQ_CTX,),
        in_specs=in_specs,
        out_specs=[pl.BlockSpec((L_CTX, Q_W), row), kv_spec, kv_spec],
        out_shape=[jax.ShapeDtypeStruct((N_CTX_TOK, Q_W), BF16), kv_shape, kv_shape],
        compiler_params=_cparams(("arbitrary",)),
        name="ctx_attn_sink" if has_sink else "ctx_attn",
    )(*args)


def _lat_attn_kernel(windowed, *refs):
    if windowed:
        sink_ref, q_ref, k_ref, v_ref, ck_ref, cv_ref, dup_ref, o_ref, k2s, v2s, ck2s, cv2s = refs
    else:
        q_ref, k_ref, v_ref, ck_ref, cv_ref, dup_ref, o_ref, k2s, v2s, ck2s, cv2s = refs
        sink_ref = None
    j = pl.program_id(1)

    @pl.when(j == 0)
    def _():
        ka, kb = _dup_heads(k_ref[...].astype(F32))
        va, vb = _dup_heads(v_ref[...].astype(F32))
        k2s[0] = ka.astype(BF16)
        k2s[1] = kb.astype(BF16)
        v2s[0] = va.astype(BF16)
        v2s[1] = vb.astype(BF16)
        dup = dup_ref[...]
        for kv in range(N_KV):
            ck2s[kv] = _dot(ck_ref[0, 0, kv].astype(BF16), dup).astype(BF16)
            cv2s[kv] = _dot(cv_ref[0, 0, kv].astype(BF16), dup).astype(BF16)

    lane = lax.broadcasted_iota(jnp.int32, (TQ, LANES), 1)
    low = lane < HD
    zero = jnp.zeros((TQ, LANES), BF16)
    if windowed:
        n_loc = 2 * TQ
        start = pl.multiple_of(jnp.clip(j * TQ - WINDOW, 0, L_LAT - n_loc), WINDOW)
        qpos = j * TQ + lax.broadcasted_iota(jnp.int32, (TQ, n_loc), 0)
        kpos = start + lax.broadcasted_iota(jnp.int32, (TQ, n_loc), 1)
        valid = jnp.abs(qpos - kpos) <= WINDOW
    for pair in range(N_Q // 2):
        kv = pair // (GROUP // 2)
        qp = q_ref[:, pair * LANES:(pair + 1) * LANES]
        if windowed:
            k_own = k2s[kv, pl.ds(start, n_loc), :]
            v_own = v2s[kv, pl.ds(start, n_loc), :]
        else:
            k_own = k2s[kv]
            v_own = v2s[kv]
        outs = []
        for half in range(2):
            qm = jnp.where(low if half == 0 else jnp.logical_not(low), qp, zero)
            s_ctx = _dot_nt(qm, ck2s[kv])
            s_own = _dot_nt(qm, k_own)
            if windowed:
                s_own = jnp.where(valid, s_own, NEG_INF)
            sink = sink_ref[2 * pair + half] if windowed else None
            outs.append(_softmax_pv([s_ctx, s_own], [cv2s[kv], v_own], sink))
        o_ref[:, pair * LANES:(pair + 1) * LANES] = jnp.where(low, outs[0], outs[1]).astype(BF16)


def _lat_attn(layer, q, k, v, cache_k, cache_v, dup, sink):
    windowed = sink is not None
    n_qt = L_LAT // TQ
    ctx_tiles = N_CTX_TOK // TQ
    ctx_seqs = N_CTX_TOK // L_LAT
    cache_spec = pl.BlockSpec((1, 1, N_KV, PAST, HD), lambda b, j: (b, layer, 0, 0, 0))
    in_specs = [
        pl.BlockSpec((TQ, Q_W), lambda b, j: (ctx_tiles + b * n_qt + j, 0)),
        pl.BlockSpec((L_LAT, KV_W), lambda b, j: (ctx_seqs + b, 0)),
        pl.BlockSpec((L_LAT, KV_W), lambda b, j: (ctx_seqs + b, 0)),
        cache_spec,
        cache_spec,
        pl.BlockSpec((HD, LANES), lambda b, j: (0, 0)),
    ]
    args = [q, k, v, cache_k, cache_v, dup]
    if windowed:
        in_specs = [pl.BlockSpec(memory_space=pltpu.SMEM)] + in_specs
        args = [sink] + args
    return pl.pallas_call(
        functools.partial(_lat_attn_kernel, windowed),
        grid=(N_SEQ_LAT, n_qt),
        in_specs=in_specs,
        out_specs=pl.BlockSpec((TQ, Q_W), lambda b, j: (b * n_qt + j, 0)),
        out_shape=jax.ShapeDtypeStruct((N_LAT_TOK, Q_W), BF16),
        scratch_shapes=[
            pltpu.VMEM((N_KV, L_LAT, LANES), BF16),
            pltpu.VMEM((N_KV, L_LAT, LANES), BF16),
            pltpu.VMEM((N_KV, PAST, LANES), BF16),
            pltpu.VMEM((N_KV, PAST, LANES), BF16),
        ],
        compiler_params=_cparams(("arbitrary", "arbitrary")),
        name="lat_attn_window" if windowed else "lat_attn",
    )(*args)


def _seq_pos(i, width):
    r = lax.broadcasted_iota(jnp.int32, (TM, width), 0)
    is_ctx = i < N_CTX_TOK // TM
    seq_len = jnp.where(is_ctx, L_CTX, L_LAT)
    return r & (seq_len - 1), seq_len


def _shift_rows(t, j, pos, seq_len):
    if j == 0:
        return t
    moved = pltpu.roll(t, (-j) % TM, axis=0)
    ok = (pos + j >= 0) & (pos + j < seq_len)
    return jnp.where(ok, moved, 0.0)


def _conv_mixer(zc_ref, cw_ref, pos, seq_len):
    bg = zc_ref[:, 0:CONV_W].astype(F32)
    u = zc_ref[:, CONV_W:2 * CONV_W].astype(F32) * zc_ref[:, 2 * CONV_W:3 * CONV_W].astype(F32)
    y = (_shift_rows(u, -1, pos, seq_len) * cw_ref[0:1, :] + u * cw_ref[1:2, :]
         + _shift_rows(u, 1, pos, seq_len) * cw_ref[2:3, :])
    return (bg * y).astype(BF16)


def _pool_mixer(xd_ref, pw_ref, ps_ref, pos, seq_len):
    outs = []
    for gi, w in enumerate(POOL_WINDOWS):
        xg = xd_ref[:, gi * POOL_G:(gi + 1) * POOL_G]
        tot = None
        for j in range(-(w // 2), w - w // 2):
            sh = _shift_rows(xg, j, pos, seq_len)
            tot = sh if tot is None else tot + sh
        lo = jnp.clip(pos - w // 2, 0, seq_len)
        hi = jnp.clip(pos - w // 2 + w, 0, seq_len)
        cnt = (hi - lo).astype(F32)
        d = tot / cnt - xg
        outs.append(_dot(d.astype(BF16), pw_ref[gi].astype(BF16)))
    return (jnp.concatenate(outs, axis=1) * ps_ref[...]).astype(BF16)


def _mix_out_kernel(even, x_ref, mod_ref, oc_ref, ol_ref, w_ref, *rest):
    if even:
        zc_ref, cw_ref, out_ref, wbf = rest
    else:
        xd_ref, pw_ref, ps_ref, out_ref, wbf = rest
    i = pl.program_id(0)

    @pl.when(i == 0)
    def _():
        _cast_rows(w_ref, wbf, D_MODEL)

    pos, seq_len = _seq_pos(i, CONV_W if even else POOL_G)
    o = jnp.where(i < N_CTX_TOK // TM, oc_ref[...], ol_ref[...])
    if even:
        ya = _conv_mixer(zc_ref, cw_ref, pos, seq_len)
        y = _dot(ya, wbf[0:CONV_W, :]) + _dot(o, wbf[CONV_W:, :])
    else:
        yd = _pool_mixer(xd_ref, pw_ref, ps_ref, pos, seq_len)
        y = _dot(o, wbf[0:Q_W, :]) + _dot(yd, wbf[Q_W:, :])
    out_ref[...] = x_ref[...] + mod_ref[0, 2:3, :] * y


def _mix_out(even, x, mod_l, o_ctx, o_lat, w_out, *extra):
    n_ctx = N_CTX_TOK // TM
    n_lat = N_LAT_TOK // TM
    row = lambda i: (i, 0)
    const = lambda i: (0, 0)
    in_specs = [
        pl.BlockSpec((TM, D_MODEL), row),
        pl.BlockSpec((1, 6, D_MODEL), lambda i: (_mod_row(i, TM), 0, 0)),
        pl.BlockSpec((TM, Q_W), lambda i: (jnp.minimum(i, n_ctx - 1), 0)),
        pl.BlockSpec((TM, Q_W), lambda i: (jnp.clip(i - n_ctx, 0, n_lat - 1), 0)),
        pl.BlockSpec((D_MODEL, D_MODEL), const),
    ]
    if even:
        in_specs += [pl.BlockSpec((TM, 3 * CONV_W), row), pl.BlockSpec((3, CONV_W), const)]
    else:
        in_specs += [pl.BlockSpec((TM, POOL_W), row),
                     pl.BlockSpec((len(POOL_WINDOWS), POOL_G, POOL_G), lambda i: (0, 0, 0)),
                     pl.BlockSpec((1, POOL_W), const)]
    return pl.pallas_call(
        functools.partial(_mix_out_kernel, even),
        grid=(N_TOK // TM,),
        in_specs=in_specs,
        out_specs=pl.BlockSpec((TM, D_MODEL), row),
        out_shape=jax.ShapeDtypeStruct((N_TOK, D_MODEL), F32),
        scratch_shapes=[pltpu.VMEM((D_MODEL, D_MODEL), BF16)],
        compiler_params=_cparams(("arbitrary",)),
        name="mix_out_even" if even else "mix_out_odd",
    )(x, mod_l, o_ctx, o_lat, w_out, *extra)


def _ffn_kernel(x_ref, mod_ref, g_ref, w1_ref, w3_ref, w2_ref, out_ref, w1s, w3s, w2s, hs, acc):
    i = pl.program_id(0)
    f = pl.program_id(1)

    @pl.when(i == 0)
    def _():
        w1s[f] = w1_ref[0].astype(BF16)
        w3s[f] = w3_ref[0].astype(BF16)
        w2s[f] = w2_ref[0].astype(BF16)

    @pl.when(f == 0)
    def _():
        hs[...] = _normmod(x_ref[...], g_ref[...], mod_ref[0, 4:5, :], mod_ref[0, 3:4, :]).astype(BF16)
        acc[...] = jnp.zeros_like(acc)

    h = hs[...]
    hid = (_silu(_dot(h, w1s[f])) * _dot(h, w3s[f])).astype(BF16)
    acc[...] += _dot(hid, w2s[f])

    @pl.when(f == pl.num_programs(1) - 1)
    def _():
        out_ref[...] = x_ref[...] + mod_ref[0, 5:6, :] * acc[...]


def _ffn(layer_i, x, mod_l, g, w1, w3, w2):
    nf = D_FF // TF_FFN
    wf = lambda i, f: jnp.where(i == 0, f, nf - 1)
    row = lambda i, f: (i, 0)
    return pl.pallas_call(
        _ffn_kernel,
        grid=(N_TOK // TM, nf),
        in_specs=[
            pl.BlockSpec((TM, D_MODEL), row),
            pl.BlockSpec((1, 6, D_MODEL), lambda i, f: (_mod_row(i, TM), 0, 0)),
            pl.BlockSpec((1, D_MODEL), lambda i, f: (0, 0)),
            pl.BlockSpec((1, D_MODEL, TF_FFN), lambda i, f: (layer_i, 0, wf(i, f))),
            pl.BlockSpec((1, D_MODEL, TF_FFN), lambda i, f: (layer_i, 0, wf(i, f))),
            pl.BlockSpec((1, TF_FFN, D_MODEL), lambda i, f: (layer_i, wf(i, f), 0)),
        ],
        out_specs=pl.BlockSpec((TM, D_MODEL), row),
        out_shape=jax.ShapeDtypeStruct((N_TOK, D_MODEL), F32),
        scratch_shapes=[
            pltpu.VMEM((nf, D_MODEL, TF_FFN), BF16),
            pltpu.VMEM((nf, D_MODEL, TF_FFN), BF16),
            pltpu.VMEM((nf, TF_FFN, D_MODEL), BF16),
            pltpu.VMEM((TM, D_MODEL), BF16),
            pltpu.VMEM((TM, D_MODEL), F32),
        ],
        compiler_params=_cparams(("arbitrary", "arbitrary")),
        name="ffn",
    )(x, mod_l, g, w1, w3, w2)


def _split_bf16(t):
    hi = t.astype(BF16)
    return hi, (t - hi.astype(F32)).astype(BF16)


def _lane_values(col_vals, ones_rows):
    q = jnp.floor(col_vals * (1.0 / 32.0))
    r = col_vals - 32.0 * q
    t = 32.0 * _dot_nt(ones_rows, q.astype(BF16)) + _dot_nt(ones_rows, r.astype(BF16))
    return t[0:1, :]


def _route_kernel(x_ref, mod_ref, g_ref, rw_ref, rb_ref, tri_ref, upper_ref, xl_ref, meta_ref, cnt_ref):
    h = _normmod(x_ref[...], g_ref[...], mod_ref[0, 4:5, :], mod_ref[0, 3:4, :])
    h_hi, h_lo = _split_bf16(h)
    w_hi, w_lo = _split_bf16(rw_ref[...])
    logits = _dot(h_hi, w_hi) + _dot(h_hi, w_lo) + _dot(h_lo, w_hi) + rb_ref[...]
    lane = lax.broadcasted_iota(jnp.int32, logits.shape, 1).astype(F32)
    logits = jnp.where(lane < N_EXP, logits, NEG_INF)
    m1 = logits.max(axis=-1, keepdims=True)
    i1 = jnp.where(logits == m1, lane, float(LANES)).min(axis=-1, keepdims=True)
    rest = jnp.where(lane == i1, NEG_INF, logits)
    m2 = rest.max(axis=-1, keepdims=True)
    i2 = jnp.where(rest == m2, lane, float(LANES)).min(axis=-1, keepdims=True)
    e2 = jnp.exp(m2 - m1)
    den = 1.0 + e2
    g1 = 1.0 / den
    g2 = e2 / den

    oh_a = jnp.where(lane == i1, 1.0, 0.0)
    oh_b = jnp.where(lane == i2, 1.0, 0.0)
    tri = tri_ref[...]
    cnt_a = oh_a.sum(axis=0, keepdims=True)
    cnt_b = oh_b.sum(axis=0, keepdims=True)
    run16 = jnp.floor((cnt_a + cnt_b + (RUN_ALIGN - 1)) * (1.0 / RUN_ALIGN))
    run16_rows = jnp.broadcast_to(run16, (8, LANES))
    start = RUN_ALIGN * _dot(run16_rows.astype(BF16), upper_ref[...])[0:1, :]
    row_a = oh_a * (start + _dot(tri, oh_a.astype(BF16)))
    row_b = oh_b * (start + cnt_a + _dot(tri, oh_b.astype(BF16)))

    ones_rows = jnp.ones((8, LANES), BF16)
    tok_a = _lane_values(row_a, ones_rows)
    tok_b = _lane_values(row_b, ones_rows)
    sorted_row = lax.broadcasted_iota(jnp.int32, (LOCAL_ROWS, TR), 0).astype(F32)
    perm = jnp.where((sorted_row == tok_a) | (sorted_row == tok_b), 1.0, 0.0).astype(BF16)
    xl_ref[...] = _dot(perm, h_hi).astype(BF16)

    meta = jnp.zeros_like(logits)
    cols = (row_a.sum(axis=-1, keepdims=True), row_b.sum(axis=-1, keepdims=True), g1, g2)
    for k, col in enumerate(cols):
        meta = jnp.where(lane == k, col, meta)
    meta_ref[...] = meta
    cnt_ref[...] = RUN_ALIGN * run16_rows


def _route(x, mod_l, g, router_w, router_b, tri, upper):
    row = lambda i: (i, 0)
    const = lambda i: (0, 0)
    n_tiles = N_TOK // TR
    return pl.pallas_call(
        _route_kernel,
        grid=(n_tiles,),
        in_specs=[
            pl.BlockSpec((TR, D_MODEL), row),
            pl.BlockSpec((1, 6, D_MODEL), lambda i: (_mod_row(i, TR), 0, 0)),
            pl.BlockSpec((1, D_MODEL), const),
            pl.BlockSpec((D_MODEL, LANES), const),
            pl.BlockSpec((1, LANES), const),
            pl.BlockSpec((TR, TR), const),
            pl.BlockSpec((LANES, LANES), const),
        ],
        out_specs=[pl.BlockSpec((LOCAL_ROWS, D_MODEL), row), pl.BlockSpec((TR, LANES), row),
                   pl.BlockSpec((8, LANES), row)],
        out_shape=[jax.ShapeDtypeStruct((n_tiles * LOCAL_ROWS, D_MODEL), BF16),
                   jax.ShapeDtypeStruct((N_TOK, LANES), F32),
                   jax.ShapeDtypeStruct((n_tiles * 8, LANES), F32)],
        compiler_params=_cparams(("arbitrary",)),
        name="moe_route",
    )(x, mod_l, g, router_w, router_b, tri, upper)


def _moe_plan(counts):
    n_tiles = N_TOK // TR
    run = counts.reshape(n_tiles, 8, LANES)[:, 0, :N_EXP].astype(jnp.int32)
    per_expert = jnp.sum(run, axis=0)
    region = (per_expert + (TMS - 1)) // TMS * TMS
    ends = jnp.cumsum(region)
    offs = ends - region
    dst = offs[None, :] + jnp.cumsum(run, axis=0) - run
    local_end = jnp.cumsum(run, axis=1)
    local_start = local_end - run
    chunk_row = jnp.arange(LOCAL_ROWS // RUN_ALIGN, dtype=jnp.int32) * RUN_ALIGN
    owner = chunk_row[None, :, None] >= local_end[:, None, :]
    expert_of_chunk = jnp.minimum(jnp.sum(owner.astype(jnp.int32), axis=2), N_EXP - 1)
    pick = expert_of_chunk[:, :, None] == jnp.arange(N_EXP, dtype=jnp.int32)[None, None, :]
    shift = jnp.sum(jnp.where(pick, (dst - local_start)[:, None, :], 0), axis=2)
    chunk_dst = (chunk_row[None, :] + shift).reshape(-1)
    n_chunks = local_end[:, N_EXP - 1] // RUN_ALIGN
    tile_start = jnp.arange(MOE_TILES, dtype=jnp.int32) * TMS
    tile_expert = jnp.sum((tile_start[:, None] >= ends[None, :]).astype(jnp.int32), axis=1)
    tile_expert = jnp.minimum(tile_expert, N_EXP - 1)
    n_used = (ends[N_EXP - 1] // TMS).reshape(1)
    return chunk_dst, n_chunks, tile_expert, n_used


def _chunk_copy(src_ref, src_row, dst_ref, dst_row, sem):
    return pltpu.make_async_copy(src_ref.at[pl.ds(src_row, RUN_ALIGN), :],
                                 dst_ref.at[pl.ds(dst_row, RUN_ALIGN), :], sem)


def _chunk_dma_loop(i, dst_ref, n_ref, copy):
    n = n_ref[i]
    base = i * (LOCAL_ROWS // RUN_ALIGN)

    def issue(j, c):
        local = pl.multiple_of(j * RUN_ALIGN, RUN_ALIGN)
        copy(local, pl.multiple_of(dst_ref[base + j], RUN_ALIGN)).start()
        return c

    def drain(j, c):
        copy(0, 0).wait()
        return c

    lax.fori_loop(0, n, issue, 0)
    lax.fori_loop(0, n, drain, 0)


def _dispatch_kernel(dst_ref, n_ref, xl_ref, zero_hbm, xs_hbm, sem):
    del zero_hbm
    _chunk_dma_loop(pl.program_id(0), dst_ref, n_ref,
                    lambda local, glob: _chunk_copy(xl_ref, local, xs_hbm, glob, sem))


def _dispatch(chunk_dst, n_chunks, xl):
    any_spec = pl.BlockSpec(memory_space=pl.ANY)
    return pl.pallas_call(
        _dispatch_kernel,
        grid_spec=pltpu.PrefetchScalarGridSpec(
            num_scalar_prefetch=2,
            grid=(N_TOK // TR,),
            in_specs=[pl.BlockSpec((LOCAL_ROWS, D_MODEL), lambda i, d, n: (i, 0)), any_spec],
            out_specs=any_spec,
            scratch_shapes=[pltpu.SemaphoreType.DMA(())],
        ),
        out_shape=jax.ShapeDtypeStruct((MOE_ROWS, D_MODEL), BF16),
        input_output_aliases={3: 0},
        compiler_params=_cparams(("arbitrary",)),
        name="moe_dispatch",
    )(chunk_dst, n_chunks, xl, jnp.zeros((MOE_ROWS, D_MODEL), BF16))


def _expert_kernel(te_ref, nu_ref, xs_ref, w1_ref, w3_ref, w2_ref, y_ref, w1s, w3s, w2s):
    t = pl.program_id(0)
    new_expert = (t == 0) | (te_ref[t] != te_ref[jnp.maximum(t - 1, 0)])

    @pl.when(new_expert)
    def _():
        _cast_rows(w1_ref.at[0, 0], w1s, D_MODEL)
        _cast_rows(w3_ref.at[0, 0], w3s, D_MODEL)
        _cast_rows(w2_ref.at[0, 0], w2s, D_FF_E)

    @pl.when(t < nu_ref[0])
    def _():
        x = xs_ref[...]
        hid = (_silu(_dot(x, w1s[...])) * _dot(x, w3s[...])).astype(BF16)
        y_ref[...] = _dot(hid, w2s[...]).astype(BF16)

    @pl.when(t >= nu_ref[0])
    def _():
        y_ref[...] = jnp.zeros_like(y_ref)


def _experts(layer_i, tile_expert, n_used, xs, w1, w3, w2):
    rows = lambda t, te, nu: (jnp.minimum(t, nu[0] - 1), 0)
    wsel = lambda t, te, nu: (layer_i, te[t], 0, 0)
    return pl.pallas_call(
        _expert_kernel,
        grid_spec=pltpu.PrefetchScalarGridSpec(
            num_scalar_prefetch=2,
            grid=(MOE_TILES,),
            in_specs=[
                pl.BlockSpec((TMS, D_MODEL), rows),
                pl.BlockSpec((1, 1, D_MODEL, D_FF_E), wsel),
                pl.BlockSpec((1, 1, D_MODEL, D_FF_E), wsel),
                pl.BlockSpec((1, 1, D_FF_E, D_MODEL), wsel),
            ],
            out_specs=pl.BlockSpec((TMS, D_MODEL), lambda t, te, nu: (t, 0)),
            scratch_shapes=[
                pltpu.VMEM((D_MODEL, D_FF_E), BF16),
                pltpu.VMEM((D_MODEL, D_FF_E), BF16),
                pltpu.VMEM((D_FF_E, D_MODEL), BF16),
            ],
        ),
        out_shape=jax.ShapeDtypeStruct((MOE_ROWS, D_MODEL), BF16),
        compiler_params=_cparams(("arbitrary",)),
        name="moe_experts",
    )(tile_expert, n_used, xs, w1, w3, w2)


def _combine_kernel(dst_ref, n_ref, x_ref, mod_ref, meta_ref, y_hbm, out_ref, ybuf, sem):
    i = pl.program_id(0)

    @pl.when(i == 0)
    def _():
        ybuf[...] = jnp.zeros_like(ybuf)

    _chunk_dma_loop(i, dst_ref, n_ref, lambda local, glob: _chunk_copy(y_hbm, glob, ybuf, local, sem))
    sorted_row = lax.broadcasted_iota(jnp.int32, (TR, LOCAL_ROWS), 1).astype(F32)
    y = ybuf[...]
    pick_a = jnp.where(sorted_row == meta_ref[:, 0:1], 1.0, 0.0).astype(BF16)
    pick_b = jnp.where(sorted_row == meta_ref[:, 1:2], 1.0, 0.0).astype(BF16)
    f = meta_ref[:, 2:3] * _dot(pick_a, y) + meta_ref[:, 3:4] * _dot(pick_b, y)
    out_ref[...] = x_ref[...] + mod_ref[0, 5:6, :] * f


def _combine(chunk_dst, n_chunks, x, mod_l, meta, y):
    row = lambda i, d, n: (i, 0)
    return pl.pallas_call(
        _combine_kernel,
        grid_spec=pltpu.PrefetchScalarGridSpec(
            num_scalar_prefetch=2,
            grid=(N_TOK // TR,),
            in_specs=[
                pl.BlockSpec((TR, D_MODEL), row),
                pl.BlockSpec((1, 6, D_MODEL), lambda i, d, n: (_mod_row(i, TR), 0, 0)),
                pl.BlockSpec((TR, LANES), row),
                pl.BlockSpec(memory_space=pl.ANY),
            ],
            out_specs=pl.BlockSpec((TR, D_MODEL), row),
            scratch_shapes=[pltpu.VMEM((LOCAL_ROWS, D_MODEL), BF16), pltpu.SemaphoreType.DMA(())],
        ),
        out_shape=jax.ShapeDtypeStruct((N_TOK, D_MODEL), F32),
        compiler_params=_cparams(("arbitrary",)),
        name="moe_combine",
    )(chunk_dst, n_chunks, x, mod_l, meta, y)


def _moe(layer_i, x, mod_l, g, router_w, router_b, tri, upper, w1, w3, w2):
    xl, meta, counts = _route(x, mod_l, g, router_w, router_b, tri, upper)
    chunk_dst, n_chunks, tile_expert, n_used = _moe_plan(counts)
    xs = _dispatch(chunk_dst, n_chunks, xl)
    y = _experts(layer_i, tile_expert, n_used, xs, w1, w3, w2)
    return _combine(chunk_dst, n_chunks, x, mod_l, meta, y)


def _rope_tables():
    n_rows = L_LAT // GRID_W
    rows = jnp.repeat(jnp.arange(n_rows, dtype=F32), GRID_W)
    cols = jnp.tile(jnp.arange(GRID_W, dtype=F32), n_rows)
    quarter = HD // 4
    inv = ROPE_THETA ** (-jnp.arange(quarter, dtype=F32) / quarter)
    ang_r = rows[:, None] * inv
    ang_c = cols[:, None] * inv
    cos = jnp.concatenate([jnp.cos(ang_r)] * 2 + [jnp.cos(ang_c)] * 2, axis=1)
    sin = jnp.concatenate([-jnp.sin(ang_r), jnp.sin(ang_r), -jnp.sin(ang_c), jnp.sin(ang_c)], axis=1)
    cos = jnp.concatenate([cos, cos], axis=1)
    sin = jnp.concatenate([sin, sin], axis=1)
    cos = jnp.concatenate([jnp.ones((L_LAT, LANES), F32), cos], axis=0)
    sin = jnp.concatenate([jnp.zeros((L_LAT, LANES), F32), sin], axis=0)
    return cos, sin


def _block_ones(width):
    r = jnp.arange(width) // HD
    return (r[:, None] == r[None, :]).astype(BF16)


def kernel(x_prompt, x_sample, cache_k, cache_v, c, c_ctx, norm1, norm2, w_mod, b_mod, ev_w_in, ev_conv, ev_q_norm, ev_k_norm, ev_w_out, od_w_in, od_q_norm, od_k_norm, od_sink, od_pool_w, od_pool_scale, od_w_out, ffn_w1, ffn_w3, ffn_w2, moe_router, moe_router_b, moe_w1, moe_w3, moe_w2):
    x = jnp.concatenate([x_prompt.reshape(N_CTX_TOK, D_MODEL), x_sample.reshape(N_LAT_TOK, D_MODEL)], axis=0)
    cond = jnp.concatenate([c_ctx[None, :], c, jnp.zeros((MOD_ROWS - 1 - N_SEQ_LAT, D_MODEL), F32)], axis=0)
    mod = _modulation(cond, w_mod, b_mod).reshape(DEPTH, MOD_ROWS, 6, D_MODEL)

    cos_tab, sin_tab = _rope_tables()
    ones_q = _block_ones(Q_W)
    ones_k = _block_ones(KV_W)
    eye = jnp.eye(HD, dtype=BF16)
    dup = jnp.concatenate([eye, eye], axis=1)
    tri = (jnp.arange(TR)[:, None] > jnp.arange(TR)[None, :]).astype(BF16)
    upper = (jnp.arange(LANES)[:, None] < jnp.arange(LANES)[None, :]).astype(BF16)

    new_k, new_v = [], []
    for l in range(DEPTH):
        i = l // 2
        even = l % 2 == 0
        mod_l = mod[l]
        g1 = norm1[l][None, :]
        g2 = norm2[l][None, :]
        if even:
            q_gain, k_gain = ev_q_norm[i], ev_k_norm[i]
            w_in, w_out = ev_w_in[i], ev_w_out[i]
        else:
            q_gain, k_gain = od_q_norm[i], od_k_norm[i]
            w_in, w_out = od_w_in[i], od_w_out[i]
        q_gain = jnp.tile(q_gain, N_Q)[None, :]
        k_gain = jnp.tile(k_gain, N_KV)[None, :]
        outs = _in_proj(even, x, mod_l, g1, w_in, ones_q, ones_k, q_gain, k_gain, cos_tab, sin_tab)
        if even:
            zc, q, k, v = outs
            sink = None
        else:
            q, k, v, xd = outs
            sink = od_sink[i]
        o_ctx, nk, nv = _ctx_attn(q, k, v, sink)
        o_lat = _lat_attn(l, q, k, v, cache_k, cache_v, dup, sink)
        new_k.append(nk)
        new_v.append(nv)
        if even:
            x = _mix_out(True, x, mod_l, o_ctx, o_lat, w_out, zc, ev_conv[i])
            x = _ffn(i, x, mod_l, g2, ffn_w1, ffn_w3, ffn_w2)
        else:
            x = _mix_out(False, x, mod_l, o_ctx, o_lat, w_out, xd, od_pool_w[i], od_pool_scale[i][None, :])
            rw = jnp.pad(moe_router[i], ((0, 0), (0, LANES - N_EXP)))
            rb = jnp.pad(moe_router_b[i], (0, LANES - N_EXP))[None, :]
            x = _moe(i, x, mod_l, g2, rw, rb, tri, upper, moe_w1, moe_w3, moe_w2)

    y_prompt = x[:N_CTX_TOK].reshape(N_SEQ_CTX, L_CTX, D_MODEL)
    y_sample = x[N_CTX_TOK:].reshape(N_SEQ_LAT, L_LAT, D_MODEL)
    return (y_prompt, y_sample, jnp.stack(new_k, axis=1), jnp.stack(new_v, axis=1))
```

```python
import functools

import jax
import jax.numpy as jnp
from jax import lax
from jax.experimental import pallas as pl
from jax.experimental.pallas import tpu as pltpu

F32 = jnp.float32
BF16 = jnp.bfloat16

D_MODEL = 1024
N_SEQ_CTX = 32
L_CTX = 256
N_SEQ_LAT = 4
L_LAT = 1024
DEPTH = 4
PAST = 512
GRID_W = 64
HD = 64
N_Q = 8
N_KV = 2
GROUP = N_Q // N_KV
Q_W = N_Q * HD
KV_W = N_KV * HD
CONV_W = 512
POOL_W = 512
POOL_WINDOWS = (2, 4, 8, 16)
POOL_G = 128
EVEN_IN = 3 * CONV_W + Q_W + 2 * KV_W
ODD_IN = Q_W + 2 * KV_W + POOL_W
WINDOW = 128
D_FF = 2816
N_EXP = 8
TOP_K = 2
D_FF_E = 1024
ROPE_THETA = 10000.0
EPS = 1e-6

N_CTX_TOK = N_SEQ_CTX * L_CTX
N_LAT_TOK = N_SEQ_LAT * L_LAT
N_TOK = N_CTX_TOK + N_LAT_TOK
MOD_ROWS = 16

LANES = 128
VMEM_LIMIT = 56 * 1024 * 1024

TM_IN = 512
TM = 1024
TQ = 256
TF_FFN = 256
TR = 512
RUN_ALIGN = 16
LOCAL_ROWS = TOP_K * TR + N_EXP * RUN_ALIGN
TMS = 512
MOE_TILES = -(-(TOP_K * N_TOK + (N_TOK // TR) * N_EXP * (RUN_ALIGN - 1) + N_EXP * (TMS - 1)) // TMS)
MOE_ROWS = MOE_TILES * TMS
NEG_INF = float("-inf")


def _cparams(sem):
    return pltpu.CompilerParams(dimension_semantics=sem, vmem_limit_bytes=VMEM_LIMIT)


def _mod_row(i, tm):
    n_ctx = N_CTX_TOK // tm
    return jnp.where(i < n_ctx, 0, 1 + (i - n_ctx) // (L_LAT // tm))


def _normmod(x, g, scale, shift):
    ms = jnp.mean(x * x, axis=-1, keepdims=True)
    y = x * lax.rsqrt(ms + EPS) * g
    return y * (1.0 + scale) + shift


def _silu(x):
    return x * jax.nn.sigmoid(x)


def _dot(a, b):
    return jnp.dot(a, b, preferred_element_type=F32)


def _dot_nt(a, b):
    return lax.dot_general(a, b, (((1,), (1,)), ((), ())), preferred_element_type=F32)


def _cast_rows(src_ref, dst_ref, rows, chunk=256):
    for r in range(0, rows, chunk):
        dst_ref[r:r + chunk, :] = src_ref[r:r + chunk, :].astype(dst_ref.dtype)


def _mod_kernel(c_ref, w_ref, b_ref, o_ref):
    s = _silu(c_ref[...]).astype(BF16)
    o_ref[0] = _dot(s, w_ref[0].astype(BF16)) + b_ref[0]


def _modulation(cond, w_mod, b_mod):
    tn = 1536
    return pl.pallas_call(
        _mod_kernel,
        grid=(DEPTH, 6 * D_MODEL // tn),
        in_specs=[
            pl.BlockSpec((MOD_ROWS, D_MODEL), lambda l, j: (0, 0)),
            pl.BlockSpec((1, D_MODEL, tn), lambda l, j: (l, 0, j)),
            pl.BlockSpec((1, 1, tn), lambda l, j: (l, 0, j)),
        ],
        out_specs=pl.BlockSpec((1, MOD_ROWS, tn), lambda l, j: (l, 0, j)),
        out_shape=jax.ShapeDtypeStruct((DEPTH, MOD_ROWS, 6 * D_MODEL), F32),
        compiler_params=_cparams(("arbitrary", "arbitrary")),
        name="modulation",
    )(cond, w_mod, b_mod.reshape(DEPTH, 1, 6 * D_MODEL))


def _head_rms(t, ones_bd, gain):
    sq = t * t
    hi = sq.astype(BF16)
    lo = (sq - hi.astype(F32)).astype(BF16)
    ssq = _dot(hi, ones_bd) + _dot(lo, ones_bd)
    return t * lax.rsqrt(ssq * (1.0 / HD) + EPS) * gain


def _rope(t, cos, sin_signed):
    lane = lax.broadcasted_iota(jnp.int32, (t.shape[0], LANES), 1)
    first = (lane & 31) < 16
    outs = []
    for c in range(t.shape[1] // LANES):
        tc = t[:, c * LANES:(c + 1) * LANES]
        nxt = pltpu.roll(tc, LANES - 16, axis=1)
        prv = pltpu.roll(tc, 16, axis=1)
        outs.append(tc * cos + jnp.where(first, nxt, prv) * sin_signed)
    return outs[0] if len(outs) == 1 else jnp.concatenate(outs, axis=1)


def _in_proj_kernel(even, x_ref, mod_ref, g_ref, w_ref, onesq_ref, onesk_ref, qg_ref, kg_ref,
                    cos_ref, sin_ref, *rest):
    if even:
        zc_ref, q_ref, k_ref, v_ref, wbf = rest
        q0 = 3 * CONV_W
    else:
        q_ref, k_ref, v_ref, xd_ref, wbf = rest
        q0 = 0
    k0 = q0 + Q_W
    v0 = k0 + KV_W

    @pl.when(pl.program_id(0) == 0)
    def _():
        _cast_rows(w_ref, wbf, D_MODEL)

    h = _normmod(x_ref[...], g_ref[...], mod_ref[0, 1:2, :], mod_ref[0, 0:1, :]).astype(BF16)
    cos = cos_ref[...]
    sin = sin_ref[...]

    q = _dot(h, wbf[:, q0:q0 + Q_W])
    q = _rope(_head_rms(q, onesq_ref[...], qg_ref[...]), cos, sin) * (HD ** -0.5)
    q_ref[...] = q.astype(BF16)

    k = _dot(h, wbf[:, k0:k0 + KV_W])
    k = _rope(_head_rms(k, onesk_ref[...], kg_ref[...]), cos, sin)
    k_ref[...] = k.astype(BF16)

    v_ref[...] = _dot(h, wbf[:, v0:v0 + KV_W]).astype(BF16)

    if even:
        zc_ref[...] = _dot(h, wbf[:, 0:3 * CONV_W]).astype(BF16)
    else:
        xd_ref[...] = _dot(h, wbf[:, v0 + KV_W:v0 + KV_W + POOL_W])


def _in_proj(even, x, mod_l, g, w, ones_q, ones_k, q_gain, k_gain, cos_tab, sin_tab):
    tm = TM_IN
    n_in = EVEN_IN if even else ODD_IN
    n_ctx = N_CTX_TOK // tm
    per_seq = L_LAT // tm

    def rope_idx(i):
        return (jnp.where(i < n_ctx, 0, per_seq + (i - n_ctx) % per_seq), 0)

    row = lambda i: (i, 0)
    const = lambda i: (0, 0)
    in_specs = [
        pl.BlockSpec((tm, D_MODEL), row),
        pl.BlockSpec((1, 6, D_MODEL), lambda i: (_mod_row(i, tm), 0, 0)),
        pl.BlockSpec((1, D_MODEL), const),
        pl.BlockSpec((D_MODEL, n_in), const),
        pl.BlockSpec((Q_W, Q_W), const),
        pl.BlockSpec((KV_W, KV_W), const),
        pl.BlockSpec((1, Q_W), const),
        pl.BlockSpec((1, KV_W), const),
        pl.BlockSpec((tm, LANES), rope_idx),
        pl.BlockSpec((tm, LANES), rope_idx),
    ]
    qkv_specs = [pl.BlockSpec((tm, Q_W), row), pl.BlockSpec((tm, KV_W), row), pl.BlockSpec((tm, KV_W), row)]
    qkv_shapes = [jax.ShapeDtypeStruct((N_TOK, Q_W), BF16), jax.ShapeDtypeStruct((N_TOK, KV_W), BF16),
                  jax.ShapeDtypeStruct((N_TOK, KV_W), BF16)]
    if even:
        out_specs = [pl.BlockSpec((tm, 3 * CONV_W), row)] + qkv_specs
        out_shape = [jax.ShapeDtypeStruct((N_TOK, 3 * CONV_W), BF16)] + qkv_shapes
    else:
        out_specs = qkv_specs + [pl.BlockSpec((tm, POOL_W), row)]
        out_shape = qkv_shapes + [jax.ShapeDtypeStruct((N_TOK, POOL_W), F32)]
    return pl.pallas_call(
        functools.partial(_in_proj_kernel, even),
        grid=(N_TOK // tm,),
        in_specs=in_specs,
        out_specs=out_specs,
        out_shape=out_shape,
        scratch_shapes=[pltpu.VMEM((D_MODEL, n_in), BF16)],
        compiler_params=_cparams(("arbitrary",)),
        name="in_proj_even" if even else "in_proj_odd",
    )(x, mod_l, g, w, ones_q, ones_k, q_gain, k_gain, cos_tab, sin_tab)


def _dup_heads(t):
    lane = lax.broadcasted_iota(jnp.int32, t.shape, 1)
    swapped = pltpu.roll(t, HD, axis=1)
    low = lane < HD
    return jnp.where(low, t, swapped), jnp.where(low, swapped, t)


def _softmax_pv(scores, values, sink):
    m = scores[0].max(axis=-1, keepdims=True)
    for s in scores[1:]:
        m = jnp.maximum(m, s.max(axis=-1, keepdims=True))
    if sink is not None:
        m = jnp.maximum(m, sink)
    den = None
    acc = None
    for s, v in zip(scores, values):
        e = jnp.exp(s - m)
        d = e.sum(axis=-1, keepdims=True)
        a = _dot(e.astype(BF16), v)
        den = d if den is None else den + d
        acc = a if acc is None else acc + a
    if sink is not None:
        den = den + jnp.exp(sink - m)
    return acc / den


def _ctx_attn_kernel(has_sink, *refs):
    if has_sink:
        sink_ref, q_ref, k_ref, v_ref, o_ref, nk_ref, nv_ref = refs
    else:
        q_ref, k_ref, v_ref, o_ref, nk_ref, nv_ref = refs
        sink_ref = None
    k = k_ref[...].astype(F32)
    v = v_ref[...].astype(F32)
    k_sw = pltpu.roll(k, HD, axis=1)
    v_sw = pltpu.roll(v, HD, axis=1)
    nk_ref[0, 0] = k[:, 0:HD]
    nk_ref[0, 1] = k_sw[:, 0:HD]
    nv_ref[0, 0] = v[:, 0:HD]
    nv_ref[0, 1] = v_sw[:, 0:HD]
    lane = lax.broadcasted_iota(jnp.int32, (L_CTX, LANES), 1)
    low = lane < HD
    k2 = (jnp.where(low, k, k_sw).astype(BF16), jnp.where(low, k_sw, k).astype(BF16))
    v2 = (jnp.where(low, v, v_sw).astype(BF16), jnp.where(low, v_sw, v).astype(BF16))
    zero = jnp.zeros((L_CTX, LANES), BF16)
    for pair in range(N_Q // 2):
        kv = pair // (GROUP // 2)
        qp = q_ref[:, pair * LANES:(pair + 1) * LANES]
        outs = []
        for half in range(2):
            qm = jnp.where(low if half == 0 else jnp.logical_not(low), qp, zero)
            s = _dot_nt(qm, k2[kv])
            sink = sink_ref[2 * pair + half] if has_sink else None
            outs.append(_softmax_pv([s], [v2[kv]], sink))
        o_ref[:, pair * LANES:(pair + 1) * LANES] = jnp.where(low, outs[0], outs[1]).astype(BF16)


def _ctx_attn(q, k, v, sink):
    has_sink = sink is not None
    row = lambda b: (b, 0)
    in_specs = [pl.BlockSpec((L_CTX, Q_W), row), pl.BlockSpec((L_CTX, KV_W), row), pl.BlockSpec((L_CTX, KV_W), row)]
    args = [q, k, v]
    if has_sink:
        in_specs = [pl.BlockSpec(memory_space=pltpu.SMEM)] + in_specs
        args = [sink] + args
    kv_spec = pl.BlockSpec((1, N_KV, L_CTX, HD), lambda b: (b, 0, 0, 0))
    kv_shape = jax.ShapeDtypeStruct((N_SEQ_CTX, N_KV, L_CTX, HD), F32)
    return pl.pallas_call(
        functools.partial(_ctx_attn_kernel, has_sink),
        grid=(N_SEQ_CTX,),
        in_specs=in_specs,
        out_specs=[pl.BlockSpec((L_CTX, Q_W), row), kv_spec, kv_spec],
        out_shape=[jax.ShapeDtypeStruct((N_CTX_TOK, Q_W), BF16), kv_shape, kv_shape],
        compiler_params=_cparams(("arbitrary",)),
        name="ctx_attn_sink" if has_sink else "ctx_attn",
    )(*args)


def _lat_attn_kernel(windowed, *refs):
    if windowed:
        sink_ref, q_ref, k_ref, v_ref, ck_ref, cv_ref, dup_ref, o_ref, k2s, v2s, ck2s, cv2s = refs
    else:
        q_ref, k_ref, v_ref, ck_ref, cv_ref, dup_ref, o_ref, k2s, v2s, ck2s, cv2s = refs
        sink_ref = None
    j = pl.program_id(1)

    @pl.when(j == 0)
    def _():
        ka, kb = _dup_heads(k_ref[...].astype(F32))
        va, vb = _dup_heads(v_ref[...].astype(F32))
        k2s[0] = ka.astype(BF16)
        k2s[1] = kb.astype(BF16)
        v2s[0] = va.astype(BF16)
        v2s[1] = vb.astype(BF16)
        dup = dup_ref[...]
        for kv in range(N_KV):
            ck2s[kv] = _dot(ck_ref[0, 0, kv].astype(BF16), dup).astype(BF16)
            cv2s[kv] = _dot(cv_ref[0, 0, kv].astype(BF16), dup).astype(BF16)

    lane = lax.broadcasted_iota(jnp.int32, (TQ, LANES), 1)
    low = lane < HD
    zero = jnp.zeros((TQ, LANES), BF16)
    if windowed:
        n_loc = 2 * TQ
        start = pl.multiple_of(jnp.clip(j * TQ - WINDOW, 0, L_LAT - n_loc), WINDOW)
        qpos = j * TQ + lax.broadcasted_iota(jnp.int32, (TQ, n_loc), 0)
        kpos = start + lax.broadcasted_iota(jnp.int32, (TQ, n_loc), 1)
        valid = jnp.abs(qpos - kpos) <= WINDOW
    for pair in range(N_Q // 2):
        kv = pair // (GROUP // 2)
        qp = q_ref[:, pair * LANES:(pair + 1) * LANES]
        if windowed:
            k_own = k2s[kv, pl.ds(start, n_loc), :]
            v_own = v2s[kv, pl.ds(start, n_loc), :]
        else:
            k_own = k2s[kv]
            v_own = v2s[kv]
        outs = []
        for half in range(2):
            qm = jnp.where(low if half == 0 else jnp.logical_not(low), qp, zero)
            s_ctx = _dot_nt(qm, ck2s[kv])
            s_own = _dot_nt(qm, k_own)
            if windowed:
                s_own = jnp.where(valid, s_own, NEG_INF)
            sink = sink_ref[2 * pair + half] if windowed else None
            outs.append(_softmax_pv([s_ctx, s_own], [cv2s[kv], v_own], sink))
        o_ref[:, pair * LANES:(pair + 1) * LANES] = jnp.where(low, outs[0], outs[1]).astype(BF16)


def _lat_attn(layer, q, k, v, cache_k, cache_v, dup, sink):
    windowed = sink is not None
    n_qt = L_LAT // TQ
    ctx_tiles = N_CTX_TOK // TQ
    ctx_seqs = N_CTX_TOK // L_LAT
    cache_spec = pl.BlockSpec((1, 1, N_KV, PAST, HD), lambda b, j: (b, layer, 0, 0, 0))
    in_specs = [
        pl.BlockSpec((TQ, Q_W), lambda b, j: (ctx_tiles + b * n_qt + j, 0)),
        pl.BlockSpec((L_LAT, KV_W), lambda b, j: (ctx_seqs + b, 0)),
        pl.BlockSpec((L_LAT, KV_W), lambda b, j: (ctx_seqs + b, 0)),
        cache_spec,
        cache_spec,
        pl.BlockSpec((HD, LANES), lambda b, j: (0, 0)),
    ]
    args = [q, k, v, cache_k, cache_v, dup]
    if windowed:
        in_specs = [pl.BlockSpec(memory_space=pltpu.SMEM)] + in_specs
        args = [sink] + args
    return pl.pallas_call(
        functools.partial(_lat_attn_kernel, windowed),
        grid=(N_SEQ_LAT, n_qt),
        in_specs=in_specs,
        out_specs=pl.BlockSpec((TQ, Q_W), lambda b, j: (b * n_qt + j, 0)),
        out_shape=jax.ShapeDtypeStruct((N_LAT_TOK, Q_W), BF16),
        scratch_shapes=[
            pltpu.VMEM((N_KV, L_LAT, LANES), BF16),
            pltpu.VMEM((N_KV, L_LAT, LANES), BF16),
            pltpu.VMEM((N_KV, PAST, LANES), BF16),
            pltpu.VMEM((N_KV, PAST, LANES), BF16),
        ],
        compiler_params=_cparams(("arbitrary", "arbitrary")),
        name="lat_attn_window" if windowed else "lat_attn",
    )(*args)


def _seq_pos(i, width):
    r = lax.broadcasted_iota(jnp.int32, (TM, width), 0)
    is_ctx = i < N_CTX_TOK // TM
    seq_len = jnp.where(is_ctx, L_CTX, L_LAT)
    return r & (seq_len - 1), seq_len


def _shift_rows(t, j, pos, seq_len):
    if j == 0:
        return t
    moved = pltpu.roll(t, (-j) % TM, axis=0)
    ok = (pos + j >= 0) & (pos + j < seq_len)
    return jnp.where(ok, moved, 0.0)


def _conv_mixer(zc_ref, cw_ref, pos, seq_len):
    bg = zc_ref[:, 0:CONV_W].astype(F32)
    u = zc_ref[:, CONV_W:2 * CONV_W].astype(F32) * zc_ref[:, 2 * CONV_W:3 * CONV_W].astype(F32)
    y = (_shift_rows(u, -1, pos, seq_len) * cw_ref[0:1, :] + u * cw_ref[1:2, :]
         + _shift_rows(u, 1, pos, seq_len) * cw_ref[2:3, :])
    return (bg * y).astype(BF16)


def _pool_mixer(xd_ref, pw_ref, ps_ref, pos, seq_len):
    outs = []
    for gi, w in enumerate(POOL_WINDOWS):
        xg = xd_ref[:, gi * POOL_G:(gi + 1) * POOL_G]
        tot = None
        for j in range(-(w // 2), w - w // 2):
            sh = _shift_rows(xg, j, pos, seq_len)
            tot = sh if tot is None else tot + sh
        lo = jnp.clip(pos - w // 2, 0, seq_len)
        hi = jnp.clip(pos - w // 2 + w, 0, seq_len)
        cnt = (hi - lo).astype(F32)
        d = tot / cnt - xg
        outs.append(_dot(d.astype(BF16), pw_ref[gi].astype(BF16)))
    return (jnp.concatenate(outs, axis=1) * ps_ref[...]).astype(BF16)


def _mix_out_kernel(even, x_ref, mod_ref, oc_ref, ol_ref, w_ref, *rest):
    if even:
        zc_ref, cw_ref, out_ref, wbf = rest
    else:
        xd_ref, pw_ref, ps_ref, out_ref, wbf = rest
    i = pl.program_id(0)

    @pl.when(i == 0)
    def _():
        _cast_rows(w_ref, wbf, D_MODEL)

    pos, seq_len = _seq_pos(i, CONV_W if even else POOL_G)
    o = jnp.where(i < N_CTX_TOK // TM, oc_ref[...], ol_ref[...])
    if even:
        ya = _conv_mixer(zc_ref, cw_ref, pos, seq_len)
        y = _dot(ya, wbf[0:CONV_W, :]) + _dot(o, wbf[CONV_W:, :])
    else:
        yd = _pool_mixer(xd_ref, pw_ref, ps_ref, pos, seq_len)
        y = _dot(o, wbf[0:Q_W, :]) + _dot(yd, wbf[Q_W:, :])
    out_ref[...] = x_ref[...] + mod_ref[0, 2:3, :] * y


def _mix_out(even, x, mod_l, o_ctx, o_lat, w_out, *extra):
    n_ctx = N_CTX_TOK // TM
    n_lat = N_LAT_TOK // TM
    row = lambda i: (i, 0)
    const = lambda i: (0, 0)
    in_specs = [
        pl.BlockSpec((TM, D_MODEL), row),
        pl.BlockSpec((1, 6, D_MODEL), lambda i: (_mod_row(i, TM), 0, 0)),
        pl.BlockSpec((TM, Q_W), lambda i: (jnp.minimum(i, n_ctx - 1), 0)),
        pl.BlockSpec((TM, Q_W), lambda i: (jnp.clip(i - n_ctx, 0, n_lat - 1), 0)),
        pl.BlockSpec((D_MODEL, D_MODEL), const),
    ]
    if even:
        in_specs += [pl.BlockSpec((TM, 3 * CONV_W), row), pl.BlockSpec((3, CONV_W), const)]
    else:
        in_specs += [pl.BlockSpec((TM, POOL_W), row),
                     pl.BlockSpec((len(POOL_WINDOWS), POOL_G, POOL_G), lambda i: (0, 0, 0)),
                     pl.BlockSpec((1, POOL_W), const)]
    return pl.pallas_call(
        functools.partial(_mix_out_kernel, even),
        grid=(N_TOK // TM,),
        in_specs=in_specs,
        out_specs=pl.BlockSpec((TM, D_MODEL), row),
        out_shape=jax.ShapeDtypeStruct((N_TOK, D_MODEL), F32),
        scratch_shapes=[pltpu.VMEM((D_MODEL, D_MODEL), BF16)],
        compiler_params=_cparams(("arbitrary",)),
        name="mix_out_even" if even else "mix_out_odd",
    )(x, mod_l, o_ctx, o_lat, w_out, *extra)


def _ffn_kernel(x_ref, mod_ref, g_ref, w1_ref, w3_ref, w2_ref, out_ref, w1s, w3s, w2s, hs, acc):
    i = pl.program_id(0)
    f = pl.program_id(1)

    @pl.when(i == 0)
    def _():
        w1s[f] = w1_ref[0].astype(BF16)
        w3s[f] = w3_ref[0].astype(BF16)
        w2s[f] = w2_ref[0].astype(BF16)

    @pl.when(f == 0)
    def _():
        hs[...] = _normmod(x_ref[...], g_ref[...], mod_ref[0, 4:5, :], mod_ref[0, 3:4, :]).astype(BF16)
        acc[...] = jnp.zeros_like(acc)

    h = hs[...]
    hid = (_silu(_dot(h, w1s[f])) * _dot(h, w3s[f])).astype(BF16)
    acc[...] += _dot(hid, w2s[f])

    @pl.when(f == pl.num_programs(1) - 1)
    def _():
        out_ref[...] = x_ref[...] + mod_ref[0, 5:6, :] * acc[...]


def _ffn(layer_i, x, mod_l, g, w1, w3, w2):
    nf = D_FF // TF_FFN
    wf = lambda i, f: jnp.where(i == 0, f, nf - 1)
    row = lambda i, f: (i, 0)
    return pl.pallas_call(
        _ffn_kernel,
        grid=(N_TOK // TM, nf),
        in_specs=[
            pl.BlockSpec((TM, D_MODEL), row),
            pl.BlockSpec((1, 6, D_MODEL), lambda i, f: (_mod_row(i, TM), 0, 0)),
            pl.BlockSpec((1, D_MODEL), lambda i, f: (0, 0)),
            pl.BlockSpec((1, D_MODEL, TF_FFN), lambda i, f: (layer_i, 0, wf(i, f))),
            pl.BlockSpec((1, D_MODEL, TF_FFN), lambda i, f: (layer_i, 0, wf(i, f))),
            pl.BlockSpec((1, TF_FFN, D_MODEL), lambda i, f: (layer_i, wf(i, f), 0)),
        ],
        out_specs=pl.BlockSpec((TM, D_MODEL), row),
        out_shape=jax.ShapeDtypeStruct((N_TOK, D_MODEL), F32),
        scratch_shapes=[
            pltpu.VMEM((nf, D_MODEL, TF_FFN), BF16),
            pltpu.VMEM((nf, D_MODEL, TF_FFN), BF16),
            pltpu.VMEM((nf, TF_FFN, D_MODEL), BF16),
            pltpu.VMEM((TM, D_MODEL), BF16),
            pltpu.VMEM((TM, D_MODEL), F32),
        ],
        compiler_params=_cparams(("arbitrary", "arbitrary")),
        name="ffn",
    )(x, mod_l, g, w1, w3, w2)


def _split_bf16(t):
    hi = t.astype(BF16)
    return hi, (t - hi.astype(F32)).astype(BF16)


def _lane_values(col_vals, ones_rows):
    q = jnp.floor(col_vals * (1.0 / 32.0))
    r = col_vals - 32.0 * q
    t = 32.0 * _dot_nt(ones_rows, q.astype(BF16)) + _dot_nt(ones_rows, r.astype(BF16))
    return t[0:1, :]


def _route_kernel(x_ref, mod_ref, g_ref, rw_ref, rb_ref, tri_ref, upper_ref, xl_ref, meta_ref, cnt_ref):
    h = _normmod(x_ref[...], g_ref[...], mod_ref[0, 4:5, :], mod_ref[0, 3:4, :])
    h_hi, h_lo = _split_bf16(h)
    w_hi, w_lo = _split_bf16(rw_ref[...])
    logits = _dot(h_hi, w_hi) + _dot(h_hi, w_lo) + _dot(h_lo, w_hi) + rb_ref[...]
    lane = lax.broadcasted_iota(jnp.int32, logits.shape, 1).astype(F32)
    logits = jnp.where(lane < N_EXP, logits, NEG_INF)
    m1 = logits.max(axis=-1, keepdims=True)
    i1 = jnp.where(logits == m1, lane, float(LANES)).min(axis=-1, keepdims=True)
    rest = jnp.where(lane == i1, NEG_INF, logits)
    m2 = rest.max(axis=-1, keepdims=True)
    i2 = jnp.where(rest == m2, lane, float(LANES)).min(axis=-1, keepdims=True)
    e2 = jnp.exp(m2 - m1)
    den = 1.0 + e2
    g1 = 1.0 / den
    g2 = e2 / den

    oh_a = jnp.where(lane == i1, 1.0, 0.0)
    oh_b = jnp.where(lane == i2, 1.0, 0.0)
    tri = tri_ref[...]
    cnt_a = oh_a.sum(axis=0, keepdims=True)
    cnt_b = oh_b.sum(axis=0, keepdims=True)
    run16 = jnp.floor((cnt_a + cnt_b + (RUN_ALIGN - 1)) * (1.0 / RUN_ALIGN))
    run16_rows = jnp.broadcast_to(run16, (8, LANES))
    start = RUN_ALIGN * _dot(run16_rows.astype(BF16), upper_ref[...])[0:1, :]
    row_a = oh_a * (start + _dot(tri, oh_a.astype(BF16)))
    row_b = oh_b * (start + cnt_a + _dot(tri, oh_b.astype(BF16)))

    ones_rows = jnp.ones((8, LANES), BF16)
    tok_a = _lane_values(row_a, ones_rows)
    tok_b = _lane_values(row_b, ones_rows)
    sorted_row = lax.broadcasted_iota(jnp.int32, (LOCAL_ROWS, TR), 0).astype(F32)
    perm = jnp.where((sorted_row == tok_a) | (sorted_row == tok_b), 1.0, 0.0).astype(BF16)
    xl_ref[...] = _dot(perm, h_hi).astype(BF16)

    meta = jnp.zeros_like(logits)
    cols = (row_a.sum(axis=-1, keepdims=True), row_b.sum(axis=-1, keepdims=True), g1, g2)
    for k, col in enumerate(cols):
        meta = jnp.where(lane == k, col, meta)
    meta_ref[...] = meta
    cnt_ref[...] = RUN_ALIGN * run16_rows


def _route(x, mod_l, g, router_w, router_b, tri, upper):
    row = lambda i: (i, 0)
    const = lambda i: (0, 0)
    n_tiles = N_TOK // TR
    return pl.pallas_call(
        _route_kernel,
        grid=(n_tiles,),
        in_specs=[
            pl.BlockSpec((TR, D_MODEL), row),
            pl.BlockSpec((1, 6, D_MODEL), lambda i: (_mod_row(i, TR), 0, 0)),
            pl.BlockSpec((1, D_MODEL), const),
            pl.BlockSpec((D_MODEL, LANES), const),
            pl.BlockSpec((1, LANES), const),
            pl.BlockSpec((TR, TR), const),
            pl.BlockSpec((LANES, LANES), const),
        ],
        out_specs=[pl.BlockSpec((LOCAL_ROWS, D_MODEL), row), pl.BlockSpec((TR, LANES), row),
                   pl.BlockSpec((8, LANES), row)],
        out_shape=[jax.ShapeDtypeStruct((n_tiles * LOCAL_ROWS, D_MODEL), BF16),
                   jax.ShapeDtypeStruct((N_TOK, LANES), F32),
                   jax.ShapeDtypeStruct((n_tiles * 8, LANES), F32)],
        compiler_params=_cparams(("arbitrary",)),
        name="moe_route",
    )(x, mod_l, g, router_w, router_b, tri, upper)


def _moe_plan(counts):
    n_tiles = N_TOK // TR
    run = counts.reshape(n_tiles, 8, LANES)[:, 0, :N_EXP].astype(jnp.int32)
    per_expert = jnp.sum(run, axis=0)
    region = (per_expert + (TMS - 1)) // TMS * TMS
    ends = jnp.cumsum(region)
    offs = ends - region
    seg_end = jnp.cumsum(run, axis=0)
    seg_start = seg_end - run
    local_end = jnp.cumsum(run, axis=1)
    local_start = local_end - run
    n_chunks = local_end[:, N_EXP - 1] // RUN_ALIGN
    tile_start = jnp.arange(MOE_TILES, dtype=jnp.int32) * TMS
    tile_expert = jnp.sum((tile_start[:, None] >= ends[None, :]).astype(jnp.int32), axis=1)
    tile_expert = jnp.minimum(tile_expert, N_EXP - 1)
    n_used = (ends[N_EXP - 1] // TMS).reshape(1)
    experts = jnp.arange(N_EXP, dtype=jnp.int32)
    g_row = jnp.arange(MOE_ROWS // RUN_ALIGN, dtype=jnp.int32) * RUN_ALIGN
    g_exp = jnp.repeat(tile_expert, TMS // RUN_ALIGN)
    pick = g_exp[:, None] == experts[None, :]
    rel = g_row - jnp.sum(jnp.where(pick, offs[None, :], 0), axis=1)
    ends_of = jnp.sum(jnp.where(pick[:, None, :], seg_end[None, :, :], 0), axis=2)
    src_tile = jnp.sum((rel[:, None] >= ends_of).astype(jnp.int32), axis=1)
    valid = (src_tile < n_tiles) & (jnp.repeat(jnp.arange(MOE_TILES), TMS // RUN_ALIGN) < n_used[0])
    src_tile = jnp.minimum(src_tile, n_tiles - 1)
    sel = (src_tile[:, None, None] == jnp.arange(n_tiles)[None, :, None]) & pick[:, None, :]
    shift = jnp.sum(jnp.where(sel, (local_start - seg_start)[None, :, :], 0), axis=(1, 2))
    chunk_src = src_tile * LOCAL_ROWS + rel + shift
    n_valid = jnp.sum(valid.reshape(MOE_TILES, TMS // RUN_ALIGN).astype(jnp.int32), axis=1)
    return chunk_src, n_valid, n_chunks, tile_expert, n_used


def _chunk_copy(src_ref, src_row, dst_ref, dst_row, sem):
    return pltpu.make_async_copy(src_ref.at[pl.ds(src_row, RUN_ALIGN), :],
                                 dst_ref.at[pl.ds(dst_row, RUN_ALIGN), :], sem)


CHUNKS = TMS // RUN_ALIGN
LOCAL_CHUNKS = LOCAL_ROWS // RUN_ALIGN


def _expert_kernel(te_ref, nu_ref, src_ref, nv_ref, nc_ref, xl_hbm, w1_ref, w3_ref, w2_ref, yl_hbm,
                   xbuf, ybuf, zbuf, w1s, w3s, w2s, sem_in, sem_out, sem_zero):
    t = pl.program_id(0)
    n_used = nu_ref[0]
    slot = t & 1

    def gather(tile, s):
        n = nv_ref[tile]

        def issue(j, c):
            src = pl.multiple_of(src_ref[tile * CHUNKS + j], RUN_ALIGN)
            _chunk_copy(xl_hbm, src, xbuf.at[s], pl.multiple_of(j * RUN_ALIGN, RUN_ALIGN), sem_in.at[s]).start()
            return c

        def pad(j, c):
            xbuf[s, pl.ds(pl.multiple_of(j * RUN_ALIGN, RUN_ALIGN), RUN_ALIGN), :] = jnp.zeros(
                (RUN_ALIGN, D_MODEL), BF16)
            return c

        lax.fori_loop(0, n, issue, 0)
        lax.fori_loop(n, CHUNKS, pad, 0)

    def scatter(tile, s):
        def issue(j, c):
            dst = pl.multiple_of(src_ref[tile * CHUNKS + j], RUN_ALIGN)
            _chunk_copy(ybuf.at[s], pl.multiple_of(j * RUN_ALIGN, RUN_ALIGN), yl_hbm, dst, sem_out.at[s]).start()
            return c

        lax.fori_loop(0, nv_ref[tile], issue, 0)

    def drain(count, sem):
        def one(j, c):
            _chunk_copy(xl_hbm, 0, xbuf.at[0], 0, sem).wait()
            return c

        lax.fori_loop(0, count, one, 0)

    @pl.when(t == 0)
    def _():
        zbuf[...] = jnp.zeros_like(zbuf)
        total = jnp.int32(0)
        for tile in range(N_TOK // TR):
            n = nc_ref[tile]

            def clear(j, c, tile=tile):
                row = pl.multiple_of(tile * LOCAL_ROWS + j * RUN_ALIGN, RUN_ALIGN)
                _chunk_copy(zbuf, 0, yl_hbm, row, sem_zero).start()
                return c

            lax.fori_loop(n, LOCAL_CHUNKS, clear, 0)
            total = total + (LOCAL_CHUNKS - n)
        drain(total, sem_zero)
        gather(0, 0)

    @pl.when(t < n_used)
    def _():
        drain(nv_ref[t], sem_in.at[slot])

    @pl.when(t + 1 < n_used)
    def _():
        gather(t + 1, 1 - slot)

    new_expert = (t == 0) | (te_ref[t] != te_ref[jnp.maximum(t - 1, 0)])

    @pl.when(new_expert)
    def _():
        _cast_rows(w1_ref.at[0, 0], w1s, D_MODEL)
        _cast_rows(w3_ref.at[0, 0], w3s, D_MODEL)
        _cast_rows(w2_ref.at[0, 0], w2s, D_FF_E)

    @pl.when((t >= 2) & (t - 2 < n_used))
    def _():
        drain(nv_ref[jnp.maximum(t - 2, 0)], sem_out.at[slot])

    @pl.when(t < n_used)
    def _():
        x = xbuf[slot]
        hid = (_silu(_dot(x, w1s[...])) * _dot(x, w3s[...])).astype(BF16)
        ybuf[slot] = _dot(hid, w2s[...]).astype(BF16)
        scatter(t, slot)

    @pl.when(t == pl.num_programs(0) - 1)
    def _():
        @pl.when((t >= 1) & (t - 1 < n_used))
        def _():
            drain(nv_ref[jnp.maximum(t - 1, 0)], sem_out.at[1 - slot])

        @pl.when(t < n_used)
        def _():
            drain(nv_ref[t], sem_out.at[slot])


def _experts(layer_i, plan, xl, w1, w3, w2):
    chunk_src, n_valid, n_chunks, tile_expert, n_used = plan
    wsel = lambda t, te, nu, src, nv, nc: (layer_i, te[t], 0, 0)
    any_spec = pl.BlockSpec(memory_space=pl.ANY)
    return pl.pallas_call(
        _expert_kernel,
        grid_spec=pltpu.PrefetchScalarGridSpec(
            num_scalar_prefetch=5,
            grid=(MOE_TILES,),
            in_specs=[
                any_spec,
                pl.BlockSpec((1, 1, D_MODEL, D_FF_E), wsel),
                pl.BlockSpec((1, 1, D_MODEL, D_FF_E), wsel),
                pl.BlockSpec((1, 1, D_FF_E, D_MODEL), wsel),
            ],
            out_specs=any_spec,
            scratch_shapes=[
                pltpu.VMEM((2, TMS, D_MODEL), BF16),
                pltpu.VMEM((2, TMS, D_MODEL), BF16),
                pltpu.VMEM((RUN_ALIGN, D_MODEL), BF16),
                pltpu.VMEM((D_MODEL, D_FF_E), BF16),
                pltpu.VMEM((D_MODEL, D_FF_E), BF16),
                pltpu.VMEM((D_FF_E, D_MODEL), BF16),
                pltpu.SemaphoreType.DMA((2,)),
                pltpu.SemaphoreType.DMA((2,)),
                pltpu.SemaphoreType.DMA(()),
            ],
        ),
        out_shape=jax.ShapeDtypeStruct(xl.shape, BF16),
        compiler_params=_cparams(("arbitrary",)),
        name="moe_experts",
    )(tile_expert, n_used, chunk_src, n_valid, n_chunks, xl, w1, w3, w2)


def _combine_kernel(x_ref, mod_ref, meta_ref, yl_ref, out_ref):
    sorted_row = lax.broadcasted_iota(jnp.int32, (TR, LOCAL_ROWS), 1).astype(F32)
    y = yl_ref[...]
    pick_a = jnp.where(sorted_row == meta_ref[:, 0:1], 1.0, 0.0).astype(BF16)
    pick_b = jnp.where(sorted_row == meta_ref[:, 1:2], 1.0, 0.0).astype(BF16)
    f = meta_ref[:, 2:3] * _dot(pick_a, y) + meta_ref[:, 3:4] * _dot(pick_b, y)
    out_ref[...] = x_ref[...] + mod_ref[0, 5:6, :] * f


def _combine(x, mod_l, meta, yl):
    row = lambda i: (i, 0)
    return pl.pallas_call(
        _combine_kernel,
        grid=(N_TOK // TR,),
        in_specs=[
            pl.BlockSpec((TR, D_MODEL), row),
            pl.BlockSpec((1, 6, D_MODEL), lambda i: (_mod_row(i, TR), 0, 0)),
            pl.BlockSpec((TR, LANES), row),
            pl.BlockSpec((LOCAL_ROWS, D_MODEL), row),
        ],
        out_specs=pl.BlockSpec((TR, D_MODEL), row),
        out_shape=jax.ShapeDtypeStruct((N_TOK, D_MODEL), F32),
        compiler_params=_cparams(("arbitrary",)),
        name="moe_combine",
    )(x, mod_l, meta, yl)


def _moe(layer_i, x, mod_l, g, router_w, router_b, tri, upper, w1, w3, w2):
    xl, meta, counts = _route(x, mod_l, g, router_w, router_b, tri, upper)
    yl = _experts(layer_i, _moe_plan(counts), xl, w1, w3, w2)
    return _combine(x, mod_l, meta, yl)


def _rope_tables():
    n_rows = L_LAT // GRID_W
    rows = jnp.repeat(jnp.arange(n_rows, dtype=F32), GRID_W)
    cols = jnp.tile(jnp.arange(GRID_W, dtype=F32), n_rows)
    quarter = HD // 4
    inv = ROPE_THETA ** (-jnp.arange(quarter, dtype=F32) / quarter)
    ang_r = rows[:, None] * inv
    ang_c = cols[:, None] * inv
    cos = jnp.concatenate([jnp.cos(ang_r)] * 2 + [jnp.cos(ang_c)] * 2, axis=1)
    sin = jnp.concatenate([-jnp.sin(ang_r), jnp.sin(ang_r), -jnp.sin(ang_c), jnp.sin(ang_c)], axis=1)
    cos = jnp.concatenate([cos, cos], axis=1)
    sin = jnp.concatenate([sin, sin], axis=1)
    cos = jnp.concatenate([jnp.ones((L_LAT, LANES), F32), cos], axis=0)
    sin = jnp.concatenate([jnp.zeros((L_LAT, LANES), F32), sin], axis=0)
    return cos, sin


def _block_ones(width):
    r = jnp.arange(width) // HD
    return (r[:, None] == r[None, :]).astype(BF16)


def kernel(x_prompt, x_sample, cache_k, cache_v, c, c_ctx, norm1, norm2, w_mod, b_mod, ev_w_in, ev_conv, ev_q_norm, ev_k_norm, ev_w_out, od_w_in, od_q_norm, od_k_norm, od_sink, od_pool_w, od_pool_scale, od_w_out, ffn_w1, ffn_w3, ffn_w2, moe_router, moe_router_b, moe_w1, moe_w3, moe_w2):
    x = jnp.concatenate([x_prompt.reshape(N_CTX_TOK, D_MODEL), x_sample.reshape(N_LAT_TOK, D_MODEL)], axis=0)
    cond = jnp.concatenate([c_ctx[None, :], c, jnp.zeros((MOD_ROWS - 1 - N_SEQ_LAT, D_MODEL), F32)], axis=0)
    mod = _modulation(cond, w_mod, b_mod).reshape(DEPTH, MOD_ROWS, 6, D_MODEL)

    cos_tab, sin_tab = _rope_tables()
    ones_q = _block_ones(Q_W)
    ones_k = _block_ones(KV_W)
    eye = jnp.eye(HD, dtype=BF16)
    dup = jnp.concatenate([eye, eye], axis=1)
    tri = (jnp.arange(TR)[:, None] > jnp.arange(TR)[None, :]).astype(BF16)
    upper = (jnp.arange(LANES)[:, None] < jnp.arange(LANES)[None, :]).astype(BF16)

    new_k, new_v = [], []
    for l in range(DEPTH):
        i = l // 2
        even = l % 2 == 0
        mod_l = mod[l]
        g1 = norm1[l][None, :]
        g2 = norm2[l][None, :]
        if even:
            q_gain, k_gain = ev_q_norm[i], ev_k_norm[i]
            w_in, w_out = ev_w_in[i], ev_w_out[i]
        else:
            q_gain, k_gain = od_q_norm[i], od_k_norm[i]
            w_in, w_out = od_w_in[i], od_w_out[i]
        q_gain = jnp.tile(q_gain, N_Q)[None, :]
        k_gain = jnp.tile(k_gain, N_KV)[None, :]
        outs = _in_proj(even, x, mod_l, g1, w_in, ones_q, ones_k, q_gain, k_gain, cos_tab, sin_tab)
        if even:
            zc, q, k, v = outs
            sink = None
        else:
            q, k, v, xd = outs
            sink = od_sink[i]
        o_ctx, nk, nv = _ctx_attn(q, k, v, sink)
        o_lat = _lat_attn(l, q, k, v, cache_k, cache_v, dup, sink)
        new_k.append(nk)
        new_v.append(nv)
        if even:
            x = _mix_out(True, x, mod_l, o_ctx, o_lat, w_out, zc, ev_conv[i])
            x = _ffn(i, x, mod_l, g2, ffn_w1, ffn_w3, ffn_w2)
        else:
            x = _mix_out(False, x, mod_l, o_ctx, o_lat, w_out, xd, od_pool_w[i], od_pool_scale[i][None, :])
            rw = jnp.pad(moe_router[i], ((0, 0), (0, LANES - N_EXP)))
            rb = jnp.pad(moe_router_b[i], (0, LANES - N_EXP))[None, :]
            x = _moe(i, x, mod_l, g2, rw, rb, tri, upper, moe_w1, moe_w3, moe_w2)

    y_prompt = x[:N_CTX_TOK].reshape(N_SEQ_CTX, L_CTX, D_MODEL)
    y_sample = x[N_CTX_TOK:].reshape(N_SEQ_LAT, L_LAT, D_MODEL)
    return (y_prompt, y_sample, jnp.stack(new_k, axis=1), jnp.stack(new_v, axis=1))
```

```python
import functools

import jax
import jax.numpy as jnp
from jax import lax
from jax.experimental import pallas as pl
from jax.experimental.pallas import tpu as pltpu

F32 = jnp.float32
BF16 = jnp.bfloat16

D_MODEL = 1024
N_SEQ_CTX = 32
L_CTX = 256
N_SEQ_LAT = 4
L_LAT = 1024
DEPTH = 4
PAST = 512
GRID_W = 64
HD = 64
N_Q = 8
N_KV = 2
GROUP = N_Q // N_KV
Q_W = N_Q * HD
KV_W = N_KV * HD
CONV_W = 512
POOL_W = 512
POOL_WINDOWS = (2, 4, 8, 16)
POOL_G = 128
EVEN_IN = 3 * CONV_W + Q_W + 2 * KV_W
ODD_IN = Q_W + 2 * KV_W + POOL_W
WINDOW = 128
D_FF = 2816
N_EXP = 8
TOP_K = 2
D_FF_E = 1024
ROPE_THETA = 10000.0
EPS = 1e-6

N_CTX_TOK = N_SEQ_CTX * L_CTX
N_LAT_TOK = N_SEQ_LAT * L_LAT
N_TOK = N_CTX_TOK + N_LAT_TOK
MOD_ROWS = 16

LANES = 128
VMEM_LIMIT = 56 * 1024 * 1024

TM_IN = 512
TM = 1024
TQ = 256
TF_FFN = 256
TR = 512
RUN_ALIGN = 16
LOCAL_ROWS = TOP_K * TR + N_EXP * RUN_ALIGN
TMS = 512
MOE_TILES = -(-(TOP_K * N_TOK + (N_TOK // TR) * N_EXP * (RUN_ALIGN - 1) + N_EXP * (TMS - 1)) // TMS)
MOE_ROWS = MOE_TILES * TMS
NEG_INF = float("-inf")


def _cparams(sem):
    return pltpu.CompilerParams(dimension_semantics=sem, vmem_limit_bytes=VMEM_LIMIT)


def _mod_row(i, tm):
    n_ctx = N_CTX_TOK // tm
    return jnp.where(i < n_ctx, 0, 1 + (i - n_ctx) // (L_LAT // tm))


def _x_specs(tm, split):
    if not split:
        return [pl.BlockSpec((tm, D_MODEL), lambda i, *_: (i, 0))]
    n_ctx = N_CTX_TOK // tm
    n_lat = N_LAT_TOK // tm
    return [pl.BlockSpec((tm, D_MODEL), lambda i, *_: (jnp.minimum(i, n_ctx - 1), 0)),
            pl.BlockSpec((tm, D_MODEL), lambda i, *_: (jnp.clip(i - n_ctx, 0, n_lat - 1), 0))]


def _x_shapes(split):
    if not split:
        return [jax.ShapeDtypeStruct((N_TOK, D_MODEL), F32)]
    return [jax.ShapeDtypeStruct((N_CTX_TOK, D_MODEL), F32), jax.ShapeDtypeStruct((N_LAT_TOK, D_MODEL), F32)]


def _read_x(i, x_refs, tm):
    if len(x_refs) == 1:
        return x_refs[0][...]
    return jnp.where(i < N_CTX_TOK // tm, x_refs[0][...], x_refs[1][...])


def _write_x(i, o_refs, tm, val):
    if len(o_refs) == 1:
        o_refs[0][...] = val
        return

    @pl.when(i < N_CTX_TOK // tm)
    def _():
        o_refs[0][...] = val

    @pl.when(i >= N_CTX_TOK // tm)
    def _():
        o_refs[1][...] = val


def _normmod(x, g, scale, shift):
    ms = jnp.mean(x * x, axis=-1, keepdims=True)
    y = x * lax.rsqrt(ms + EPS) * g
    return y * (1.0 + scale) + shift


def _silu(x):
    return x * jax.nn.sigmoid(x)


def _dot(a, b):
    return jnp.dot(a, b, preferred_element_type=F32)


def _dot_nt(a, b):
    return lax.dot_general(a, b, (((1,), (1,)), ((), ())), preferred_element_type=F32)


def _cast_rows(src_ref, dst_ref, rows, chunk=256):
    for r in range(0, rows, chunk):
        dst_ref[r:r + chunk, :] = src_ref[r:r + chunk, :].astype(dst_ref.dtype)


def _mod_kernel(c_ref, w_ref, b_ref, o_ref):
    s = _silu(c_ref[...]).astype(BF16)
    o_ref[0] = _dot(s, w_ref[0].astype(BF16)) + b_ref[0]


def _modulation(cond, w_mod, b_mod):
    tn = 1536
    return pl.pallas_call(
        _mod_kernel,
        grid=(DEPTH, 6 * D_MODEL // tn),
        in_specs=[
            pl.BlockSpec((MOD_ROWS, D_MODEL), lambda l, j: (0, 0)),
            pl.BlockSpec((1, D_MODEL, tn), lambda l, j: (l, 0, j)),
            pl.BlockSpec((1, 1, tn), lambda l, j: (l, 0, j)),
        ],
        out_specs=pl.BlockSpec((1, MOD_ROWS, tn), lambda l, j: (l, 0, j)),
        out_shape=jax.ShapeDtypeStruct((DEPTH, MOD_ROWS, 6 * D_MODEL), F32),
        compiler_params=_cparams(("arbitrary", "arbitrary")),
        name="modulation",
    )(cond, w_mod, b_mod.reshape(DEPTH, 1, 6 * D_MODEL))


def _head_rms(t, ones_bd, gain):
    sq = t * t
    hi = sq.astype(BF16)
    lo = (sq - hi.astype(F32)).astype(BF16)
    ssq = _dot(hi, ones_bd) + _dot(lo, ones_bd)
    return t * lax.rsqrt(ssq * (1.0 / HD) + EPS) * gain


def _rope(t, cos, sin_signed):
    lane = lax.broadcasted_iota(jnp.int32, (t.shape[0], LANES), 1)
    first = (lane & 31) < 16
    outs = []
    for c in range(t.shape[1] // LANES):
        tc = t[:, c * LANES:(c + 1) * LANES]
        nxt = pltpu.roll(tc, LANES - 16, axis=1)
        prv = pltpu.roll(tc, 16, axis=1)
        outs.append(tc * cos + jnp.where(first, nxt, prv) * sin_signed)
    return outs[0] if len(outs) == 1 else jnp.concatenate(outs, axis=1)


def _in_proj_kernel(even, nx, *refs):
    x_refs = refs[:nx]
    mod_ref, g_ref, w_ref, onesq_ref, onesk_ref, qg_ref, kg_ref, cos_ref, sin_ref = refs[nx:nx + 9]
    rest = refs[nx + 9:]
    if even:
        zc_ref, q_ref, k_ref, v_ref, wbf = rest
        q0 = 3 * CONV_W
    else:
        q_ref, k_ref, v_ref, xd_ref, wbf = rest
        q0 = 0
    k0 = q0 + Q_W
    v0 = k0 + KV_W

    @pl.when(pl.program_id(0) == 0)
    def _():
        _cast_rows(w_ref, wbf, D_MODEL)

    x = _read_x(pl.program_id(0), x_refs, TM_IN)
    h = _normmod(x, g_ref[...], mod_ref[0, 1:2, :], mod_ref[0, 0:1, :]).astype(BF16)
    cos = cos_ref[...]
    sin = sin_ref[...]

    q = _dot(h, wbf[:, q0:q0 + Q_W])
    q = _rope(_head_rms(q, onesq_ref[...], qg_ref[...]), cos, sin) * (HD ** -0.5)
    q_ref[...] = q.astype(BF16)

    k = _dot(h, wbf[:, k0:k0 + KV_W])
    k = _rope(_head_rms(k, onesk_ref[...], kg_ref[...]), cos, sin)
    k_ref[...] = k.astype(BF16)

    v_ref[...] = _dot(h, wbf[:, v0:v0 + KV_W]).astype(BF16)

    if even:
        zc_ref[...] = _dot(h, wbf[:, 0:3 * CONV_W]).astype(BF16)
    else:
        xd_ref[...] = _dot(h, wbf[:, v0 + KV_W:v0 + KV_W + POOL_W])


def _in_proj(even, x, mod_l, g, w, ones_q, ones_k, q_gain, k_gain, cos_tab, sin_tab):
    tm = TM_IN
    n_in = EVEN_IN if even else ODD_IN
    n_ctx = N_CTX_TOK // tm
    per_seq = L_LAT // tm

    def rope_idx(i):
        return (jnp.where(i < n_ctx, 0, per_seq + (i - n_ctx) % per_seq), 0)

    row = lambda i: (i, 0)
    const = lambda i: (0, 0)
    in_specs = _x_specs(tm, len(x) == 2) + [
        pl.BlockSpec((1, 6, D_MODEL), lambda i: (_mod_row(i, tm), 0, 0)),
        pl.BlockSpec((1, D_MODEL), const),
        pl.BlockSpec((D_MODEL, n_in), const),
        pl.BlockSpec((Q_W, Q_W), const),
        pl.BlockSpec((KV_W, KV_W), const),
        pl.BlockSpec((1, Q_W), const),
        pl.BlockSpec((1, KV_W), const),
        pl.BlockSpec((tm, LANES), rope_idx),
        pl.BlockSpec((tm, LANES), rope_idx),
    ]
    qkv_specs = [pl.BlockSpec((tm, Q_W), row), pl.BlockSpec((tm, KV_W), row), pl.BlockSpec((tm, KV_W), row)]
    qkv_shapes = [jax.ShapeDtypeStruct((N_TOK, Q_W), BF16), jax.ShapeDtypeStruct((N_TOK, KV_W), BF16),
                  jax.ShapeDtypeStruct((N_TOK, KV_W), BF16)]
    if even:
        out_specs = [pl.BlockSpec((tm, 3 * CONV_W), row)] + qkv_specs
        out_shape = [jax.ShapeDtypeStruct((N_TOK, 3 * CONV_W), BF16)] + qkv_shapes
    else:
        out_specs = qkv_specs + [pl.BlockSpec((tm, POOL_W), row)]
        out_shape = qkv_shapes + [jax.ShapeDtypeStruct((N_TOK, POOL_W), F32)]
    return pl.pallas_call(
        functools.partial(_in_proj_kernel, even, len(x)),
        grid=(N_TOK // tm,),
        in_specs=in_specs,
        out_specs=out_specs,
        out_shape=out_shape,
        scratch_shapes=[pltpu.VMEM((D_MODEL, n_in), BF16)],
        compiler_params=_cparams(("arbitrary",)),
        name="in_proj_even" if even else "in_proj_odd",
    )(*x, mod_l, g, w, ones_q, ones_k, q_gain, k_gain, cos_tab, sin_tab)


def _dup_heads(t):
    lane = lax.broadcasted_iota(jnp.int32, t.shape, 1)
    swapped = pltpu.roll(t, HD, axis=1)
    low = lane < HD
    return jnp.where(low, t, swapped), jnp.where(low, swapped, t)


def _softmax_pv(scores, values, sink):
    m = scores[0].max(axis=-1, keepdims=True)
    for s in scores[1:]:
        m = jnp.maximum(m, s.max(axis=-1, keepdims=True))
    if sink is not None:
        m = jnp.maximum(m, sink)
    den = None
    acc = None
    for s, v in zip(scores, values):
        e = jnp.exp(s - m)
        d = e.sum(axis=-1, keepdims=True)
        a = _dot(e.astype(BF16), v)
        den = d if den is None else den + d
        acc = a if acc is None else acc + a
    if sink is not None:
        den = den + jnp.exp(sink - m)
    return acc / den


def _ctx_attn_kernel(has_sink, *refs):
    if has_sink:
        sink_ref, q_ref, k_ref, v_ref, o_ref, nk_ref, nv_ref = refs
    else:
        q_ref, k_ref, v_ref, o_ref, nk_ref, nv_ref = refs
        sink_ref = None
    k = k_ref[...].astype(F32)
    v = v_ref[...].astype(F32)
    k_sw = pltpu.roll(k, HD, axis=1)
    v_sw = pltpu.roll(v, HD, axis=1)
    nk_ref[0, 0] = k[:, 0:HD]
    nk_ref[0, 1] = k_sw[:, 0:HD]
    nv_ref[0, 0] = v[:, 0:HD]
    nv_ref[0, 1] = v_sw[:, 0:HD]
    lane = lax.broadcasted_iota(jnp.int32, (L_CTX, LANES), 1)
    low = lane < HD
    k2 = (jnp.where(low, k, k_sw).astype(BF16), jnp.where(low, k_sw, k).astype(BF16))
    v2 = (jnp.where(low, v, v_sw).astype(BF16), jnp.where(low, v_sw, v).astype(BF16))
    zero = jnp.zeros((L_CTX, LANES), BF16)
    for pair in range(N_Q // 2):
        kv = pair // (GROUP // 2)
        qp = q_ref[:, pair * LANES:(pair + 1) * LANES]
        outs = []
        for half in range(2):
            qm = jnp.where(low if half == 0 else jnp.logical_not(low), qp, zero)
            s = _dot_nt(qm, k2[kv])
            sink = sink_ref[2 * pair + half] if has_sink else None
            outs.append(_softmax_pv([s], [v2[kv]], sink))
        o_ref[:, pair * LANES:(pair + 1) * LANES] = jnp.where(low, outs[0], outs[1]).astype(BF16)


def _ctx_attn(q, k, v, sink):
    has_sink = sink is not None
    row = lambda b: (b, 0)
    in_specs = [pl.BlockSpec((L_CTX, Q_W), row), pl.BlockSpec((L_CTX, KV_W), row), pl.BlockSpec((L_CTX, KV_W), row)]
    args = [q, k, v]
    if has_sink:
        in_specs = [pl.BlockSpec(memory_space=pltpu.SMEM)] + in_specs
        args = [sink] + args
    kv_spec = pl.BlockSpec((1, N_KV, L_CTX, HD), lambda b: (b, 0, 0, 0))
    kv_shape = jax.ShapeDtypeStruct((N_SEQ_CTX, N_KV, L_CTX, HD), F32)
    return pl.pallas_call(
        functools.partial(_ctx_attn_kernel, has_sink),
        grid=(N_SEQ_CTX,),
        in_specs=in_specs,
        out_specs=[pl.BlockSpec((L_CTX, Q_W), row), kv_spec, kv_spec],
        out_shape=[jax.ShapeDtypeStruct((N_CTX_TOK, Q_W), BF16), kv_shape, kv_shape],
        compiler_params=_cparams(("arbitrary",)),
        name="ctx_attn_sink" if has_sink else "ctx_attn",
    )(*args)


def _lat_attn_kernel(windowed, *refs):
    if windowed:
        sink_ref, q_ref, k_ref, v_ref, ck_ref, cv_ref, dup_ref, o_ref, k2s, v2s, ck2s, cv2s = refs
    else:
        q_ref, k_ref, v_ref, ck_ref, cv_ref, dup_ref, o_ref, k2s, v2s, ck2s, cv2s = refs
        sink_ref = None
    j = pl.program_id(1)

    @pl.when(j == 0)
    def _():
        ka, kb = _dup_heads(k_ref[...].astype(F32))
        va, vb = _dup_heads(v_ref[...].astype(F32))
        k2s[0] = ka.astype(BF16)
        k2s[1] = kb.astype(BF16)
        v2s[0] = va.astype(BF16)
        v2s[1] = vb.astype(BF16)
        dup = dup_ref[...]
        for kv in range(N_KV):
            ck2s[kv] = _dot(ck_ref[0, 0, kv].astype(BF16), dup).astype(BF16)
            cv2s[kv] = _dot(cv_ref[0, 0, kv].astype(BF16), dup).astype(BF16)

    lane = lax.broadcasted_iota(jnp.int32, (TQ, LANES), 1)
    low = lane < HD
    zero = jnp.zeros((TQ, LANES), BF16)
    if windowed:
        n_loc = 2 * TQ
        start = pl.multiple_of(jnp.clip(j * TQ - WINDOW, 0, L_LAT - n_loc), WINDOW)
        qpos = j * TQ + lax.broadcasted_iota(jnp.int32, (TQ, n_loc), 0)
        kpos = start + lax.broadcasted_iota(jnp.int32, (TQ, n_loc), 1)
        valid = jnp.abs(qpos - kpos) <= WINDOW
    for pair in range(N_Q // 2):
        kv = pair // (GROUP // 2)
        qp = q_ref[:, pair * LANES:(pair + 1) * LANES]
        if windowed:
            k_own = k2s[kv, pl.ds(start, n_loc), :]
            v_own = v2s[kv, pl.ds(start, n_loc), :]
        else:
            k_own = k2s[kv]
            v_own = v2s[kv]
        outs = []
        for half in range(2):
            qm = jnp.where(low if half == 0 else jnp.logical_not(low), qp, zero)
            s_ctx = _dot_nt(qm, ck2s[kv])
            s_own = _dot_nt(qm, k_own)
            if windowed:
                s_own = jnp.where(valid, s_own, NEG_INF)
            sink = sink_ref[2 * pair + half] if windowed else None
            outs.append(_softmax_pv([s_ctx, s_own], [cv2s[kv], v_own], sink))
        o_ref[:, pair * LANES:(pair + 1) * LANES] = jnp.where(low, outs[0], outs[1]).astype(BF16)


def _lat_attn(layer, q, k, v, cache_k, cache_v, dup, sink):
    windowed = sink is not None
    n_qt = L_LAT // TQ
    ctx_tiles = N_CTX_TOK // TQ
    ctx_seqs = N_CTX_TOK // L_LAT
    cache_spec = pl.BlockSpec((1, 1, N_KV, PAST, HD), lambda b, j: (b, layer, 0, 0, 0))
    in_specs = [
        pl.BlockSpec((TQ, Q_W), lambda b, j: (ctx_tiles + b * n_qt + j, 0)),
        pl.BlockSpec((L_LAT, KV_W), lambda b, j: (ctx_seqs + b, 0)),
        pl.BlockSpec((L_LAT, KV_W), lambda b, j: (ctx_seqs + b, 0)),
        cache_spec,
        cache_spec,
        pl.BlockSpec((HD, LANES), lambda b, j: (0, 0)),
    ]
    args = [q, k, v, cache_k, cache_v, dup]
    if windowed:
        in_specs = [pl.BlockSpec(memory_space=pltpu.SMEM)] + in_specs
        args = [sink] + args
    return pl.pallas_call(
        functools.partial(_lat_attn_kernel, windowed),
        grid=(N_SEQ_LAT, n_qt),
        in_specs=in_specs,
        out_specs=pl.BlockSpec((TQ, Q_W), lambda b, j: (b * n_qt + j, 0)),
        out_shape=jax.ShapeDtypeStruct((N_LAT_TOK, Q_W), BF16),
        scratch_shapes=[
            pltpu.VMEM((N_KV, L_LAT, LANES), BF16),
            pltpu.VMEM((N_KV, L_LAT, LANES), BF16),
            pltpu.VMEM((N_KV, PAST, LANES), BF16),
            pltpu.VMEM((N_KV, PAST, LANES), BF16),
        ],
        compiler_params=_cparams(("arbitrary", "arbitrary")),
        name="lat_attn_window" if windowed else "lat_attn",
    )(*args)


def _seq_pos(i, width):
    r = lax.broadcasted_iota(jnp.int32, (TM, width), 0)
    is_ctx = i < N_CTX_TOK // TM
    seq_len = jnp.where(is_ctx, L_CTX, L_LAT)
    return r & (seq_len - 1), seq_len


def _shift_rows(t, j, pos, seq_len):
    if j == 0:
        return t
    moved = pltpu.roll(t, (-j) % TM, axis=0)
    ok = (pos + j >= 0) & (pos + j < seq_len)
    return jnp.where(ok, moved, 0.0)


def _conv_mixer(zc_ref, cw_ref, pos, seq_len):
    bg = zc_ref[:, 0:CONV_W].astype(F32)
    u = zc_ref[:, CONV_W:2 * CONV_W].astype(F32) * zc_ref[:, 2 * CONV_W:3 * CONV_W].astype(F32)
    y = (_shift_rows(u, -1, pos, seq_len) * cw_ref[0:1, :] + u * cw_ref[1:2, :]
         + _shift_rows(u, 1, pos, seq_len) * cw_ref[2:3, :])
    return (bg * y).astype(BF16)


def _pool_mixer(xd_ref, pw_ref, ps_ref, pos, seq_len):
    outs = []
    for gi, w in enumerate(POOL_WINDOWS):
        xg = xd_ref[:, gi * POOL_G:(gi + 1) * POOL_G]
        tot = None
        for j in range(-(w // 2), w - w // 2):
            sh = _shift_rows(xg, j, pos, seq_len)
            tot = sh if tot is None else tot + sh
        lo = jnp.clip(pos - w // 2, 0, seq_len)
        hi = jnp.clip(pos - w // 2 + w, 0, seq_len)
        cnt = (hi - lo).astype(F32)
        d = tot / cnt - xg
        outs.append(_dot(d.astype(BF16), pw_ref[gi].astype(BF16)))
    return (jnp.concatenate(outs, axis=1) * ps_ref[...]).astype(BF16)


def _mix_out_kernel(even, nx, *refs):
    x_refs = refs[:nx]
    mod_ref, oc_ref, ol_ref, w_ref = refs[nx:nx + 4]
    rest = refs[nx + 4:]
    if even:
        zc_ref, cw_ref, out_ref, wbf = rest
    else:
        xd_ref, pw_ref, ps_ref, out_ref, wbf = rest
    i = pl.program_id(0)

    @pl.when(i == 0)
    def _():
        _cast_rows(w_ref, wbf, D_MODEL)

    pos, seq_len = _seq_pos(i, CONV_W if even else POOL_G)
    o = jnp.where(i < N_CTX_TOK // TM, oc_ref[...], ol_ref[...])
    if even:
        ya = _conv_mixer(zc_ref, cw_ref, pos, seq_len)
        y = _dot(ya, wbf[0:CONV_W, :]) + _dot(o, wbf[CONV_W:, :])
    else:
        yd = _pool_mixer(xd_ref, pw_ref, ps_ref, pos, seq_len)
        y = _dot(o, wbf[0:Q_W, :]) + _dot(yd, wbf[Q_W:, :])
    out_ref[...] = _read_x(i, x_refs, TM) + mod_ref[0, 2:3, :] * y


def _mix_out(even, x, mod_l, o_ctx, o_lat, w_out, *extra):
    n_ctx = N_CTX_TOK // TM
    n_lat = N_LAT_TOK // TM
    row = lambda i: (i, 0)
    const = lambda i: (0, 0)
    in_specs = _x_specs(TM, len(x) == 2) + [
        pl.BlockSpec((1, 6, D_MODEL), lambda i: (_mod_row(i, TM), 0, 0)),
        pl.BlockSpec((TM, Q_W), lambda i: (jnp.minimum(i, n_ctx - 1), 0)),
        pl.BlockSpec((TM, Q_W), lambda i: (jnp.clip(i - n_ctx, 0, n_lat - 1), 0)),
        pl.BlockSpec((D_MODEL, D_MODEL), const),
    ]
    if even:
        in_specs += [pl.BlockSpec((TM, 3 * CONV_W), row), pl.BlockSpec((3, CONV_W), const)]
    else:
        in_specs += [pl.BlockSpec((TM, POOL_W), row),
                     pl.BlockSpec((len(POOL_WINDOWS), POOL_G, POOL_G), lambda i: (0, 0, 0)),
                     pl.BlockSpec((1, POOL_W), const)]
    return pl.pallas_call(
        functools.partial(_mix_out_kernel, even, len(x)),
        grid=(N_TOK // TM,),
        in_specs=in_specs,
        out_specs=pl.BlockSpec((TM, D_MODEL), row),
        out_shape=jax.ShapeDtypeStruct((N_TOK, D_MODEL), F32),
        scratch_shapes=[pltpu.VMEM((D_MODEL, D_MODEL), BF16)],
        compiler_params=_cparams(("arbitrary",)),
        name="mix_out_even" if even else "mix_out_odd",
    )(*x, mod_l, o_ctx, o_lat, w_out, *extra)


def _ffn_kernel(x_ref, mod_ref, g_ref, w1_ref, w3_ref, w2_ref, out_ref, w1s, w3s, w2s, hs, acc):
    i = pl.program_id(0)
    f = pl.program_id(1)

    @pl.when(i == 0)
    def _():
        w1s[f] = w1_ref[0].astype(BF16)
        w3s[f] = w3_ref[0].astype(BF16)
        w2s[f] = w2_ref[0].astype(BF16)

    @pl.when(f == 0)
    def _():
        hs[...] = _normmod(x_ref[...], g_ref[...], mod_ref[0, 4:5, :], mod_ref[0, 3:4, :]).astype(BF16)
        acc[...] = jnp.zeros_like(acc)

    h = hs[...]
    hid = (_silu(_dot(h, w1s[f])) * _dot(h, w3s[f])).astype(BF16)
    acc[...] += _dot(hid, w2s[f])

    @pl.when(f == pl.num_programs(1) - 1)
    def _():
        out_ref[...] = x_ref[...] + mod_ref[0, 5:6, :] * acc[...]


def _ffn(layer_i, x, mod_l, g, w1, w3, w2):
    nf = D_FF // TF_FFN
    wf = lambda i, f: jnp.where(i == 0, f, nf - 1)
    row = lambda i, f: (i, 0)
    return pl.pallas_call(
        _ffn_kernel,
        grid=(N_TOK // TM, nf),
        in_specs=[
            pl.BlockSpec((TM, D_MODEL), row),
            pl.BlockSpec((1, 6, D_MODEL), lambda i, f: (_mod_row(i, TM), 0, 0)),
            pl.BlockSpec((1, D_MODEL), lambda i, f: (0, 0)),
            pl.BlockSpec((1, D_MODEL, TF_FFN), lambda i, f: (layer_i, 0, wf(i, f))),
            pl.BlockSpec((1, D_MODEL, TF_FFN), lambda i, f: (layer_i, 0, wf(i, f))),
            pl.BlockSpec((1, TF_FFN, D_MODEL), lambda i, f: (layer_i, wf(i, f), 0)),
        ],
        out_specs=pl.BlockSpec((TM, D_MODEL), row),
        out_shape=jax.ShapeDtypeStruct((N_TOK, D_MODEL), F32),
        scratch_shapes=[
            pltpu.VMEM((nf, D_MODEL, TF_FFN), BF16),
            pltpu.VMEM((nf, D_MODEL, TF_FFN), BF16),
            pltpu.VMEM((nf, TF_FFN, D_MODEL), BF16),
            pltpu.VMEM((TM, D_MODEL), BF16),
            pltpu.VMEM((TM, D_MODEL), F32),
        ],
        compiler_params=_cparams(("arbitrary", "arbitrary")),
        name="ffn",
    )(x, mod_l, g, w1, w3, w2)


def _split_bf16(t):
    hi = t.astype(BF16)
    return hi, (t - hi.astype(F32)).astype(BF16)


def _lane_values(col_vals, ones_rows):
    q = jnp.floor(col_vals * (1.0 / 32.0))
    r = col_vals - 32.0 * q
    t = 32.0 * _dot_nt(ones_rows, q.astype(BF16)) + _dot_nt(ones_rows, r.astype(BF16))
    return t[0:1, :]


def _route_kernel(x_ref, mod_ref, g_ref, rw_ref, rb_ref, tri_ref, upper_ref, xl_ref, meta_ref, cnt_ref):
    h = _normmod(x_ref[...], g_ref[...], mod_ref[0, 4:5, :], mod_ref[0, 3:4, :])
    h_hi, h_lo = _split_bf16(h)
    w_hi, w_lo = _split_bf16(rw_ref[...])
    logits = _dot(h_hi, w_hi) + _dot(h_hi, w_lo) + _dot(h_lo, w_hi) + rb_ref[...]
    lane = lax.broadcasted_iota(jnp.int32, logits.shape, 1).astype(F32)
    logits = jnp.where(lane < N_EXP, logits, NEG_INF)
    m1 = logits.max(axis=-1, keepdims=True)
    i1 = jnp.where(logits == m1, lane, float(LANES)).min(axis=-1, keepdims=True)
    rest = jnp.where(lane == i1, NEG_INF, logits)
    m2 = rest.max(axis=-1, keepdims=True)
    i2 = jnp.where(rest == m2, lane, float(LANES)).min(axis=-1, keepdims=True)
    e2 = jnp.exp(m2 - m1)
    den = 1.0 + e2
    g1 = 1.0 / den
    g2 = e2 / den

    oh_a = jnp.where(lane == i1, 1.0, 0.0)
    oh_b = jnp.where(lane == i2, 1.0, 0.0)
    tri = tri_ref[...]
    cnt_a = oh_a.sum(axis=0, keepdims=True)
    cnt_b = oh_b.sum(axis=0, keepdims=True)
    run16 = jnp.floor((cnt_a + cnt_b + (RUN_ALIGN - 1)) * (1.0 / RUN_ALIGN))
    run16_rows = jnp.broadcast_to(run16, (8, LANES))
    start = RUN_ALIGN * _dot(run16_rows.astype(BF16), upper_ref[...])[0:1, :]
    row_a = oh_a * (start + _dot(tri, oh_a.astype(BF16)))
    row_b = oh_b * (start + cnt_a + _dot(tri, oh_b.astype(BF16)))

    ones_rows = jnp.ones((8, LANES), BF16)
    tok_a = _lane_values(row_a, ones_rows)
    tok_b = _lane_values(row_b, ones_rows)
    sorted_row = lax.broadcasted_iota(jnp.int32, (LOCAL_ROWS, TR), 0).astype(F32)
    perm = jnp.where((sorted_row == tok_a) | (sorted_row == tok_b), 1.0, 0.0).astype(BF16)
    xl_ref[...] = _dot(perm, h_hi).astype(BF16)

    meta = jnp.zeros_like(logits)
    cols = (row_a.sum(axis=-1, keepdims=True), row_b.sum(axis=-1, keepdims=True), g1, g2)
    for k, col in enumerate(cols):
        meta = jnp.where(lane == k, col, meta)
    meta_ref[...] = meta
    cnt_ref[...] = RUN_ALIGN * run16_rows


def _route(x, mod_l, g, router_w, router_b, tri, upper):
    row = lambda i: (i, 0)
    const = lambda i: (0, 0)
    n_tiles = N_TOK // TR
    return pl.pallas_call(
        _route_kernel,
        grid=(n_tiles,),
        in_specs=[
            pl.BlockSpec((TR, D_MODEL), row),
            pl.BlockSpec((1, 6, D_MODEL), lambda i: (_mod_row(i, TR), 0, 0)),
            pl.BlockSpec((1, D_MODEL), const),
            pl.BlockSpec((D_MODEL, LANES), const),
            pl.BlockSpec((1, LANES), const),
            pl.BlockSpec((TR, TR), const),
            pl.BlockSpec((LANES, LANES), const),
        ],
        out_specs=[pl.BlockSpec((LOCAL_ROWS, D_MODEL), row), pl.BlockSpec((TR, LANES), row),
                   pl.BlockSpec((8, LANES), row)],
        out_shape=[jax.ShapeDtypeStruct((n_tiles * LOCAL_ROWS, D_MODEL), BF16),
                   jax.ShapeDtypeStruct((N_TOK, LANES), F32),
                   jax.ShapeDtypeStruct((n_tiles * 8, LANES), F32)],
        compiler_params=_cparams(("arbitrary",)),
        name="moe_route",
    )(x, mod_l, g, router_w, router_b, tri, upper)


def _moe_plan(counts):
    n_tiles = N_TOK // TR
    run = counts.reshape(n_tiles, 8, LANES)[:, 0, :N_EXP].astype(jnp.int32)
    per_expert = jnp.sum(run, axis=0)
    region = (per_expert + (TMS - 1)) // TMS * TMS
    ends = jnp.cumsum(region)
    offs = ends - region
    seg_end = jnp.cumsum(run, axis=0)
    seg_start = seg_end - run
    local_end = jnp.cumsum(run, axis=1)
    local_start = local_end - run
    n_chunks = local_end[:, N_EXP - 1] // RUN_ALIGN
    tile_start = jnp.arange(MOE_TILES, dtype=jnp.int32) * TMS
    tile_expert = jnp.sum((tile_start[:, None] >= ends[None, :]).astype(jnp.int32), axis=1)
    tile_expert = jnp.minimum(tile_expert, N_EXP - 1)
    n_used = (ends[N_EXP - 1] // TMS).reshape(1)
    experts = jnp.arange(N_EXP, dtype=jnp.int32)
    g_row = jnp.arange(MOE_ROWS // RUN_ALIGN, dtype=jnp.int32) * RUN_ALIGN
    g_exp = jnp.repeat(tile_expert, TMS // RUN_ALIGN)
    pick = g_exp[:, None] == experts[None, :]
    rel = g_row - jnp.sum(jnp.where(pick, offs[None, :], 0), axis=1)
    ends_of = jnp.sum(jnp.where(pick[:, None, :], seg_end[None, :, :], 0), axis=2)
    src_tile = jnp.sum((rel[:, None] >= ends_of).astype(jnp.int32), axis=1)
    valid = (src_tile < n_tiles) & (jnp.repeat(jnp.arange(MOE_TILES), TMS // RUN_ALIGN) < n_used[0])
    src_tile = jnp.minimum(src_tile, n_tiles - 1)
    sel = (src_tile[:, None, None] == jnp.arange(n_tiles)[None, :, None]) & pick[:, None, :]
    shift = jnp.sum(jnp.where(sel, (local_start - seg_start)[None, :, :], 0), axis=(1, 2))
    chunk_src = src_tile * LOCAL_ROWS + rel + shift
    n_valid = jnp.sum(valid.reshape(MOE_TILES, TMS // RUN_ALIGN).astype(jnp.int32), axis=1)
    return chunk_src, n_valid, n_chunks, tile_expert, n_used


def _chunk_copy(src_ref, src_row, dst_ref, dst_row, sem):
    return pltpu.make_async_copy(src_ref.at[pl.ds(src_row, RUN_ALIGN), :],
                                 dst_ref.at[pl.ds(dst_row, RUN_ALIGN), :], sem)


CHUNKS = TMS // RUN_ALIGN
LOCAL_CHUNKS = LOCAL_ROWS // RUN_ALIGN


def _expert_kernel(te_ref, nu_ref, src_ref, nv_ref, nc_ref, xl_hbm, w1_ref, w3_ref, w2_ref, yl_hbm,
                   xbuf, ybuf, zbuf, w1s, w3s, w2s, sem_in, sem_out, sem_zero):
    t = pl.program_id(0)
    n_used = nu_ref[0]
    slot = t & 1

    def gather(tile, s):
        n = nv_ref[tile]

        def issue(j, c):
            src = pl.multiple_of(src_ref[tile * CHUNKS + j], RUN_ALIGN)
            _chunk_copy(xl_hbm, src, xbuf.at[s], pl.multiple_of(j * RUN_ALIGN, RUN_ALIGN), sem_in.at[s]).start()
            return c

        def pad(j, c):
            xbuf[s, pl.ds(pl.multiple_of(j * RUN_ALIGN, RUN_ALIGN), RUN_ALIGN), :] = jnp.zeros(
                (RUN_ALIGN, D_MODEL), BF16)
            return c

        lax.fori_loop(0, n, issue, 0)
        lax.fori_loop(n, CHUNKS, pad, 0)

    def scatter(tile, s):
        def issue(j, c):
            dst = pl.multiple_of(src_ref[tile * CHUNKS + j], RUN_ALIGN)
            _chunk_copy(ybuf.at[s], pl.multiple_of(j * RUN_ALIGN, RUN_ALIGN), yl_hbm, dst, sem_out.at[s]).start()
            return c

        lax.fori_loop(0, nv_ref[tile], issue, 0)

    def drain(count, sem):
        def one(j, c):
            _chunk_copy(xl_hbm, 0, xbuf.at[0], 0, sem).wait()
            return c

        lax.fori_loop(0, count, one, 0)

    @pl.when(t == 0)
    def _():
        zbuf[...] = jnp.zeros_like(zbuf)
        total = jnp.int32(0)
        for tile in range(N_TOK // TR):
            n = nc_ref[tile]

            def clear(j, c, tile=tile):
                row = pl.multiple_of(tile * LOCAL_ROWS + j * RUN_ALIGN, RUN_ALIGN)
                _chunk_copy(zbuf, 0, yl_hbm, row, sem_zero).start()
                return c

            lax.fori_loop(n, LOCAL_CHUNKS, clear, 0)
            total = total + (LOCAL_CHUNKS - n)
        drain(total, sem_zero)
        gather(0, 0)

    @pl.when(t < n_used)
    def _():
        drain(nv_ref[t], sem_in.at[slot])

    @pl.when(t + 1 < n_used)
    def _():
        gather(t + 1, 1 - slot)

    new_expert = (t == 0) | (te_ref[t] != te_ref[jnp.maximum(t - 1, 0)])

    @pl.when(new_expert)
    def _():
        _cast_rows(w1_ref.at[0, 0], w1s, D_MODEL)
        _cast_rows(w3_ref.at[0, 0], w3s, D_MODEL)
        _cast_rows(w2_ref.at[0, 0], w2s, D_FF_E)

    @pl.when((t >= 2) & (t - 2 < n_used))
    def _():
        drain(nv_ref[jnp.maximum(t - 2, 0)], sem_out.at[slot])

    @pl.when(t < n_used)
    def _():
        x = xbuf[slot]
        hid = (_silu(_dot(x, w1s[...])) * _dot(x, w3s[...])).astype(BF16)
        ybuf[slot] = _dot(hid, w2s[...]).astype(BF16)
        scatter(t, slot)

    @pl.when(t == pl.num_programs(0) - 1)
    def _():
        @pl.when((t >= 1) & (t - 1 < n_used))
        def _():
            drain(nv_ref[jnp.maximum(t - 1, 0)], sem_out.at[1 - slot])

        @pl.when(t < n_used)
        def _():
            drain(nv_ref[t], sem_out.at[slot])


def _experts(layer_i, plan, xl, w1, w3, w2):
    chunk_src, n_valid, n_chunks, tile_expert, n_used = plan
    wsel = lambda t, te, nu, src, nv, nc: (layer_i, te[t], 0, 0)
    any_spec = pl.BlockSpec(memory_space=pl.ANY)
    return pl.pallas_call(
        _expert_kernel,
        grid_spec=pltpu.PrefetchScalarGridSpec(
            num_scalar_prefetch=5,
            grid=(MOE_TILES,),
            in_specs=[
                any_spec,
                pl.BlockSpec((1, 1, D_MODEL, D_FF_E), wsel),
                pl.BlockSpec((1, 1, D_MODEL, D_FF_E), wsel),
                pl.BlockSpec((1, 1, D_FF_E, D_MODEL), wsel),
            ],
            out_specs=any_spec,
            scratch_shapes=[
                pltpu.VMEM((2, TMS, D_MODEL), BF16),
                pltpu.VMEM((2, TMS, D_MODEL), BF16),
                pltpu.VMEM((RUN_ALIGN, D_MODEL), BF16),
                pltpu.VMEM((D_MODEL, D_FF_E), BF16),
                pltpu.VMEM((D_MODEL, D_FF_E), BF16),
                pltpu.VMEM((D_FF_E, D_MODEL), BF16),
                pltpu.SemaphoreType.DMA((2,)),
                pltpu.SemaphoreType.DMA((2,)),
                pltpu.SemaphoreType.DMA(()),
            ],
        ),
        out_shape=jax.ShapeDtypeStruct(xl.shape, BF16),
        compiler_params=_cparams(("arbitrary",)),
        name="moe_experts",
    )(tile_expert, n_used, chunk_src, n_valid, n_chunks, xl, w1, w3, w2)


def _combine_kernel(x_ref, mod_ref, meta_ref, yl_ref, *out_refs):
    sorted_row = lax.broadcasted_iota(jnp.int32, (TR, LOCAL_ROWS), 1).astype(F32)
    y = yl_ref[...]
    pick_a = jnp.where(sorted_row == meta_ref[:, 0:1], 1.0, 0.0).astype(BF16)
    pick_b = jnp.where(sorted_row == meta_ref[:, 1:2], 1.0, 0.0).astype(BF16)
    f = meta_ref[:, 2:3] * _dot(pick_a, y) + meta_ref[:, 3:4] * _dot(pick_b, y)
    _write_x(pl.program_id(0), out_refs, TR, x_ref[...] + mod_ref[0, 5:6, :] * f)


def _combine(x, mod_l, meta, yl, split_out):
    row = lambda i: (i, 0)
    return pl.pallas_call(
        _combine_kernel,
        grid=(N_TOK // TR,),
        in_specs=[
            pl.BlockSpec((TR, D_MODEL), row),
            pl.BlockSpec((1, 6, D_MODEL), lambda i: (_mod_row(i, TR), 0, 0)),
            pl.BlockSpec((TR, LANES), row),
            pl.BlockSpec((LOCAL_ROWS, D_MODEL), row),
        ],
        out_specs=_x_specs(TR, split_out),
        out_shape=_x_shapes(split_out),
        compiler_params=_cparams(("arbitrary",)),
        name="moe_combine",
    )(x, mod_l, meta, yl)


def _moe(layer_i, x, mod_l, g, router_w, router_b, tri, upper, w1, w3, w2, split_out):
    xl, meta, counts = _route(x, mod_l, g, router_w, router_b, tri, upper)
    yl = _experts(layer_i, _moe_plan(counts), xl, w1, w3, w2)
    return _combine(x, mod_l, meta, yl, split_out)


def _rope_tables():
    n_rows = L_LAT // GRID_W
    rows = jnp.repeat(jnp.arange(n_rows, dtype=F32), GRID_W)
    cols = jnp.tile(jnp.arange(GRID_W, dtype=F32), n_rows)
    quarter = HD // 4
    inv = ROPE_THETA ** (-jnp.arange(quarter, dtype=F32) / quarter)
    ang_r = rows[:, None] * inv
    ang_c = cols[:, None] * inv
    cos = jnp.concatenate([jnp.cos(ang_r)] * 2 + [jnp.cos(ang_c)] * 2, axis=1)
    sin = jnp.concatenate([-jnp.sin(ang_r), jnp.sin(ang_r), -jnp.sin(ang_c), jnp.sin(ang_c)], axis=1)
    cos = jnp.concatenate([cos, cos], axis=1)
    sin = jnp.concatenate([sin, sin], axis=1)
    cos = jnp.concatenate([jnp.ones((L_LAT, LANES), F32), cos], axis=0)
    sin = jnp.concatenate([jnp.zeros((L_LAT, LANES), F32), sin], axis=0)
    return cos, sin


def _block_ones(width):
    r = jnp.arange(width) // HD
    return (r[:, None] == r[None, :]).astype(BF16)


def kernel(x_prompt, x_sample, cache_k, cache_v, c, c_ctx, norm1, norm2, w_mod, b_mod, ev_w_in, ev_conv, ev_q_norm, ev_k_norm, ev_w_out, od_w_in, od_q_norm, od_k_norm, od_sink, od_pool_w, od_pool_scale, od_w_out, ffn_w1, ffn_w3, ffn_w2, moe_router, moe_router_b, moe_w1, moe_w3, moe_w2):
    x = (x_prompt.reshape(N_CTX_TOK, D_MODEL), x_sample.reshape(N_LAT_TOK, D_MODEL))
    cond = jnp.concatenate([c_ctx[None, :], c, jnp.zeros((MOD_ROWS - 1 - N_SEQ_LAT, D_MODEL), F32)], axis=0)
    mod = _modulation(cond, w_mod, b_mod).reshape(DEPTH, MOD_ROWS, 6, D_MODEL)

    cos_tab, sin_tab = _rope_tables()
    ones_q = _block_ones(Q_W)
    ones_k = _block_ones(KV_W)
    eye = jnp.eye(HD, dtype=BF16)
    dup = jnp.concatenate([eye, eye], axis=1)
    tri = (jnp.arange(TR)[:, None] > jnp.arange(TR)[None, :]).astype(BF16)
    upper = (jnp.arange(LANES)[:, None] < jnp.arange(LANES)[None, :]).astype(BF16)

    new_k, new_v = [], []
    for l in range(DEPTH):
        i = l // 2
        even = l % 2 == 0
        mod_l = mod[l]
        g1 = norm1[l][None, :]
        g2 = norm2[l][None, :]
        if even:
            q_gain, k_gain = ev_q_norm[i], ev_k_norm[i]
            w_in, w_out = ev_w_in[i], ev_w_out[i]
        else:
            q_gain, k_gain = od_q_norm[i], od_k_norm[i]
            w_in, w_out = od_w_in[i], od_w_out[i]
        q_gain = jnp.tile(q_gain, N_Q)[None, :]
        k_gain = jnp.tile(k_gain, N_KV)[None, :]
        outs = _in_proj(even, x, mod_l, g1, w_in, ones_q, ones_k, q_gain, k_gain, cos_tab, sin_tab)
        if even:
            zc, q, k, v = outs
            sink = None
        else:
            q, k, v, xd = outs
            sink = od_sink[i]
        o_ctx, nk, nv = _ctx_attn(q, k, v, sink)
        o_lat = _lat_attn(l, q, k, v, cache_k, cache_v, dup, sink)
        new_k.append(nk)
        new_v.append(nv)
        if even:
            x1 = _mix_out(True, x, mod_l, o_ctx, o_lat, w_out, zc, ev_conv[i])
            x = (_ffn(i, x1, mod_l, g2, ffn_w1, ffn_w3, ffn_w2),)
        else:
            x1 = _mix_out(False, x, mod_l, o_ctx, o_lat, w_out, xd, od_pool_w[i], od_pool_scale[i][None, :])
            rw = jnp.pad(moe_router[i], ((0, 0), (0, LANES - N_EXP)))
            rb = jnp.pad(moe_router_b[i], (0, LANES - N_EXP))[None, :]
            x = tuple(_moe(i, x1, mod_l, g2, rw, rb, tri, upper, moe_w1, moe_w3, moe_w2, l == DEPTH - 1))

    y_prompt = x[0].reshape(N_SEQ_CTX, L_CTX, D_MODEL)
    y_sample = x[1].reshape(N_SEQ_LAT, L_LAT, D_MODEL)
    return (y_prompt, y_sample, jnp.stack(new_k, axis=1), jnp.stack(new_v, axis=1))
```

```python
import functools

import jax
import jax.numpy as jnp
from jax import lax
from jax.experimental import pallas as pl
from jax.experimental.pallas import tpu as pltpu

F32 = jnp.float32
BF16 = jnp.bfloat16

D_MODEL = 1024
N_SEQ_CTX = 32
L_CTX = 256
N_SEQ_LAT = 4
L_LAT = 1024
DEPTH = 4
PAST = 512
GRID_W = 64
HD = 64
N_Q = 8
N_KV = 2
GROUP = N_Q // N_KV
Q_W = N_Q * HD
KV_W = N_KV * HD
CONV_W = 512
POOL_W = 512
POOL_WINDOWS = (2, 4, 8, 16)
POOL_G = 128
EVEN_IN = 3 * CONV_W + Q_W + 2 * KV_W
ODD_IN = Q_W + 2 * KV_W + POOL_W
WINDOW = 128
D_FF = 2816
N_EXP = 8
TOP_K = 2
D_FF_E = 1024
ROPE_THETA = 10000.0
EPS = 1e-6

N_CTX_TOK = N_SEQ_CTX * L_CTX
N_LAT_TOK = N_SEQ_LAT * L_LAT
N_TOK = N_CTX_TOK + N_LAT_TOK
MOD_ROWS = 16

LANES = 128
VMEM_LIMIT = 56 * 1024 * 1024
FFN_VMEM_LIMIT = 62 * 1024 * 1024

TM_IN = 512
TM = 1024
TQ = 256
TF_FFN = 256
TR = 512
RUN_ALIGN = 16
LOCAL_ROWS = TOP_K * TR + N_EXP * RUN_ALIGN
TMS = 512
MOE_TILES = -(-(TOP_K * N_TOK + (N_TOK // TR) * N_EXP * (RUN_ALIGN - 1) + N_EXP * (TMS - 1)) // TMS)
MOE_ROWS = MOE_TILES * TMS
NEG_INF = float("-inf")


def _cparams(sem):
    return pltpu.CompilerParams(dimension_semantics=sem, vmem_limit_bytes=VMEM_LIMIT)


def _mod_row(i, tm):
    n_ctx = N_CTX_TOK // tm
    return jnp.where(i < n_ctx, 0, 1 + (i - n_ctx) // (L_LAT // tm))


def _x_specs(tm, split):
    if not split:
        return [pl.BlockSpec((tm, D_MODEL), lambda i, *_: (i, 0))]
    n_ctx = N_CTX_TOK // tm
    n_lat = N_LAT_TOK // tm
    return [pl.BlockSpec((tm, D_MODEL), lambda i, *_: (jnp.minimum(i, n_ctx - 1), 0)),
            pl.BlockSpec((tm, D_MODEL), lambda i, *_: (jnp.clip(i - n_ctx, 0, n_lat - 1), 0))]


def _x_shapes(split):
    if not split:
        return [jax.ShapeDtypeStruct((N_TOK, D_MODEL), F32)]
    return [jax.ShapeDtypeStruct((N_CTX_TOK, D_MODEL), F32), jax.ShapeDtypeStruct((N_LAT_TOK, D_MODEL), F32)]


def _read_x(i, x_refs, tm):
    if len(x_refs) == 1:
        return x_refs[0][...]
    return jnp.where(i < N_CTX_TOK // tm, x_refs[0][...], x_refs[1][...])


def _write_x(i, o_refs, tm, val):
    if len(o_refs) == 1:
        o_refs[0][...] = val
        return

    @pl.when(i < N_CTX_TOK // tm)
    def _():
        o_refs[0][...] = val

    @pl.when(i >= N_CTX_TOK // tm)
    def _():
        o_refs[1][...] = val


def _normmod(x, g, scale, shift):
    ms = jnp.mean(x * x, axis=-1, keepdims=True)
    y = x * lax.rsqrt(ms + EPS) * g
    return y * (1.0 + scale) + shift


def _silu(x):
    return x * jax.nn.sigmoid(x)


def _dot(a, b):
    return jnp.dot(a, b, preferred_element_type=F32)


def _dot_nt(a, b):
    return lax.dot_general(a, b, (((1,), (1,)), ((), ())), preferred_element_type=F32)


def _cast_rows(src_ref, dst_ref, rows, chunk=256):
    for r in range(0, rows, chunk):
        dst_ref[r:r + chunk, :] = src_ref[r:r + chunk, :].astype(dst_ref.dtype)


def _mod_kernel(c_ref, w_ref, b_ref, o_ref):
    s = _silu(c_ref[...]).astype(BF16)
    o_ref[0] = _dot(s, w_ref[0].astype(BF16)) + b_ref[0]


def _modulation(cond, w_mod, b_mod):
    tn = 1536
    return pl.pallas_call(
        _mod_kernel,
        grid=(DEPTH, 6 * D_MODEL // tn),
        in_specs=[
            pl.BlockSpec((MOD_ROWS, D_MODEL), lambda l, j: (0, 0)),
            pl.BlockSpec((1, D_MODEL, tn), lambda l, j: (l, 0, j)),
            pl.BlockSpec((1, 1, tn), lambda l, j: (l, 0, j)),
        ],
        out_specs=pl.BlockSpec((1, MOD_ROWS, tn), lambda l, j: (l, 0, j)),
        out_shape=jax.ShapeDtypeStruct((DEPTH, MOD_ROWS, 6 * D_MODEL), F32),
        compiler_params=_cparams(("arbitrary", "arbitrary")),
        name="modulation",
    )(cond, w_mod, b_mod.reshape(DEPTH, 1, 6 * D_MODEL))


def _head_rms(t, ones_bd, gain):
    sq = t * t
    hi = sq.astype(BF16)
    lo = (sq - hi.astype(F32)).astype(BF16)
    ssq = _dot(hi, ones_bd) + _dot(lo, ones_bd)
    return t * lax.rsqrt(ssq * (1.0 / HD) + EPS) * gain


def _rope(t, cos, sin_signed):
    lane = lax.broadcasted_iota(jnp.int32, (t.shape[0], LANES), 1)
    first = (lane & 31) < 16
    outs = []
    for c in range(t.shape[1] // LANES):
        tc = t[:, c * LANES:(c + 1) * LANES]
        nxt = pltpu.roll(tc, LANES - 16, axis=1)
        prv = pltpu.roll(tc, 16, axis=1)
        outs.append(tc * cos + jnp.where(first, nxt, prv) * sin_signed)
    return outs[0] if len(outs) == 1 else jnp.concatenate(outs, axis=1)


def _in_proj_kernel(even, nx, *refs):
    x_refs = refs[:nx]
    mod_ref, g_ref, w_ref, onesq_ref, onesk_ref, qg_ref, kg_ref, cos_ref, sin_ref = refs[nx:nx + 9]
    rest = refs[nx + 9:]
    if even:
        zc_ref, q_ref, k_ref, v_ref, wbf = rest
        q0 = 3 * CONV_W
    else:
        q_ref, k_ref, v_ref, xd_ref, wbf = rest
        q0 = 0
    k0 = q0 + Q_W
    v0 = k0 + KV_W

    @pl.when(pl.program_id(0) == 0)
    def _():
        _cast_rows(w_ref.at[0], wbf, D_MODEL)

    x = _read_x(pl.program_id(0), x_refs, TM_IN)
    h = _normmod(x, g_ref[...], mod_ref[0, 1:2, :], mod_ref[0, 0:1, :]).astype(BF16)
    cos = cos_ref[...]
    sin = sin_ref[...]

    q = _dot(h, wbf[:, q0:q0 + Q_W])
    q = _rope(_head_rms(q, onesq_ref[...], qg_ref[...]), cos, sin) * (HD ** -0.5)
    q_ref[...] = q.astype(BF16)

    k = _dot(h, wbf[:, k0:k0 + KV_W])
    k = _rope(_head_rms(k, onesk_ref[...], kg_ref[...]), cos, sin)
    k_ref[...] = k.astype(BF16)

    v_ref[...] = _dot(h, wbf[:, v0:v0 + KV_W]).astype(BF16)

    if even:
        zc_ref[...] = _dot(h, wbf[:, 0:3 * CONV_W]).astype(BF16)
    else:
        xd_ref[...] = _dot(h, wbf[:, v0 + KV_W:v0 + KV_W + POOL_W])


def _in_proj(even, layer_i, x, mod_l, g, w, ones_q, ones_k, q_gain, k_gain, cos_tab, sin_tab):
    tm = TM_IN
    n_in = EVEN_IN if even else ODD_IN
    n_ctx = N_CTX_TOK // tm
    per_seq = L_LAT // tm

    def rope_idx(i):
        return (jnp.where(i < n_ctx, 0, per_seq + (i - n_ctx) % per_seq), 0)

    row = lambda i: (i, 0)
    const = lambda i: (0, 0)
    in_specs = _x_specs(tm, len(x) == 2) + [
        pl.BlockSpec((1, 6, D_MODEL), lambda i: (_mod_row(i, tm), 0, 0)),
        pl.BlockSpec((1, D_MODEL), const),
        pl.BlockSpec((1, D_MODEL, n_in), lambda i: (layer_i, 0, 0)),
        pl.BlockSpec((Q_W, Q_W), const),
        pl.BlockSpec((KV_W, KV_W), const),
        pl.BlockSpec((1, Q_W), const),
        pl.BlockSpec((1, KV_W), const),
        pl.BlockSpec((tm, LANES), rope_idx),
        pl.BlockSpec((tm, LANES), rope_idx),
    ]
    qkv_specs = [pl.BlockSpec((tm, Q_W), row), pl.BlockSpec((tm, KV_W), row), pl.BlockSpec((tm, KV_W), row)]
    qkv_shapes = [jax.ShapeDtypeStruct((N_TOK, Q_W), BF16), jax.ShapeDtypeStruct((N_TOK, KV_W), BF16),
                  jax.ShapeDtypeStruct((N_TOK, KV_W), BF16)]
    if even:
        out_specs = [pl.BlockSpec((tm, 3 * CONV_W), row)] + qkv_specs
        out_shape = [jax.ShapeDtypeStruct((N_TOK, 3 * CONV_W), BF16)] + qkv_shapes
    else:
        out_specs = qkv_specs + [pl.BlockSpec((tm, POOL_W), row)]
        out_shape = qkv_shapes + [jax.ShapeDtypeStruct((N_TOK, POOL_W), F32)]
    return pl.pallas_call(
        functools.partial(_in_proj_kernel, even, len(x)),
        grid=(N_TOK // tm,),
        in_specs=in_specs,
        out_specs=out_specs,
        out_shape=out_shape,
        scratch_shapes=[pltpu.VMEM((D_MODEL, n_in), BF16)],
        compiler_params=_cparams(("arbitrary",)),
        name="in_proj_even" if even else "in_proj_odd",
    )(*x, mod_l, g, w, ones_q, ones_k, q_gain, k_gain, cos_tab, sin_tab)


def _dup_heads(t):
    lane = lax.broadcasted_iota(jnp.int32, t.shape, 1)
    swapped = pltpu.roll(t, HD, axis=1)
    low = lane < HD
    return jnp.where(low, t, swapped), jnp.where(low, swapped, t)


def _softmax_pv(scores, values, sink):
    m = scores[0].max(axis=-1, keepdims=True)
    for s in scores[1:]:
        m = jnp.maximum(m, s.max(axis=-1, keepdims=True))
    if sink is not None:
        m = jnp.maximum(m, sink)
    den = None
    acc = None
    for s, v in zip(scores, values):
        e = jnp.exp(s - m)
        d = e.sum(axis=-1, keepdims=True)
        a = _dot(e.astype(BF16), v)
        den = d if den is None else den + d
        acc = a if acc is None else acc + a
    if sink is not None:
        den = den + jnp.exp(sink - m)
    return acc / den


def _ctx_attn_kernel(has_sink, first, *refs):
    if has_sink:
        sink_ref, refs = refs[0], refs[1:]
    else:
        sink_ref = None
    q_ref, k_ref, v_ref = refs[:3]
    o_ref, nk_ref, nv_ref = refs[-3:]
    k = k_ref[...].astype(F32)
    v = v_ref[...].astype(F32)
    k_sw = pltpu.roll(k, HD, axis=1)
    v_sw = pltpu.roll(v, HD, axis=1)
    nk_ref[0, 0, 0] = k[:, 0:HD]
    nk_ref[0, 0, 1] = k_sw[:, 0:HD]
    nv_ref[0, 0, 0] = v[:, 0:HD]
    nv_ref[0, 0, 1] = v_sw[:, 0:HD]
    if first:
        nk_ref[0, 1:] = jnp.zeros((DEPTH - 1, N_KV, L_CTX, HD), F32)
        nv_ref[0, 1:] = jnp.zeros((DEPTH - 1, N_KV, L_CTX, HD), F32)
    lane = lax.broadcasted_iota(jnp.int32, (L_CTX, LANES), 1)
    low = lane < HD
    k2 = (jnp.where(low, k, k_sw).astype(BF16), jnp.where(low, k_sw, k).astype(BF16))
    v2 = (jnp.where(low, v, v_sw).astype(BF16), jnp.where(low, v_sw, v).astype(BF16))
    zero = jnp.zeros((L_CTX, LANES), BF16)
    for pair in range(N_Q // 2):
        kv = pair // (GROUP // 2)
        qp = q_ref[:, pair * LANES:(pair + 1) * LANES]
        outs = []
        for half in range(2):
            qm = jnp.where(low if half == 0 else jnp.logical_not(low), qp, zero)
            s = _dot_nt(qm, k2[kv])
            sink = sink_ref[2 * pair + half] if has_sink else None
            outs.append(_softmax_pv([s], [v2[kv]], sink))
        o_ref[:, pair * LANES:(pair + 1) * LANES] = jnp.where(low, outs[0], outs[1]).astype(BF16)


def _ctx_attn(layer, q, k, v, sink, new_kv):
    has_sink = sink is not None
    first = new_kv is None
    row = lambda b: (b, 0)
    in_specs = [pl.BlockSpec((L_CTX, Q_W), row), pl.BlockSpec((L_CTX, KV_W), row), pl.BlockSpec((L_CTX, KV_W), row)]
    args = [q, k, v]
    if has_sink:
        in_specs = [pl.BlockSpec(memory_space=pltpu.SMEM)] + in_specs
        args = [sink] + args
    aliases = {}
    if first:
        kv_spec = pl.BlockSpec((1, DEPTH, N_KV, L_CTX, HD), lambda b: (b, 0, 0, 0, 0))
    else:
        kv_spec = pl.BlockSpec((1, 1, N_KV, L_CTX, HD), lambda b: (b, layer, 0, 0, 0))
        aliases = {len(args): 1, len(args) + 1: 2}
        in_specs = in_specs + [pl.BlockSpec(memory_space=pl.ANY)] * 2
        args = args + list(new_kv)
    kv_shape = jax.ShapeDtypeStruct((N_SEQ_CTX, DEPTH, N_KV, L_CTX, HD), F32)
    o, nk, nv = pl.pallas_call(
        functools.partial(_ctx_attn_kernel, has_sink, first),
        grid=(N_SEQ_CTX,),
        in_specs=in_specs,
        out_specs=[pl.BlockSpec((L_CTX, Q_W), row), kv_spec, kv_spec],
        out_shape=[jax.ShapeDtypeStruct((N_CTX_TOK, Q_W), BF16), kv_shape, kv_shape],
        input_output_aliases=aliases,
        compiler_params=_cparams(("arbitrary",)),
        name="ctx_attn_sink" if has_sink else "ctx_attn",
    )(*args)
    return o, (nk, nv)


def _lat_attn_kernel(windowed, *refs):
    if windowed:
        sink_ref, q_ref, k_ref, v_ref, ck_ref, cv_ref, dup_ref, o_ref, k2s, v2s, ck2s, cv2s = refs
    else:
        q_ref, k_ref, v_ref, ck_ref, cv_ref, dup_ref, o_ref, k2s, v2s, ck2s, cv2s = refs
        sink_ref = None
    j = pl.program_id(1)

    @pl.when(j == 0)
    def _():
        ka, kb = _dup_heads(k_ref[...].astype(F32))
        va, vb = _dup_heads(v_ref[...].astype(F32))
        k2s[0] = ka.astype(BF16)
        k2s[1] = kb.astype(BF16)
        v2s[0] = va.astype(BF16)
        v2s[1] = vb.astype(BF16)
        dup = dup_ref[...]
        for kv in range(N_KV):
            ck2s[kv] = _dot(ck_ref[0, 0, kv].astype(BF16), dup).astype(BF16)
            cv2s[kv] = _dot(cv_ref[0, 0, kv].astype(BF16), dup).astype(BF16)

    lane = lax.broadcasted_iota(jnp.int32, (TQ, LANES), 1)
    low = lane < HD
    zero = jnp.zeros((TQ, LANES), BF16)
    if windowed:
        n_loc = 2 * TQ
        start = pl.multiple_of(jnp.clip(j * TQ - WINDOW, 0, L_LAT - n_loc), WINDOW)
        qpos = j * TQ + lax.broadcasted_iota(jnp.int32, (TQ, n_loc), 0)
        kpos = start + lax.broadcasted_iota(jnp.int32, (TQ, n_loc), 1)
        valid = jnp.abs(qpos - kpos) <= WINDOW
    for pair in range(N_Q // 2):
        kv = pair // (GROUP // 2)
        qp = q_ref[:, pair * LANES:(pair + 1) * LANES]
        if windowed:
            k_own = k2s[kv, pl.ds(start, n_loc), :]
            v_own = v2s[kv, pl.ds(start, n_loc), :]
        else:
            k_own = k2s[kv]
            v_own = v2s[kv]
        outs = []
        for half in range(2):
            qm = jnp.where(low if half == 0 else jnp.logical_not(low), qp, zero)
            s_ctx = _dot_nt(qm, ck2s[kv])
            s_own = _dot_nt(qm, k_own)
            if windowed:
                s_own = jnp.where(valid, s_own, NEG_INF)
            sink = sink_ref[2 * pair + half] if windowed else None
            outs.append(_softmax_pv([s_ctx, s_own], [cv2s[kv], v_own], sink))
        o_ref[:, pair * LANES:(pair + 1) * LANES] = jnp.where(low, outs[0], outs[1]).astype(BF16)


def _lat_attn(layer, q, k, v, cache_k, cache_v, dup, sink):
    windowed = sink is not None
    n_qt = L_LAT // TQ
    ctx_tiles = N_CTX_TOK // TQ
    ctx_seqs = N_CTX_TOK // L_LAT
    cache_spec = pl.BlockSpec((1, 1, N_KV, PAST, HD), lambda b, j: (b, layer, 0, 0, 0))
    in_specs = [
        pl.BlockSpec((TQ, Q_W), lambda b, j: (ctx_tiles + b * n_qt + j, 0)),
        pl.BlockSpec((L_LAT, KV_W), lambda b, j: (ctx_seqs + b, 0)),
        pl.BlockSpec((L_LAT, KV_W), lambda b, j: (ctx_seqs + b, 0)),
        cache_spec,
        cache_spec,
        pl.BlockSpec((HD, LANES), lambda b, j: (0, 0)),
    ]
    args = [q, k, v, cache_k, cache_v, dup]
    if windowed:
        in_specs = [pl.BlockSpec(memory_space=pltpu.SMEM)] + in_specs
        args = [sink] + args
    return pl.pallas_call(
        functools.partial(_lat_attn_kernel, windowed),
        grid=(N_SEQ_LAT, n_qt),
        in_specs=in_specs,
        out_specs=pl.BlockSpec((TQ, Q_W), lambda b, j: (b * n_qt + j, 0)),
        out_shape=jax.ShapeDtypeStruct((N_LAT_TOK, Q_W), BF16),
        scratch_shapes=[
            pltpu.VMEM((N_KV, L_LAT, LANES), BF16),
            pltpu.VMEM((N_KV, L_LAT, LANES), BF16),
            pltpu.VMEM((N_KV, PAST, LANES), BF16),
            pltpu.VMEM((N_KV, PAST, LANES), BF16),
        ],
        compiler_params=_cparams(("arbitrary", "arbitrary")),
        name="lat_attn_window" if windowed else "lat_attn",
    )(*args)


def _seq_pos(i, width):
    r = lax.broadcasted_iota(jnp.int32, (TM, width), 0)
    is_ctx = i < N_CTX_TOK // TM
    seq_len = jnp.where(is_ctx, L_CTX, L_LAT)
    return r & (seq_len - 1), seq_len


def _shift_rows(t, j, pos, seq_len):
    if j == 0:
        return t
    moved = pltpu.roll(t, (-j) % TM, axis=0)
    ok = (pos + j >= 0) & (pos + j < seq_len)
    return jnp.where(ok, moved, 0.0)


def _conv_mixer(zc_ref, cw_ref, pos, seq_len):
    bg = zc_ref[:, 0:CONV_W].astype(F32)
    u = zc_ref[:, CONV_W:2 * CONV_W].astype(F32) * zc_ref[:, 2 * CONV_W:3 * CONV_W].astype(F32)
    y = (_shift_rows(u, -1, pos, seq_len) * cw_ref[0:1, :] + u * cw_ref[1:2, :]
         + _shift_rows(u, 1, pos, seq_len) * cw_ref[2:3, :])
    return (bg * y).astype(BF16)


def _pool_mixer(xd_ref, pw_ref, ps_ref, pos, seq_len):
    outs = []
    for gi, w in enumerate(POOL_WINDOWS):
        xg = xd_ref[:, gi * POOL_G:(gi + 1) * POOL_G]
        tot = None
        for j in range(-(w // 2), w - w // 2):
            sh = _shift_rows(xg, j, pos, seq_len)
            tot = sh if tot is None else tot + sh
        lo = jnp.clip(pos - w // 2, 0, seq_len)
        hi = jnp.clip(pos - w // 2 + w, 0, seq_len)
        cnt = (hi - lo).astype(F32)
        d = tot / cnt - xg
        outs.append(_dot(d.astype(BF16), pw_ref[gi].astype(BF16)))
    return (jnp.concatenate(outs, axis=1) * ps_ref[...]).astype(BF16)


def _mix_out_kernel(even, nx, *refs):
    x_refs = refs[:nx]
    mod_ref, oc_ref, ol_ref, w_ref = refs[nx:nx + 4]
    rest = refs[nx + 4:]
    if even:
        zc_ref, cw_ref, out_ref, wbf = rest
    else:
        xd_ref, pw_ref, ps_ref, out_ref, wbf = rest
    i = pl.program_id(0)

    @pl.when(i == 0)
    def _():
        _cast_rows(w_ref.at[0], wbf, D_MODEL)

    pos, seq_len = _seq_pos(i, CONV_W if even else POOL_G)
    o = jnp.where(i < N_CTX_TOK // TM, oc_ref[...], ol_ref[...])
    if even:
        ya = _conv_mixer(zc_ref, cw_ref, pos, seq_len)
        y = _dot(ya, wbf[0:CONV_W, :]) + _dot(o, wbf[CONV_W:, :])
    else:
        yd = _pool_mixer(xd_ref, pw_ref, ps_ref, pos, seq_len)
        y = _dot(o, wbf[0:Q_W, :]) + _dot(yd, wbf[Q_W:, :])
    out_ref[...] = _read_x(i, x_refs, TM) + mod_ref[0, 2:3, :] * y


def _mix_out(even, layer_i, x, mod_l, o_ctx, o_lat, w_out, *extra):
    n_ctx = N_CTX_TOK // TM
    n_lat = N_LAT_TOK // TM
    row = lambda i: (i, 0)
    const = lambda i: (0, 0)
    in_specs = _x_specs(TM, len(x) == 2) + [
        pl.BlockSpec((1, 6, D_MODEL), lambda i: (_mod_row(i, TM), 0, 0)),
        pl.BlockSpec((TM, Q_W), lambda i: (jnp.minimum(i, n_ctx - 1), 0)),
        pl.BlockSpec((TM, Q_W), lambda i: (jnp.clip(i - n_ctx, 0, n_lat - 1), 0)),
        pl.BlockSpec((1, D_MODEL, D_MODEL), lambda i: (layer_i, 0, 0)),
    ]
    if even:
        in_specs += [pl.BlockSpec((TM, 3 * CONV_W), row), pl.BlockSpec((3, CONV_W), const)]
    else:
        in_specs += [pl.BlockSpec((TM, POOL_W), row),
                     pl.BlockSpec((len(POOL_WINDOWS), POOL_G, POOL_G), lambda i: (0, 0, 0)),
                     pl.BlockSpec((1, POOL_W), const)]
    return pl.pallas_call(
        functools.partial(_mix_out_kernel, even, len(x)),
        grid=(N_TOK // TM,),
        in_specs=in_specs,
        out_specs=pl.BlockSpec((TM, D_MODEL), row),
        out_shape=jax.ShapeDtypeStruct((N_TOK, D_MODEL), F32),
        scratch_shapes=[pltpu.VMEM((D_MODEL, D_MODEL), BF16)],
        compiler_params=_cparams(("arbitrary",)),
        name="mix_out_even" if even else "mix_out_odd",
    )(*x, mod_l, o_ctx, o_lat, w_out, *extra)


def _ffn_kernel(layer_i, x_ref, mod_ref, g_ref, w1_hbm, w3_hbm, w2_hbm, out_ref,
                w1s, w3s, w2s, st1, st3, st2, hs, hid, sem):
    i = pl.program_id(0)
    nf = D_FF // TF_FFN

    def chunk_copies(f, slot):
        cols = pl.ds(f * TF_FFN, TF_FFN)
        return (pltpu.make_async_copy(w1_hbm.at[layer_i, :, cols], st1.at[slot], sem.at[0, slot]),
                pltpu.make_async_copy(w3_hbm.at[layer_i, :, cols], st3.at[slot], sem.at[1, slot]),
                pltpu.make_async_copy(w2_hbm.at[layer_i, cols, :], st2.at[slot], sem.at[2, slot]))

    @pl.when(i == 0)
    def _():
        for c in chunk_copies(0, 0):
            c.start()

    hs[...] = _normmod(x_ref[...], g_ref[...], mod_ref[0, 4:5, :], mod_ref[0, 3:4, :]).astype(BF16)
    for f in range(nf):
        slot = f % 2
        lo, hi = f * TF_FFN, (f + 1) * TF_FFN

        @pl.when(i == 0)
        def _(f=f, slot=slot, lo=lo, hi=hi):
            if f + 1 < nf:
                for c in chunk_copies(f + 1, 1 - slot):
                    c.start()
            for c in chunk_copies(f, slot):
                c.wait()
            w1s[:, lo:hi] = st1[slot].astype(BF16)
            w3s[:, lo:hi] = st3[slot].astype(BF16)
            w2s[lo:hi, :] = st2[slot].astype(BF16)

        h = hs[...]
        hid[:, lo:hi] = (_silu(_dot(h, w1s[:, lo:hi])) * _dot(h, w3s[:, lo:hi])).astype(BF16)
    out_ref[...] = x_ref[...] + mod_ref[0, 5:6, :] * _dot(hid[...], w2s[...])


def _ffn(layer_i, x, mod_l, g, w1, w3, w2):
    row = lambda i: (i, 0)
    any_spec = pl.BlockSpec(memory_space=pl.ANY)
    return pl.pallas_call(
        functools.partial(_ffn_kernel, layer_i),
        grid=(N_TOK // TM,),
        in_specs=[
            pl.BlockSpec((TM, D_MODEL), row),
            pl.BlockSpec((1, 6, D_MODEL), lambda i: (_mod_row(i, TM), 0, 0)),
            pl.BlockSpec((1, D_MODEL), lambda i: (0, 0)),
            any_spec, any_spec, any_spec,
        ],
        out_specs=pl.BlockSpec((TM, D_MODEL), row),
        out_shape=jax.ShapeDtypeStruct((N_TOK, D_MODEL), F32),
        scratch_shapes=[
            pltpu.VMEM((D_MODEL, D_FF), BF16),
            pltpu.VMEM((D_MODEL, D_FF), BF16),
            pltpu.VMEM((D_FF, D_MODEL), BF16),
            pltpu.VMEM((2, D_MODEL, TF_FFN), F32),
            pltpu.VMEM((2, D_MODEL, TF_FFN), F32),
            pltpu.VMEM((2, TF_FFN, D_MODEL), F32),
            pltpu.VMEM((TM, D_MODEL), BF16),
            pltpu.VMEM((TM, D_FF), BF16),
            pltpu.SemaphoreType.DMA((3, 2)),
        ],
        compiler_params=pltpu.CompilerParams(dimension_semantics=("arbitrary",), vmem_limit_bytes=FFN_VMEM_LIMIT),
        name="ffn",
    )(x, mod_l, g, w1, w3, w2)


def _split_bf16(t):
    hi = t.astype(BF16)
    return hi, (t - hi.astype(F32)).astype(BF16)


def _lane_values(col_vals, ones_rows):
    q = jnp.floor(col_vals * (1.0 / 32.0))
    r = col_vals - 32.0 * q
    t = 32.0 * _dot_nt(ones_rows, q.astype(BF16)) + _dot_nt(ones_rows, r.astype(BF16))
    return t[0:1, :]


def _route_kernel(x_ref, mod_ref, g_ref, rw_ref, rb_ref, tri_ref, upper_ref, xl_ref, meta_ref, cnt_ref):
    h = _normmod(x_ref[...], g_ref[...], mod_ref[0, 4:5, :], mod_ref[0, 3:4, :])
    h_hi, h_lo = _split_bf16(h)
    w_hi, w_lo = _split_bf16(rw_ref[...])
    logits = _dot(h_hi, w_hi) + _dot(h_hi, w_lo) + _dot(h_lo, w_hi) + rb_ref[...]
    lane = lax.broadcasted_iota(jnp.int32, logits.shape, 1).astype(F32)
    logits = jnp.where(lane < N_EXP, logits, NEG_INF)
    m1 = logits.max(axis=-1, keepdims=True)
    i1 = jnp.where(logits == m1, lane, float(LANES)).min(axis=-1, keepdims=True)
    rest = jnp.where(lane == i1, NEG_INF, logits)
    m2 = rest.max(axis=-1, keepdims=True)
    i2 = jnp.where(rest == m2, lane, float(LANES)).min(axis=-1, keepdims=True)
    e2 = jnp.exp(m2 - m1)
    den = 1.0 + e2
    g1 = 1.0 / den
    g2 = e2 / den

    oh_a = jnp.where(lane == i1, 1.0, 0.0)
    oh_b = jnp.where(lane == i2, 1.0, 0.0)
    tri = tri_ref[...]
    cnt_a = oh_a.sum(axis=0, keepdims=True)
    cnt_b = oh_b.sum(axis=0, keepdims=True)
    run16 = jnp.floor((cnt_a + cnt_b + (RUN_ALIGN - 1)) * (1.0 / RUN_ALIGN))
    run16_rows = jnp.broadcast_to(run16, (8, LANES))
    start = RUN_ALIGN * _dot(run16_rows.astype(BF16), upper_ref[...])[0:1, :]
    row_a = oh_a * (start + _dot(tri, oh_a.astype(BF16)))
    row_b = oh_b * (start + cnt_a + _dot(tri, oh_b.astype(BF16)))

    ones_rows = jnp.ones((8, LANES), BF16)
    tok_a = _lane_values(row_a, ones_rows)
    tok_b = _lane_values(row_b, ones_rows)
    sorted_row = lax.broadcasted_iota(jnp.int32, (LOCAL_ROWS, TR), 0).astype(F32)
    perm = jnp.where((sorted_row == tok_a) | (sorted_row == tok_b), 1.0, 0.0).astype(BF16)
    xl_ref[...] = _dot(perm, h_hi).astype(BF16)

    meta = jnp.zeros_like(logits)
    cols = (row_a.sum(axis=-1, keepdims=True), row_b.sum(axis=-1, keepdims=True), g1, g2)
    for k, col in enumerate(cols):
        meta = jnp.where(lane == k, col, meta)
    meta_ref[...] = meta
    cnt_ref[...] = RUN_ALIGN * run16_rows


def _route(x, mod_l, g, router_w, router_b, tri, upper):
    row = lambda i: (i, 0)
    const = lambda i: (0, 0)
    n_tiles = N_TOK // TR
    return pl.pallas_call(
        _route_kernel,
        grid=(n_tiles,),
        in_specs=[
            pl.BlockSpec((TR, D_MODEL), row),
            pl.BlockSpec((1, 6, D_MODEL), lambda i: (_mod_row(i, TR), 0, 0)),
            pl.BlockSpec((1, D_MODEL), const),
            pl.BlockSpec((D_MODEL, LANES), const),
            pl.BlockSpec((1, LANES), const),
            pl.BlockSpec((TR, TR), const),
            pl.BlockSpec((LANES, LANES), const),
        ],
        out_specs=[pl.BlockSpec((LOCAL_ROWS, D_MODEL), row), pl.BlockSpec((TR, LANES), row),
                   pl.BlockSpec((8, LANES), row)],
        out_shape=[jax.ShapeDtypeStruct((n_tiles * LOCAL_ROWS, D_MODEL), BF16),
                   jax.ShapeDtypeStruct((N_TOK, LANES), F32),
                   jax.ShapeDtypeStruct((n_tiles * 8, LANES), F32)],
        compiler_params=_cparams(("arbitrary",)),
        name="moe_route",
    )(x, mod_l, g, router_w, router_b, tri, upper)


def _moe_plan(counts):
    n_tiles = N_TOK // TR
    run = counts.reshape(n_tiles, 8, LANES)[:, 0, :N_EXP].astype(jnp.int32)
    per_expert = jnp.sum(run, axis=0)
    region = (per_expert + (TMS - 1)) // TMS * TMS
    ends = jnp.cumsum(region)
    offs = ends - region
    seg_end = jnp.cumsum(run, axis=0)
    seg_start = seg_end - run
    local_end = jnp.cumsum(run, axis=1)
    local_start = local_end - run
    n_chunks = local_end[:, N_EXP - 1] // RUN_ALIGN
    tile_start = jnp.arange(MOE_TILES, dtype=jnp.int32) * TMS
    tile_expert = jnp.sum((tile_start[:, None] >= ends[None, :]).astype(jnp.int32), axis=1)
    tile_expert = jnp.minimum(tile_expert, N_EXP - 1)
    n_used = (ends[N_EXP - 1] // TMS).reshape(1)
    experts = jnp.arange(N_EXP, dtype=jnp.int32)
    g_row = jnp.arange(MOE_ROWS // RUN_ALIGN, dtype=jnp.int32) * RUN_ALIGN
    g_exp = jnp.repeat(tile_expert, TMS // RUN_ALIGN)
    pick = g_exp[:, None] == experts[None, :]
    rel = g_row - jnp.sum(jnp.where(pick, offs[None, :], 0), axis=1)
    ends_of = jnp.sum(jnp.where(pick[:, None, :], seg_end[None, :, :], 0), axis=2)
    src_tile = jnp.sum((rel[:, None] >= ends_of).astype(jnp.int32), axis=1)
    valid = (src_tile < n_tiles) & (jnp.repeat(jnp.arange(MOE_TILES), TMS // RUN_ALIGN) < n_used[0])
    src_tile = jnp.minimum(src_tile, n_tiles - 1)
    sel = (src_tile[:, None, None] == jnp.arange(n_tiles)[None, :, None]) & pick[:, None, :]
    shift = jnp.sum(jnp.where(sel, (local_start - seg_start)[None, :, :], 0), axis=(1, 2))
    chunk_src = src_tile * LOCAL_ROWS + rel + shift
    n_valid = jnp.sum(valid.reshape(MOE_TILES, TMS // RUN_ALIGN).astype(jnp.int32), axis=1)
    return chunk_src, n_valid, n_chunks, tile_expert, n_used


def _chunk_copy(src_ref, src_row, dst_ref, dst_row, sem):
    return pltpu.make_async_copy(src_ref.at[pl.ds(src_row, RUN_ALIGN), :],
                                 dst_ref.at[pl.ds(dst_row, RUN_ALIGN), :], sem)


CHUNKS = TMS // RUN_ALIGN
LOCAL_CHUNKS = LOCAL_ROWS // RUN_ALIGN


def _expert_kernel(te_ref, nu_ref, src_ref, nv_ref, nc_ref, xl_hbm, w1_ref, w3_ref, w2_ref, yl_hbm,
                   xbuf, ybuf, zbuf, w1s, w3s, w2s, sem_in, sem_out, sem_zero):
    t = pl.program_id(0)
    n_used = nu_ref[0]
    slot = t & 1

    def gather(tile, s):
        n = nv_ref[tile]

        def issue(j, c):
            src = pl.multiple_of(src_ref[tile * CHUNKS + j], RUN_ALIGN)
            _chunk_copy(xl_hbm, src, xbuf.at[s], pl.multiple_of(j * RUN_ALIGN, RUN_ALIGN), sem_in.at[s]).start()
            return c

        def pad(j, c):
            xbuf[s, pl.ds(pl.multiple_of(j * RUN_ALIGN, RUN_ALIGN), RUN_ALIGN), :] = jnp.zeros(
                (RUN_ALIGN, D_MODEL), BF16)
            return c

        lax.fori_loop(0, n, issue, 0)
        lax.fori_loop(n, CHUNKS, pad, 0)

    def scatter(tile, s):
        def issue(j, c):
            dst = pl.multiple_of(src_ref[tile * CHUNKS + j], RUN_ALIGN)
            _chunk_copy(ybuf.at[s], pl.multiple_of(j * RUN_ALIGN, RUN_ALIGN), yl_hbm, dst, sem_out.at[s]).start()
            return c

        lax.fori_loop(0, nv_ref[tile], issue, 0)

    def drain(count, sem):
        def one(j, c):
            _chunk_copy(xl_hbm, 0, xbuf.at[0], 0, sem).wait()
            return c

        lax.fori_loop(0, count, one, 0)

    @pl.when(t == 0)
    def _():
        zbuf[...] = jnp.zeros_like(zbuf)
        total = jnp.int32(0)
        for tile in range(N_TOK // TR):
            n = nc_ref[tile]

            def clear(j, c, tile=tile):
                row = pl.multiple_of(tile * LOCAL_ROWS + j * RUN_ALIGN, RUN_ALIGN)
                _chunk_copy(zbuf, 0, yl_hbm, row, sem_zero).start()
                return c

            lax.fori_loop(n, LOCAL_CHUNKS, clear, 0)
            total = total + (LOCAL_CHUNKS - n)
        drain(total, sem_zero)
        gather(0, 0)

    @pl.when(t < n_used)
    def _():
        drain(nv_ref[t], sem_in.at[slot])

    @pl.when(t + 1 < n_used)
    def _():
        gather(t + 1, 1 - slot)

    new_expert = (t == 0) | (te_ref[t] != te_ref[jnp.maximum(t - 1, 0)])

    @pl.when(new_expert)
    def _():
        _cast_rows(w1_ref.at[0, 0], w1s, D_MODEL)
        _cast_rows(w3_ref.at[0, 0], w3s, D_MODEL)
        _cast_rows(w2_ref.at[0, 0], w2s, D_FF_E)

    @pl.when((t >= 2) & (t - 2 < n_used))
    def _():
        drain(nv_ref[jnp.maximum(t - 2, 0)], sem_out.at[slot])

    @pl.when(t < n_used)
    def _():
        x = xbuf[slot]
        hid = (_silu(_dot(x, w1s[...])) * _dot(x, w3s[...])).astype(BF16)
        ybuf[slot] = _dot(hid, w2s[...]).astype(BF16)
        scatter(t, slot)

    @pl.when(t == pl.num_programs(0) - 1)
    def _():
        @pl.when((t >= 1) & (t - 1 < n_used))
        def _():
            drain(nv_ref[jnp.maximum(t - 1, 0)], sem_out.at[1 - slot])

        @pl.when(t < n_used)
        def _():
            drain(nv_ref[t], sem_out.at[slot])


def _experts(layer_i, plan, xl, w1, w3, w2):
    chunk_src, n_valid, n_chunks, tile_expert, n_used = plan
    wsel = lambda t, te, nu, src, nv, nc: (layer_i, te[t], 0, 0)
    any_spec = pl.BlockSpec(memory_space=pl.ANY)
    return pl.pallas_call(
        _expert_kernel,
        grid_spec=pltpu.PrefetchScalarGridSpec(
            num_scalar_prefetch=5,
            grid=(MOE_TILES,),
            in_specs=[
                any_spec,
                pl.BlockSpec((1, 1, D_MODEL, D_FF_E), wsel),
                pl.BlockSpec((1, 1, D_MODEL, D_FF_E), wsel),
                pl.BlockSpec((1, 1, D_FF_E, D_MODEL), wsel),
            ],
            out_specs=any_spec,
            scratch_shapes=[
                pltpu.VMEM((2, TMS, D_MODEL), BF16),
                pltpu.VMEM((2, TMS, D_MODEL), BF16),
                pltpu.VMEM((RUN_ALIGN, D_MODEL), BF16),
                pltpu.VMEM((D_MODEL, D_FF_E), BF16),
                pltpu.VMEM((D_MODEL, D_FF_E), BF16),
                pltpu.VMEM((D_FF_E, D_MODEL), BF16),
                pltpu.SemaphoreType.DMA((2,)),
                pltpu.SemaphoreType.DMA((2,)),
                pltpu.SemaphoreType.DMA(()),
            ],
        ),
        out_shape=jax.ShapeDtypeStruct(xl.shape, BF16),
        compiler_params=_cparams(("arbitrary",)),
        name="moe_experts",
    )(tile_expert, n_used, chunk_src, n_valid, n_chunks, xl, w1, w3, w2)


def _combine_kernel(x_ref, mod_ref, meta_ref, yl_ref, *out_refs):
    sorted_row = lax.broadcasted_iota(jnp.int32, (TR, LOCAL_ROWS), 1).astype(F32)
    y = yl_ref[...]
    pick_a = jnp.where(sorted_row == meta_ref[:, 0:1], 1.0, 0.0).astype(BF16)
    pick_b = jnp.where(sorted_row == meta_ref[:, 1:2], 1.0, 0.0).astype(BF16)
    f = meta_ref[:, 2:3] * _dot(pick_a, y) + meta_ref[:, 3:4] * _dot(pick_b, y)
    _write_x(pl.program_id(0), out_refs, TR, x_ref[...] + mod_ref[0, 5:6, :] * f)


def _combine(x, mod_l, meta, yl, split_out):
    row = lambda i: (i, 0)
    return pl.pallas_call(
        _combine_kernel,
        grid=(N_TOK // TR,),
        in_specs=[
            pl.BlockSpec((TR, D_MODEL), row),
            pl.BlockSpec((1, 6, D_MODEL), lambda i: (_mod_row(i, TR), 0, 0)),
            pl.BlockSpec((TR, LANES), row),
            pl.BlockSpec((LOCAL_ROWS, D_MODEL), row),
        ],
        out_specs=_x_specs(TR, split_out),
        out_shape=_x_shapes(split_out),
        compiler_params=_cparams(("arbitrary",)),
        name="moe_combine",
    )(x, mod_l, meta, yl)


def _moe(layer_i, x, mod_l, g, router_w, router_b, tri, upper, w1, w3, w2, split_out):
    xl, meta, counts = _route(x, mod_l, g, router_w, router_b, tri, upper)
    yl = _experts(layer_i, _moe_plan(counts), xl, w1, w3, w2)
    return _combine(x, mod_l, meta, yl, split_out)


def _rope_tables():
    n_rows = L_LAT // GRID_W
    rows = jnp.repeat(jnp.arange(n_rows, dtype=F32), GRID_W)
    cols = jnp.tile(jnp.arange(GRID_W, dtype=F32), n_rows)
    quarter = HD // 4
    inv = ROPE_THETA ** (-jnp.arange(quarter, dtype=F32) / quarter)
    ang_r = rows[:, None] * inv
    ang_c = cols[:, None] * inv
    cos = jnp.concatenate([jnp.cos(ang_r)] * 2 + [jnp.cos(ang_c)] * 2, axis=1)
    sin = jnp.concatenate([-jnp.sin(ang_r), jnp.sin(ang_r), -jnp.sin(ang_c), jnp.sin(ang_c)], axis=1)
    cos = jnp.concatenate([cos, cos], axis=1)
    sin = jnp.concatenate([sin, sin], axis=1)
    cos = jnp.concatenate([jnp.ones((L_LAT, LANES), F32), cos], axis=0)
    sin = jnp.concatenate([jnp.zeros((L_LAT, LANES), F32), sin], axis=0)
    return cos, sin


def _block_ones(width):
    r = jnp.arange(width) // HD
    return (r[:, None] == r[None, :]).astype(BF16)


def kernel(x_prompt, x_sample, cache_k, cache_v, c, c_ctx, norm1, norm2, w_mod, b_mod, ev_w_in, ev_conv, ev_q_norm, ev_k_norm, ev_w_out, od_w_in, od_q_norm, od_k_norm, od_sink, od_pool_w, od_pool_scale, od_w_out, ffn_w1, ffn_w3, ffn_w2, moe_router, moe_router_b, moe_w1, moe_w3, moe_w2):
    x = (x_prompt.reshape(N_CTX_TOK, D_MODEL), x_sample.reshape(N_LAT_TOK, D_MODEL))
    cond = jnp.concatenate([c_ctx[None, :], c, jnp.zeros((MOD_ROWS - 1 - N_SEQ_LAT, D_MODEL), F32)], axis=0)
    mod = _modulation(cond, w_mod, b_mod).reshape(DEPTH, MOD_ROWS, 6, D_MODEL)

    cos_tab, sin_tab = _rope_tables()
    ones_q = _block_ones(Q_W)
    ones_k = _block_ones(KV_W)
    eye = jnp.eye(HD, dtype=BF16)
    dup = jnp.concatenate([eye, eye], axis=1)
    tri = (jnp.arange(TR)[:, None] > jnp.arange(TR)[None, :]).astype(BF16)
    upper = (jnp.arange(LANES)[:, None] < jnp.arange(LANES)[None, :]).astype(BF16)

    new_kv = None
    for l in range(DEPTH):
        i = l // 2
        even = l % 2 == 0
        mod_l = mod[l]
        g1 = norm1[l][None, :]
        g2 = norm2[l][None, :]
        if even:
            q_gain, k_gain = ev_q_norm[i], ev_k_norm[i]
            w_in, w_out = ev_w_in, ev_w_out
        else:
            q_gain, k_gain = od_q_norm[i], od_k_norm[i]
            w_in, w_out = od_w_in, od_w_out
        q_gain = jnp.tile(q_gain, N_Q)[None, :]
        k_gain = jnp.tile(k_gain, N_KV)[None, :]
        outs = _in_proj(even, i, x, mod_l, g1, w_in, ones_q, ones_k, q_gain, k_gain, cos_tab, sin_tab)
        if even:
            zc, q, k, v = outs
            sink = None
        else:
            q, k, v, xd = outs
            sink = od_sink[i]
        o_ctx, new_kv = _ctx_attn(l, q, k, v, sink, new_kv)
        o_lat = _lat_attn(l, q, k, v, cache_k, cache_v, dup, sink)
        if even:
            x1 = _mix_out(True, i, x, mod_l, o_ctx, o_lat, w_out, zc, ev_conv[i])
            x = (_ffn(i, x1, mod_l, g2, ffn_w1, ffn_w3, ffn_w2),)
        else:
            x1 = _mix_out(False, i, x, mod_l, o_ctx, o_lat, w_out, xd, od_pool_w[i], od_pool_scale[i][None, :])
            rw = jnp.pad(moe_router[i], ((0, 0), (0, LANES - N_EXP)))
            rb = jnp.pad(moe_router_b[i], (0, LANES - N_EXP))[None, :]
            x = tuple(_moe(i, x1, mod_l, g2, rw, rb, tri, upper, moe_w1, moe_w3, moe_w2, l == DEPTH - 1))

    y_prompt = x[0].reshape(N_SEQ_CTX, L_CTX, D_MODEL)
    y_sample = x[1].reshape(N_SEQ_LAT, L_LAT, D_MODEL)
    return (y_prompt, y_sample, new_kv[0], new_kv[1])
```

```python
import functools

import jax
import jax.numpy as jnp
from jax import lax
from jax.experimental import pallas as pl
from jax.experimental.pallas import tpu as pltpu

F32 = jnp.float32
BF16 = jnp.bfloat16

D_MODEL = 1024
N_SEQ_CTX = 32
L_CTX = 256
N_SEQ_LAT = 4
L_LAT = 1024
DEPTH = 4
PAST = 512
GRID_W = 64
HD = 64
N_Q = 8
N_KV = 2
GROUP = N_Q // N_KV
Q_W = N_Q * HD
KV_W = N_KV * HD
CONV_W = 512
POOL_W = 512
POOL_WINDOWS = (2, 4, 8, 16)
POOL_G = 128
EVEN_IN = 3 * CONV_W + Q_W + 2 * KV_W
ODD_IN = Q_W + 2 * KV_W + POOL_W
WINDOW = 128
D_FF = 2816
N_EXP = 8
TOP_K = 2
D_FF_E = 1024
ROPE_THETA = 10000.0
EPS = 1e-6

N_CTX_TOK = N_SEQ_CTX * L_CTX
N_LAT_TOK = N_SEQ_LAT * L_LAT
N_TOK = N_CTX_TOK + N_LAT_TOK
MOD_ROWS = 16

LANES = 128
VMEM_LIMIT = 56 * 1024 * 1024
FFN_VMEM_LIMIT = 62 * 1024 * 1024

TM_IN = 512
TM = 1024
TQ = 256
CTX_SEQS = 2
TF_FFN = 256
TR = 512
RUN_ALIGN = 16
LOCAL_ROWS = TOP_K * TR + N_EXP * RUN_ALIGN
TMS = 512
MOE_TILES = -(-(TOP_K * N_TOK + (N_TOK // TR) * N_EXP * (RUN_ALIGN - 1) + N_EXP * (TMS - 1)) // TMS)
MOE_ROWS = MOE_TILES * TMS
NEG_INF = float("-inf")


def _cparams(sem):
    return pltpu.CompilerParams(dimension_semantics=sem, vmem_limit_bytes=VMEM_LIMIT)


def _mod_row(i, tm):
    n_ctx = N_CTX_TOK // tm
    return jnp.where(i < n_ctx, 0, 1 + (i - n_ctx) // (L_LAT // tm))


def _x_specs(tm, split):
    if not split:
        return [pl.BlockSpec((tm, D_MODEL), lambda i, *_: (i, 0))]
    n_ctx = N_CTX_TOK // tm
    n_lat = N_LAT_TOK // tm
    return [pl.BlockSpec((tm, D_MODEL), lambda i, *_: (jnp.minimum(i, n_ctx - 1), 0)),
            pl.BlockSpec((tm, D_MODEL), lambda i, *_: (jnp.clip(i - n_ctx, 0, n_lat - 1), 0))]


def _x_shapes(split):
    if not split:
        return [jax.ShapeDtypeStruct((N_TOK, D_MODEL), F32)]
    return [jax.ShapeDtypeStruct((N_CTX_TOK, D_MODEL), F32), jax.ShapeDtypeStruct((N_LAT_TOK, D_MODEL), F32)]


def _read_x(i, x_refs, tm):
    if len(x_refs) == 1:
        return x_refs[0][...]
    return jnp.where(i < N_CTX_TOK // tm, x_refs[0][...], x_refs[1][...])


def _write_x(i, o_refs, tm, val):
    if len(o_refs) == 1:
        o_refs[0][...] = val
        return

    @pl.when(i < N_CTX_TOK // tm)
    def _():
        o_refs[0][...] = val

    @pl.when(i >= N_CTX_TOK // tm)
    def _():
        o_refs[1][...] = val


def _normmod(x, g, scale, shift):
    ms = jnp.mean(x * x, axis=-1, keepdims=True)
    y = x * lax.rsqrt(ms + EPS) * g
    return y * (1.0 + scale) + shift


def _silu(x):
    return x * jax.nn.sigmoid(x)


def _dot(a, b):
    return jnp.dot(a, b, preferred_element_type=F32)


def _dot_nt(a, b):
    return lax.dot_general(a, b, (((1,), (1,)), ((), ())), preferred_element_type=F32)


def _cast_rows(src_ref, dst_ref, rows, chunk=256):
    for r in range(0, rows, chunk):
        dst_ref[r:r + chunk, :] = src_ref[r:r + chunk, :].astype(dst_ref.dtype)


def _mod_kernel(c_ref, w_ref, b_ref, o_ref):
    s = _silu(c_ref[...]).astype(BF16)
    o_ref[0] = _dot(s, w_ref[0].astype(BF16)) + b_ref[0]


def _modulation(cond, w_mod, b_mod):
    tn = 1536
    return pl.pallas_call(
        _mod_kernel,
        grid=(DEPTH, 6 * D_MODEL // tn),
        in_specs=[
            pl.BlockSpec((MOD_ROWS, D_MODEL), lambda l, j: (0, 0)),
            pl.BlockSpec((1, D_MODEL, tn), lambda l, j: (l, 0, j)),
            pl.BlockSpec((1, 1, tn), lambda l, j: (l, 0, j)),
        ],
        out_specs=pl.BlockSpec((1, MOD_ROWS, tn), lambda l, j: (l, 0, j)),
        out_shape=jax.ShapeDtypeStruct((DEPTH, MOD_ROWS, 6 * D_MODEL), F32),
        compiler_params=_cparams(("arbitrary", "arbitrary")),
        name="modulation",
    )(cond, w_mod, b_mod.reshape(DEPTH, 1, 6 * D_MODEL))


def _head_rms(t, ones_bd, gain):
    ssq = _dot((t * t).astype(BF16), ones_bd)
    return t * lax.rsqrt(ssq * (1.0 / HD) + EPS) * gain


def _rope(t, cos, sin_signed):
    lane = lax.broadcasted_iota(jnp.int32, (t.shape[0], LANES), 1)
    first = (lane & 31) < 16
    outs = []
    for c in range(t.shape[1] // LANES):
        tc = t[:, c * LANES:(c + 1) * LANES]
        nxt = pltpu.roll(tc, LANES - 16, axis=1)
        prv = pltpu.roll(tc, 16, axis=1)
        outs.append(tc * cos + jnp.where(first, nxt, prv) * sin_signed)
    return outs[0] if len(outs) == 1 else jnp.concatenate(outs, axis=1)


def _in_proj_kernel(even, nx, *refs):
    x_refs = refs[:nx]
    mod_ref, g_ref, w_ref, onesq_ref, onesk_ref, qg_ref, kg_ref, cos_ref, sin_ref = refs[nx:nx + 9]
    rest = refs[nx + 9:]
    if even:
        zc_ref, q_ref, k_ref, v_ref, wbf = rest
        q0 = 3 * CONV_W
    else:
        q_ref, k_ref, v_ref, xd_ref, wbf = rest
        q0 = 0
    k0 = q0 + Q_W
    v0 = k0 + KV_W

    @pl.when(pl.program_id(0) == 0)
    def _():
        _cast_rows(w_ref.at[0], wbf, D_MODEL)

    x = _read_x(pl.program_id(0), x_refs, TM_IN)
    h = _normmod(x, g_ref[...], mod_ref[0, 1:2, :], mod_ref[0, 0:1, :]).astype(BF16)
    cos = cos_ref[...]
    sin = sin_ref[...]

    q = _dot(h, wbf[:, q0:q0 + Q_W])
    q = _rope(_head_rms(q, onesq_ref[...], qg_ref[...]), cos, sin) * (HD ** -0.5)
    q_ref[...] = q.astype(BF16)

    k = _dot(h, wbf[:, k0:k0 + KV_W])
    k = _rope(_head_rms(k, onesk_ref[...], kg_ref[...]), cos, sin)
    k_ref[...] = k.astype(BF16)

    v_ref[...] = _dot(h, wbf[:, v0:v0 + KV_W]).astype(BF16)

    if even:
        zc_ref[...] = _dot(h, wbf[:, 0:3 * CONV_W]).astype(BF16)
    else:
        xd_ref[...] = _dot(h, wbf[:, v0 + KV_W:v0 + KV_W + POOL_W])


def _in_proj(even, layer_i, x, mod_l, g, w, ones_q, ones_k, q_gain, k_gain, cos_tab, sin_tab):
    tm = TM_IN
    n_in = EVEN_IN if even else ODD_IN
    n_ctx = N_CTX_TOK // tm
    per_seq = L_LAT // tm

    def rope_idx(i):
        return (jnp.where(i < n_ctx, 0, per_seq + (i - n_ctx) % per_seq), 0)

    row = lambda i: (i, 0)
    const = lambda i: (0, 0)
    in_specs = _x_specs(tm, len(x) == 2) + [
        pl.BlockSpec((1, 6, D_MODEL), lambda i: (_mod_row(i, tm), 0, 0)),
        pl.BlockSpec((1, D_MODEL), const),
        pl.BlockSpec((1, D_MODEL, n_in), lambda i: (layer_i, 0, 0)),
        pl.BlockSpec((Q_W, Q_W), const),
        pl.BlockSpec((KV_W, KV_W), const),
        pl.BlockSpec((1, Q_W), const),
        pl.BlockSpec((1, KV_W), const),
        pl.BlockSpec((tm, LANES), rope_idx),
        pl.BlockSpec((tm, LANES), rope_idx),
    ]
    qkv_specs = [pl.BlockSpec((tm, Q_W), row), pl.BlockSpec((tm, KV_W), row), pl.BlockSpec((tm, KV_W), row)]
    qkv_shapes = [jax.ShapeDtypeStruct((N_TOK, Q_W), BF16), jax.ShapeDtypeStruct((N_TOK, KV_W), BF16),
                  jax.ShapeDtypeStruct((N_TOK, KV_W), BF16)]
    if even:
        out_specs = [pl.BlockSpec((tm, 3 * CONV_W), row)] + qkv_specs
        out_shape = [jax.ShapeDtypeStruct((N_TOK, 3 * CONV_W), BF16)] + qkv_shapes
    else:
        out_specs = qkv_specs + [pl.BlockSpec((tm, POOL_W), row)]
        out_shape = qkv_shapes + [jax.ShapeDtypeStruct((N_TOK, POOL_W), F32)]
    return pl.pallas_call(
        functools.partial(_in_proj_kernel, even, len(x)),
        grid=(N_TOK // tm,),
        in_specs=in_specs,
        out_specs=out_specs,
        out_shape=out_shape,
        scratch_shapes=[pltpu.VMEM((D_MODEL, n_in), BF16)],
        compiler_params=_cparams(("arbitrary",)),
        name="in_proj_even" if even else "in_proj_odd",
    )(*x, mod_l, g, w, ones_q, ones_k, q_gain, k_gain, cos_tab, sin_tab)


def _dup_heads(t):
    lane = lax.broadcasted_iota(jnp.int32, t.shape, 1)
    swapped = pltpu.roll(t, HD, axis=1)
    low = lane < HD
    return jnp.where(low, t, swapped), jnp.where(low, swapped, t)


def _softmax_pv(scores, values, sink):
    m = scores[0].max(axis=-1, keepdims=True)
    for s in scores[1:]:
        m = jnp.maximum(m, s.max(axis=-1, keepdims=True))
    if sink is not None:
        m = jnp.maximum(m, sink)
    den = None
    acc = None
    for s, v in zip(scores, values):
        e = jnp.exp(s - m)
        d = e.sum(axis=-1, keepdims=True)
        a = _dot(e.astype(BF16), v)
        den = d if den is None else den + d
        acc = a if acc is None else acc + a
    if sink is not None:
        den = den + jnp.exp(sink - m)
    return acc / den


def _group_attention(q_ref, kv, keys, values, sink_ref, mask, stack):
    t = q_ref.shape[0]
    lane = lax.broadcasted_iota(jnp.int32, (t, LANES), 1)
    low = lane < HD
    zero = jnp.zeros((t, LANES), BF16)
    chunks = [q_ref[:, (kv * 2 + c) * LANES:(kv * 2 + c + 1) * LANES] for c in range(2)]
    if not stack:
        outs = []
        for h in range(GROUP):
            qm = jnp.where(low, chunks[h // 2], zero) if h % 2 == 0 else jnp.where(low, zero, chunks[h // 2])
            scores = [_dot_nt(qm, k) for k in keys]
            if mask is not None:
                scores = [s if m is None else jnp.where(m, s, NEG_INF) for s, m in zip(scores, mask)]
            sink = None if sink_ref is None else sink_ref[kv * GROUP + h]
            outs.append(_softmax_pv(scores, values, sink))
        return (jnp.where(low, outs[0], outs[1]).astype(BF16), jnp.where(low, outs[2], outs[3]).astype(BF16))
    qs = jnp.concatenate([jnp.where(low, chunks[0], zero), jnp.where(low, zero, chunks[0]),
                          jnp.where(low, chunks[1], zero), jnp.where(low, zero, chunks[1])], axis=0)
    scores = [_dot_nt(qs, k) for k in keys]
    if mask is not None:
        scores = [s if m is None else jnp.where(m, s, NEG_INF) for s, m in zip(scores, mask)]
    sink = None
    if sink_ref is not None:
        head = lax.broadcasted_iota(jnp.int32, (GROUP * t, 1), 0) // t
        sink = jnp.full((GROUP * t, 1), sink_ref[kv * GROUP], F32)
        for h in range(1, GROUP):
            sink = jnp.where(head == h, sink_ref[kv * GROUP + h], sink)
    out = _softmax_pv(scores, values, sink)
    return (jnp.where(low, out[0:t], out[t:2 * t]).astype(BF16),
            jnp.where(low, out[2 * t:3 * t], out[3 * t:4 * t]).astype(BF16))


def _ctx_attn_kernel(has_sink, first, *refs):
    if has_sink:
        sink_ref, refs = refs[0], refs[1:]
    else:
        sink_ref = None
    q_ref, k_ref, v_ref = refs[:3]
    o_ref, nk_ref, nv_ref = refs[-3:]
    lane = lax.broadcasted_iota(jnp.int32, (L_CTX, LANES), 1)
    low = lane < HD
    for s in range(CTX_SEQS):
        rows = pl.ds(s * L_CTX, L_CTX)
        k = k_ref[rows, :].astype(F32)
        v = v_ref[rows, :].astype(F32)
        k_sw = pltpu.roll(k, HD, axis=1)
        v_sw = pltpu.roll(v, HD, axis=1)
        nk_ref[s, 0, 0] = k[:, 0:HD]
        nk_ref[s, 0, 1] = k_sw[:, 0:HD]
        nv_ref[s, 0, 0] = v[:, 0:HD]
        nv_ref[s, 0, 1] = v_sw[:, 0:HD]
        if first:
            nk_ref[s, 1:] = jnp.zeros((DEPTH - 1, N_KV, L_CTX, HD), F32)
            nv_ref[s, 1:] = jnp.zeros((DEPTH - 1, N_KV, L_CTX, HD), F32)
        k2 = (jnp.where(low, k, k_sw).astype(BF16), jnp.where(low, k_sw, k).astype(BF16))
        v2 = (jnp.where(low, v, v_sw).astype(BF16), jnp.where(low, v_sw, v).astype(BF16))
        for kv in range(N_KV):
            o0, o1 = _group_attention(q_ref.at[rows, :], kv, [k2[kv]], [v2[kv]], sink_ref, None, True)
            o_ref[rows, (2 * kv) * LANES:(2 * kv + 1) * LANES] = o0
            o_ref[rows, (2 * kv + 1) * LANES:(2 * kv + 2) * LANES] = o1


def _ctx_attn(layer, q, k, v, sink, new_kv):
    has_sink = sink is not None
    first = new_kv is None
    row = lambda b: (b, 0)
    rows = CTX_SEQS * L_CTX
    in_specs = [pl.BlockSpec((rows, Q_W), row), pl.BlockSpec((rows, KV_W), row), pl.BlockSpec((rows, KV_W), row)]
    args = [q, k, v]
    if has_sink:
        in_specs = [pl.BlockSpec(memory_space=pltpu.SMEM)] + in_specs
        args = [sink] + args
    aliases = {}
    if first:
        kv_spec = pl.BlockSpec((CTX_SEQS, DEPTH, N_KV, L_CTX, HD), lambda b: (b, 0, 0, 0, 0))
    else:
        kv_spec = pl.BlockSpec((CTX_SEQS, 1, N_KV, L_CTX, HD), lambda b: (b, layer, 0, 0, 0))
        aliases = {len(args): 1, len(args) + 1: 2}
        in_specs = in_specs + [pl.BlockSpec(memory_space=pl.ANY)] * 2
        args = args + list(new_kv)
    kv_shape = jax.ShapeDtypeStruct((N_SEQ_CTX, DEPTH, N_KV, L_CTX, HD), F32)
    o, nk, nv = pl.pallas_call(
        functools.partial(_ctx_attn_kernel, has_sink, first),
        grid=(N_SEQ_CTX // CTX_SEQS,),
        in_specs=in_specs,
        out_specs=[pl.BlockSpec((rows, Q_W), row), kv_spec, kv_spec],
        out_shape=[jax.ShapeDtypeStruct((N_CTX_TOK, Q_W), BF16), kv_shape, kv_shape],
        input_output_aliases=aliases,
        compiler_params=_cparams(("arbitrary",)),
        name="ctx_attn_sink" if has_sink else "ctx_attn",
    )(*args)
    return o, (nk, nv)


def _lat_attn_kernel(windowed, *refs):
    if windowed:
        sink_ref, q_ref, k_ref, v_ref, ck_ref, cv_ref, dup_ref, o_ref, k2s, v2s, ck2s, cv2s = refs
    else:
        q_ref, k_ref, v_ref, ck_ref, cv_ref, dup_ref, o_ref, k2s, v2s, ck2s, cv2s = refs
        sink_ref = None
    j = pl.program_id(1)

    @pl.when(j == 0)
    def _():
        ka, kb = _dup_heads(k_ref[...].astype(F32))
        va, vb = _dup_heads(v_ref[...].astype(F32))
        k2s[0] = ka.astype(BF16)
        k2s[1] = kb.astype(BF16)
        v2s[0] = va.astype(BF16)
        v2s[1] = vb.astype(BF16)
        dup = dup_ref[...]
        for kv in range(N_KV):
            ck2s[kv] = _dot(ck_ref[0, 0, kv].astype(BF16), dup).astype(BF16)
            cv2s[kv] = _dot(cv_ref[0, 0, kv].astype(BF16), dup).astype(BF16)

    mask = None
    if windowed:
        n_loc = TQ + 2 * WINDOW
        start = pl.multiple_of(jnp.clip(j * TQ - WINDOW, 0, L_LAT - n_loc), WINDOW)
        qpos = j * TQ + lax.broadcasted_iota(jnp.int32, (TQ, n_loc), 0)
        kpos = start + lax.broadcasted_iota(jnp.int32, (TQ, n_loc), 1)
        mask = [None, jnp.abs(qpos - kpos) <= WINDOW]
    for kv in range(N_KV):
        if windowed:
            k_own = k2s[kv, pl.ds(start, n_loc), :]
            v_own = v2s[kv, pl.ds(start, n_loc), :]
        else:
            k_own = k2s[kv]
            v_own = v2s[kv]
        o0, o1 = _group_attention(q_ref, kv, [ck2s[kv], k_own], [cv2s[kv], v_own], sink_ref, mask, False)
        o_ref[:, (2 * kv) * LANES:(2 * kv + 1) * LANES] = o0
        o_ref[:, (2 * kv + 1) * LANES:(2 * kv + 2) * LANES] = o1


def _lat_attn(layer, q, k, v, cache_k, cache_v, dup, sink):
    windowed = sink is not None
    n_qt = L_LAT // TQ
    ctx_tiles = N_CTX_TOK // TQ
    ctx_seqs = N_CTX_TOK // L_LAT
    cache_spec = pl.BlockSpec((1, 1, N_KV, PAST, HD), lambda b, j: (b, layer, 0, 0, 0))
    in_specs = [
        pl.BlockSpec((TQ, Q_W), lambda b, j: (ctx_tiles + b * n_qt + j, 0)),
        pl.BlockSpec((L_LAT, KV_W), lambda b, j: (ctx_seqs + b, 0)),
        pl.BlockSpec((L_LAT, KV_W), lambda b, j: (ctx_seqs + b, 0)),
        cache_spec,
        cache_spec,
        pl.BlockSpec((HD, LANES), lambda b, j: (0, 0)),
    ]
    args = [q, k, v, cache_k, cache_v, dup]
    if windowed:
        in_specs = [pl.BlockSpec(memory_space=pltpu.SMEM)] + in_specs
        args = [sink] + args
    return pl.pallas_call(
        functools.partial(_lat_attn_kernel, windowed),
        grid=(N_SEQ_LAT, n_qt),
        in_specs=in_specs,
        out_specs=pl.BlockSpec((TQ, Q_W), lambda b, j: (b * n_qt + j, 0)),
        out_shape=jax.ShapeDtypeStruct((N_LAT_TOK, Q_W), BF16),
        scratch_shapes=[
            pltpu.VMEM((N_KV, L_LAT, LANES), BF16),
            pltpu.VMEM((N_KV, L_LAT, LANES), BF16),
            pltpu.VMEM((N_KV, PAST, LANES), BF16),
            pltpu.VMEM((N_KV, PAST, LANES), BF16),
        ],
        compiler_params=_cparams(("arbitrary", "arbitrary")),
        name="lat_attn_window" if windowed else "lat_attn",
    )(*args)


def _seq_pos(i, width):
    r = lax.broadcasted_iota(jnp.int32, (TM, width), 0)
    is_ctx = i < N_CTX_TOK // TM
    seq_len = jnp.where(is_ctx, L_CTX, L_LAT)
    return r & (seq_len - 1), seq_len


def _shift_rows(t, j, pos, seq_len):
    if j == 0:
        return t
    moved = pltpu.roll(t, (-j) % TM, axis=0)
    ok = (pos + j >= 0) & (pos + j < seq_len)
    return jnp.where(ok, moved, 0.0)


def _conv_mixer(zc_ref, cw_ref, pos, seq_len):
    bg = zc_ref[:, 0:CONV_W].astype(F32)
    u = zc_ref[:, CONV_W:2 * CONV_W].astype(F32) * zc_ref[:, 2 * CONV_W:3 * CONV_W].astype(F32)
    y = (_shift_rows(u, -1, pos, seq_len) * cw_ref[0:1, :] + u * cw_ref[1:2, :]
         + _shift_rows(u, 1, pos, seq_len) * cw_ref[2:3, :])
    return (bg * y).astype(BF16)


def _pool_mixer(xd_ref, pw_ref, ps_ref, pos, seq_len):
    outs = []
    for gi, w in enumerate(POOL_WINDOWS):
        xg = xd_ref[:, gi * POOL_G:(gi + 1) * POOL_G]
        tot = None
        for j in range(-(w // 2), w - w // 2):
            sh = _shift_rows(xg, j, pos, seq_len)
            tot = sh if tot is None else tot + sh
        lo = jnp.clip(pos - w // 2, 0, seq_len)
        hi = jnp.clip(pos - w // 2 + w, 0, seq_len)
        cnt = (hi - lo).astype(F32)
        d = tot / cnt - xg
        outs.append(_dot(d.astype(BF16), pw_ref[gi].astype(BF16)))
    return (jnp.concatenate(outs, axis=1) * ps_ref[...]).astype(BF16)


def _mix_out_kernel(even, nx, *refs):
    x_refs = refs[:nx]
    mod_ref, oc_ref, ol_ref, w_ref = refs[nx:nx + 4]
    rest = refs[nx + 4:]
    if even:
        zc_ref, cw_ref, out_ref, wbf = rest
    else:
        xd_ref, pw_ref, ps_ref, out_ref, wbf = rest
    i = pl.program_id(0)

    @pl.when(i == 0)
    def _():
        _cast_rows(w_ref.at[0], wbf, D_MODEL)

    pos, seq_len = _seq_pos(i, CONV_W if even else POOL_G)
    o = jnp.where(i < N_CTX_TOK // TM, oc_ref[...], ol_ref[...])
    if even:
        ya = _conv_mixer(zc_ref, cw_ref, pos, seq_len)
        y = _dot(ya, wbf[0:CONV_W, :]) + _dot(o, wbf[CONV_W:, :])
    else:
        yd = _pool_mixer(xd_ref, pw_ref, ps_ref, pos, seq_len)
        y = _dot(o, wbf[0:Q_W, :]) + _dot(yd, wbf[Q_W:, :])
    out_ref[...] = _read_x(i, x_refs, TM) + mod_ref[0, 2:3, :] * y


def _mix_out(even, layer_i, x, mod_l, o_ctx, o_lat, w_out, *extra):
    n_ctx = N_CTX_TOK // TM
    n_lat = N_LAT_TOK // TM
    row = lambda i: (i, 0)
    const = lambda i: (0, 0)
    in_specs = _x_specs(TM, len(x) == 2) + [
        pl.BlockSpec((1, 6, D_MODEL), lambda i: (_mod_row(i, TM), 0, 0)),
        pl.BlockSpec((TM, Q_W), lambda i: (jnp.minimum(i, n_ctx - 1), 0)),
        pl.BlockSpec((TM, Q_W), lambda i: (jnp.clip(i - n_ctx, 0, n_lat - 1), 0)),
        pl.BlockSpec((1, D_MODEL, D_MODEL), lambda i: (layer_i, 0, 0)),
    ]
    if even:
        in_specs += [pl.BlockSpec((TM, 3 * CONV_W), row), pl.BlockSpec((3, CONV_W), const)]
    else:
        in_specs += [pl.BlockSpec((TM, POOL_W), row),
                     pl.BlockSpec((len(POOL_WINDOWS), POOL_G, POOL_G), lambda i: (0, 0, 0)),
                     pl.BlockSpec((1, POOL_W), const)]
    return pl.pallas_call(
        functools.partial(_mix_out_kernel, even, len(x)),
        grid=(N_TOK // TM,),
        in_specs=in_specs,
        out_specs=pl.BlockSpec((TM, D_MODEL), row),
        out_shape=jax.ShapeDtypeStruct((N_TOK, D_MODEL), F32),
        scratch_shapes=[pltpu.VMEM((D_MODEL, D_MODEL), BF16)],
        compiler_params=_cparams(("arbitrary",)),
        name="mix_out_even" if even else "mix_out_odd",
    )(*x, mod_l, o_ctx, o_lat, w_out, *extra)


def _ffn_kernel(layer_i, x_ref, mod_ref, g_ref, w1_hbm, w3_hbm, w2_hbm, out_ref,
                w1s, w3s, w2s, st1, st3, st2, hs, hid, sem):
    i = pl.program_id(0)
    nf = D_FF // TF_FFN

    def chunk_copies(f, slot):
        cols = pl.ds(f * TF_FFN, TF_FFN)
        return (pltpu.make_async_copy(w1_hbm.at[layer_i, :, cols], st1.at[slot], sem.at[0, slot]),
                pltpu.make_async_copy(w3_hbm.at[layer_i, :, cols], st3.at[slot], sem.at[1, slot]),
                pltpu.make_async_copy(w2_hbm.at[layer_i, cols, :], st2.at[slot], sem.at[2, slot]))

    @pl.when(i == 0)
    def _():
        for c in chunk_copies(0, 0):
            c.start()

    hs[...] = _normmod(x_ref[...], g_ref[...], mod_ref[0, 4:5, :], mod_ref[0, 3:4, :]).astype(BF16)
    for f in range(nf):
        slot = f % 2
        lo, hi = f * TF_FFN, (f + 1) * TF_FFN

        @pl.when(i == 0)
        def _(f=f, slot=slot, lo=lo, hi=hi):
            if f + 1 < nf:
                for c in chunk_copies(f + 1, 1 - slot):
                    c.start()
            for c in chunk_copies(f, slot):
                c.wait()
            w1s[:, lo:hi] = st1[slot].astype(BF16)
            w3s[:, lo:hi] = st3[slot].astype(BF16)
            w2s[lo:hi, :] = st2[slot].astype(BF16)

        h = hs[...]
        hid[:, lo:hi] = (_silu(_dot(h, w1s[:, lo:hi])) * _dot(h, w3s[:, lo:hi])).astype(BF16)
    out_ref[...] = x_ref[...] + mod_ref[0, 5:6, :] * _dot(hid[...], w2s[...])


def _ffn(layer_i, x, mod_l, g, w1, w3, w2):
    row = lambda i: (i, 0)
    any_spec = pl.BlockSpec(memory_space=pl.ANY)
    return pl.pallas_call(
        functools.partial(_ffn_kernel, layer_i),
        grid=(N_TOK // TM,),
        in_specs=[
            pl.BlockSpec((TM, D_MODEL), row),
            pl.BlockSpec((1, 6, D_MODEL), lambda i: (_mod_row(i, TM), 0, 0)),
            pl.BlockSpec((1, D_MODEL), lambda i: (0, 0)),
            any_spec, any_spec, any_spec,
        ],
        out_specs=pl.BlockSpec((TM, D_MODEL), row),
        out_shape=jax.ShapeDtypeStruct((N_TOK, D_MODEL), F32),
        scratch_shapes=[
            pltpu.VMEM((D_MODEL, D_FF), BF16),
            pltpu.VMEM((D_MODEL, D_FF), BF16),
            pltpu.VMEM((D_FF, D_MODEL), BF16),
            pltpu.VMEM((2, D_MODEL, TF_FFN), F32),
            pltpu.VMEM((2, D_MODEL, TF_FFN), F32),
            pltpu.VMEM((2, TF_FFN, D_MODEL), F32),
            pltpu.VMEM((TM, D_MODEL), BF16),
            pltpu.VMEM((TM, D_FF), BF16),
            pltpu.SemaphoreType.DMA((3, 2)),
        ],
        compiler_params=pltpu.CompilerParams(dimension_semantics=("arbitrary",), vmem_limit_bytes=FFN_VMEM_LIMIT),
        name="ffn",
    )(x, mod_l, g, w1, w3, w2)


def _split_bf16(t):
    hi = t.astype(BF16)
    return hi, (t - hi.astype(F32)).astype(BF16)


def _lane_values(col_vals, ones_rows):
    q = jnp.floor(col_vals * (1.0 / 32.0))
    r = col_vals - 32.0 * q
    t = 32.0 * _dot_nt(ones_rows, q.astype(BF16)) + _dot_nt(ones_rows, r.astype(BF16))
    return t[0:1, :]


def _route_kernel(x_ref, mod_ref, g_ref, rw_ref, rb_ref, tri_ref, upper_ref, xl_ref, meta_ref, cnt_ref):
    h = _normmod(x_ref[...], g_ref[...], mod_ref[0, 4:5, :], mod_ref[0, 3:4, :])
    h_hi, h_lo = _split_bf16(h)
    w_hi, w_lo = _split_bf16(rw_ref[...])
    logits = _dot(h_hi, w_hi) + _dot(h_hi, w_lo) + _dot(h_lo, w_hi) + rb_ref[...]
    lane = lax.broadcasted_iota(jnp.int32, logits.shape, 1).astype(F32)
    logits = jnp.where(lane < N_EXP, logits, NEG_INF)
    m1 = logits.max(axis=-1, keepdims=True)
    i1 = jnp.where(logits == m1, lane, float(LANES)).min(axis=-1, keepdims=True)
    rest = jnp.where(lane == i1, NEG_INF, logits)
    m2 = rest.max(axis=-1, keepdims=True)
    i2 = jnp.where(rest == m2, lane, float(LANES)).min(axis=-1, keepdims=True)
    e2 = jnp.exp(m2 - m1)
    den = 1.0 + e2
    g1 = 1.0 / den
    g2 = e2 / den

    oh_a = jnp.where(lane == i1, 1.0, 0.0)
    oh_b = jnp.where(lane == i2, 1.0, 0.0)
    tri = tri_ref[...]
    cnt_a = oh_a.sum(axis=0, keepdims=True)
    cnt_b = oh_b.sum(axis=0, keepdims=True)
    run16 = jnp.floor((cnt_a + cnt_b + (RUN_ALIGN - 1)) * (1.0 / RUN_ALIGN))
    run16_rows = jnp.broadcast_to(run16, (8, LANES))
    start = RUN_ALIGN * _dot(run16_rows.astype(BF16), upper_ref[...])[0:1, :]
    row_a = oh_a * (start + _dot(tri, oh_a.astype(BF16)))
    row_b = oh_b * (start + cnt_a + _dot(tri, oh_b.astype(BF16)))

    ones_rows = jnp.ones((8, LANES), BF16)
    tok_a = _lane_values(row_a, ones_rows)
    tok_b = _lane_values(row_b, ones_rows)
    sorted_row = lax.broadcasted_iota(jnp.int32, (LOCAL_ROWS, TR), 0).astype(F32)
    perm = jnp.where((sorted_row == tok_a) | (sorted_row == tok_b), 1.0, 0.0).astype(BF16)
    xl_ref[...] = _dot(perm, h_hi).astype(BF16)

    meta = jnp.zeros_like(logits)
    cols = (row_a.sum(axis=-1, keepdims=True), row_b.sum(axis=-1, keepdims=True), g1, g2)
    for k, col in enumerate(cols):
        meta = jnp.where(lane == k, col, meta)
    meta_ref[...] = meta
    cnt_ref[...] = RUN_ALIGN * run16_rows


def _route(x, mod_l, g, router_w, router_b, tri, upper):
    row = lambda i: (i, 0)
    const = lambda i: (0, 0)
    n_tiles = N_TOK // TR
    return pl.pallas_call(
        _route_kernel,
        grid=(n_tiles,),
        in_specs=[
            pl.BlockSpec((TR, D_MODEL), row),
            pl.BlockSpec((1, 6, D_MODEL), lambda i: (_mod_row(i, TR), 0, 0)),
            pl.BlockSpec((1, D_MODEL), const),
            pl.BlockSpec((D_MODEL, LANES), const),
            pl.BlockSpec((1, LANES), const),
            pl.BlockSpec((TR, TR), const),
            pl.BlockSpec((LANES, LANES), const),
        ],
        out_specs=[pl.BlockSpec((LOCAL_ROWS, D_MODEL), row), pl.BlockSpec((TR, LANES), row),
                   pl.BlockSpec((8, LANES), row)],
        out_shape=[jax.ShapeDtypeStruct((n_tiles * LOCAL_ROWS, D_MODEL), BF16),
                   jax.ShapeDtypeStruct((N_TOK, LANES), F32),
                   jax.ShapeDtypeStruct((n_tiles * 8, LANES), F32)],
        compiler_params=_cparams(("arbitrary",)),
        name="moe_route",
    )(x, mod_l, g, router_w, router_b, tri, upper)


def _moe_plan(counts):
    n_tiles = N_TOK // TR
    run = counts.reshape(n_tiles, 8, LANES)[:, 0, :N_EXP].astype(jnp.int32)
    per_expert = jnp.sum(run, axis=0)
    region = (per_expert + (TMS - 1)) // TMS * TMS
    ends = jnp.cumsum(region)
    offs = ends - region
    seg_end = jnp.cumsum(run, axis=0)
    seg_start = seg_end - run
    local_end = jnp.cumsum(run, axis=1)
    local_start = local_end - run
    n_chunks = local_end[:, N_EXP - 1] // RUN_ALIGN
    tile_start = jnp.arange(MOE_TILES, dtype=jnp.int32) * TMS
    tile_expert = jnp.sum((tile_start[:, None] >= ends[None, :]).astype(jnp.int32), axis=1)
    tile_expert = jnp.minimum(tile_expert, N_EXP - 1)
    n_used = (ends[N_EXP - 1] // TMS).reshape(1)
    experts = jnp.arange(N_EXP, dtype=jnp.int32)
    g_row = jnp.arange(MOE_ROWS // RUN_ALIGN, dtype=jnp.int32) * RUN_ALIGN
    g_exp = jnp.repeat(tile_expert, TMS // RUN_ALIGN)
    pick = g_exp[:, None] == experts[None, :]
    rel = g_row - jnp.sum(jnp.where(pick, offs[None, :], 0), axis=1)
    ends_of = jnp.sum(jnp.where(pick[:, None, :], seg_end[None, :, :], 0), axis=2)
    src_tile = jnp.sum((rel[:, None] >= ends_of).astype(jnp.int32), axis=1)
    valid = (src_tile < n_tiles) & (jnp.repeat(jnp.arange(MOE_TILES), TMS // RUN_ALIGN) < n_used[0])
    src_tile = jnp.minimum(src_tile, n_tiles - 1)
    sel = (src_tile[:, None, None] == jnp.arange(n_tiles)[None, :, None]) & pick[:, None, :]
    shift = jnp.sum(jnp.where(sel, (local_start - seg_start)[None, :, :], 0), axis=(1, 2))
    chunk_src = src_tile * LOCAL_ROWS + rel + shift
    n_valid = jnp.sum(valid.reshape(MOE_TILES, TMS // RUN_ALIGN).astype(jnp.int32), axis=1)
    return chunk_src, n_valid, n_chunks, tile_expert, n_used


def _chunk_copy(src_ref, src_row, dst_ref, dst_row, sem):
    return pltpu.make_async_copy(src_ref.at[pl.ds(src_row, RUN_ALIGN), :],
                                 dst_ref.at[pl.ds(dst_row, RUN_ALIGN), :], sem)


CHUNKS = TMS // RUN_ALIGN
LOCAL_CHUNKS = LOCAL_ROWS // RUN_ALIGN


def _expert_kernel(te_ref, nu_ref, src_ref, nv_ref, nc_ref, xl_hbm, w1_ref, w3_ref, w2_ref, yl_hbm,
                   xbuf, ybuf, zbuf, w1s, w3s, w2s, sem_in, sem_out, sem_zero):
    t = pl.program_id(0)
    n_used = nu_ref[0]
    slot = t & 1

    def gather(tile, s):
        n = nv_ref[tile]

        def issue(j, c):
            src = pl.multiple_of(src_ref[tile * CHUNKS + j], RUN_ALIGN)
            _chunk_copy(xl_hbm, src, xbuf.at[s], pl.multiple_of(j * RUN_ALIGN, RUN_ALIGN), sem_in.at[s]).start()
            return c

        def pad(j, c):
            xbuf[s, pl.ds(pl.multiple_of(j * RUN_ALIGN, RUN_ALIGN), RUN_ALIGN), :] = jnp.zeros(
                (RUN_ALIGN, D_MODEL), BF16)
            return c

        lax.fori_loop(0, n, issue, 0)
        lax.fori_loop(n, CHUNKS, pad, 0)

    def scatter(tile, s):
        def issue(j, c):
            dst = pl.multiple_of(src_ref[tile * CHUNKS + j], RUN_ALIGN)
            _chunk_copy(ybuf.at[s], pl.multiple_of(j * RUN_ALIGN, RUN_ALIGN), yl_hbm, dst, sem_out.at[s]).start()
            return c

        lax.fori_loop(0, nv_ref[tile], issue, 0)

    def drain(count, sem):
        def one(j, c):
            _chunk_copy(xl_hbm, 0, xbuf.at[0], 0, sem).wait()
            return c

        lax.fori_loop(0, count, one, 0)

    @pl.when(t == 0)
    def _():
        zbuf[...] = jnp.zeros_like(zbuf)
        total = jnp.int32(0)
        for tile in range(N_TOK // TR):
            n = nc_ref[tile]

            def clear(j, c, tile=tile):
                row = pl.multiple_of(tile * LOCAL_ROWS + j * RUN_ALIGN, RUN_ALIGN)
                _chunk_copy(zbuf, 0, yl_hbm, row, sem_zero).start()
                return c

            lax.fori_loop(n, LOCAL_CHUNKS, clear, 0)
            total = total + (LOCAL_CHUNKS - n)
        drain(total, sem_zero)
        gather(0, 0)

    @pl.when(t < n_used)
    def _():
        drain(nv_ref[t], sem_in.at[slot])

    @pl.when(t + 1 < n_used)
    def _():
        gather(t + 1, 1 - slot)

    new_expert = (t == 0) | (te_ref[t] != te_ref[jnp.maximum(t - 1, 0)])

    @pl.when(new_expert)
    def _():
        _cast_rows(w1_ref.at[0, 0], w1s, D_MODEL)
        _cast_rows(w3_ref.at[0, 0], w3s, D_MODEL)
        _cast_rows(w2_ref.at[0, 0], w2s, D_FF_E)

    @pl.when((t >= 2) & (t - 2 < n_used))
    def _():
        drain(nv_ref[jnp.maximum(t - 2, 0)], sem_out.at[slot])

    @pl.when(t < n_used)
    def _():
        x = xbuf[slot]
        hid = (_silu(_dot(x, w1s[...])) * _dot(x, w3s[...])).astype(BF16)
        ybuf[slot] = _dot(hid, w2s[...]).astype(BF16)
        scatter(t, slot)

    @pl.when(t == pl.num_programs(0) - 1)
    def _():
        @pl.when((t >= 1) & (t - 1 < n_used))
        def _():
            drain(nv_ref[jnp.maximum(t - 1, 0)], sem_out.at[1 - slot])

        @pl.when(t < n_used)
        def _():
            drain(nv_ref[t], sem_out.at[slot])


def _experts(layer_i, plan, xl, w1, w3, w2):
    chunk_src, n_valid, n_chunks, tile_expert, n_used = plan
    wsel = lambda t, te, nu, src, nv, nc: (layer_i, te[t], 0, 0)
    any_spec = pl.BlockSpec(memory_space=pl.ANY)
    return pl.pallas_call(
        _expert_kernel,
        grid_spec=pltpu.PrefetchScalarGridSpec(
            num_scalar_prefetch=5,
            grid=(MOE_TILES,),
            in_specs=[
                any_spec,
                pl.BlockSpec((1, 1, D_MODEL, D_FF_E), wsel),
                pl.BlockSpec((1, 1, D_MODEL, D_FF_E), wsel),
                pl.BlockSpec((1, 1, D_FF_E, D_MODEL), wsel),
            ],
            out_specs=any_spec,
            scratch_shapes=[
                pltpu.VMEM((2, TMS, D_MODEL), BF16),
                pltpu.VMEM((2, TMS, D_MODEL), BF16),
                pltpu.VMEM((RUN_ALIGN, D_MODEL), BF16),
                pltpu.VMEM((D_MODEL, D_FF_E), BF16),
                pltpu.VMEM((D_MODEL, D_FF_E), BF16),
                pltpu.VMEM((D_FF_E, D_MODEL), BF16),
                pltpu.SemaphoreType.DMA((2,)),
                pltpu.SemaphoreType.DMA((2,)),
                pltpu.SemaphoreType.DMA(()),
            ],
        ),
        out_shape=jax.ShapeDtypeStruct(xl.shape, BF16),
        compiler_params=_cparams(("arbitrary",)),
        name="moe_experts",
    )(tile_expert, n_used, chunk_src, n_valid, n_chunks, xl, w1, w3, w2)


def _combine_kernel(x_ref, mod_ref, meta_ref, yl_ref, *out_refs):
    sorted_row = lax.broadcasted_iota(jnp.int32, (TR, LOCAL_ROWS), 1).astype(F32)
    y = yl_ref[...]
    pick_a = jnp.where(sorted_row == meta_ref[:, 0:1], 1.0, 0.0).astype(BF16)
    pick_b = jnp.where(sorted_row == meta_ref[:, 1:2], 1.0, 0.0).astype(BF16)
    f = meta_ref[:, 2:3] * _dot(pick_a, y) + meta_ref[:, 3:4] * _dot(pick_b, y)
    _write_x(pl.program_id(0), out_refs, TR, x_ref[...] + mod_ref[0, 5:6, :] * f)


def _combine(x, mod_l, meta, yl, split_out):
    row = lambda i: (i, 0)
    return pl.pallas_call(
        _combine_kernel,
        grid=(N_TOK // TR,),
        in_specs=[
            pl.BlockSpec((TR, D_MODEL), row),
            pl.BlockSpec((1, 6, D_MODEL), lambda i: (_mod_row(i, TR), 0, 0)),
            pl.BlockSpec((TR, LANES), row),
            pl.BlockSpec((LOCAL_ROWS, D_MODEL), row),
        ],
        out_specs=_x_specs(TR, split_out),
        out_shape=_x_shapes(split_out),
        compiler_params=_cparams(("arbitrary",)),
        name="moe_combine",
    )(x, mod_l, meta, yl)


def _moe(layer_i, x, mod_l, g, router_w, router_b, tri, upper, w1, w3, w2, split_out):
    xl, meta, counts = _route(x, mod_l, g, router_w, router_b, tri, upper)
    yl = _experts(layer_i, _moe_plan(counts), xl, w1, w3, w2)
    return _combine(x, mod_l, meta, yl, split_out)


def _rope_tables():
    n_rows = L_LAT // GRID_W
    rows = jnp.repeat(jnp.arange(n_rows, dtype=F32), GRID_W)
    cols = jnp.tile(jnp.arange(GRID_W, dtype=F32), n_rows)
    quarter = HD // 4
    inv = ROPE_THETA ** (-jnp.arange(quarter, dtype=F32) / quarter)
    ang_r = rows[:, None] * inv
    ang_c = cols[:, None] * inv
    cos = jnp.concatenate([jnp.cos(ang_r)] * 2 + [jnp.cos(ang_c)] * 2, axis=1)
    sin = jnp.concatenate([-jnp.sin(ang_r), jnp.sin(ang_r), -jnp.sin(ang_c), jnp.sin(ang_c)], axis=1)
    cos = jnp.concatenate([cos, cos], axis=1)
    sin = jnp.concatenate([sin, sin], axis=1)
    cos = jnp.concatenate([jnp.ones((L_LAT, LANES), F32), cos], axis=0)
    sin = jnp.concatenate([jnp.zeros((L_LAT, LANES), F32), sin], axis=0)
    return cos, sin


def _block_ones(width):
    r = jnp.arange(width) // HD
    return (r[:, None] == r[None, :]).astype(BF16)


def kernel(x_prompt, x_sample, cache_k, cache_v, c, c_ctx, norm1, norm2, w_mod, b_mod, ev_w_in, ev_conv, ev_q_norm, ev_k_norm, ev_w_out, od_w_in, od_q_norm, od_k_norm, od_sink, od_pool_w, od_pool_scale, od_w_out, ffn_w1, ffn_w3, ffn_w2, moe_router, moe_router_b, moe_w1, moe_w3, moe_w2):
    x = (x_prompt.reshape(N_CTX_TOK, D_MODEL), x_sample.reshape(N_LAT_TOK, D_MODEL))
    cond = jnp.concatenate([c_ctx[None, :], c, jnp.zeros((MOD_ROWS - 1 - N_SEQ_LAT, D_MODEL), F32)], axis=0)
    mod = _modulation(cond, w_mod, b_mod).reshape(DEPTH, MOD_ROWS, 6, D_MODEL)

    cos_tab, sin_tab = _rope_tables()
    ones_q = _block_ones(Q_W)
    ones_k = _block_ones(KV_W)
    eye = jnp.eye(HD, dtype=BF16)
    dup = jnp.concatenate([eye, eye], axis=1)
    tri = (jnp.arange(TR)[:, None] > jnp.arange(TR)[None, :]).astype(BF16)
    upper = (jnp.arange(LANES)[:, None] < jnp.arange(LANES)[None, :]).astype(BF16)

    new_kv = None
    for l in range(DEPTH):
        i = l // 2
        even = l % 2 == 0
        mod_l = mod[l]
        g1 = norm1[l][None, :]
        g2 = norm2[l][None, :]
        if even:
            q_gain, k_gain = ev_q_norm[i], ev_k_norm[i]
            w_in, w_out = ev_w_in, ev_w_out
        else:
            q_gain, k_gain = od_q_norm[i], od_k_norm[i]
            w_in, w_out = od_w_in, od_w_out
        q_gain = jnp.tile(q_gain, N_Q)[None, :]
        k_gain = jnp.tile(k_gain, N_KV)[None, :]
        outs = _in_proj(even, i, x, mod_l, g1, w_in, ones_q, ones_k, q_gain, k_gain, cos_tab, sin_tab)
        if even:
            zc, q, k, v = outs
            sink = None
        else:
            q, k, v, xd = outs
            sink = od_sink[i]
        o_ctx, new_kv = _ctx_attn(l, q, k, v, sink, new_kv)
        o_lat = _lat_attn(l, q, k, v, cache_k, cache_v, dup, sink)
        if even:
            x1 = _mix_out(True, i, x, mod_l, o_ctx, o_lat, w_out, zc, ev_conv[i])
            x = (_ffn(i, x1, mod_l, g2, ffn_w1, ffn_w3, ffn_w2),)
        else:
            x1 = _mix_out(False, i, x, mod_l, o_ctx, o_lat, w_out, xd, od_pool_w[i], od_pool_scale[i][None, :])
            rw = jnp.pad(moe_router[i], ((0, 0), (0, LANES - N_EXP)))
            rb = jnp.pad(moe_router_b[i], (0, LANES - N_EXP))[None, :]
            x = tuple(_moe(i, x1, mod_l, g2, rw, rb, tri, upper, moe_w1, moe_w3, moe_w2, l == DEPTH - 1))

    y_prompt = x[0].reshape(N_SEQ_CTX, L_CTX, D_MODEL)
    y_sample = x[1].reshape(N_SEQ_LAT, L_LAT, D_MODEL)
    return (y_prompt, y_sample, new_kv[0], new_kv[1])
```

```python
import functools

import jax
import jax.numpy as jnp
from jax import lax
from jax.experimental import pallas as pl
from jax.experimental.pallas import tpu as pltpu

F32 = jnp.float32
BF16 = jnp.bfloat16

D_MODEL = 1024
N_SEQ_CTX = 32
L_CTX = 256
N_SEQ_LAT = 4
L_LAT = 1024
DEPTH = 4
PAST = 512
GRID_W = 64
HD = 64
N_Q = 8
N_KV = 2
GROUP = N_Q // N_KV
Q_W = N_Q * HD
KV_W = N_KV * HD
CONV_W = 512
POOL_W = 512
POOL_WINDOWS = (2, 4, 8, 16)
POOL_G = 128
POOL_PAD = 16
EVEN_IN = 3 * CONV_W + Q_W + 2 * KV_W
ODD_IN = Q_W + 2 * KV_W + POOL_W
WINDOW = 128
D_FF = 2816
N_EXP = 8
TOP_K = 2
D_FF_E = 1024
ROPE_THETA = 10000.0
EPS = 1e-6

N_CTX_TOK = N_SEQ_CTX * L_CTX
N_LAT_TOK = N_SEQ_LAT * L_LAT
N_TOK = N_CTX_TOK + N_LAT_TOK
MOD_ROWS = 16

LANES = 128
VMEM_LIMIT = 56 * 1024 * 1024
FFN_VMEM_LIMIT = 62 * 1024 * 1024

TM_IN = 512
TM = 1024
TQ = 256
CTX_SEQS = 2
TF_FFN = 256
TR = 512
RUN_ALIGN = 16
LOCAL_ROWS = TOP_K * TR + N_EXP * RUN_ALIGN
TMS = 512
MOE_TILES = -(-(TOP_K * N_TOK + (N_TOK // TR) * N_EXP * (RUN_ALIGN - 1) + N_EXP * (TMS - 1)) // TMS)
MOE_ROWS = MOE_TILES * TMS
NEG_INF = float("-inf")


def _cparams(sem):
    return pltpu.CompilerParams(dimension_semantics=sem, vmem_limit_bytes=VMEM_LIMIT)


def _mod_row(i, tm):
    n_ctx = N_CTX_TOK // tm
    return jnp.where(i < n_ctx, 0, 1 + (i - n_ctx) // (L_LAT // tm))


def _x_specs(tm, split):
    if not split:
        return [pl.BlockSpec((tm, D_MODEL), lambda i, *_: (i, 0))]
    n_ctx = N_CTX_TOK // tm
    n_lat = N_LAT_TOK // tm
    return [pl.BlockSpec((tm, D_MODEL), lambda i, *_: (jnp.minimum(i, n_ctx - 1), 0)),
            pl.BlockSpec((tm, D_MODEL), lambda i, *_: (jnp.clip(i - n_ctx, 0, n_lat - 1), 0))]


def _x_shapes(split):
    if not split:
        return [jax.ShapeDtypeStruct((N_TOK, D_MODEL), F32)]
    return [jax.ShapeDtypeStruct((N_CTX_TOK, D_MODEL), F32), jax.ShapeDtypeStruct((N_LAT_TOK, D_MODEL), F32)]


def _read_x(i, x_refs, tm):
    if len(x_refs) == 1:
        return x_refs[0][...]
    return jnp.where(i < N_CTX_TOK // tm, x_refs[0][...], x_refs[1][...])


def _write_x(i, o_refs, tm, val):
    if len(o_refs) == 1:
        o_refs[0][...] = val
        return

    @pl.when(i < N_CTX_TOK // tm)
    def _():
        o_refs[0][...] = val

    @pl.when(i >= N_CTX_TOK // tm)
    def _():
        o_refs[1][...] = val


def _normmod(x, g, scale, shift):
    ms = jnp.mean(x * x, axis=-1, keepdims=True)
    y = x * lax.rsqrt(ms + EPS) * g
    return y * (1.0 + scale) + shift


def _silu(x):
    return x * jax.nn.sigmoid(x)


def _dot(a, b):
    return jnp.dot(a, b, preferred_element_type=F32)


def _dot_nt(a, b):
    return lax.dot_general(a, b, (((1,), (1,)), ((), ())), preferred_element_type=F32)


def _cast_rows(src_ref, dst_ref, rows, chunk=256):
    for r in range(0, rows, chunk):
        dst_ref[r:r + chunk, :] = src_ref[r:r + chunk, :].astype(dst_ref.dtype)


def _mod_kernel(c_ref, w_ref, b_ref, o_ref):
    s = _silu(c_ref[...]).astype(BF16)
    o_ref[0] = _dot(s, w_ref[0].astype(BF16)) + b_ref[0]


def _modulation(cond, w_mod, b_mod):
    tn = 1536
    return pl.pallas_call(
        _mod_kernel,
        grid=(DEPTH, 6 * D_MODEL // tn),
        in_specs=[
            pl.BlockSpec((MOD_ROWS, D_MODEL), lambda l, j: (0, 0)),
            pl.BlockSpec((1, D_MODEL, tn), lambda l, j: (l, 0, j)),
            pl.BlockSpec((1, 1, tn), lambda l, j: (l, 0, j)),
        ],
        out_specs=pl.BlockSpec((1, MOD_ROWS, tn), lambda l, j: (l, 0, j)),
        out_shape=jax.ShapeDtypeStruct((DEPTH, MOD_ROWS, 6 * D_MODEL), F32),
        compiler_params=_cparams(("arbitrary", "arbitrary")),
        name="modulation",
    )(cond, w_mod, b_mod.reshape(DEPTH, 1, 6 * D_MODEL))


def _head_rms(t, ones_bd, gain):
    ssq = _dot((t * t).astype(BF16), ones_bd)
    return t * lax.rsqrt(ssq * (1.0 / HD) + EPS) * gain


def _rope(t, cos, sin_signed):
    lane = lax.broadcasted_iota(jnp.int32, (t.shape[0], LANES), 1)
    first = (lane & 31) < 16
    outs = []
    for c in range(t.shape[1] // LANES):
        tc = t[:, c * LANES:(c + 1) * LANES]
        nxt = pltpu.roll(tc, LANES - 16, axis=1)
        prv = pltpu.roll(tc, 16, axis=1)
        outs.append(tc * cos + jnp.where(first, nxt, prv) * sin_signed)
    return outs[0] if len(outs) == 1 else jnp.concatenate(outs, axis=1)


def _in_proj_kernel(even, nx, *refs):
    x_refs = refs[:nx]
    mod_ref, g_ref, w_ref, onesq_ref, onesk_ref, qg_ref, kg_ref, cos_ref, sin_ref = refs[nx:nx + 9]
    rest = refs[nx + 9:]
    if even:
        zc_ref, q_ref, k_ref, v_ref, wbf = rest
        q0 = 3 * CONV_W
    else:
        q_ref, k_ref, v_ref, xd_ref, wbf = rest
        q0 = 0
    k0 = q0 + Q_W
    v0 = k0 + KV_W

    @pl.when(pl.program_id(0) == 0)
    def _():
        _cast_rows(w_ref.at[0], wbf, D_MODEL)

    x = _read_x(pl.program_id(0), x_refs, TM_IN)
    h = _normmod(x, g_ref[...], mod_ref[0, 1:2, :], mod_ref[0, 0:1, :]).astype(BF16)
    cos = cos_ref[...]
    sin = sin_ref[...]

    q = _dot(h, wbf[:, q0:q0 + Q_W])
    q = _rope(_head_rms(q, onesq_ref[...], qg_ref[...]), cos, sin) * (HD ** -0.5)
    q_ref[...] = q.astype(BF16)

    k = _dot(h, wbf[:, k0:k0 + KV_W])
    k = _rope(_head_rms(k, onesk_ref[...], kg_ref[...]), cos, sin)
    k_ref[...] = k.astype(BF16)

    v_ref[...] = _dot(h, wbf[:, v0:v0 + KV_W]).astype(BF16)

    if even:
        zc_ref[...] = _dot(h, wbf[:, 0:3 * CONV_W]).astype(BF16)
    else:
        xd_ref[...] = _dot(h, wbf[:, v0 + KV_W:v0 + KV_W + POOL_W])


def _in_proj(even, layer_i, x, mod_l, g, w, ones_q, ones_k, q_gain, k_gain, cos_tab, sin_tab):
    tm = TM_IN
    n_in = EVEN_IN if even else ODD_IN
    n_ctx = N_CTX_TOK // tm
    per_seq = L_LAT // tm

    def rope_idx(i):
        return (jnp.where(i < n_ctx, 0, per_seq + (i - n_ctx) % per_seq), 0)

    row = lambda i: (i, 0)
    const = lambda i: (0, 0)
    in_specs = _x_specs(tm, len(x) == 2) + [
        pl.BlockSpec((1, 6, D_MODEL), lambda i: (_mod_row(i, tm), 0, 0)),
        pl.BlockSpec((1, D_MODEL), const),
        pl.BlockSpec((1, D_MODEL, n_in), lambda i: (layer_i, 0, 0)),
        pl.BlockSpec((Q_W, Q_W), const),
        pl.BlockSpec((KV_W, KV_W), const),
        pl.BlockSpec((1, Q_W), const),
        pl.BlockSpec((1, KV_W), const),
        pl.BlockSpec((tm, LANES), rope_idx),
        pl.BlockSpec((tm, LANES), rope_idx),
    ]
    qkv_specs = [pl.BlockSpec((tm, Q_W), row), pl.BlockSpec((tm, KV_W), row), pl.BlockSpec((tm, KV_W), row)]
    qkv_shapes = [jax.ShapeDtypeStruct((N_TOK, Q_W), BF16), jax.ShapeDtypeStruct((N_TOK, KV_W), BF16),
                  jax.ShapeDtypeStruct((N_TOK, KV_W), BF16)]
    if even:
        out_specs = [pl.BlockSpec((tm, 3 * CONV_W), row)] + qkv_specs
        out_shape = [jax.ShapeDtypeStruct((N_TOK, 3 * CONV_W), BF16)] + qkv_shapes
    else:
        out_specs = qkv_specs + [pl.BlockSpec((tm, POOL_W), row)]
        out_shape = qkv_shapes + [jax.ShapeDtypeStruct((N_TOK, POOL_W), F32)]
    return pl.pallas_call(
        functools.partial(_in_proj_kernel, even, len(x)),
        grid=(N_TOK // tm,),
        in_specs=in_specs,
        out_specs=out_specs,
        out_shape=out_shape,
        scratch_shapes=[pltpu.VMEM((D_MODEL, n_in), BF16)],
        compiler_params=_cparams(("arbitrary",)),
        name="in_proj_even" if even else "in_proj_odd",
    )(*x, mod_l, g, w, ones_q, ones_k, q_gain, k_gain, cos_tab, sin_tab)


def _dup_heads(t):
    lane = lax.broadcasted_iota(jnp.int32, t.shape, 1)
    swapped = pltpu.roll(t, HD, axis=1)
    low = lane < HD
    return jnp.where(low, t, swapped), jnp.where(low, swapped, t)


def _softmax_pv(scores, values, sink):
    m = scores[0].max(axis=-1, keepdims=True)
    for s in scores[1:]:
        m = jnp.maximum(m, s.max(axis=-1, keepdims=True))
    if sink is not None:
        m = jnp.maximum(m, sink)
    den = None
    acc = None
    for s, v in zip(scores, values):
        e = jnp.exp(s - m)
        d = e.sum(axis=-1, keepdims=True)
        a = _dot(e.astype(BF16), v)
        den = d if den is None else den + d
        acc = a if acc is None else acc + a
    if sink is not None:
        den = den + jnp.exp(sink - m)
    return acc / den


def _group_attention(q_ref, kv, keys, values, sink_ref, mask, stack):
    t = q_ref.shape[0]
    lane = lax.broadcasted_iota(jnp.int32, (t, LANES), 1)
    low = lane < HD
    zero = jnp.zeros((t, LANES), BF16)
    chunks = [q_ref[:, (kv * 2 + c) * LANES:(kv * 2 + c + 1) * LANES] for c in range(2)]
    if not stack:
        outs = []
        for h in range(GROUP):
            qm = jnp.where(low, chunks[h // 2], zero) if h % 2 == 0 else jnp.where(low, zero, chunks[h // 2])
            scores = [_dot_nt(qm, k) for k in keys]
            if mask is not None:
                scores = [s if m is None else jnp.where(m, s, NEG_INF) for s, m in zip(scores, mask)]
            sink = None if sink_ref is None else sink_ref[kv * GROUP + h]
            outs.append(_softmax_pv(scores, values, sink))
        return (jnp.where(low, outs[0], outs[1]).astype(BF16), jnp.where(low, outs[2], outs[3]).astype(BF16))
    qs = jnp.concatenate([jnp.where(low, chunks[0], zero), jnp.where(low, zero, chunks[0]),
                          jnp.where(low, chunks[1], zero), jnp.where(low, zero, chunks[1])], axis=0)
    scores = [_dot_nt(qs, k) for k in keys]
    if mask is not None:
        scores = [s if m is None else jnp.where(m, s, NEG_INF) for s, m in zip(scores, mask)]
    sink = None
    if sink_ref is not None:
        head = lax.broadcasted_iota(jnp.int32, (GROUP * t, 1), 0) // t
        sink = jnp.full((GROUP * t, 1), sink_ref[kv * GROUP], F32)
        for h in range(1, GROUP):
            sink = jnp.where(head == h, sink_ref[kv * GROUP + h], sink)
    out = _softmax_pv(scores, values, sink)
    return (jnp.where(low, out[0:t], out[t:2 * t]).astype(BF16),
            jnp.where(low, out[2 * t:3 * t], out[3 * t:4 * t]).astype(BF16))


def _ctx_attn_kernel(has_sink, first, *refs):
    if has_sink:
        sink_ref, refs = refs[0], refs[1:]
    else:
        sink_ref = None
    q_ref, k_ref, v_ref = refs[:3]
    o_ref, nk_ref, nv_ref = refs[-3:]
    lane = lax.broadcasted_iota(jnp.int32, (L_CTX, LANES), 1)
    low = lane < HD
    for s in range(CTX_SEQS):
        rows = pl.ds(s * L_CTX, L_CTX)
        k = k_ref[rows, :].astype(F32)
        v = v_ref[rows, :].astype(F32)
        k_sw = pltpu.roll(k, HD, axis=1)
        v_sw = pltpu.roll(v, HD, axis=1)
        nk_ref[s, 0, 0] = k[:, 0:HD]
        nk_ref[s, 0, 1] = k_sw[:, 0:HD]
        nv_ref[s, 0, 0] = v[:, 0:HD]
        nv_ref[s, 0, 1] = v_sw[:, 0:HD]
        if first:
            nk_ref[s, 1:] = jnp.zeros((DEPTH - 1, N_KV, L_CTX, HD), F32)
            nv_ref[s, 1:] = jnp.zeros((DEPTH - 1, N_KV, L_CTX, HD), F32)
        k2 = (jnp.where(low, k, k_sw).astype(BF16), jnp.where(low, k_sw, k).astype(BF16))
        v2 = (jnp.where(low, v, v_sw).astype(BF16), jnp.where(low, v_sw, v).astype(BF16))
        for kv in range(N_KV):
            o0, o1 = _group_attention(q_ref.at[rows, :], kv, [k2[kv]], [v2[kv]], sink_ref, None, True)
            o_ref[rows, (2 * kv) * LANES:(2 * kv + 1) * LANES] = o0
            o_ref[rows, (2 * kv + 1) * LANES:(2 * kv + 2) * LANES] = o1


def _ctx_attn(layer, q, k, v, sink, new_kv):
    has_sink = sink is not None
    first = new_kv is None
    row = lambda b: (b, 0)
    rows = CTX_SEQS * L_CTX
    in_specs = [pl.BlockSpec((rows, Q_W), row), pl.BlockSpec((rows, KV_W), row), pl.BlockSpec((rows, KV_W), row)]
    args = [q, k, v]
    if has_sink:
        in_specs = [pl.BlockSpec(memory_space=pltpu.SMEM)] + in_specs
        args = [sink] + args
    aliases = {}
    if first:
        kv_spec = pl.BlockSpec((CTX_SEQS, DEPTH, N_KV, L_CTX, HD), lambda b: (b, 0, 0, 0, 0))
    else:
        kv_spec = pl.BlockSpec((CTX_SEQS, 1, N_KV, L_CTX, HD), lambda b: (b, layer, 0, 0, 0))
        aliases = {len(args): 1, len(args) + 1: 2}
        in_specs = in_specs + [pl.BlockSpec(memory_space=pl.ANY)] * 2
        args = args + list(new_kv)
    kv_shape = jax.ShapeDtypeStruct((N_SEQ_CTX, DEPTH, N_KV, L_CTX, HD), F32)
    o, nk, nv = pl.pallas_call(
        functools.partial(_ctx_attn_kernel, has_sink, first),
        grid=(N_SEQ_CTX // CTX_SEQS,),
        in_specs=in_specs,
        out_specs=[pl.BlockSpec((rows, Q_W), row), kv_spec, kv_spec],
        out_shape=[jax.ShapeDtypeStruct((N_CTX_TOK, Q_W), BF16), kv_shape, kv_shape],
        input_output_aliases=aliases,
        compiler_params=_cparams(("arbitrary",)),
        name="ctx_attn_sink" if has_sink else "ctx_attn",
    )(*args)
    return o, (nk, nv)


def _lat_attn_kernel(windowed, *refs):
    if windowed:
        sink_ref, q_ref, k_ref, v_ref, ck_ref, cv_ref, dup_ref, o_ref, k2s, v2s, ck2s, cv2s = refs
    else:
        q_ref, k_ref, v_ref, ck_ref, cv_ref, dup_ref, o_ref, k2s, v2s, ck2s, cv2s = refs
        sink_ref = None
    j = pl.program_id(1)

    @pl.when(j == 0)
    def _():
        ka, kb = _dup_heads(k_ref[...].astype(F32))
        va, vb = _dup_heads(v_ref[...].astype(F32))
        k2s[0] = ka.astype(BF16)
        k2s[1] = kb.astype(BF16)
        v2s[0] = va.astype(BF16)
        v2s[1] = vb.astype(BF16)
        dup = dup_ref[...]
        for kv in range(N_KV):
            ck2s[kv] = _dot(ck_ref[0, 0, kv].astype(BF16), dup).astype(BF16)
            cv2s[kv] = _dot(cv_ref[0, 0, kv].astype(BF16), dup).astype(BF16)

    mask = None
    if windowed:
        n_loc = TQ + 2 * WINDOW
        start = pl.multiple_of(jnp.clip(j * TQ - WINDOW, 0, L_LAT - n_loc), WINDOW)
        qpos = j * TQ + lax.broadcasted_iota(jnp.int32, (TQ, n_loc), 0)
        kpos = start + lax.broadcasted_iota(jnp.int32, (TQ, n_loc), 1)
        mask = [None, jnp.abs(qpos - kpos) <= WINDOW]
    for kv in range(N_KV):
        if windowed:
            k_own = k2s[kv, pl.ds(start, n_loc), :]
            v_own = v2s[kv, pl.ds(start, n_loc), :]
        else:
            k_own = k2s[kv]
            v_own = v2s[kv]
        o0, o1 = _group_attention(q_ref, kv, [ck2s[kv], k_own], [cv2s[kv], v_own], sink_ref, mask, False)
        o_ref[:, (2 * kv) * LANES:(2 * kv + 1) * LANES] = o0
        o_ref[:, (2 * kv + 1) * LANES:(2 * kv + 2) * LANES] = o1


def _lat_attn(layer, q, k, v, cache_k, cache_v, dup, sink):
    windowed = sink is not None
    n_qt = L_LAT // TQ
    ctx_tiles = N_CTX_TOK // TQ
    ctx_seqs = N_CTX_TOK // L_LAT
    cache_spec = pl.BlockSpec((1, 1, N_KV, PAST, HD), lambda b, j: (b, layer, 0, 0, 0))
    in_specs = [
        pl.BlockSpec((TQ, Q_W), lambda b, j: (ctx_tiles + b * n_qt + j, 0)),
        pl.BlockSpec((L_LAT, KV_W), lambda b, j: (ctx_seqs + b, 0)),
        pl.BlockSpec((L_LAT, KV_W), lambda b, j: (ctx_seqs + b, 0)),
        cache_spec,
        cache_spec,
        pl.BlockSpec((HD, LANES), lambda b, j: (0, 0)),
    ]
    args = [q, k, v, cache_k, cache_v, dup]
    if windowed:
        in_specs = [pl.BlockSpec(memory_space=pltpu.SMEM)] + in_specs
        args = [sink] + args
    return pl.pallas_call(
        functools.partial(_lat_attn_kernel, windowed),
        grid=(N_SEQ_LAT, n_qt),
        in_specs=in_specs,
        out_specs=pl.BlockSpec((TQ, Q_W), lambda b, j: (b * n_qt + j, 0)),
        out_shape=jax.ShapeDtypeStruct((N_LAT_TOK, Q_W), BF16),
        scratch_shapes=[
            pltpu.VMEM((N_KV, L_LAT, LANES), BF16),
            pltpu.VMEM((N_KV, L_LAT, LANES), BF16),
            pltpu.VMEM((N_KV, PAST, LANES), BF16),
            pltpu.VMEM((N_KV, PAST, LANES), BF16),
        ],
        compiler_params=_cparams(("arbitrary", "arbitrary")),
        name="lat_attn_window" if windowed else "lat_attn",
    )(*args)


def _seq_pos(i, width):
    r = lax.broadcasted_iota(jnp.int32, (TM, width), 0)
    is_ctx = i < N_CTX_TOK // TM
    seq_len = jnp.where(is_ctx, L_CTX, L_LAT)
    return r & (seq_len - 1), seq_len


def _shift_rows(t, j, pos, seq_len):
    if j == 0:
        return t
    moved = pltpu.roll(t, (-j) % TM, axis=0)
    ok = (pos + j >= 0) & (pos + j < seq_len)
    return jnp.where(ok, moved, 0.0)


def _conv_mixer(zc_ref, cw_ref, pos, seq_len):
    bg = zc_ref[:, 0:CONV_W].astype(F32)
    u = zc_ref[:, CONV_W:2 * CONV_W].astype(F32) * zc_ref[:, 2 * CONV_W:3 * CONV_W].astype(F32)
    y = (_shift_rows(u, -1, pos, seq_len) * cw_ref[0:1, :] + u * cw_ref[1:2, :]
         + _shift_rows(u, 1, pos, seq_len) * cw_ref[2:3, :])
    return (bg * y).astype(BF16)


def _window_sum(x_seg, w):
    n = x_seg.shape[0] + 2 * POOL_PAD
    z = jnp.zeros((POOL_PAD, x_seg.shape[1]), F32)
    a = jnp.concatenate([z, x_seg, z], axis=0)
    a = a + pltpu.roll(a, 1, axis=0)
    half = 1
    while 2 * half < w:
        a = pltpu.roll(a, half, axis=0) + pltpu.roll(a, n - half, axis=0)
        half *= 2
    return a[POOL_PAD:POOL_PAD + x_seg.shape[0]]


def _pool_mixer(xd_ref, pw_ref, ps_ref, yd_ref, seq_len):
    t = lax.broadcasted_iota(jnp.int32, (seq_len, POOL_G), 0)
    for gi, w in enumerate(POOL_WINDOWS):
        cnt = (jnp.minimum(t + w // 2, seq_len) - jnp.maximum(t - w // 2, 0)).astype(F32)
        wg = pw_ref[gi].astype(BF16)
        lanes = slice(gi * POOL_G, (gi + 1) * POOL_G)
        for s in range(TM // seq_len):
            rows = slice(s * seq_len, (s + 1) * seq_len)
            xg = xd_ref[rows, lanes]
            d = _window_sum(xg, w) / cnt - xg
            yd_ref[rows, lanes] = (_dot(d.astype(BF16), wg) * ps_ref[:, lanes]).astype(BF16)


def _mix_out_kernel(even, nx, *refs):
    x_refs = refs[:nx]
    mod_ref, oc_ref, ol_ref, w_ref = refs[nx:nx + 4]
    rest = refs[nx + 4:]
    if even:
        zc_ref, cw_ref, out_ref, wbf = rest
    else:
        xd_ref, pw_ref, ps_ref, out_ref, wbf, yd_s = rest
    i = pl.program_id(0)

    @pl.when(i == 0)
    def _():
        _cast_rows(w_ref.at[0], wbf, D_MODEL)

    o = jnp.where(i < N_CTX_TOK // TM, oc_ref[...], ol_ref[...])
    if even:
        pos, seq_len = _seq_pos(i, CONV_W)
        ya = _conv_mixer(zc_ref, cw_ref, pos, seq_len)
        y = _dot(ya, wbf[0:CONV_W, :]) + _dot(o, wbf[CONV_W:, :])
    else:
        @pl.when(i < N_CTX_TOK // TM)
        def _():
            _pool_mixer(xd_ref, pw_ref, ps_ref, yd_s, L_CTX)

        @pl.when(i >= N_CTX_TOK // TM)
        def _():
            _pool_mixer(xd_ref, pw_ref, ps_ref, yd_s, L_LAT)

        y = _dot(o, wbf[0:Q_W, :]) + _dot(yd_s[...], wbf[Q_W:, :])
    out_ref[...] = _read_x(i, x_refs, TM) + mod_ref[0, 2:3, :] * y


def _mix_out(even, layer_i, x, mod_l, o_ctx, o_lat, w_out, *extra):
    n_ctx = N_CTX_TOK // TM
    n_lat = N_LAT_TOK // TM
    row = lambda i: (i, 0)
    const = lambda i: (0, 0)
    in_specs = _x_specs(TM, len(x) == 2) + [
        pl.BlockSpec((1, 6, D_MODEL), lambda i: (_mod_row(i, TM), 0, 0)),
        pl.BlockSpec((TM, Q_W), lambda i: (jnp.minimum(i, n_ctx - 1), 0)),
        pl.BlockSpec((TM, Q_W), lambda i: (jnp.clip(i - n_ctx, 0, n_lat - 1), 0)),
        pl.BlockSpec((1, D_MODEL, D_MODEL), lambda i: (layer_i, 0, 0)),
    ]
    if even:
        in_specs += [pl.BlockSpec((TM, 3 * CONV_W), row), pl.BlockSpec((3, CONV_W), const)]
    else:
        in_specs += [pl.BlockSpec((TM, POOL_W), row),
                     pl.BlockSpec((len(POOL_WINDOWS), POOL_G, POOL_G), lambda i: (0, 0, 0)),
                     pl.BlockSpec((1, POOL_W), const)]
    return pl.pallas_call(
        functools.partial(_mix_out_kernel, even, len(x)),
        grid=(N_TOK // TM,),
        in_specs=in_specs,
        out_specs=pl.BlockSpec((TM, D_MODEL), row),
        out_shape=jax.ShapeDtypeStruct((N_TOK, D_MODEL), F32),
        scratch_shapes=[pltpu.VMEM((D_MODEL, D_MODEL), BF16)] + ([] if even else [pltpu.VMEM((TM, POOL_W), BF16)]),
        compiler_params=_cparams(("arbitrary",)),
        name="mix_out_even" if even else "mix_out_odd",
    )(*x, mod_l, o_ctx, o_lat, w_out, *extra)


def _ffn_kernel(layer_i, x_ref, mod_ref, g_ref, w1_hbm, w3_hbm, w2_hbm, out_ref,
                w1s, w3s, w2s, st1, st3, st2, hs, hid, sem):
    i = pl.program_id(0)
    nf = D_FF // TF_FFN

    def chunk_copies(f, slot):
        cols = pl.ds(f * TF_FFN, TF_FFN)
        return (pltpu.make_async_copy(w1_hbm.at[layer_i, :, cols], st1.at[slot], sem.at[0, slot]),
                pltpu.make_async_copy(w3_hbm.at[layer_i, :, cols], st3.at[slot], sem.at[1, slot]),
                pltpu.make_async_copy(w2_hbm.at[layer_i, cols, :], st2.at[slot], sem.at[2, slot]))

    @pl.when(i == 0)
    def _():
        for c in chunk_copies(0, 0):
            c.start()

    hs[...] = _normmod(x_ref[...], g_ref[...], mod_ref[0, 4:5, :], mod_ref[0, 3:4, :]).astype(BF16)
    for f in range(nf):
        slot = f % 2
        lo, hi = f * TF_FFN, (f + 1) * TF_FFN

        @pl.when(i == 0)
        def _(f=f, slot=slot, lo=lo, hi=hi):
            if f + 1 < nf:
                for c in chunk_copies(f + 1, 1 - slot):
                    c.start()
            for c in chunk_copies(f, slot):
                c.wait()
            w1s[:, lo:hi] = st1[slot].astype(BF16)
            w3s[:, lo:hi] = st3[slot].astype(BF16)
            w2s[lo:hi, :] = st2[slot].astype(BF16)

        h = hs[...]
        hid[:, lo:hi] = (_silu(_dot(h, w1s[:, lo:hi])) * _dot(h, w3s[:, lo:hi])).astype(BF16)
    out_ref[...] = x_ref[...] + mod_ref[0, 5:6, :] * _dot(hid[...], w2s[...])


def _ffn(layer_i, x, mod_l, g, w1, w3, w2):
    row = lambda i: (i, 0)
    any_spec = pl.BlockSpec(memory_space=pl.ANY)
    return pl.pallas_call(
        functools.partial(_ffn_kernel, layer_i),
        grid=(N_TOK // TM,),
        in_specs=[
            pl.BlockSpec((TM, D_MODEL), row),
            pl.BlockSpec((1, 6, D_MODEL), lambda i: (_mod_row(i, TM), 0, 0)),
            pl.BlockSpec((1, D_MODEL), lambda i: (0, 0)),
            any_spec, any_spec, any_spec,
        ],
        out_specs=pl.BlockSpec((TM, D_MODEL), row),
        out_shape=jax.ShapeDtypeStruct((N_TOK, D_MODEL), F32),
        scratch_shapes=[
            pltpu.VMEM((D_MODEL, D_FF), BF16),
            pltpu.VMEM((D_MODEL, D_FF), BF16),
            pltpu.VMEM((D_FF, D_MODEL), BF16),
            pltpu.VMEM((2, D_MODEL, TF_FFN), F32),
            pltpu.VMEM((2, D_MODEL, TF_FFN), F32),
            pltpu.VMEM((2, TF_FFN, D_MODEL), F32),
            pltpu.VMEM((TM, D_MODEL), BF16),
            pltpu.VMEM((TM, D_FF), BF16),
            pltpu.SemaphoreType.DMA((3, 2)),
        ],
        compiler_params=pltpu.CompilerParams(dimension_semantics=("arbitrary",), vmem_limit_bytes=FFN_VMEM_LIMIT),
        name="ffn",
    )(x, mod_l, g, w1, w3, w2)


def _split_bf16(t):
    hi = t.astype(BF16)
    return hi, (t - hi.astype(F32)).astype(BF16)


def _lane_values(col_vals, ones_rows):
    q = jnp.floor(col_vals * (1.0 / 32.0))
    r = col_vals - 32.0 * q
    t = 32.0 * _dot_nt(ones_rows, q.astype(BF16)) + _dot_nt(ones_rows, r.astype(BF16))
    return t[0:1, :]


def _route_kernel(x_ref, mod_ref, g_ref, rw_ref, rb_ref, tri_ref, upper_ref, xl_ref, meta_ref, cnt_ref):
    h = _normmod(x_ref[...], g_ref[...], mod_ref[0, 4:5, :], mod_ref[0, 3:4, :])
    h_hi, h_lo = _split_bf16(h)
    w_hi, w_lo = _split_bf16(rw_ref[...])
    logits = _dot(h_hi, w_hi) + _dot(h_hi, w_lo) + _dot(h_lo, w_hi) + rb_ref[...]
    lane = lax.broadcasted_iota(jnp.int32, logits.shape, 1).astype(F32)
    logits = jnp.where(lane < N_EXP, logits, NEG_INF)
    m1 = logits.max(axis=-1, keepdims=True)
    i1 = jnp.where(logits == m1, lane, float(LANES)).min(axis=-1, keepdims=True)
    rest = jnp.where(lane == i1, NEG_INF, logits)
    m2 = rest.max(axis=-1, keepdims=True)
    i2 = jnp.where(rest == m2, lane, float(LANES)).min(axis=-1, keepdims=True)
    e2 = jnp.exp(m2 - m1)
    den = 1.0 + e2
    g1 = 1.0 / den
    g2 = e2 / den

    oh_a = jnp.where(lane == i1, 1.0, 0.0)
    oh_b = jnp.where(lane == i2, 1.0, 0.0)
    tri = tri_ref[...]
    cnt_a = oh_a.sum(axis=0, keepdims=True)
    cnt_b = oh_b.sum(axis=0, keepdims=True)
    run16 = jnp.floor((cnt_a + cnt_b + (RUN_ALIGN - 1)) * (1.0 / RUN_ALIGN))
    run16_rows = jnp.broadcast_to(run16, (8, LANES))
    start = RUN_ALIGN * _dot(run16_rows.astype(BF16), upper_ref[...])[0:1, :]
    row_a = oh_a * (start + _dot(tri, oh_a.astype(BF16)))
    row_b = oh_b * (start + cnt_a + _dot(tri, oh_b.astype(BF16)))

    ones_rows = jnp.ones((8, LANES), BF16)
    tok_a = _lane_values(row_a, ones_rows)
    tok_b = _lane_values(row_b, ones_rows)
    sorted_row = lax.broadcasted_iota(jnp.int32, (LOCAL_ROWS, TR), 0).astype(F32)
    perm = jnp.where((sorted_row == tok_a) | (sorted_row == tok_b), 1.0, 0.0).astype(BF16)
    xl_ref[...] = _dot(perm, h_hi).astype(BF16)

    meta = jnp.zeros_like(logits)
    cols = (row_a.sum(axis=-1, keepdims=True), row_b.sum(axis=-1, keepdims=True), g1, g2)
    for k, col in enumerate(cols):
        meta = jnp.where(lane == k, col, meta)
    meta_ref[...] = meta
    cnt_ref[...] = RUN_ALIGN * run16_rows


def _route(x, mod_l, g, router_w, router_b, tri, upper):
    row = lambda i: (i, 0)
    const = lambda i: (0, 0)
    n_tiles = N_TOK // TR
    return pl.pallas_call(
        _route_kernel,
        grid=(n_tiles,),
        in_specs=[
            pl.BlockSpec((TR, D_MODEL), row),
            pl.BlockSpec((1, 6, D_MODEL), lambda i: (_mod_row(i, TR), 0, 0)),
            pl.BlockSpec((1, D_MODEL), const),
            pl.BlockSpec((D_MODEL, LANES), const),
            pl.BlockSpec((1, LANES), const),
            pl.BlockSpec((TR, TR), const),
            pl.BlockSpec((LANES, LANES), const),
        ],
        out_specs=[pl.BlockSpec((LOCAL_ROWS, D_MODEL), row), pl.BlockSpec((TR, LANES), row),
                   pl.BlockSpec((8, LANES), row)],
        out_shape=[jax.ShapeDtypeStruct((n_tiles * LOCAL_ROWS, D_MODEL), BF16),
                   jax.ShapeDtypeStruct((N_TOK, LANES), F32),
                   jax.ShapeDtypeStruct((n_tiles * 8, LANES), F32)],
        compiler_params=_cparams(("arbitrary",)),
        name="moe_route",
    )(x, mod_l, g, router_w, router_b, tri, upper)


def _moe_plan(counts):
    n_tiles = N_TOK // TR
    run = counts.reshape(n_tiles, 8, LANES)[:, 0, :N_EXP].astype(jnp.int32)
    per_expert = jnp.sum(run, axis=0)
    region = (per_expert + (TMS - 1)) // TMS * TMS
    ends = jnp.cumsum(region)
    offs = ends - region
    seg_end = jnp.cumsum(run, axis=0)
    seg_start = seg_end - run
    local_end = jnp.cumsum(run, axis=1)
    local_start = local_end - run
    n_chunks = local_end[:, N_EXP - 1] // RUN_ALIGN
    tile_start = jnp.arange(MOE_TILES, dtype=jnp.int32) * TMS
    tile_expert = jnp.sum((tile_start[:, None] >= ends[None, :]).astype(jnp.int32), axis=1)
    tile_expert = jnp.minimum(tile_expert, N_EXP - 1)
    n_used = (ends[N_EXP - 1] // TMS).reshape(1)
    experts = jnp.arange(N_EXP, dtype=jnp.int32)
    g_row = jnp.arange(MOE_ROWS // RUN_ALIGN, dtype=jnp.int32) * RUN_ALIGN
    g_exp = jnp.repeat(tile_expert, TMS // RUN_ALIGN)
    pick = g_exp[:, None] == experts[None, :]
    rel = g_row - jnp.sum(jnp.where(pick, offs[None, :], 0), axis=1)
    ends_of = jnp.sum(jnp.where(pick[:, None, :], seg_end[None, :, :], 0), axis=2)
    src_tile = jnp.sum((rel[:, None] >= ends_of).astype(jnp.int32), axis=1)
    valid = (src_tile < n_tiles) & (jnp.repeat(jnp.arange(MOE_TILES), TMS // RUN_ALIGN) < n_used[0])
    src_tile = jnp.minimum(src_tile, n_tiles - 1)
    sel = (src_tile[:, None, None] == jnp.arange(n_tiles)[None, :, None]) & pick[:, None, :]
    shift = jnp.sum(jnp.where(sel, (local_start - seg_start)[None, :, :], 0), axis=(1, 2))
    chunk_src = src_tile * LOCAL_ROWS + rel + shift
    n_valid = jnp.sum(valid.reshape(MOE_TILES, TMS // RUN_ALIGN).astype(jnp.int32), axis=1)
    return chunk_src, n_valid, n_chunks, tile_expert, n_used


def _chunk_copy(src_ref, src_row, dst_ref, dst_row, sem):
    return pltpu.make_async_copy(src_ref.at[pl.ds(src_row, RUN_ALIGN), :],
                                 dst_ref.at[pl.ds(dst_row, RUN_ALIGN), :], sem)


CHUNKS = TMS // RUN_ALIGN
LOCAL_CHUNKS = LOCAL_ROWS // RUN_ALIGN


def _expert_kernel(te_ref, nu_ref, src_ref, nv_ref, nc_ref, xl_hbm, w1_ref, w3_ref, w2_ref, yl_hbm,
                   xbuf, ybuf, zbuf, w1s, w3s, w2s, sem_in, sem_out, sem_zero):
    t = pl.program_id(0)
    n_used = nu_ref[0]
    slot = t & 1

    def for_chunks(n, body):
        @pl.when(n == CHUNKS)
        def _():
            for j in range(CHUNKS):
                body(j)

        @pl.when(n < CHUNKS)
        def _():
            def step(j, c):
                body(j)
                return c

            lax.fori_loop(0, n, step, 0)

    def gather(tile, s):
        n = nv_ref[tile]

        def issue(j):
            src = pl.multiple_of(src_ref[tile * CHUNKS + j], RUN_ALIGN)
            _chunk_copy(xl_hbm, src, xbuf.at[s], pl.multiple_of(j * RUN_ALIGN, RUN_ALIGN), sem_in.at[s]).start()

        def pad(j, c):
            xbuf[s, pl.ds(pl.multiple_of(j * RUN_ALIGN, RUN_ALIGN), RUN_ALIGN), :] = jnp.zeros(
                (RUN_ALIGN, D_MODEL), BF16)
            return c

        for_chunks(n, issue)
        lax.fori_loop(n, CHUNKS, pad, 0)

    def scatter(tile, s):
        def issue(j):
            dst = pl.multiple_of(src_ref[tile * CHUNKS + j], RUN_ALIGN)
            _chunk_copy(ybuf.at[s], pl.multiple_of(j * RUN_ALIGN, RUN_ALIGN), yl_hbm, dst, sem_out.at[s]).start()

        for_chunks(nv_ref[tile], issue)

    def drain(count, sem):
        @pl.when(count == CHUNKS)
        def _():
            pltpu.make_async_copy(xl_hbm.at[pl.ds(0, TMS), :], xbuf.at[0], sem).wait()

        @pl.when(count != CHUNKS)
        def _():
            def one(j, c):
                _chunk_copy(xl_hbm, 0, xbuf.at[0], 0, sem).wait()
                return c

            lax.fori_loop(0, count, one, 0)

    @pl.when(t == 0)
    def _():
        zbuf[...] = jnp.zeros_like(zbuf)
        total = jnp.int32(0)
        for tile in range(N_TOK // TR):
            n = nc_ref[tile]

            def clear(j, c, tile=tile):
                row = pl.multiple_of(tile * LOCAL_ROWS + j * RUN_ALIGN, RUN_ALIGN)
                _chunk_copy(zbuf, 0, yl_hbm, row, sem_zero).start()
                return c

            lax.fori_loop(n, LOCAL_CHUNKS, clear, 0)
            total = total + (LOCAL_CHUNKS - n)
        drain(total, sem_zero)
        gather(0, 0)

    @pl.when(t < n_used)
    def _():
        drain(nv_ref[t], sem_in.at[slot])

    @pl.when(t + 1 < n_used)
    def _():
        gather(t + 1, 1 - slot)

    new_expert = (t == 0) | (te_ref[t] != te_ref[jnp.maximum(t - 1, 0)])

    @pl.when(new_expert)
    def _():
        _cast_rows(w1_ref.at[0, 0], w1s, D_MODEL)
        _cast_rows(w3_ref.at[0, 0], w3s, D_MODEL)
        _cast_rows(w2_ref.at[0, 0], w2s, D_FF_E)

    @pl.when((t >= 2) & (t - 2 < n_used))
    def _():
        drain(nv_ref[jnp.maximum(t - 2, 0)], sem_out.at[slot])

    @pl.when(t < n_used)
    def _():
        x = xbuf[slot]
        hid = (_silu(_dot(x, w1s[...])) * _dot(x, w3s[...])).astype(BF16)
        ybuf[slot] = _dot(hid, w2s[...]).astype(BF16)
        scatter(t, slot)

    @pl.when(t == pl.num_programs(0) - 1)
    def _():
        @pl.when((t >= 1) & (t - 1 < n_used))
        def _():
            drain(nv_ref[jnp.maximum(t - 1, 0)], sem_out.at[1 - slot])

        @pl.when(t < n_used)
        def _():
            drain(nv_ref[t], sem_out.at[slot])


def _experts(layer_i, plan, xl, w1, w3, w2):
    chunk_src, n_valid, n_chunks, tile_expert, n_used = plan
    wsel = lambda t, te, nu, src, nv, nc: (layer_i, te[t], 0, 0)
    any_spec = pl.BlockSpec(memory_space=pl.ANY)
    return pl.pallas_call(
        _expert_kernel,
        grid_spec=pltpu.PrefetchScalarGridSpec(
            num_scalar_prefetch=5,
            grid=(MOE_TILES,),
            in_specs=[
                any_spec,
                pl.BlockSpec((1, 1, D_MODEL, D_FF_E), wsel),
                pl.BlockSpec((1, 1, D_MODEL, D_FF_E), wsel),
                pl.BlockSpec((1, 1, D_FF_E, D_MODEL), wsel),
            ],
            out_specs=any_spec,
            scratch_shapes=[
                pltpu.VMEM((2, TMS, D_MODEL), BF16),
                pltpu.VMEM((2, TMS, D_MODEL), BF16),
                pltpu.VMEM((RUN_ALIGN, D_MODEL), BF16),
                pltpu.VMEM((D_MODEL, D_FF_E), BF16),
                pltpu.VMEM((D_MODEL, D_FF_E), BF16),
                pltpu.VMEM((D_FF_E, D_MODEL), BF16),
                pltpu.SemaphoreType.DMA((2,)),
                pltpu.SemaphoreType.DMA((2,)),
                pltpu.SemaphoreType.DMA(()),
            ],
        ),
        out_shape=jax.ShapeDtypeStruct(xl.shape, BF16),
        compiler_params=_cparams(("arbitrary",)),
        name="moe_experts",
    )(tile_expert, n_used, chunk_src, n_valid, n_chunks, xl, w1, w3, w2)


def _combine_kernel(x_ref, mod_ref, meta_ref, yl_ref, *out_refs):
    sorted_row = lax.broadcasted_iota(jnp.int32, (TR, LOCAL_ROWS), 1).astype(F32)
    y = yl_ref[...]
    pick_a = jnp.where(sorted_row == meta_ref[:, 0:1], 1.0, 0.0).astype(BF16)
    pick_b = jnp.where(sorted_row == meta_ref[:, 1:2], 1.0, 0.0).astype(BF16)
    f = meta_ref[:, 2:3] * _dot(pick_a, y) + meta_ref[:, 3:4] * _dot(pick_b, y)
    _write_x(pl.program_id(0), out_refs, TR, x_ref[...] + mod_ref[0, 5:6, :] * f)


def _combine(x, mod_l, meta, yl, split_out):
    row = lambda i: (i, 0)
    return pl.pallas_call(
        _combine_kernel,
        grid=(N_TOK // TR,),
        in_specs=[
            pl.BlockSpec((TR, D_MODEL), row),
            pl.BlockSpec((1, 6, D_MODEL), lambda i: (_mod_row(i, TR), 0, 0)),
            pl.BlockSpec((TR, LANES), row),
            pl.BlockSpec((LOCAL_ROWS, D_MODEL), row),
        ],
        out_specs=_x_specs(TR, split_out),
        out_shape=_x_shapes(split_out),
        compiler_params=_cparams(("arbitrary",)),
        name="moe_combine",
    )(x, mod_l, meta, yl)


def _moe(layer_i, x, mod_l, g, router_w, router_b, tri, upper, w1, w3, w2, split_out):
    xl, meta, counts = _route(x, mod_l, g, router_w, router_b, tri, upper)
    yl = _experts(layer_i, _moe_plan(counts), xl, w1, w3, w2)
    return _combine(x, mod_l, meta, yl, split_out)


def _rope_tables():
    n_rows = L_LAT // GRID_W
    rows = jnp.repeat(jnp.arange(n_rows, dtype=F32), GRID_W)
    cols = jnp.tile(jnp.arange(GRID_W, dtype=F32), n_rows)
    quarter = HD // 4
    inv = ROPE_THETA ** (-jnp.arange(quarter, dtype=F32) / quarter)
    ang_r = rows[:, None] * inv
    ang_c = cols[:, None] * inv
    cos = jnp.concatenate([jnp.cos(ang_r)] * 2 + [jnp.cos(ang_c)] * 2, axis=1)
    sin = jnp.concatenate([-jnp.sin(ang_r), jnp.sin(ang_r), -jnp.sin(ang_c), jnp.sin(ang_c)], axis=1)
    cos = jnp.concatenate([cos, cos], axis=1)
    sin = jnp.concatenate([sin, sin], axis=1)
    cos = jnp.concatenate([jnp.ones((L_LAT, LANES), F32), cos], axis=0)
    sin = jnp.concatenate([jnp.zeros((L_LAT, LANES), F32), sin], axis=0)
    return cos, sin


def _block_ones(width):
    r = jnp.arange(width) // HD
    return (r[:, None] == r[None, :]).astype(BF16)


def kernel(x_prompt, x_sample, cache_k, cache_v, c, c_ctx, norm1, norm2, w_mod, b_mod, ev_w_in, ev_conv, ev_q_norm, ev_k_norm, ev_w_out, od_w_in, od_q_norm, od_k_norm, od_sink, od_pool_w, od_pool_scale, od_w_out, ffn_w1, ffn_w3, ffn_w2, moe_router, moe_router_b, moe_w1, moe_w3, moe_w2):
    x = (x_prompt.reshape(N_CTX_TOK, D_MODEL), x_sample.reshape(N_LAT_TOK, D_MODEL))
    cond = jnp.concatenate([c_ctx[None, :], c, jnp.zeros((MOD_ROWS - 1 - N_SEQ_LAT, D_MODEL), F32)], axis=0)
    mod = _modulation(cond, w_mod, b_mod).reshape(DEPTH, MOD_ROWS, 6, D_MODEL)

    cos_tab, sin_tab = _rope_tables()
    ones_q = _block_ones(Q_W)
    ones_k = _block_ones(KV_W)
    eye = jnp.eye(HD, dtype=BF16)
    dup = jnp.concatenate([eye, eye], axis=1)
    tri = (jnp.arange(TR)[:, None] > jnp.arange(TR)[None, :]).astype(BF16)
    upper = (jnp.arange(LANES)[:, None] < jnp.arange(LANES)[None, :]).astype(BF16)

    new_kv = None
    for l in range(DEPTH):
        i = l // 2
        even = l % 2 == 0
        mod_l = mod[l]
        g1 = norm1[l][None, :]
        g2 = norm2[l][None, :]
        if even:
            q_gain, k_gain = ev_q_norm[i], ev_k_norm[i]
            w_in, w_out = ev_w_in, ev_w_out
        else:
            q_gain, k_gain = od_q_norm[i], od_k_norm[i]
            w_in, w_out = od_w_in, od_w_out
        q_gain = jnp.tile(q_gain, N_Q)[None, :]
        k_gain = jnp.tile(k_gain, N_KV)[None, :]
        outs = _in_proj(even, i, x, mod_l, g1, w_in, ones_q, ones_k, q_gain, k_gain, cos_tab, sin_tab)
        if even:
            zc, q, k, v = outs
            sink = None
        else:
            q, k, v, xd = outs
            sink = od_sink[i]
        o_ctx, new_kv = _ctx_attn(l, q, k, v, sink, new_kv)
        o_lat = _lat_attn(l, q, k, v, cache_k, cache_v, dup, sink)
        if even:
            x1 = _mix_out(True, i, x, mod_l, o_ctx, o_lat, w_out, zc, ev_conv[i])
            x = (_ffn(i, x1, mod_l, g2, ffn_w1, ffn_w3, ffn_w2),)
        else:
            x1 = _mix_out(False, i, x, mod_l, o_ctx, o_lat, w_out, xd, od_pool_w[i], od_pool_scale[i][None, :])
            rw = jnp.pad(moe_router[i], ((0, 0), (0, LANES - N_EXP)))
            rb = jnp.pad(moe_router_b[i], (0, LANES - N_EXP))[None, :]
            x = tuple(_moe(i, x1, mod_l, g2, rw, rb, tri, upper, moe_w1, moe_w3, moe_w2, l == DEPTH - 1))

    y_prompt = x[0].reshape(N_SEQ_CTX, L_CTX, D_MODEL)
    y_sample = x[1].reshape(N_SEQ_LAT, L_LAT, D_MODEL)
    return (y_prompt, y_sample, new_kv[0], new_kv[1])
```

```python
import functools

import jax
import jax.numpy as jnp
from jax import lax
from jax.experimental import pallas as pl
from jax.experimental.pallas import tpu as pltpu

F32 = jnp.float32
BF16 = jnp.bfloat16

D_MODEL = 1024
N_SEQ_CTX = 32
L_CTX = 256
N_SEQ_LAT = 4
L_LAT = 1024
DEPTH = 4
PAST = 512
GRID_W = 64
HD = 64
N_Q = 8
N_KV = 2
GROUP = N_Q // N_KV
Q_W = N_Q * HD
KV_W = N_KV * HD
CONV_W = 512
POOL_W = 512
POOL_WINDOWS = (2, 4, 8, 16)
POOL_G = 128
POOL_PAD = 16
EVEN_IN = 3 * CONV_W + Q_W + 2 * KV_W
ODD_IN = Q_W + 2 * KV_W + POOL_W
WINDOW = 128
D_FF = 2816
N_EXP = 8
TOP_K = 2
D_FF_E = 1024
ROPE_THETA = 10000.0
EPS = 1e-6

N_CTX_TOK = N_SEQ_CTX * L_CTX
N_LAT_TOK = N_SEQ_LAT * L_LAT
N_TOK = N_CTX_TOK + N_LAT_TOK
MOD_ROWS = 16

LANES = 128
VMEM_LIMIT = 56 * 1024 * 1024
FFN_VMEM_LIMIT = 62 * 1024 * 1024

TM_IN = 1024
TM = 1024
TQ = 256
CTX_SEQS = 4
TF_FFN = 256
TR = 512
RUN_ALIGN = 16
LOCAL_ROWS = TOP_K * TR + N_EXP * RUN_ALIGN
TMS = 512
MOE_TILES = -(-(TOP_K * N_TOK + (N_TOK // TR) * N_EXP * (RUN_ALIGN - 1) + N_EXP * (TMS - 1)) // TMS)
MOE_ROWS = MOE_TILES * TMS
NEG_INF = float("-inf")


def _cparams(sem):
    return pltpu.CompilerParams(dimension_semantics=sem, vmem_limit_bytes=VMEM_LIMIT)


def _mod_row(i, tm):
    n_ctx = N_CTX_TOK // tm
    return jnp.where(i < n_ctx, 0, 1 + (i - n_ctx) // (L_LAT // tm))


def _x_specs(tm, split):
    if not split:
        return [pl.BlockSpec((tm, D_MODEL), lambda i, *_: (i, 0))]
    n_ctx = N_CTX_TOK // tm
    n_lat = N_LAT_TOK // tm
    return [pl.BlockSpec((tm, D_MODEL), lambda i, *_: (jnp.minimum(i, n_ctx - 1), 0)),
            pl.BlockSpec((tm, D_MODEL), lambda i, *_: (jnp.clip(i - n_ctx, 0, n_lat - 1), 0))]


def _x_shapes(split):
    if not split:
        return [jax.ShapeDtypeStruct((N_TOK, D_MODEL), F32)]
    return [jax.ShapeDtypeStruct((N_CTX_TOK, D_MODEL), F32), jax.ShapeDtypeStruct((N_LAT_TOK, D_MODEL), F32)]


def _read_x(i, x_refs, tm):
    if len(x_refs) == 1:
        return x_refs[0][...]
    return jnp.where(i < N_CTX_TOK // tm, x_refs[0][...], x_refs[1][...])


def _write_x(i, o_refs, tm, val):
    if len(o_refs) == 1:
        o_refs[0][...] = val
        return

    @pl.when(i < N_CTX_TOK // tm)
    def _():
        o_refs[0][...] = val

    @pl.when(i >= N_CTX_TOK // tm)
    def _():
        o_refs[1][...] = val


def _normmod(x, g, scale, shift):
    ms = jnp.mean(x * x, axis=-1, keepdims=True)
    y = x * lax.rsqrt(ms + EPS) * g
    return y * (1.0 + scale) + shift


def _silu(x):
    return x * jax.nn.sigmoid(x)


def _dot(a, b):
    return jnp.dot(a, b, preferred_element_type=F32)


def _dot_nt(a, b):
    return lax.dot_general(a, b, (((1,), (1,)), ((), ())), preferred_element_type=F32)


def _cast_rows(src_ref, dst_ref, rows, chunk=256):
    for r in range(0, rows, chunk):
        dst_ref[r:r + chunk, :] = src_ref[r:r + chunk, :].astype(dst_ref.dtype)


def _mod_kernel(c_ref, w_ref, b_ref, o_ref):
    s = _silu(c_ref[...]).astype(BF16)
    o_ref[0] = _dot(s, w_ref[0].astype(BF16)) + b_ref[0]


def _modulation(cond, w_mod, b_mod):
    tn = 1536
    return pl.pallas_call(
        _mod_kernel,
        grid=(DEPTH, 6 * D_MODEL // tn),
        in_specs=[
            pl.BlockSpec((MOD_ROWS, D_MODEL), lambda l, j: (0, 0)),
            pl.BlockSpec((1, D_MODEL, tn), lambda l, j: (l, 0, j)),
            pl.BlockSpec((1, 1, tn), lambda l, j: (l, 0, j)),
        ],
        out_specs=pl.BlockSpec((1, MOD_ROWS, tn), lambda l, j: (l, 0, j)),
        out_shape=jax.ShapeDtypeStruct((DEPTH, MOD_ROWS, 6 * D_MODEL), F32),
        compiler_params=_cparams(("arbitrary", "arbitrary")),
        name="modulation",
    )(cond, w_mod, b_mod.reshape(DEPTH, 1, 6 * D_MODEL))


def _head_rms(t, ones_bd, gain):
    ssq = _dot((t * t).astype(BF16), ones_bd)
    return t * lax.rsqrt(ssq * (1.0 / HD) + EPS) * gain


def _rope(t, cos, sin_signed):
    lane = lax.broadcasted_iota(jnp.int32, (t.shape[0], LANES), 1)
    first = (lane & 31) < 16
    outs = []
    for c in range(t.shape[1] // LANES):
        tc = t[:, c * LANES:(c + 1) * LANES]
        nxt = pltpu.roll(tc, LANES - 16, axis=1)
        prv = pltpu.roll(tc, 16, axis=1)
        outs.append(tc * cos + jnp.where(first, nxt, prv) * sin_signed)
    return outs[0] if len(outs) == 1 else jnp.concatenate(outs, axis=1)


def _in_proj_kernel(even, nx, *refs):
    x_refs = refs[:nx]
    mod_ref, g_ref, w_ref, onesq_ref, onesk_ref, qg_ref, kg_ref, cos_ref, sin_ref = refs[nx:nx + 9]
    rest = refs[nx + 9:]
    if even:
        zc_ref, q_ref, k_ref, v_ref, wbf = rest
        q0 = 3 * CONV_W
    else:
        q_ref, k_ref, v_ref, xd_ref, wbf = rest
        q0 = 0
    k0 = q0 + Q_W
    v0 = k0 + KV_W

    @pl.when(pl.program_id(0) == 0)
    def _():
        _cast_rows(w_ref.at[0], wbf, D_MODEL)

    x = _read_x(pl.program_id(0), x_refs, TM_IN)
    h = _normmod(x, g_ref[...], mod_ref[0, 1:2, :], mod_ref[0, 0:1, :]).astype(BF16)
    cos = cos_ref[...]
    sin = sin_ref[...]

    q = _dot(h, wbf[:, q0:q0 + Q_W])
    q = _rope(_head_rms(q, onesq_ref[...], qg_ref[...]), cos, sin) * (HD ** -0.5)
    q_ref[...] = q.astype(BF16)

    k = _dot(h, wbf[:, k0:k0 + KV_W])
    k = _rope(_head_rms(k, onesk_ref[...], kg_ref[...]), cos, sin)
    k_ref[...] = k.astype(BF16)

    v_ref[...] = _dot(h, wbf[:, v0:v0 + KV_W]).astype(BF16)

    if even:
        for c in range(3):
            cols = slice(c * CONV_W, (c + 1) * CONV_W)
            zc_ref[:, cols] = _dot(h, wbf[:, cols]).astype(BF16)
    else:
        xd_ref[...] = _dot(h, wbf[:, v0 + KV_W:v0 + KV_W + POOL_W])


def _in_proj(even, layer_i, x, mod_l, g, w, ones_q, ones_k, q_gain, k_gain, cos_tab, sin_tab):
    tm = TM_IN
    n_in = EVEN_IN if even else ODD_IN
    n_ctx = N_CTX_TOK // tm
    per_seq = L_LAT // tm

    def rope_idx(i):
        return (jnp.where(i < n_ctx, 0, per_seq + (i - n_ctx) % per_seq), 0)

    row = lambda i: (i, 0)
    const = lambda i: (0, 0)
    in_specs = _x_specs(tm, len(x) == 2) + [
        pl.BlockSpec((1, 6, D_MODEL), lambda i: (_mod_row(i, tm), 0, 0)),
        pl.BlockSpec((1, D_MODEL), const),
        pl.BlockSpec((1, D_MODEL, n_in), lambda i: (layer_i, 0, 0)),
        pl.BlockSpec((Q_W, Q_W), const),
        pl.BlockSpec((KV_W, KV_W), const),
        pl.BlockSpec((1, Q_W), const),
        pl.BlockSpec((1, KV_W), const),
        pl.BlockSpec((tm, LANES), rope_idx),
        pl.BlockSpec((tm, LANES), rope_idx),
    ]
    qkv_specs = [pl.BlockSpec((tm, Q_W), row), pl.BlockSpec((tm, KV_W), row), pl.BlockSpec((tm, KV_W), row)]
    qkv_shapes = [jax.ShapeDtypeStruct((N_TOK, Q_W), BF16), jax.ShapeDtypeStruct((N_TOK, KV_W), BF16),
                  jax.ShapeDtypeStruct((N_TOK, KV_W), BF16)]
    if even:
        out_specs = [pl.BlockSpec((tm, 3 * CONV_W), row)] + qkv_specs
        out_shape = [jax.ShapeDtypeStruct((N_TOK, 3 * CONV_W), BF16)] + qkv_shapes
    else:
        out_specs = qkv_specs + [pl.BlockSpec((tm, POOL_W), row)]
        out_shape = qkv_shapes + [jax.ShapeDtypeStruct((N_TOK, POOL_W), F32)]
    return pl.pallas_call(
        functools.partial(_in_proj_kernel, even, len(x)),
        grid=(N_TOK // tm,),
        in_specs=in_specs,
        out_specs=out_specs,
        out_shape=out_shape,
        scratch_shapes=[pltpu.VMEM((D_MODEL, n_in), BF16)],
        compiler_params=_cparams(("arbitrary",)),
        name="in_proj_even" if even else "in_proj_odd",
    )(*x, mod_l, g, w, ones_q, ones_k, q_gain, k_gain, cos_tab, sin_tab)


def _dup_heads(t):
    lane = lax.broadcasted_iota(jnp.int32, t.shape, 1)
    swapped = pltpu.roll(t, HD, axis=1)
    low = lane < HD
    return jnp.where(low, t, swapped), jnp.where(low, swapped, t)


def _softmax_pv(scores, values, sink):
    m = scores[0].max(axis=-1, keepdims=True)
    for s in scores[1:]:
        m = jnp.maximum(m, s.max(axis=-1, keepdims=True))
    if sink is not None:
        m = jnp.maximum(m, sink)
    den = None
    acc = None
    for s, v in zip(scores, values):
        e = jnp.exp(s - m)
        d = e.sum(axis=-1, keepdims=True)
        a = _dot(e.astype(BF16), v)
        den = d if den is None else den + d
        acc = a if acc is None else acc + a
    if sink is not None:
        den = den + jnp.exp(sink - m)
    return acc / den


def _group_attention(q_ref, kv, keys, values, sink_ref, mask, stack):
    t = q_ref.shape[0]
    lane = lax.broadcasted_iota(jnp.int32, (t, LANES), 1)
    low = lane < HD
    zero = jnp.zeros((t, LANES), BF16)
    chunks = [q_ref[:, (kv * 2 + c) * LANES:(kv * 2 + c + 1) * LANES] for c in range(2)]
    if not stack:
        outs = []
        for h in range(GROUP):
            qm = jnp.where(low, chunks[h // 2], zero) if h % 2 == 0 else jnp.where(low, zero, chunks[h // 2])
            scores = [_dot_nt(qm, k) for k in keys]
            if mask is not None:
                scores = [s if m is None else jnp.where(m, s, NEG_INF) for s, m in zip(scores, mask)]
            sink = None if sink_ref is None else sink_ref[kv * GROUP + h]
            outs.append(_softmax_pv(scores, values, sink))
        return (jnp.where(low, outs[0], outs[1]).astype(BF16), jnp.where(low, outs[2], outs[3]).astype(BF16))
    qs = jnp.concatenate([jnp.where(low, chunks[0], zero), jnp.where(low, zero, chunks[0]),
                          jnp.where(low, chunks[1], zero), jnp.where(low, zero, chunks[1])], axis=0)
    scores = [_dot_nt(qs, k) for k in keys]
    if mask is not None:
        scores = [s if m is None else jnp.where(m, s, NEG_INF) for s, m in zip(scores, mask)]
    sink = None
    if sink_ref is not None:
        head = lax.broadcasted_iota(jnp.int32, (GROUP * t, 1), 0) // t
        sink = jnp.full((GROUP * t, 1), sink_ref[kv * GROUP], F32)
        for h in range(1, GROUP):
            sink = jnp.where(head == h, sink_ref[kv * GROUP + h], sink)
    out = _softmax_pv(scores, values, sink)
    return (jnp.where(low, out[0:t], out[t:2 * t]).astype(BF16),
            jnp.where(low, out[2 * t:3 * t], out[3 * t:4 * t]).astype(BF16))


def _ctx_attn_kernel(has_sink, first, *refs):
    if has_sink:
        sink_ref, refs = refs[0], refs[1:]
    else:
        sink_ref = None
    q_ref, k_ref, v_ref = refs[:3]
    o_ref, nk_ref, nv_ref = refs[-3:]
    lane = lax.broadcasted_iota(jnp.int32, (L_CTX, LANES), 1)
    low = lane < HD
    for s in range(CTX_SEQS):
        rows = pl.ds(s * L_CTX, L_CTX)
        k = k_ref[rows, :].astype(F32)
        v = v_ref[rows, :].astype(F32)
        k_sw = pltpu.roll(k, HD, axis=1)
        v_sw = pltpu.roll(v, HD, axis=1)
        nk_ref[s, 0, 0] = k[:, 0:HD]
        nk_ref[s, 0, 1] = k_sw[:, 0:HD]
        nv_ref[s, 0, 0] = v[:, 0:HD]
        nv_ref[s, 0, 1] = v_sw[:, 0:HD]
        if first:
            nk_ref[s, 1:] = jnp.zeros((DEPTH - 1, N_KV, L_CTX, HD), F32)
            nv_ref[s, 1:] = jnp.zeros((DEPTH - 1, N_KV, L_CTX, HD), F32)
        k2 = (jnp.where(low, k, k_sw).astype(BF16), jnp.where(low, k_sw, k).astype(BF16))
        v2 = (jnp.where(low, v, v_sw).astype(BF16), jnp.where(low, v_sw, v).astype(BF16))
        for kv in range(N_KV):
            o0, o1 = _group_attention(q_ref.at[rows, :], kv, [k2[kv]], [v2[kv]], sink_ref, None, True)
            o_ref[rows, (2 * kv) * LANES:(2 * kv + 1) * LANES] = o0
            o_ref[rows, (2 * kv + 1) * LANES:(2 * kv + 2) * LANES] = o1


def _ctx_attn(layer, q, k, v, sink, new_kv):
    has_sink = sink is not None
    first = new_kv is None
    row = lambda b: (b, 0)
    rows = CTX_SEQS * L_CTX
    in_specs = [pl.BlockSpec((rows, Q_W), row), pl.BlockSpec((rows, KV_W), row), pl.BlockSpec((rows, KV_W), row)]
    args = [q, k, v]
    if has_sink:
        in_specs = [pl.BlockSpec(memory_space=pltpu.SMEM)] + in_specs
        args = [sink] + args
    aliases = {}
    if first:
        kv_spec = pl.BlockSpec((CTX_SEQS, DEPTH, N_KV, L_CTX, HD), lambda b: (b, 0, 0, 0, 0))
    else:
        kv_spec = pl.BlockSpec((CTX_SEQS, 1, N_KV, L_CTX, HD), lambda b: (b, layer, 0, 0, 0))
        aliases = {len(args): 1, len(args) + 1: 2}
        in_specs = in_specs + [pl.BlockSpec(memory_space=pl.ANY)] * 2
        args = args + list(new_kv)
    kv_shape = jax.ShapeDtypeStruct((N_SEQ_CTX, DEPTH, N_KV, L_CTX, HD), F32)
    o, nk, nv = pl.pallas_call(
        functools.partial(_ctx_attn_kernel, has_sink, first),
        grid=(N_SEQ_CTX // CTX_SEQS,),
        in_specs=in_specs,
        out_specs=[pl.BlockSpec((rows, Q_W), row), kv_spec, kv_spec],
        out_shape=[jax.ShapeDtypeStruct((N_CTX_TOK, Q_W), BF16), kv_shape, kv_shape],
        input_output_aliases=aliases,
        compiler_params=_cparams(("arbitrary",)),
        name="ctx_attn_sink" if has_sink else "ctx_attn",
    )(*args)
    return o, (nk, nv)


def _lat_attn_kernel(windowed, *refs):
    if windowed:
        sink_ref, q_ref, k_ref, v_ref, ck_ref, cv_ref, dup_ref, o_ref, k2s, v2s, ck2s, cv2s = refs
    else:
        q_ref, k_ref, v_ref, ck_ref, cv_ref, dup_ref, o_ref, k2s, v2s, ck2s, cv2s = refs
        sink_ref = None
    j = pl.program_id(1)

    @pl.when(j == 0)
    def _():
        ka, kb = _dup_heads(k_ref[...].astype(F32))
        va, vb = _dup_heads(v_ref[...].astype(F32))
        k2s[0] = ka.astype(BF16)
        k2s[1] = kb.astype(BF16)
        v2s[0] = va.astype(BF16)
        v2s[1] = vb.astype(BF16)
        dup = dup_ref[...]
        for kv in range(N_KV):
            ck2s[kv] = _dot(ck_ref[0, 0, kv].astype(BF16), dup).astype(BF16)
            cv2s[kv] = _dot(cv_ref[0, 0, kv].astype(BF16), dup).astype(BF16)

    mask = None
    if windowed:
        n_loc = TQ + 2 * WINDOW
        start = pl.multiple_of(jnp.clip(j * TQ - WINDOW, 0, L_LAT - n_loc), WINDOW)
        qpos = j * TQ + lax.broadcasted_iota(jnp.int32, (TQ, n_loc), 0)
        kpos = start + lax.broadcasted_iota(jnp.int32, (TQ, n_loc), 1)
        mask = [None, jnp.abs(qpos - kpos) <= WINDOW]
    for kv in range(N_KV):
        if windowed:
            k_own = k2s[kv, pl.ds(start, n_loc), :]
            v_own = v2s[kv, pl.ds(start, n_loc), :]
        else:
            k_own = k2s[kv]
            v_own = v2s[kv]
        o0, o1 = _group_attention(q_ref, kv, [ck2s[kv], k_own], [cv2s[kv], v_own], sink_ref, mask, False)
        o_ref[:, (2 * kv) * LANES:(2 * kv + 1) * LANES] = o0
        o_ref[:, (2 * kv + 1) * LANES:(2 * kv + 2) * LANES] = o1


def _lat_attn(layer, q, k, v, cache_k, cache_v, dup, sink):
    windowed = sink is not None
    n_qt = L_LAT // TQ
    ctx_tiles = N_CTX_TOK // TQ
    ctx_seqs = N_CTX_TOK // L_LAT
    cache_spec = pl.BlockSpec((1, 1, N_KV, PAST, HD), lambda b, j: (b, layer, 0, 0, 0))
    in_specs = [
        pl.BlockSpec((TQ, Q_W), lambda b, j: (ctx_tiles + b * n_qt + j, 0)),
        pl.BlockSpec((L_LAT, KV_W), lambda b, j: (ctx_seqs + b, 0)),
        pl.BlockSpec((L_LAT, KV_W), lambda b, j: (ctx_seqs + b, 0)),
        cache_spec,
        cache_spec,
        pl.BlockSpec((HD, LANES), lambda b, j: (0, 0)),
    ]
    args = [q, k, v, cache_k, cache_v, dup]
    if windowed:
        in_specs = [pl.BlockSpec(memory_space=pltpu.SMEM)] + in_specs
        args = [sink] + args
    return pl.pallas_call(
        functools.partial(_lat_attn_kernel, windowed),
        grid=(N_SEQ_LAT, n_qt),
        in_specs=in_specs,
        out_specs=pl.BlockSpec((TQ, Q_W), lambda b, j: (b * n_qt + j, 0)),
        out_shape=jax.ShapeDtypeStruct((N_LAT_TOK, Q_W), BF16),
        scratch_shapes=[
            pltpu.VMEM((N_KV, L_LAT, LANES), BF16),
            pltpu.VMEM((N_KV, L_LAT, LANES), BF16),
            pltpu.VMEM((N_KV, PAST, LANES), BF16),
            pltpu.VMEM((N_KV, PAST, LANES), BF16),
        ],
        compiler_params=_cparams(("arbitrary", "arbitrary")),
        name="lat_attn_window" if windowed else "lat_attn",
    )(*args)


def _seq_pos(i, width):
    r = lax.broadcasted_iota(jnp.int32, (TM, width), 0)
    is_ctx = i < N_CTX_TOK // TM
    seq_len = jnp.where(is_ctx, L_CTX, L_LAT)
    return r & (seq_len - 1), seq_len


def _shift_rows(t, j, pos, seq_len):
    if j == 0:
        return t
    moved = pltpu.roll(t, (-j) % TM, axis=0)
    ok = (pos + j >= 0) & (pos + j < seq_len)
    return jnp.where(ok, moved, 0.0)


def _conv_mixer(zc_ref, cw_ref, pos, seq_len):
    bg = zc_ref[:, 0:CONV_W].astype(F32)
    u = zc_ref[:, CONV_W:2 * CONV_W].astype(F32) * zc_ref[:, 2 * CONV_W:3 * CONV_W].astype(F32)
    y = (_shift_rows(u, -1, pos, seq_len) * cw_ref[0:1, :] + u * cw_ref[1:2, :]
         + _shift_rows(u, 1, pos, seq_len) * cw_ref[2:3, :])
    return (bg * y).astype(BF16)


def _window_sum(x_seg, w):
    n = x_seg.shape[0] + 2 * POOL_PAD
    z = jnp.zeros((POOL_PAD, x_seg.shape[1]), F32)
    a = jnp.concatenate([z, x_seg, z], axis=0)
    a = a + pltpu.roll(a, 1, axis=0)
    half = 1
    while 2 * half < w:
        a = pltpu.roll(a, half, axis=0) + pltpu.roll(a, n - half, axis=0)
        half *= 2
    return a[POOL_PAD:POOL_PAD + x_seg.shape[0]]


def _pool_mixer(xd_ref, pw_ref, ps_ref, yd_ref, seq_len):
    t = lax.broadcasted_iota(jnp.int32, (seq_len, POOL_G), 0)
    for gi, w in enumerate(POOL_WINDOWS):
        cnt = (jnp.minimum(t + w // 2, seq_len) - jnp.maximum(t - w // 2, 0)).astype(F32)
        wg = pw_ref[gi].astype(BF16)
        lanes = slice(gi * POOL_G, (gi + 1) * POOL_G)
        for s in range(TM // seq_len):
            rows = slice(s * seq_len, (s + 1) * seq_len)
            xg = xd_ref[rows, lanes]
            d = _window_sum(xg, w) / cnt - xg
            yd_ref[rows, lanes] = (_dot(d.astype(BF16), wg) * ps_ref[:, lanes]).astype(BF16)


def _mix_out_kernel(even, nx, *refs):
    x_refs = refs[:nx]
    mod_ref, oc_ref, ol_ref, w_ref = refs[nx:nx + 4]
    rest = refs[nx + 4:]
    if even:
        zc_ref, cw_ref, out_ref, wbf = rest
    else:
        xd_ref, pw_ref, ps_ref, out_ref, wbf, yd_s = rest
    i = pl.program_id(0)

    @pl.when(i == 0)
    def _():
        _cast_rows(w_ref.at[0], wbf, D_MODEL)

    o = jnp.where(i < N_CTX_TOK // TM, oc_ref[...], ol_ref[...])
    if even:
        y_o = _dot(o, wbf[CONV_W:, :])
        pos, seq_len = _seq_pos(i, 1)
        ya = _conv_mixer(zc_ref, cw_ref, pos, seq_len)
        y = _dot(ya, wbf[0:CONV_W, :]) + y_o
    else:
        @pl.when(i < N_CTX_TOK // TM)
        def _():
            _pool_mixer(xd_ref, pw_ref, ps_ref, yd_s, L_CTX)

        @pl.when(i >= N_CTX_TOK // TM)
        def _():
            _pool_mixer(xd_ref, pw_ref, ps_ref, yd_s, L_LAT)

        y = _dot(o, wbf[0:Q_W, :]) + _dot(yd_s[...], wbf[Q_W:, :])
    out_ref[...] = _read_x(i, x_refs, TM) + mod_ref[0, 2:3, :] * y


def _mix_out(even, layer_i, x, mod_l, o_ctx, o_lat, w_out, *extra):
    n_ctx = N_CTX_TOK // TM
    n_lat = N_LAT_TOK // TM
    row = lambda i: (i, 0)
    const = lambda i: (0, 0)
    in_specs = _x_specs(TM, len(x) == 2) + [
        pl.BlockSpec((1, 6, D_MODEL), lambda i: (_mod_row(i, TM), 0, 0)),
        pl.BlockSpec((TM, Q_W), lambda i: (jnp.minimum(i, n_ctx - 1), 0)),
        pl.BlockSpec((TM, Q_W), lambda i: (jnp.clip(i - n_ctx, 0, n_lat - 1), 0)),
        pl.BlockSpec((1, D_MODEL, D_MODEL), lambda i: (layer_i, 0, 0)),
    ]
    if even:
        in_specs += [pl.BlockSpec((TM, 3 * CONV_W), row), pl.BlockSpec((3, CONV_W), const)]
    else:
        in_specs += [pl.BlockSpec((TM, POOL_W), row),
                     pl.BlockSpec((len(POOL_WINDOWS), POOL_G, POOL_G), lambda i: (0, 0, 0)),
                     pl.BlockSpec((1, POOL_W), const)]
    return pl.pallas_call(
        functools.partial(_mix_out_kernel, even, len(x)),
        grid=(N_TOK // TM,),
        in_specs=in_specs,
        out_specs=pl.BlockSpec((TM, D_MODEL), row),
        out_shape=jax.ShapeDtypeStruct((N_TOK, D_MODEL), F32),
        scratch_shapes=[pltpu.VMEM((D_MODEL, D_MODEL), BF16)] + ([] if even else [pltpu.VMEM((TM, POOL_W), BF16)]),
        compiler_params=_cparams(("arbitrary",)),
        name="mix_out_even" if even else "mix_out_odd",
    )(*x, mod_l, o_ctx, o_lat, w_out, *extra)


def _ffn_kernel(layer_i, x_ref, mod_ref, g_ref, w1_hbm, w3_hbm, w2_hbm, out_ref,
                w1s, w3s, w2s, st1, st3, st2, hs, hid, sem):
    i = pl.program_id(0)
    nf = D_FF // TF_FFN

    def chunk_copies(f, slot):
        cols = pl.ds(f * TF_FFN, TF_FFN)
        return (pltpu.make_async_copy(w1_hbm.at[layer_i, :, cols], st1.at[slot], sem.at[0, slot]),
                pltpu.make_async_copy(w3_hbm.at[layer_i, :, cols], st3.at[slot], sem.at[1, slot]),
                pltpu.make_async_copy(w2_hbm.at[layer_i, cols, :], st2.at[slot], sem.at[2, slot]))

    @pl.when(i == 0)
    def _():
        for c in chunk_copies(0, 0):
            c.start()

    hs[...] = _normmod(x_ref[...], g_ref[...], mod_ref[0, 4:5, :], mod_ref[0, 3:4, :]).astype(BF16)
    for f in range(nf):
        slot = f % 2
        lo, hi = f * TF_FFN, (f + 1) * TF_FFN

        @pl.when(i == 0)
        def _(f=f, slot=slot, lo=lo, hi=hi):
            if f + 1 < nf:
                for c in chunk_copies(f + 1, 1 - slot):
                    c.start()
            for c in chunk_copies(f, slot):
                c.wait()
            w1s[:, lo:hi] = st1[slot].astype(BF16)
            w3s[:, lo:hi] = st3[slot].astype(BF16)
            w2s[lo:hi, :] = st2[slot].astype(BF16)

        h = hs[...]
        hid[:, lo:hi] = (_silu(_dot(h, w1s[:, lo:hi])) * _dot(h, w3s[:, lo:hi])).astype(BF16)
    out_ref[...] = x_ref[...] + mod_ref[0, 5:6, :] * _dot(hid[...], w2s[...])


def _ffn(layer_i, x, mod_l, g, w1, w3, w2):
    row = lambda i: (i, 0)
    any_spec = pl.BlockSpec(memory_space=pl.ANY)
    return pl.pallas_call(
        functools.partial(_ffn_kernel, layer_i),
        grid=(N_TOK // TM,),
        in_specs=[
            pl.BlockSpec((TM, D_MODEL), row),
            pl.BlockSpec((1, 6, D_MODEL), lambda i: (_mod_row(i, TM), 0, 0)),
            pl.BlockSpec((1, D_MODEL), lambda i: (0, 0)),
            any_spec, any_spec, any_spec,
        ],
        out_specs=pl.BlockSpec((TM, D_MODEL), row),
        out_shape=jax.ShapeDtypeStruct((N_TOK, D_MODEL), F32),
        scratch_shapes=[
            pltpu.VMEM((D_MODEL, D_FF), BF16),
            pltpu.VMEM((D_MODEL, D_FF), BF16),
            pltpu.VMEM((D_FF, D_MODEL), BF16),
            pltpu.VMEM((2, D_MODEL, TF_FFN), F32),
            pltpu.VMEM((2, D_MODEL, TF_FFN), F32),
            pltpu.VMEM((2, TF_FFN, D_MODEL), F32),
            pltpu.VMEM((TM, D_MODEL), BF16),
            pltpu.VMEM((TM, D_FF), BF16),
            pltpu.SemaphoreType.DMA((3, 2)),
        ],
        compiler_params=pltpu.CompilerParams(dimension_semantics=("arbitrary",), vmem_limit_bytes=FFN_VMEM_LIMIT),
        name="ffn",
    )(x, mod_l, g, w1, w3, w2)


def _split_bf16(t):
    hi = t.astype(BF16)
    return hi, (t - hi.astype(F32)).astype(BF16)


def _lane_values(col_vals, ones_rows):
    q = jnp.floor(col_vals * (1.0 / 32.0))
    r = col_vals - 32.0 * q
    t = 32.0 * _dot_nt(ones_rows, q.astype(BF16)) + _dot_nt(ones_rows, r.astype(BF16))
    return t[0:1, :]


def _route_kernel(x_ref, mod_ref, g_ref, rw_ref, rb_ref, tri_ref, upper_ref, xl_ref, meta_ref, cnt_ref):
    h = _normmod(x_ref[...], g_ref[...], mod_ref[0, 4:5, :], mod_ref[0, 3:4, :])
    h_hi, h_lo = _split_bf16(h)
    w_hi, w_lo = _split_bf16(rw_ref[...])
    logits = _dot(h_hi, w_hi) + _dot(h_hi, w_lo) + _dot(h_lo, w_hi) + rb_ref[...]
    lane = lax.broadcasted_iota(jnp.int32, logits.shape, 1).astype(F32)
    logits = jnp.where(lane < N_EXP, logits, NEG_INF)
    m1 = logits.max(axis=-1, keepdims=True)
    i1 = jnp.where(logits == m1, lane, float(LANES)).min(axis=-1, keepdims=True)
    rest = jnp.where(lane == i1, NEG_INF, logits)
    m2 = rest.max(axis=-1, keepdims=True)
    i2 = jnp.where(rest == m2, lane, float(LANES)).min(axis=-1, keepdims=True)
    e2 = jnp.exp(m2 - m1)
    den = 1.0 + e2
    g1 = 1.0 / den
    g2 = e2 / den

    oh_a = jnp.where(lane == i1, 1.0, 0.0)
    oh_b = jnp.where(lane == i2, 1.0, 0.0)
    tri = tri_ref[...]
    cnt_a = oh_a.sum(axis=0, keepdims=True)
    cnt_b = oh_b.sum(axis=0, keepdims=True)
    run16 = jnp.floor((cnt_a + cnt_b + (RUN_ALIGN - 1)) * (1.0 / RUN_ALIGN))
    run16_rows = jnp.broadcast_to(run16, (8, LANES))
    start = RUN_ALIGN * _dot(run16_rows.astype(BF16), upper_ref[...])[0:1, :]
    row_a = oh_a * (start + _dot(tri, oh_a.astype(BF16)))
    row_b = oh_b * (start + cnt_a + _dot(tri, oh_b.astype(BF16)))

    ones_rows = jnp.ones((8, LANES), BF16)
    tok_a = _lane_values(row_a, ones_rows)
    tok_b = _lane_values(row_b, ones_rows)
    sorted_row = lax.broadcasted_iota(jnp.int32, (LOCAL_ROWS, TR), 0).astype(F32)
    perm = jnp.where((sorted_row == tok_a) | (sorted_row == tok_b), 1.0, 0.0).astype(BF16)
    xl_ref[...] = _dot(perm, h_hi).astype(BF16)

    meta = jnp.zeros_like(logits)
    cols = (row_a.sum(axis=-1, keepdims=True), row_b.sum(axis=-1, keepdims=True), g1, g2)
    for k, col in enumerate(cols):
        meta = jnp.where(lane == k, col, meta)
    meta_ref[...] = meta
    cnt_ref[...] = RUN_ALIGN * run16_rows


def _route(x, mod_l, g, router_w, router_b, tri, upper):
    row = lambda i: (i, 0)
    const = lambda i: (0, 0)
    n_tiles = N_TOK // TR
    return pl.pallas_call(
        _route_kernel,
        grid=(n_tiles,),
        in_specs=[
            pl.BlockSpec((TR, D_MODEL), row),
            pl.BlockSpec((1, 6, D_MODEL), lambda i: (_mod_row(i, TR), 0, 0)),
            pl.BlockSpec((1, D_MODEL), const),
            pl.BlockSpec((D_MODEL, LANES), const),
            pl.BlockSpec((1, LANES), const),
            pl.BlockSpec((TR, TR), const),
            pl.BlockSpec((LANES, LANES), const),
        ],
        out_specs=[pl.BlockSpec((LOCAL_ROWS, D_MODEL), row), pl.BlockSpec((TR, LANES), row),
                   pl.BlockSpec((8, LANES), row)],
        out_shape=[jax.ShapeDtypeStruct((n_tiles * LOCAL_ROWS, D_MODEL), BF16),
                   jax.ShapeDtypeStruct((N_TOK, LANES), F32),
                   jax.ShapeDtypeStruct((n_tiles * 8, LANES), F32)],
        compiler_params=_cparams(("arbitrary",)),
        name="moe_route",
    )(x, mod_l, g, router_w, router_b, tri, upper)


def _moe_plan(counts):
    n_tiles = N_TOK // TR
    run = counts.reshape(n_tiles, 8, LANES)[:, 0, :N_EXP].astype(jnp.int32)
    per_expert = jnp.sum(run, axis=0)
    region = (per_expert + (TMS - 1)) // TMS * TMS
    ends = jnp.cumsum(region)
    offs = ends - region
    seg_end = jnp.cumsum(run, axis=0)
    seg_start = seg_end - run
    local_end = jnp.cumsum(run, axis=1)
    local_start = local_end - run
    n_chunks = local_end[:, N_EXP - 1] // RUN_ALIGN
    tile_start = jnp.arange(MOE_TILES, dtype=jnp.int32) * TMS
    tile_expert = jnp.sum((tile_start[:, None] >= ends[None, :]).astype(jnp.int32), axis=1)
    tile_expert = jnp.minimum(tile_expert, N_EXP - 1)
    n_used = (ends[N_EXP - 1] // TMS).reshape(1)
    experts = jnp.arange(N_EXP, dtype=jnp.int32)
    g_row = jnp.arange(MOE_ROWS // RUN_ALIGN, dtype=jnp.int32) * RUN_ALIGN
    g_exp = jnp.repeat(tile_expert, TMS // RUN_ALIGN)
    pick = g_exp[:, None] == experts[None, :]
    rel = g_row - jnp.sum(jnp.where(pick, offs[None, :], 0), axis=1)
    ends_of = jnp.sum(jnp.where(pick[:, None, :], seg_end[None, :, :], 0), axis=2)
    src_tile = jnp.sum((rel[:, None] >= ends_of).astype(jnp.int32), axis=1)
    valid = (src_tile < n_tiles) & (jnp.repeat(jnp.arange(MOE_TILES), TMS // RUN_ALIGN) < n_used[0])
    src_tile = jnp.minimum(src_tile, n_tiles - 1)
    sel = (src_tile[:, None, None] == jnp.arange(n_tiles)[None, :, None]) & pick[:, None, :]
    shift = jnp.sum(jnp.where(sel, (local_start - seg_start)[None, :, :], 0), axis=(1, 2))
    chunk_src = src_tile * LOCAL_ROWS + rel + shift
    n_valid = jnp.sum(valid.reshape(MOE_TILES, TMS // RUN_ALIGN).astype(jnp.int32), axis=1)
    return chunk_src, n_valid, n_chunks, tile_expert, n_used


def _chunk_copy(src_ref, src_row, dst_ref, dst_row, sem):
    return pltpu.make_async_copy(src_ref.at[pl.ds(src_row, RUN_ALIGN), :],
                                 dst_ref.at[pl.ds(dst_row, RUN_ALIGN), :], sem)


CHUNKS = TMS // RUN_ALIGN
LOCAL_CHUNKS = LOCAL_ROWS // RUN_ALIGN


def _expert_kernel(te_ref, nu_ref, src_ref, nv_ref, nc_ref, xl_hbm, w1_ref, w3_ref, w2_ref, yl_hbm,
                   xbuf, ybuf, zbuf, w1s, w3s, w2s, sem_in, sem_out, sem_zero):
    t = pl.program_id(0)
    n_used = nu_ref[0]
    slot = t & 1

    def for_chunks(n, body):
        @pl.when(n == CHUNKS)
        def _():
            for j in range(CHUNKS):
                body(j)

        @pl.when(n < CHUNKS)
        def _():
            def step(j, c):
                body(j)
                return c

            lax.fori_loop(0, n, step, 0)

    def gather(tile, s):
        n = nv_ref[tile]

        def issue(j):
            src = pl.multiple_of(src_ref[tile * CHUNKS + j], RUN_ALIGN)
            _chunk_copy(xl_hbm, src, xbuf.at[s], pl.multiple_of(j * RUN_ALIGN, RUN_ALIGN), sem_in.at[s]).start()

        def pad(j, c):
            xbuf[s, pl.ds(pl.multiple_of(j * RUN_ALIGN, RUN_ALIGN), RUN_ALIGN), :] = jnp.zeros(
                (RUN_ALIGN, D_MODEL), BF16)
            return c

        for_chunks(n, issue)
        lax.fori_loop(n, CHUNKS, pad, 0)

    def scatter(tile, s):
        def issue(j):
            dst = pl.multiple_of(src_ref[tile * CHUNKS + j], RUN_ALIGN)
            _chunk_copy(ybuf.at[s], pl.multiple_of(j * RUN_ALIGN, RUN_ALIGN), yl_hbm, dst, sem_out.at[s]).start()

        for_chunks(nv_ref[tile], issue)

    def drain(count, sem):
        @pl.when(count == CHUNKS)
        def _():
            pltpu.make_async_copy(xl_hbm.at[pl.ds(0, TMS), :], xbuf.at[0], sem).wait()

        @pl.when(count != CHUNKS)
        def _():
            def one(j, c):
                _chunk_copy(xl_hbm, 0, xbuf.at[0], 0, sem).wait()
                return c

            lax.fori_loop(0, count, one, 0)

    @pl.when(t == 0)
    def _():
        zbuf[...] = jnp.zeros_like(zbuf)
        total = jnp.int32(0)
        for tile in range(N_TOK // TR):
            n = nc_ref[tile]

            def clear(j, c, tile=tile):
                row = pl.multiple_of(tile * LOCAL_ROWS + j * RUN_ALIGN, RUN_ALIGN)
                _chunk_copy(zbuf, 0, yl_hbm, row, sem_zero).start()
                return c

            lax.fori_loop(n, LOCAL_CHUNKS, clear, 0)
            total = total + (LOCAL_CHUNKS - n)
        drain(total, sem_zero)
        gather(0, 0)

    @pl.when(t < n_used)
    def _():
        drain(nv_ref[t], sem_in.at[slot])

    @pl.when(t + 1 < n_used)
    def _():
        gather(t + 1, 1 - slot)

    new_expert = (t == 0) | (te_ref[t] != te_ref[jnp.maximum(t - 1, 0)])

    @pl.when(new_expert)
    def _():
        _cast_rows(w1_ref.at[0, 0], w1s, D_MODEL)
        _cast_rows(w3_ref.at[0, 0], w3s, D_MODEL)
        _cast_rows(w2_ref.at[0, 0], w2s, D_FF_E)

    @pl.when((t >= 2) & (t - 2 < n_used))
    def _():
        drain(nv_ref[jnp.maximum(t - 2, 0)], sem_out.at[slot])

    @pl.when(t < n_used)
    def _():
        x = xbuf[slot]
        hid = (_silu(_dot(x, w1s[...])) * _dot(x, w3s[...])).astype(BF16)
        ybuf[slot] = _dot(hid, w2s[...]).astype(BF16)
        scatter(t, slot)

    @pl.when(t == pl.num_programs(0) - 1)
    def _():
        @pl.when((t >= 1) & (t - 1 < n_used))
        def _():
            drain(nv_ref[jnp.maximum(t - 1, 0)], sem_out.at[1 - slot])

        @pl.when(t < n_used)
        def _():
            drain(nv_ref[t], sem_out.at[slot])


def _experts(layer_i, plan, xl, w1, w3, w2):
    chunk_src, n_valid, n_chunks, tile_expert, n_used = plan
    wsel = lambda t, te, nu, src, nv, nc: (layer_i, te[t], 0, 0)
    any_spec = pl.BlockSpec(memory_space=pl.ANY)
    return pl.pallas_call(
        _expert_kernel,
        grid_spec=pltpu.PrefetchScalarGridSpec(
            num_scalar_prefetch=5,
            grid=(MOE_TILES,),
            in_specs=[
                any_spec,
                pl.BlockSpec((1, 1, D_MODEL, D_FF_E), wsel),
                pl.BlockSpec((1, 1, D_MODEL, D_FF_E), wsel),
                pl.BlockSpec((1, 1, D_FF_E, D_MODEL), wsel),
            ],
            out_specs=any_spec,
            scratch_shapes=[
                pltpu.VMEM((2, TMS, D_MODEL), BF16),
                pltpu.VMEM((2, TMS, D_MODEL), BF16),
                pltpu.VMEM((RUN_ALIGN, D_MODEL), BF16),
                pltpu.VMEM((D_MODEL, D_FF_E), BF16),
                pltpu.VMEM((D_MODEL, D_FF_E), BF16),
                pltpu.VMEM((D_FF_E, D_MODEL), BF16),
                pltpu.SemaphoreType.DMA((2,)),
                pltpu.SemaphoreType.DMA((2,)),
                pltpu.SemaphoreType.DMA(()),
            ],
        ),
        out_shape=jax.ShapeDtypeStruct(xl.shape, BF16),
        compiler_params=_cparams(("arbitrary",)),
        name="moe_experts",
    )(tile_expert, n_used, chunk_src, n_valid, n_chunks, xl, w1, w3, w2)


def _combine_kernel(x_ref, mod_ref, meta_ref, yl_ref, *out_refs):
    sorted_row = lax.broadcasted_iota(jnp.int32, (TR, LOCAL_ROWS), 1).astype(F32)
    y = yl_ref[...]
    pick_a = jnp.where(sorted_row == meta_ref[:, 0:1], 1.0, 0.0).astype(BF16)
    pick_b = jnp.where(sorted_row == meta_ref[:, 1:2], 1.0, 0.0).astype(BF16)
    f = meta_ref[:, 2:3] * _dot(pick_a, y) + meta_ref[:, 3:4] * _dot(pick_b, y)
    _write_x(pl.program_id(0), out_refs, TR, x_ref[...] + mod_ref[0, 5:6, :] * f)


def _combine(x, mod_l, meta, yl, split_out):
    row = lambda i: (i, 0)
    return pl.pallas_call(
        _combine_kernel,
        grid=(N_TOK // TR,),
        in_specs=[
            pl.BlockSpec((TR, D_MODEL), row),
            pl.BlockSpec((1, 6, D_MODEL), lambda i: (_mod_row(i, TR), 0, 0)),
            pl.BlockSpec((TR, LANES), row),
            pl.BlockSpec((LOCAL_ROWS, D_MODEL), row),
        ],
        out_specs=_x_specs(TR, split_out),
        out_shape=_x_shapes(split_out),
        compiler_params=_cparams(("arbitrary",)),
        name="moe_combine",
    )(x, mod_l, meta, yl)


def _moe(layer_i, x, mod_l, g, router_w, router_b, tri, upper, w1, w3, w2, split_out):
    xl, meta, counts = _route(x, mod_l, g, router_w, router_b, tri, upper)
    yl = _experts(layer_i, _moe_plan(counts), xl, w1, w3, w2)
    return _combine(x, mod_l, meta, yl, split_out)


def _rope_tables():
    n_rows = L_LAT // GRID_W
    rows = jnp.repeat(jnp.arange(n_rows, dtype=F32), GRID_W)
    cols = jnp.tile(jnp.arange(GRID_W, dtype=F32), n_rows)
    quarter = HD // 4
    inv = ROPE_THETA ** (-jnp.arange(quarter, dtype=F32) / quarter)
    ang_r = rows[:, None] * inv
    ang_c = cols[:, None] * inv
    cos = jnp.concatenate([jnp.cos(ang_r)] * 2 + [jnp.cos(ang_c)] * 2, axis=1)
    sin = jnp.concatenate([-jnp.sin(ang_r), jnp.sin(ang_r), -jnp.sin(ang_c), jnp.sin(ang_c)], axis=1)
    cos = jnp.concatenate([cos, cos], axis=1)
    sin = jnp.concatenate([sin, sin], axis=1)
    cos = jnp.concatenate([jnp.ones((L_LAT, LANES), F32), cos], axis=0)
    sin = jnp.concatenate([jnp.zeros((L_LAT, LANES), F32), sin], axis=0)
    return cos, sin


def _block_ones(width):
    r = jnp.arange(width) // HD
    return (r[:, None] == r[None, :]).astype(BF16)


def kernel(x_prompt, x_sample, cache_k, cache_v, c, c_ctx, norm1, norm2, w_mod, b_mod, ev_w_in, ev_conv, ev_q_norm, ev_k_norm, ev_w_out, od_w_in, od_q_norm, od_k_norm, od_sink, od_pool_w, od_pool_scale, od_w_out, ffn_w1, ffn_w3, ffn_w2, moe_router, moe_router_b, moe_w1, moe_w3, moe_w2):
    x = (x_prompt.reshape(N_CTX_TOK, D_MODEL), x_sample.reshape(N_LAT_TOK, D_MODEL))
    cond = jnp.concatenate([c_ctx[None, :], c, jnp.zeros((MOD_ROWS - 1 - N_SEQ_LAT, D_MODEL), F32)], axis=0)
    mod = _modulation(cond, w_mod, b_mod).reshape(DEPTH, MOD_ROWS, 6, D_MODEL)

    cos_tab, sin_tab = _rope_tables()
    ones_q = _block_ones(Q_W)
    ones_k = _block_ones(KV_W)
    eye = jnp.eye(HD, dtype=BF16)
    dup = jnp.concatenate([eye, eye], axis=1)
    tri = (jnp.arange(TR)[:, None] > jnp.arange(TR)[None, :]).astype(BF16)
    upper = (jnp.arange(LANES)[:, None] < jnp.arange(LANES)[None, :]).astype(BF16)

    new_kv = None
    for l in range(DEPTH):
        i = l // 2
        even = l % 2 == 0
        mod_l = mod[l]
        g1 = norm1[l][None, :]
        g2 = norm2[l][None, :]
        if even:
            q_gain, k_gain = ev_q_norm[i], ev_k_norm[i]
            w_in, w_out = ev_w_in, ev_w_out
        else:
            q_gain, k_gain = od_q_norm[i], od_k_norm[i]
            w_in, w_out = od_w_in, od_w_out
        q_gain = jnp.tile(q_gain, N_Q)[None, :]
        k_gain = jnp.tile(k_gain, N_KV)[None, :]
        outs = _in_proj(even, i, x, mod_l, g1, w_in, ones_q, ones_k, q_gain, k_gain, cos_tab, sin_tab)
        if even:
            zc, q, k, v = outs
            sink = None
        else:
            q, k, v, xd = outs
            sink = od_sink[i]
        o_ctx, new_kv = _ctx_attn(l, q, k, v, sink, new_kv)
        o_lat = _lat_attn(l, q, k, v, cache_k, cache_v, dup, sink)
        if even:
            x1 = _mix_out(True, i, x, mod_l, o_ctx, o_lat, w_out, zc, ev_conv[i])
            x = (_ffn(i, x1, mod_l, g2, ffn_w1, ffn_w3, ffn_w2),)
        else:
            x1 = _mix_out(False, i, x, mod_l, o_ctx, o_lat, w_out, xd, od_pool_w[i], od_pool_scale[i][None, :])
            rw = jnp.pad(moe_router[i], ((0, 0), (0, LANES - N_EXP)))
            rb = jnp.pad(moe_router_b[i], (0, LANES - N_EXP))[None, :]
            x = tuple(_moe(i, x1, mod_l, g2, rw, rb, tri, upper, moe_w1, moe_w3, moe_w2, l == DEPTH - 1))

    y_prompt = x[0].reshape(N_SEQ_CTX, L_CTX, D_MODEL)
    y_sample = x[1].reshape(N_SEQ_LAT, L_LAT, D_MODEL)
    return (y_prompt, y_sample, new_kv[0], new_kv[1])
```

```python
import functools

import jax
import jax.numpy as jnp
from jax import lax
from jax.experimental import pallas as pl
from jax.experimental.pallas import tpu as pltpu

F32 = jnp.float32
BF16 = jnp.bfloat16

D_MODEL = 1024
N_SEQ_CTX = 32
L_CTX = 256
N_SEQ_LAT = 4
L_LAT = 1024
DEPTH = 4
PAST = 512
GRID_W = 64
HD = 64
N_Q = 8
N_KV = 2
GROUP = N_Q // N_KV
Q_W = N_Q * HD
KV_W = N_KV * HD
CONV_W = 512
POOL_W = 512
POOL_WINDOWS = (2, 4, 8, 16)
POOL_G = 128
POOL_PAD = 16
EVEN_IN = 3 * CONV_W + Q_W + 2 * KV_W
ODD_IN = Q_W + 2 * KV_W + POOL_W
WINDOW = 128
D_FF = 2816
N_EXP = 8
TOP_K = 2
D_FF_E = 1024
ROPE_THETA = 10000.0
EPS = 1e-6

N_CTX_TOK = N_SEQ_CTX * L_CTX
N_LAT_TOK = N_SEQ_LAT * L_LAT
N_TOK = N_CTX_TOK + N_LAT_TOK
MOD_ROWS = 16

LANES = 128
VMEM_LIMIT = 56 * 1024 * 1024
FFN_VMEM_LIMIT = 62 * 1024 * 1024

TM_IN = 1024
TM = 1024
TQ = 256
CTX_SEQS = 4
TF_FFN = 256
TR = 512
RUN_ALIGN = 16
LOCAL_ROWS = TOP_K * TR + N_EXP * RUN_ALIGN
TMS = 512
MOE_TILES = -(-(TOP_K * N_TOK + (N_TOK // TR) * N_EXP * (RUN_ALIGN - 1) + N_EXP * (TMS - 1)) // TMS)
MOE_ROWS = MOE_TILES * TMS
NEG_INF = float("-inf")


def _cparams(sem):
    return pltpu.CompilerParams(dimension_semantics=sem, vmem_limit_bytes=VMEM_LIMIT)


def _mod_row(i, tm):
    n_ctx = N_CTX_TOK // tm
    return jnp.where(i < n_ctx, 0, 1 + (i - n_ctx) // (L_LAT // tm))


def _x_specs(tm, split):
    if not split:
        return [pl.BlockSpec((tm, D_MODEL), lambda i, *_: (i, 0))]
    n_ctx = N_CTX_TOK // tm
    n_lat = N_LAT_TOK // tm
    return [pl.BlockSpec((tm, D_MODEL), lambda i, *_: (jnp.minimum(i, n_ctx - 1), 0)),
            pl.BlockSpec((tm, D_MODEL), lambda i, *_: (jnp.clip(i - n_ctx, 0, n_lat - 1), 0))]


def _x_shapes(split):
    if not split:
        return [jax.ShapeDtypeStruct((N_TOK, D_MODEL), F32)]
    return [jax.ShapeDtypeStruct((N_CTX_TOK, D_MODEL), F32), jax.ShapeDtypeStruct((N_LAT_TOK, D_MODEL), F32)]


def _read_x(i, x_refs, tm):
    if len(x_refs) == 1:
        return x_refs[0][...]
    return jnp.where(i < N_CTX_TOK // tm, x_refs[0][...], x_refs[1][...])


def _write_x(i, o_refs, tm, val):
    if len(o_refs) == 1:
        o_refs[0][...] = val
        return

    @pl.when(i < N_CTX_TOK // tm)
    def _():
        o_refs[0][...] = val

    @pl.when(i >= N_CTX_TOK // tm)
    def _():
        o_refs[1][...] = val


def _normmod(x, g, scale, shift):
    ms = jnp.mean(x * x, axis=-1, keepdims=True)
    y = x * lax.rsqrt(ms + EPS) * g
    return y * (1.0 + scale) + shift


def _silu(x):
    return x * jax.nn.sigmoid(x)


def _dot(a, b):
    return jnp.dot(a, b, preferred_element_type=F32)


def _dot_nt(a, b):
    return lax.dot_general(a, b, (((1,), (1,)), ((), ())), preferred_element_type=F32)


def _cast_rows(src_ref, dst_ref, rows, chunk=256):
    for r in range(0, rows, chunk):
        dst_ref[r:r + chunk, :] = src_ref[r:r + chunk, :].astype(dst_ref.dtype)


def _mod_kernel(c_ref, w_ref, b_ref, o_ref):
    s = _silu(c_ref[...]).astype(BF16)
    o_ref[0] = _dot(s, w_ref[0].astype(BF16)) + b_ref[0]


def _modulation(cond, w_mod, b_mod):
    tn = 1536
    return pl.pallas_call(
        _mod_kernel,
        grid=(DEPTH, 6 * D_MODEL // tn),
        in_specs=[
            pl.BlockSpec((MOD_ROWS, D_MODEL), lambda l, j: (0, 0)),
            pl.BlockSpec((1, D_MODEL, tn), lambda l, j: (l, 0, j)),
            pl.BlockSpec((1, 1, tn), lambda l, j: (l, 0, j)),
        ],
        out_specs=pl.BlockSpec((1, MOD_ROWS, tn), lambda l, j: (l, 0, j)),
        out_shape=jax.ShapeDtypeStruct((DEPTH, MOD_ROWS, 6 * D_MODEL), F32),
        compiler_params=_cparams(("arbitrary", "arbitrary")),
        name="modulation",
    )(cond, w_mod, b_mod.reshape(DEPTH, 1, 6 * D_MODEL))


def _head_rms(t, ones_bd, gain):
    ssq = _dot((t * t).astype(BF16), ones_bd)
    return t * lax.rsqrt(ssq * (1.0 / HD) + EPS) * gain


def _rope(t, cos, sin_signed):
    lane = lax.broadcasted_iota(jnp.int32, (t.shape[0], LANES), 1)
    first = (lane & 31) < 16
    outs = []
    for c in range(t.shape[1] // LANES):
        tc = t[:, c * LANES:(c + 1) * LANES]
        nxt = pltpu.roll(tc, LANES - 16, axis=1)
        prv = pltpu.roll(tc, 16, axis=1)
        outs.append(tc * cos + jnp.where(first, nxt, prv) * sin_signed)
    return outs[0] if len(outs) == 1 else jnp.concatenate(outs, axis=1)


def _in_proj_kernel(even, nx, *refs):
    x_refs = refs[:nx]
    mod_ref, g_ref, w_ref, onesq_ref, onesk_ref, qg_ref, kg_ref, cos_ref, sin_ref = refs[nx:nx + 9]
    rest = refs[nx + 9:]
    if even:
        zc_ref, q_ref, k_ref, v_ref, wbf = rest
        q0 = 3 * CONV_W
    else:
        q_ref, k_ref, v_ref, xd_ref, wbf = rest
        q0 = 0
    k0 = q0 + Q_W
    v0 = k0 + KV_W

    @pl.when(pl.program_id(0) == 0)
    def _():
        _cast_rows(w_ref.at[0], wbf, D_MODEL)

    x = _read_x(pl.program_id(0), x_refs, TM_IN)
    h = _normmod(x, g_ref[...], mod_ref[0, 1:2, :], mod_ref[0, 0:1, :]).astype(BF16)
    cos = cos_ref[...]
    sin = sin_ref[...]

    q = _dot(h, wbf[:, q0:q0 + Q_W])
    q = _rope(_head_rms(q, onesq_ref[...], qg_ref[...]), cos, sin) * (HD ** -0.5)
    q_ref[...] = q.astype(BF16)

    k = _dot(h, wbf[:, k0:k0 + KV_W])
    k = _rope(_head_rms(k, onesk_ref[...], kg_ref[...]), cos, sin)
    k_ref[...] = k.astype(BF16)

    v_ref[...] = _dot(h, wbf[:, v0:v0 + KV_W]).astype(BF16)

    if even:
        zc_ref[...] = _dot(h, wbf[:, 0:3 * CONV_W]).astype(BF16)
    else:
        xd_ref[...] = _dot(h, wbf[:, v0 + KV_W:v0 + KV_W + POOL_W])


def _in_proj(even, layer_i, x, mod_l, g, w, ones_q, ones_k, q_gain, k_gain, cos_tab, sin_tab):
    tm = TM_IN
    n_in = EVEN_IN if even else ODD_IN
    n_ctx = N_CTX_TOK // tm
    per_seq = L_LAT // tm

    def rope_idx(i):
        return (jnp.where(i < n_ctx, 0, per_seq + (i - n_ctx) % per_seq), 0)

    row = lambda i: (i, 0)
    const = lambda i: (0, 0)
    in_specs = _x_specs(tm, len(x) == 2) + [
        pl.BlockSpec((1, 6, D_MODEL), lambda i: (_mod_row(i, tm), 0, 0)),
        pl.BlockSpec((1, D_MODEL), const),
        pl.BlockSpec((1, D_MODEL, n_in), lambda i: (layer_i, 0, 0)),
        pl.BlockSpec((Q_W, Q_W), const),
        pl.BlockSpec((KV_W, KV_W), const),
        pl.BlockSpec((1, Q_W), const),
        pl.BlockSpec((1, KV_W), const),
        pl.BlockSpec((tm, LANES), rope_idx),
        pl.BlockSpec((tm, LANES), rope_idx),
    ]
    qkv_specs = [pl.BlockSpec((tm, Q_W), row), pl.BlockSpec((tm, KV_W), row), pl.BlockSpec((tm, KV_W), row)]
    qkv_shapes = [jax.ShapeDtypeStruct((N_TOK, Q_W), BF16), jax.ShapeDtypeStruct((N_TOK, KV_W), BF16),
                  jax.ShapeDtypeStruct((N_TOK, KV_W), BF16)]
    if even:
        out_specs = [pl.BlockSpec((tm, 3 * CONV_W), row)] + qkv_specs
        out_shape = [jax.ShapeDtypeStruct((N_TOK, 3 * CONV_W), BF16)] + qkv_shapes
    else:
        out_specs = qkv_specs + [pl.BlockSpec((tm, POOL_W), row)]
        out_shape = qkv_shapes + [jax.ShapeDtypeStruct((N_TOK, POOL_W), F32)]
    return pl.pallas_call(
        functools.partial(_in_proj_kernel, even, len(x)),
        grid=(N_TOK // tm,),
        in_specs=in_specs,
        out_specs=out_specs,
        out_shape=out_shape,
        scratch_shapes=[pltpu.VMEM((D_MODEL, n_in), BF16)],
        compiler_params=_cparams(("arbitrary",)),
        name="in_proj_even" if even else "in_proj_odd",
    )(*x, mod_l, g, w, ones_q, ones_k, q_gain, k_gain, cos_tab, sin_tab)


def _dup_heads(t):
    lane = lax.broadcasted_iota(jnp.int32, t.shape, 1)
    swapped = pltpu.roll(t, HD, axis=1)
    low = lane < HD
    return jnp.where(low, t, swapped), jnp.where(low, swapped, t)


def _softmax_pv(scores, values, sink):
    m = scores[0].max(axis=-1, keepdims=True)
    for s in scores[1:]:
        m = jnp.maximum(m, s.max(axis=-1, keepdims=True))
    if sink is not None:
        m = jnp.maximum(m, sink)
    den = None
    acc = None
    for s, v in zip(scores, values):
        e = jnp.exp(s - m)
        d = e.sum(axis=-1, keepdims=True)
        a = _dot(e.astype(BF16), v)
        den = d if den is None else den + d
        acc = a if acc is None else acc + a
    if sink is not None:
        den = den + jnp.exp(sink - m)
    return acc / den


def _group_attention(q_ref, kv, keys, values, sink_ref, mask, stack):
    t = q_ref.shape[0]
    lane = lax.broadcasted_iota(jnp.int32, (t, LANES), 1)
    low = lane < HD
    zero = jnp.zeros((t, LANES), BF16)
    chunks = [q_ref[:, (kv * 2 + c) * LANES:(kv * 2 + c + 1) * LANES] for c in range(2)]
    if not stack:
        outs = []
        for h in range(GROUP):
            qm = jnp.where(low, chunks[h // 2], zero) if h % 2 == 0 else jnp.where(low, zero, chunks[h // 2])
            scores = [_dot_nt(qm, k) for k in keys]
            if mask is not None:
                scores = [s if m is None else jnp.where(m, s, NEG_INF) for s, m in zip(scores, mask)]
            sink = None if sink_ref is None else sink_ref[kv * GROUP + h]
            outs.append(_softmax_pv(scores, values, sink))
        return (jnp.where(low, outs[0], outs[1]).astype(BF16), jnp.where(low, outs[2], outs[3]).astype(BF16))
    qs = jnp.concatenate([jnp.where(low, chunks[0], zero), jnp.where(low, zero, chunks[0]),
                          jnp.where(low, chunks[1], zero), jnp.where(low, zero, chunks[1])], axis=0)
    scores = [_dot_nt(qs, k) for k in keys]
    if mask is not None:
        scores = [s if m is None else jnp.where(m, s, NEG_INF) for s, m in zip(scores, mask)]
    sink = None
    if sink_ref is not None:
        head = lax.broadcasted_iota(jnp.int32, (GROUP * t, 1), 0) // t
        sink = jnp.full((GROUP * t, 1), sink_ref[kv * GROUP], F32)
        for h in range(1, GROUP):
            sink = jnp.where(head == h, sink_ref[kv * GROUP + h], sink)
    out = _softmax_pv(scores, values, sink)
    return (jnp.where(low, out[0:t], out[t:2 * t]).astype(BF16),
            jnp.where(low, out[2 * t:3 * t], out[3 * t:4 * t]).astype(BF16))


def _ctx_attn_kernel(has_sink, first, *refs):
    if has_sink:
        sink_ref, refs = refs[0], refs[1:]
    else:
        sink_ref = None
    q_ref, k_ref, v_ref = refs[:3]
    o_ref, nk_ref, nv_ref = refs[-3:]
    lane = lax.broadcasted_iota(jnp.int32, (L_CTX, LANES), 1)
    low = lane < HD
    for s in range(CTX_SEQS):
        rows = pl.ds(s * L_CTX, L_CTX)
        k = k_ref[rows, :].astype(F32)
        v = v_ref[rows, :].astype(F32)
        k_sw = pltpu.roll(k, HD, axis=1)
        v_sw = pltpu.roll(v, HD, axis=1)
        nk_ref[s, 0, 0] = k[:, 0:HD]
        nk_ref[s, 0, 1] = k_sw[:, 0:HD]
        nv_ref[s, 0, 0] = v[:, 0:HD]
        nv_ref[s, 0, 1] = v_sw[:, 0:HD]
        if first:
            nk_ref[s, 1:] = jnp.zeros((DEPTH - 1, N_KV, L_CTX, HD), F32)
            nv_ref[s, 1:] = jnp.zeros((DEPTH - 1, N_KV, L_CTX, HD), F32)
        k2 = (jnp.where(low, k, k_sw).astype(BF16), jnp.where(low, k_sw, k).astype(BF16))
        v2 = (jnp.where(low, v, v_sw).astype(BF16), jnp.where(low, v_sw, v).astype(BF16))
        for kv in range(N_KV):
            o0, o1 = _group_attention(q_ref.at[rows, :], kv, [k2[kv]], [v2[kv]], sink_ref, None, True)
            o_ref[rows, (2 * kv) * LANES:(2 * kv + 1) * LANES] = o0
            o_ref[rows, (2 * kv + 1) * LANES:(2 * kv + 2) * LANES] = o1


def _ctx_attn(layer, q, k, v, sink, new_kv):
    has_sink = sink is not None
    first = new_kv is None
    row = lambda b: (b, 0)
    rows = CTX_SEQS * L_CTX
    in_specs = [pl.BlockSpec((rows, Q_W), row), pl.BlockSpec((rows, KV_W), row), pl.BlockSpec((rows, KV_W), row)]
    args = [q, k, v]
    if has_sink:
        in_specs = [pl.BlockSpec(memory_space=pltpu.SMEM)] + in_specs
        args = [sink] + args
    aliases = {}
    if first:
        kv_spec = pl.BlockSpec((CTX_SEQS, DEPTH, N_KV, L_CTX, HD), lambda b: (b, 0, 0, 0, 0))
    else:
        kv_spec = pl.BlockSpec((CTX_SEQS, 1, N_KV, L_CTX, HD), lambda b: (b, layer, 0, 0, 0))
        aliases = {len(args): 1, len(args) + 1: 2}
        in_specs = in_specs + [pl.BlockSpec(memory_space=pl.ANY)] * 2
        args = args + list(new_kv)
    kv_shape = jax.ShapeDtypeStruct((N_SEQ_CTX, DEPTH, N_KV, L_CTX, HD), F32)
    o, nk, nv = pl.pallas_call(
        functools.partial(_ctx_attn_kernel, has_sink, first),
        grid=(N_SEQ_CTX // CTX_SEQS,),
        in_specs=in_specs,
        out_specs=[pl.BlockSpec((rows, Q_W), row), kv_spec, kv_spec],
        out_shape=[jax.ShapeDtypeStruct((N_CTX_TOK, Q_W), BF16), kv_shape, kv_shape],
        input_output_aliases=aliases,
        compiler_params=_cparams(("arbitrary",)),
        name="ctx_attn_sink" if has_sink else "ctx_attn",
    )(*args)
    return o, (nk, nv)


def _lat_attn_kernel(windowed, *refs):
    if windowed:
        sink_ref, q_ref, k_ref, v_ref, ck_ref, cv_ref, dup_ref, o_ref, k2s, v2s, ck2s, cv2s = refs
    else:
        q_ref, k_ref, v_ref, ck_ref, cv_ref, dup_ref, o_ref, k2s, v2s, ck2s, cv2s = refs
        sink_ref = None
    j = pl.program_id(1)

    @pl.when(j == 0)
    def _():
        ka, kb = _dup_heads(k_ref[...].astype(F32))
        va, vb = _dup_heads(v_ref[...].astype(F32))
        k2s[0] = ka.astype(BF16)
        k2s[1] = kb.astype(BF16)
        v2s[0] = va.astype(BF16)
        v2s[1] = vb.astype(BF16)
        dup = dup_ref[...]
        for kv in range(N_KV):
            ck2s[kv] = _dot(ck_ref[0, 0, kv].astype(BF16), dup).astype(BF16)
            cv2s[kv] = _dot(cv_ref[0, 0, kv].astype(BF16), dup).astype(BF16)

    mask = None
    if windowed:
        n_loc = TQ + 2 * WINDOW
        start = pl.multiple_of(jnp.clip(j * TQ - WINDOW, 0, L_LAT - n_loc), WINDOW)
        qpos = j * TQ + lax.broadcasted_iota(jnp.int32, (TQ, n_loc), 0)
        kpos = start + lax.broadcasted_iota(jnp.int32, (TQ, n_loc), 1)
        mask = [None, jnp.abs(qpos - kpos) <= WINDOW]
    for kv in range(N_KV):
        if windowed:
            k_own = k2s[kv, pl.ds(start, n_loc), :]
            v_own = v2s[kv, pl.ds(start, n_loc), :]
        else:
            k_own = k2s[kv]
            v_own = v2s[kv]
        o0, o1 = _group_attention(q_ref, kv, [ck2s[kv], k_own], [cv2s[kv], v_own], sink_ref, mask, False)
        o_ref[:, (2 * kv) * LANES:(2 * kv + 1) * LANES] = o0
        o_ref[:, (2 * kv + 1) * LANES:(2 * kv + 2) * LANES] = o1


def _lat_attn(layer, q, k, v, cache_k, cache_v, dup, sink):
    windowed = sink is not None
    n_qt = L_LAT // TQ
    ctx_tiles = N_CTX_TOK // TQ
    ctx_seqs = N_CTX_TOK // L_LAT
    cache_spec = pl.BlockSpec((1, 1, N_KV, PAST, HD), lambda b, j: (b, layer, 0, 0, 0))
    in_specs = [
        pl.BlockSpec((TQ, Q_W), lambda b, j: (ctx_tiles + b * n_qt + j, 0)),
        pl.BlockSpec((L_LAT, KV_W), lambda b, j: (ctx_seqs + b, 0)),
        pl.BlockSpec((L_LAT, KV_W), lambda b, j: (ctx_seqs + b, 0)),
        cache_spec,
        cache_spec,
        pl.BlockSpec((HD, LANES), lambda b, j: (0, 0)),
    ]
    args = [q, k, v, cache_k, cache_v, dup]
    if windowed:
        in_specs = [pl.BlockSpec(memory_space=pltpu.SMEM)] + in_specs
        args = [sink] + args
    return pl.pallas_call(
        functools.partial(_lat_attn_kernel, windowed),
        grid=(N_SEQ_LAT, n_qt),
        in_specs=in_specs,
        out_specs=pl.BlockSpec((TQ, Q_W), lambda b, j: (b * n_qt + j, 0)),
        out_shape=jax.ShapeDtypeStruct((N_LAT_TOK, Q_W), BF16),
        scratch_shapes=[
            pltpu.VMEM((N_KV, L_LAT, LANES), BF16),
            pltpu.VMEM((N_KV, L_LAT, LANES), BF16),
            pltpu.VMEM((N_KV, PAST, LANES), BF16),
            pltpu.VMEM((N_KV, PAST, LANES), BF16),
        ],
        compiler_params=_cparams(("arbitrary", "arbitrary")),
        name="lat_attn_window" if windowed else "lat_attn",
    )(*args)


def _seq_pos(i, width):
    r = lax.broadcasted_iota(jnp.int32, (TM, width), 0)
    is_ctx = i < N_CTX_TOK // TM
    seq_len = jnp.where(is_ctx, L_CTX, L_LAT)
    return r & (seq_len - 1), seq_len


def _shift_rows(t, j, pos, seq_len):
    if j == 0:
        return t
    moved = pltpu.roll(t, (-j) % TM, axis=0)
    ok = (pos + j >= 0) & (pos + j < seq_len)
    return jnp.where(ok, moved, 0.0)


def _conv_mixer(zc_ref, cw_ref, pos, seq_len):
    bg = zc_ref[:, 0:CONV_W].astype(F32)
    u = zc_ref[:, CONV_W:2 * CONV_W].astype(F32) * zc_ref[:, 2 * CONV_W:3 * CONV_W].astype(F32)
    y = (_shift_rows(u, -1, pos, seq_len) * cw_ref[0:1, :] + u * cw_ref[1:2, :]
         + _shift_rows(u, 1, pos, seq_len) * cw_ref[2:3, :])
    return (bg * y).astype(BF16)


def _window_sum(x_seg, w):
    n = x_seg.shape[0] + 2 * POOL_PAD
    z = jnp.zeros((POOL_PAD, x_seg.shape[1]), F32)
    a = jnp.concatenate([z, x_seg, z], axis=0)
    a = a + pltpu.roll(a, 1, axis=0)
    half = 1
    while 2 * half < w:
        a = pltpu.roll(a, half, axis=0) + pltpu.roll(a, n - half, axis=0)
        half *= 2
    return a[POOL_PAD:POOL_PAD + x_seg.shape[0]]


def _pool_mixer(xd_ref, pw_ref, ps_ref, yd_ref, seq_len):
    t = lax.broadcasted_iota(jnp.int32, (seq_len, POOL_G), 0)
    for gi, w in enumerate(POOL_WINDOWS):
        cnt = (jnp.minimum(t + w // 2, seq_len) - jnp.maximum(t - w // 2, 0)).astype(F32)
        wg = pw_ref[gi].astype(BF16)
        lanes = slice(gi * POOL_G, (gi + 1) * POOL_G)
        for s in range(TM // seq_len):
            rows = slice(s * seq_len, (s + 1) * seq_len)
            xg = xd_ref[rows, lanes]
            d = _window_sum(xg, w) / cnt - xg
            yd_ref[rows, lanes] = (_dot(d.astype(BF16), wg) * ps_ref[:, lanes]).astype(BF16)


def _mix_out_kernel(even, nx, *refs):
    x_refs = refs[:nx]
    mod_ref, oc_ref, ol_ref, w_ref = refs[nx:nx + 4]
    rest = refs[nx + 4:]
    if even:
        zc_ref, cw_ref, out_ref, wbf = rest
    else:
        (xd_ref, pw_ref, ps_ref, g2_ref, rw_ref, rb_ref, tri_ref, upper_ref,
         out_ref, xl_ref, meta_ref, cnt_ref, wbf, yd_s) = rest
    i = pl.program_id(0)

    @pl.when(i == 0)
    def _():
        _cast_rows(w_ref.at[0], wbf, D_MODEL)

    o = jnp.where(i < N_CTX_TOK // TM, oc_ref[...], ol_ref[...])
    if even:
        y_o = _dot(o, wbf[CONV_W:, :])
        pos, seq_len = _seq_pos(i, 1)
        ya = _conv_mixer(zc_ref, cw_ref, pos, seq_len)
        y = _dot(ya, wbf[0:CONV_W, :]) + y_o
    else:
        @pl.when(i < N_CTX_TOK // TM)
        def _():
            _pool_mixer(xd_ref, pw_ref, ps_ref, yd_s, L_CTX)

        @pl.when(i >= N_CTX_TOK // TM)
        def _():
            _pool_mixer(xd_ref, pw_ref, ps_ref, yd_s, L_LAT)

        y = _dot(o, wbf[0:Q_W, :]) + _dot(yd_s[...], wbf[Q_W:, :])
    out_ref[...] = _read_x(i, x_refs, TM) + mod_ref[0, 2:3, :] * y
    if not even:
        for part in range(TM // TR):
            rows = pl.ds(part * TR, TR)
            _route_tile(out_ref[rows, :], mod_ref, g2_ref, rw_ref, rb_ref, tri_ref, upper_ref,
                        xl_ref.at[pl.ds(part * LOCAL_ROWS, LOCAL_ROWS), :], meta_ref.at[rows, :],
                        cnt_ref.at[pl.ds(part * 8, 8), :])


def _mix_out(even, layer_i, x, mod_l, o_ctx, o_lat, w_out, *extra):
    n_ctx = N_CTX_TOK // TM
    n_lat = N_LAT_TOK // TM
    row = lambda i: (i, 0)
    const = lambda i: (0, 0)
    in_specs = _x_specs(TM, len(x) == 2) + [
        pl.BlockSpec((1, 6, D_MODEL), lambda i: (_mod_row(i, TM), 0, 0)),
        pl.BlockSpec((TM, Q_W), lambda i: (jnp.minimum(i, n_ctx - 1), 0)),
        pl.BlockSpec((TM, Q_W), lambda i: (jnp.clip(i - n_ctx, 0, n_lat - 1), 0)),
        pl.BlockSpec((1, D_MODEL, D_MODEL), lambda i: (layer_i, 0, 0)),
    ]
    if even:
        in_specs += [pl.BlockSpec((TM, 3 * CONV_W), row), pl.BlockSpec((3, CONV_W), const)]
    else:
        in_specs += [pl.BlockSpec((TM, POOL_W), row),
                     pl.BlockSpec((len(POOL_WINDOWS), POOL_G, POOL_G), lambda i: (0, 0, 0)),
                     pl.BlockSpec((1, POOL_W), const),
                     pl.BlockSpec((1, D_MODEL), const),
                     pl.BlockSpec((D_MODEL, LANES), const),
                     pl.BlockSpec((1, LANES), const),
                     pl.BlockSpec((TR, TR), const),
                     pl.BlockSpec((LANES, LANES), const)]
    out_specs = [pl.BlockSpec((TM, D_MODEL), row)]
    out_shape = [jax.ShapeDtypeStruct((N_TOK, D_MODEL), F32)]
    if not even:
        parts = TM // TR
        out_specs += [pl.BlockSpec((parts * LOCAL_ROWS, D_MODEL), row), pl.BlockSpec((TM, LANES), row),
                      pl.BlockSpec((parts * 8, LANES), row)]
        out_shape += [jax.ShapeDtypeStruct((N_TOK // TR * LOCAL_ROWS, D_MODEL), BF16),
                      jax.ShapeDtypeStruct((N_TOK, LANES), F32),
                      jax.ShapeDtypeStruct((N_TOK // TR * 8, LANES), F32)]
    return pl.pallas_call(
        functools.partial(_mix_out_kernel, even, len(x)),
        grid=(N_TOK // TM,),
        in_specs=in_specs,
        out_specs=out_specs,
        out_shape=out_shape,
        scratch_shapes=[pltpu.VMEM((D_MODEL, D_MODEL), BF16)] + ([] if even else [pltpu.VMEM((TM, POOL_W), BF16)]),
        compiler_params=_cparams(("arbitrary",)),
        name="mix_out_even" if even else "mix_out_odd",
    )(*x, mod_l, o_ctx, o_lat, w_out, *extra)


def _ffn_kernel(layer_i, x_ref, mod_ref, g_ref, w1_hbm, w3_hbm, w2_hbm, out_ref,
                w1s, w3s, w2s, st1, st3, st2, hs, hid, sem):
    i = pl.program_id(0)
    nf = D_FF // TF_FFN

    def chunk_copies(f, slot):
        cols = pl.ds(f * TF_FFN, TF_FFN)
        return (pltpu.make_async_copy(w1_hbm.at[layer_i, :, cols], st1.at[slot], sem.at[0, slot]),
                pltpu.make_async_copy(w3_hbm.at[layer_i, :, cols], st3.at[slot], sem.at[1, slot]),
                pltpu.make_async_copy(w2_hbm.at[layer_i, cols, :], st2.at[slot], sem.at[2, slot]))

    @pl.when(i == 0)
    def _():
        for c in chunk_copies(0, 0):
            c.start()

    hs[...] = _normmod(x_ref[...], g_ref[...], mod_ref[0, 4:5, :], mod_ref[0, 3:4, :]).astype(BF16)
    for f in range(nf):
        slot = f % 2
        lo, hi = f * TF_FFN, (f + 1) * TF_FFN

        @pl.when(i == 0)
        def _(f=f, slot=slot, lo=lo, hi=hi):
            if f + 1 < nf:
                for c in chunk_copies(f + 1, 1 - slot):
                    c.start()
            for c in chunk_copies(f, slot):
                c.wait()
            w1s[:, lo:hi] = st1[slot].astype(BF16)
            w3s[:, lo:hi] = st3[slot].astype(BF16)
            w2s[lo:hi, :] = st2[slot].astype(BF16)

        h = hs[...]
        hid[:, lo:hi] = (_silu(_dot(h, w1s[:, lo:hi])) * _dot(h, w3s[:, lo:hi])).astype(BF16)
    out_ref[...] = x_ref[...] + mod_ref[0, 5:6, :] * _dot(hid[...], w2s[...])


def _ffn(layer_i, x, mod_l, g, w1, w3, w2):
    row = lambda i: (i, 0)
    any_spec = pl.BlockSpec(memory_space=pl.ANY)
    return pl.pallas_call(
        functools.partial(_ffn_kernel, layer_i),
        grid=(N_TOK // TM,),
        in_specs=[
            pl.BlockSpec((TM, D_MODEL), row),
            pl.BlockSpec((1, 6, D_MODEL), lambda i: (_mod_row(i, TM), 0, 0)),
            pl.BlockSpec((1, D_MODEL), lambda i: (0, 0)),
            any_spec, any_spec, any_spec,
        ],
        out_specs=pl.BlockSpec((TM, D_MODEL), row),
        out_shape=jax.ShapeDtypeStruct((N_TOK, D_MODEL), F32),
        scratch_shapes=[
            pltpu.VMEM((D_MODEL, D_FF), BF16),
            pltpu.VMEM((D_MODEL, D_FF), BF16),
            pltpu.VMEM((D_FF, D_MODEL), BF16),
            pltpu.VMEM((2, D_MODEL, TF_FFN), F32),
            pltpu.VMEM((2, D_MODEL, TF_FFN), F32),
            pltpu.VMEM((2, TF_FFN, D_MODEL), F32),
            pltpu.VMEM((TM, D_MODEL), BF16),
            pltpu.VMEM((TM, D_FF), BF16),
            pltpu.SemaphoreType.DMA((3, 2)),
        ],
        compiler_params=pltpu.CompilerParams(dimension_semantics=("arbitrary",), vmem_limit_bytes=FFN_VMEM_LIMIT),
        name="ffn",
    )(x, mod_l, g, w1, w3, w2)


def _split_bf16(t):
    hi = t.astype(BF16)
    return hi, (t - hi.astype(F32)).astype(BF16)


def _lane_values(col_vals, ones_rows):
    q = jnp.floor(col_vals * (1.0 / 32.0))
    r = col_vals - 32.0 * q
    t = 32.0 * _dot_nt(ones_rows, q.astype(BF16)) + _dot_nt(ones_rows, r.astype(BF16))
    return t[0:1, :]


def _route_tile(x, mod_ref, g_ref, rw_ref, rb_ref, tri_ref, upper_ref, xl_ref, meta_ref, cnt_ref):
    h = _normmod(x, g_ref[...], mod_ref[0, 4:5, :], mod_ref[0, 3:4, :])
    h_hi, h_lo = _split_bf16(h)
    w_hi, w_lo = _split_bf16(rw_ref[...])
    logits = _dot(h_hi, w_hi) + _dot(h_hi, w_lo) + _dot(h_lo, w_hi) + rb_ref[...]
    lane = lax.broadcasted_iota(jnp.int32, logits.shape, 1).astype(F32)
    logits = jnp.where(lane < N_EXP, logits, NEG_INF)
    m1 = logits.max(axis=-1, keepdims=True)
    i1 = jnp.where(logits == m1, lane, float(LANES)).min(axis=-1, keepdims=True)
    rest = jnp.where(lane == i1, NEG_INF, logits)
    m2 = rest.max(axis=-1, keepdims=True)
    i2 = jnp.where(rest == m2, lane, float(LANES)).min(axis=-1, keepdims=True)
    e2 = jnp.exp(m2 - m1)
    den = 1.0 + e2
    g1 = 1.0 / den
    g2 = e2 / den

    oh_a = jnp.where(lane == i1, 1.0, 0.0)
    oh_b = jnp.where(lane == i2, 1.0, 0.0)
    tri = tri_ref[...]
    cnt_a = oh_a.sum(axis=0, keepdims=True)
    cnt_b = oh_b.sum(axis=0, keepdims=True)
    run16 = jnp.floor((cnt_a + cnt_b + (RUN_ALIGN - 1)) * (1.0 / RUN_ALIGN))
    run16_rows = jnp.broadcast_to(run16, (8, LANES))
    start = RUN_ALIGN * _dot(run16_rows.astype(BF16), upper_ref[...])[0:1, :]
    row_a = oh_a * (start + _dot(tri, oh_a.astype(BF16)))
    row_b = oh_b * (start + cnt_a + _dot(tri, oh_b.astype(BF16)))

    ones_rows = jnp.ones((8, LANES), BF16)
    tok_a = _lane_values(row_a, ones_rows)
    tok_b = _lane_values(row_b, ones_rows)
    sorted_row = lax.broadcasted_iota(jnp.int32, (LOCAL_ROWS, TR), 0).astype(F32)
    perm = jnp.where((sorted_row == tok_a) | (sorted_row == tok_b), 1.0, 0.0).astype(BF16)
    xl_ref[...] = _dot(perm, h_hi).astype(BF16)

    meta = jnp.zeros_like(logits)
    cols = (row_a.sum(axis=-1, keepdims=True), row_b.sum(axis=-1, keepdims=True), g1, g2)
    for k, col in enumerate(cols):
        meta = jnp.where(lane == k, col, meta)
    meta_ref[...] = meta
    cnt_ref[...] = RUN_ALIGN * run16_rows


def _moe_plan(counts):
    n_tiles = N_TOK // TR
    run = counts.reshape(n_tiles, 8, LANES)[:, 0, :N_EXP].astype(jnp.int32)
    per_expert = jnp.sum(run, axis=0)
    region = (per_expert + (TMS - 1)) // TMS * TMS
    ends = jnp.cumsum(region)
    offs = ends - region
    seg_end = jnp.cumsum(run, axis=0)
    seg_start = seg_end - run
    local_end = jnp.cumsum(run, axis=1)
    local_start = local_end - run
    n_chunks = local_end[:, N_EXP - 1] // RUN_ALIGN
    tile_start = jnp.arange(MOE_TILES, dtype=jnp.int32) * TMS
    tile_expert = jnp.sum((tile_start[:, None] >= ends[None, :]).astype(jnp.int32), axis=1)
    tile_expert = jnp.minimum(tile_expert, N_EXP - 1)
    n_used = (ends[N_EXP - 1] // TMS).reshape(1)
    experts = jnp.arange(N_EXP, dtype=jnp.int32)
    g_row = jnp.arange(MOE_ROWS // RUN_ALIGN, dtype=jnp.int32) * RUN_ALIGN
    g_exp = jnp.repeat(tile_expert, TMS // RUN_ALIGN)
    pick = g_exp[:, None] == experts[None, :]
    rel = g_row - jnp.sum(jnp.where(pick, offs[None, :], 0), axis=1)
    ends_of = jnp.sum(jnp.where(pick[:, None, :], seg_end[None, :, :], 0), axis=2)
    src_tile = jnp.sum((rel[:, None] >= ends_of).astype(jnp.int32), axis=1)
    valid = (src_tile < n_tiles) & (jnp.repeat(jnp.arange(MOE_TILES), TMS // RUN_ALIGN) < n_used[0])
    src_tile = jnp.minimum(src_tile, n_tiles - 1)
    sel = (src_tile[:, None, None] == jnp.arange(n_tiles)[None, :, None]) & pick[:, None, :]
    shift = jnp.sum(jnp.where(sel, (local_start - seg_start)[None, :, :], 0), axis=(1, 2))
    chunk_src = src_tile * LOCAL_ROWS + rel + shift
    n_valid = jnp.sum(valid.reshape(MOE_TILES, TMS // RUN_ALIGN).astype(jnp.int32), axis=1)
    return chunk_src, n_valid, n_chunks, tile_expert, n_used


def _chunk_copy(src_ref, src_row, dst_ref, dst_row, sem):
    return pltpu.make_async_copy(src_ref.at[pl.ds(src_row, RUN_ALIGN), :],
                                 dst_ref.at[pl.ds(dst_row, RUN_ALIGN), :], sem)


CHUNKS = TMS // RUN_ALIGN
LOCAL_CHUNKS = LOCAL_ROWS // RUN_ALIGN


def _expert_kernel(te_ref, nu_ref, src_ref, nv_ref, nc_ref, xl_hbm, w1_ref, w3_ref, w2_ref, yl_hbm,
                   xbuf, ybuf, zbuf, w1s, w3s, w2s, sem_in, sem_out, sem_zero):
    t = pl.program_id(0)
    n_used = nu_ref[0]
    slot = t & 1

    def for_chunks(n, body):
        @pl.when(n == CHUNKS)
        def _():
            for j in range(CHUNKS):
                body(j)

        @pl.when(n < CHUNKS)
        def _():
            def step(j, c):
                body(j)
                return c

            lax.fori_loop(0, n, step, 0)

    def gather(tile, s):
        n = nv_ref[tile]

        def issue(j):
            src = pl.multiple_of(src_ref[tile * CHUNKS + j], RUN_ALIGN)
            _chunk_copy(xl_hbm, src, xbuf.at[s], pl.multiple_of(j * RUN_ALIGN, RUN_ALIGN), sem_in.at[s]).start()

        def pad(j, c):
            xbuf[s, pl.ds(pl.multiple_of(j * RUN_ALIGN, RUN_ALIGN), RUN_ALIGN), :] = jnp.zeros(
                (RUN_ALIGN, D_MODEL), BF16)
            return c

        for_chunks(n, issue)
        lax.fori_loop(n, CHUNKS, pad, 0)

    def scatter(tile, s):
        def issue(j):
            dst = pl.multiple_of(src_ref[tile * CHUNKS + j], RUN_ALIGN)
            _chunk_copy(ybuf.at[s], pl.multiple_of(j * RUN_ALIGN, RUN_ALIGN), yl_hbm, dst, sem_out.at[s]).start()

        for_chunks(nv_ref[tile], issue)

    def drain(count, sem):
        @pl.when(count == CHUNKS)
        def _():
            pltpu.make_async_copy(xl_hbm.at[pl.ds(0, TMS), :], xbuf.at[0], sem).wait()

        @pl.when(count != CHUNKS)
        def _():
            def one(j, c):
                _chunk_copy(xl_hbm, 0, xbuf.at[0], 0, sem).wait()
                return c

            lax.fori_loop(0, count, one, 0)

    @pl.when(t == 0)
    def _():
        zbuf[...] = jnp.zeros_like(zbuf)
        total = jnp.int32(0)
        for tile in range(N_TOK // TR):
            n = nc_ref[tile]

            def clear(j, c, tile=tile):
                row = pl.multiple_of(tile * LOCAL_ROWS + j * RUN_ALIGN, RUN_ALIGN)
                _chunk_copy(zbuf, 0, yl_hbm, row, sem_zero).start()
                return c

            lax.fori_loop(n, LOCAL_CHUNKS, clear, 0)
            total = total + (LOCAL_CHUNKS - n)
        drain(total, sem_zero)
        gather(0, 0)

    @pl.when(t < n_used)
    def _():
        drain(nv_ref[t], sem_in.at[slot])

    @pl.when(t + 1 < n_used)
    def _():
        gather(t + 1, 1 - slot)

    new_expert = (t == 0) | (te_ref[t] != te_ref[jnp.maximum(t - 1, 0)])

    @pl.when(new_expert)
    def _():
        _cast_rows(w1_ref.at[0, 0], w1s, D_MODEL)
        _cast_rows(w3_ref.at[0, 0], w3s, D_MODEL)
        _cast_rows(w2_ref.at[0, 0], w2s, D_FF_E)

    @pl.when((t >= 2) & (t - 2 < n_used))
    def _():
        drain(nv_ref[jnp.maximum(t - 2, 0)], sem_out.at[slot])

    @pl.when(t < n_used)
    def _():
        x = xbuf[slot]
        hid = (_silu(_dot(x, w1s[...])) * _dot(x, w3s[...])).astype(BF16)
        ybuf[slot] = _dot(hid, w2s[...]).astype(BF16)
        scatter(t, slot)

    @pl.when(t == pl.num_programs(0) - 1)
    def _():
        @pl.when((t >= 1) & (t - 1 < n_used))
        def _():
            drain(nv_ref[jnp.maximum(t - 1, 0)], sem_out.at[1 - slot])

        @pl.when(t < n_used)
        def _():
            drain(nv_ref[t], sem_out.at[slot])


def _experts(layer_i, plan, xl, w1, w3, w2):
    chunk_src, n_valid, n_chunks, tile_expert, n_used = plan
    wsel = lambda t, te, nu, src, nv, nc: (layer_i, te[t], 0, 0)
    any_spec = pl.BlockSpec(memory_space=pl.ANY)
    return pl.pallas_call(
        _expert_kernel,
        grid_spec=pltpu.PrefetchScalarGridSpec(
            num_scalar_prefetch=5,
            grid=(MOE_TILES,),
            in_specs=[
                any_spec,
                pl.BlockSpec((1, 1, D_MODEL, D_FF_E), wsel),
                pl.BlockSpec((1, 1, D_MODEL, D_FF_E), wsel),
                pl.BlockSpec((1, 1, D_FF_E, D_MODEL), wsel),
            ],
            out_specs=any_spec,
            scratch_shapes=[
                pltpu.VMEM((2, TMS, D_MODEL), BF16),
                pltpu.VMEM((2, TMS, D_MODEL), BF16),
                pltpu.VMEM((RUN_ALIGN, D_MODEL), BF16),
                pltpu.VMEM((D_MODEL, D_FF_E), BF16),
                pltpu.VMEM((D_MODEL, D_FF_E), BF16),
                pltpu.VMEM((D_FF_E, D_MODEL), BF16),
                pltpu.SemaphoreType.DMA((2,)),
                pltpu.SemaphoreType.DMA((2,)),
                pltpu.SemaphoreType.DMA(()),
            ],
        ),
        out_shape=jax.ShapeDtypeStruct(xl.shape, BF16),
        compiler_params=_cparams(("arbitrary",)),
        name="moe_experts",
    )(tile_expert, n_used, chunk_src, n_valid, n_chunks, xl, w1, w3, w2)


def _combine_kernel(x_ref, mod_ref, meta_ref, yl_ref, *out_refs):
    sorted_row = lax.broadcasted_iota(jnp.int32, (TR, LOCAL_ROWS), 1).astype(F32)
    y = yl_ref[...]
    pick_a = jnp.where(sorted_row == meta_ref[:, 0:1], 1.0, 0.0).astype(BF16)
    pick_b = jnp.where(sorted_row == meta_ref[:, 1:2], 1.0, 0.0).astype(BF16)
    f = meta_ref[:, 2:3] * _dot(pick_a, y) + meta_ref[:, 3:4] * _dot(pick_b, y)
    _write_x(pl.program_id(0), out_refs, TR, x_ref[...] + mod_ref[0, 5:6, :] * f)


def _combine(x, mod_l, meta, yl, split_out):
    row = lambda i: (i, 0)
    return pl.pallas_call(
        _combine_kernel,
        grid=(N_TOK // TR,),
        in_specs=[
            pl.BlockSpec((TR, D_MODEL), row),
            pl.BlockSpec((1, 6, D_MODEL), lambda i: (_mod_row(i, TR), 0, 0)),
            pl.BlockSpec((TR, LANES), row),
            pl.BlockSpec((LOCAL_ROWS, D_MODEL), row),
        ],
        out_specs=_x_specs(TR, split_out),
        out_shape=_x_shapes(split_out),
        compiler_params=_cparams(("arbitrary",)),
        name="moe_combine",
    )(x, mod_l, meta, yl)


def _moe(layer_i, x, routed, mod_l, w1, w3, w2, split_out):
    xl, meta, counts = routed
    yl = _experts(layer_i, _moe_plan(counts), xl, w1, w3, w2)
    return _combine(x, mod_l, meta, yl, split_out)


def _rope_tables():
    n_rows = L_LAT // GRID_W
    rows = jnp.repeat(jnp.arange(n_rows, dtype=F32), GRID_W)
    cols = jnp.tile(jnp.arange(GRID_W, dtype=F32), n_rows)
    quarter = HD // 4
    inv = ROPE_THETA ** (-jnp.arange(quarter, dtype=F32) / quarter)
    ang_r = rows[:, None] * inv
    ang_c = cols[:, None] * inv
    cos = jnp.concatenate([jnp.cos(ang_r)] * 2 + [jnp.cos(ang_c)] * 2, axis=1)
    sin = jnp.concatenate([-jnp.sin(ang_r), jnp.sin(ang_r), -jnp.sin(ang_c), jnp.sin(ang_c)], axis=1)
    cos = jnp.concatenate([cos, cos], axis=1)
    sin = jnp.concatenate([sin, sin], axis=1)
    cos = jnp.concatenate([jnp.ones((L_LAT, LANES), F32), cos], axis=0)
    sin = jnp.concatenate([jnp.zeros((L_LAT, LANES), F32), sin], axis=0)
    return cos, sin


def _block_ones(width):
    r = jnp.arange(width) // HD
    return (r[:, None] == r[None, :]).astype(BF16)


def kernel(x_prompt, x_sample, cache_k, cache_v, c, c_ctx, norm1, norm2, w_mod, b_mod, ev_w_in, ev_conv, ev_q_norm, ev_k_norm, ev_w_out, od_w_in, od_q_norm, od_k_norm, od_sink, od_pool_w, od_pool_scale, od_w_out, ffn_w1, ffn_w3, ffn_w2, moe_router, moe_router_b, moe_w1, moe_w3, moe_w2):
    x = (x_prompt.reshape(N_CTX_TOK, D_MODEL), x_sample.reshape(N_LAT_TOK, D_MODEL))
    cond = jnp.concatenate([c_ctx[None, :], c, jnp.zeros((MOD_ROWS - 1 - N_SEQ_LAT, D_MODEL), F32)], axis=0)
    mod = _modulation(cond, w_mod, b_mod).reshape(DEPTH, MOD_ROWS, 6, D_MODEL)

    cos_tab, sin_tab = _rope_tables()
    ones_q = _block_ones(Q_W)
    ones_k = _block_ones(KV_W)
    eye = jnp.eye(HD, dtype=BF16)
    dup = jnp.concatenate([eye, eye], axis=1)
    tri = (jnp.arange(TR)[:, None] > jnp.arange(TR)[None, :]).astype(BF16)
    upper = (jnp.arange(LANES)[:, None] < jnp.arange(LANES)[None, :]).astype(BF16)

    new_kv = None
    for l in range(DEPTH):
        i = l // 2
        even = l % 2 == 0
        mod_l = mod[l]
        g1 = norm1[l][None, :]
        g2 = norm2[l][None, :]
        if even:
            q_gain, k_gain = ev_q_norm[i], ev_k_norm[i]
            w_in, w_out = ev_w_in, ev_w_out
        else:
            q_gain, k_gain = od_q_norm[i], od_k_norm[i]
            w_in, w_out = od_w_in, od_w_out
        q_gain = jnp.tile(q_gain, N_Q)[None, :]
        k_gain = jnp.tile(k_gain, N_KV)[None, :]
        outs = _in_proj(even, i, x, mod_l, g1, w_in, ones_q, ones_k, q_gain, k_gain, cos_tab, sin_tab)
        if even:
            zc, q, k, v = outs
            sink = None
        else:
            q, k, v, xd = outs
            sink = od_sink[i]
        o_ctx, new_kv = _ctx_attn(l, q, k, v, sink, new_kv)
        o_lat = _lat_attn(l, q, k, v, cache_k, cache_v, dup, sink)
        if even:
            (x1,) = _mix_out(True, i, x, mod_l, o_ctx, o_lat, w_out, zc, ev_conv[i])
            x = (_ffn(i, x1, mod_l, g2, ffn_w1, ffn_w3, ffn_w2),)
        else:
            rw = jnp.pad(moe_router[i], ((0, 0), (0, LANES - N_EXP)))
            rb = jnp.pad(moe_router_b[i], (0, LANES - N_EXP))[None, :]
            x1, *routed = _mix_out(False, i, x, mod_l, o_ctx, o_lat, w_out, xd, od_pool_w[i],
                                   od_pool_scale[i][None, :], g2, rw, rb, tri, upper)
            x = tuple(_moe(i, x1, routed, mod_l, moe_w1, moe_w3, moe_w2, l == DEPTH - 1))

    y_prompt = x[0].reshape(N_SEQ_CTX, L_CTX, D_MODEL)
    y_sample = x[1].reshape(N_SEQ_LAT, L_LAT, D_MODEL)
    return (y_prompt, y_sample, new_kv[0], new_kv[1])
```

```python
import functools

import jax
import jax.numpy as jnp
import numpy as np
from jax import lax
from jax.experimental import pallas as pl
from jax.experimental.pallas import tpu as pltpu

F32 = jnp.float32
BF16 = jnp.bfloat16

D_MODEL = 1024
N_SEQ_CTX = 32
L_CTX = 256
N_SEQ_LAT = 4
L_LAT = 1024
DEPTH = 4
PAST = 512
GRID_W = 64
HD = 64
N_Q = 8
N_KV = 2
GROUP = N_Q // N_KV
Q_W = N_Q * HD
KV_W = N_KV * HD
CONV_W = 512
POOL_W = 512
POOL_WINDOWS = (2, 4, 8, 16)
POOL_G = 128
POOL_PAD = 16
EVEN_IN = 3 * CONV_W + Q_W + 2 * KV_W
ODD_IN = Q_W + 2 * KV_W + POOL_W
WINDOW = 128
D_FF = 2816
N_EXP = 8
TOP_K = 2
D_FF_E = 1024
ROPE_THETA = 10000.0
EPS = 1e-6

N_CTX_TOK = N_SEQ_CTX * L_CTX
N_LAT_TOK = N_SEQ_LAT * L_LAT
N_TOK = N_CTX_TOK + N_LAT_TOK
MOD_ROWS = 16

LANES = 128
VMEM_LIMIT = 56 * 1024 * 1024
FFN_VMEM_LIMIT = 62 * 1024 * 1024

TM_IN = 1024
TM = 1024
TQ = 256
CTX_SEQS = 4
TF_FFN = 256
TR = 512
RUN_ALIGN = 16
LOCAL_ROWS = TOP_K * TR + N_EXP * RUN_ALIGN
TMS = 512
MOE_TILES = -(-(TOP_K * N_TOK + (N_TOK // TR) * N_EXP * (RUN_ALIGN - 1) + N_EXP * (TMS - 1)) // TMS)
MOE_ROWS = MOE_TILES * TMS
NEG_INF = float("-inf")


def _cparams(sem):
    return pltpu.CompilerParams(dimension_semantics=sem, vmem_limit_bytes=VMEM_LIMIT)


def _mod_row(i, tm):
    n_ctx = N_CTX_TOK // tm
    return jnp.where(i < n_ctx, 0, 1 + (i - n_ctx) // (L_LAT // tm))


def _x_specs(tm, split):
    if not split:
        return [pl.BlockSpec((tm, D_MODEL), lambda i, *_: (i, 0))]
    n_ctx = N_CTX_TOK // tm
    n_lat = N_LAT_TOK // tm
    return [pl.BlockSpec((tm, D_MODEL), lambda i, *_: (jnp.minimum(i, n_ctx - 1), 0)),
            pl.BlockSpec((tm, D_MODEL), lambda i, *_: (jnp.clip(i - n_ctx, 0, n_lat - 1), 0))]


def _x_shapes(split):
    if not split:
        return [jax.ShapeDtypeStruct((N_TOK, D_MODEL), F32)]
    return [jax.ShapeDtypeStruct((N_CTX_TOK, D_MODEL), F32), jax.ShapeDtypeStruct((N_LAT_TOK, D_MODEL), F32)]


def _read_x(i, x_refs, tm):
    if len(x_refs) == 1:
        return x_refs[0][...]
    return jnp.where(i < N_CTX_TOK // tm, x_refs[0][...], x_refs[1][...])


def _write_x(i, o_refs, tm, val):
    if len(o_refs) == 1:
        o_refs[0][...] = val
        return

    @pl.when(i < N_CTX_TOK // tm)
    def _():
        o_refs[0][...] = val

    @pl.when(i >= N_CTX_TOK // tm)
    def _():
        o_refs[1][...] = val


def _normmod(x, g, scale, shift):
    ms = jnp.mean(x * x, axis=-1, keepdims=True)
    y = x * lax.rsqrt(ms + EPS) * g
    return y * (1.0 + scale) + shift


def _silu(x):
    return x * jax.nn.sigmoid(x)


def _dot(a, b):
    return jnp.dot(a, b, preferred_element_type=F32)


def _dot_nt(a, b):
    return lax.dot_general(a, b, (((1,), (1,)), ((), ())), preferred_element_type=F32)


def _cast_rows(src_ref, dst_ref, rows, chunk=256):
    for r in range(0, rows, chunk):
        dst_ref[r:r + chunk, :] = src_ref[r:r + chunk, :].astype(dst_ref.dtype)


def _mod_kernel(c_ref, w_ref, b_ref, o_ref):
    s = _silu(c_ref[...]).astype(BF16)
    o_ref[0] = _dot(s, w_ref[0].astype(BF16)) + b_ref[0]


def _modulation(cond, w_mod, b_mod):
    tn = 1536
    return pl.pallas_call(
        _mod_kernel,
        grid=(DEPTH, 6 * D_MODEL // tn),
        in_specs=[
            pl.BlockSpec((MOD_ROWS, D_MODEL), lambda l, j: (0, 0)),
            pl.BlockSpec((1, D_MODEL, tn), lambda l, j: (l, 0, j)),
            pl.BlockSpec((1, 1, tn), lambda l, j: (l, 0, j)),
        ],
        out_specs=pl.BlockSpec((1, MOD_ROWS, tn), lambda l, j: (l, 0, j)),
        out_shape=jax.ShapeDtypeStruct((DEPTH, MOD_ROWS, 6 * D_MODEL), F32),
        compiler_params=_cparams(("arbitrary", "arbitrary")),
        name="modulation",
    )(cond, w_mod, b_mod.reshape(DEPTH, 1, 6 * D_MODEL))


def _head_rms(t, ones_bd, gain):
    ssq = _dot((t * t).astype(BF16), ones_bd)
    return t * lax.rsqrt(ssq * (1.0 / HD) + EPS) * gain


def _rope(t, cos, sin_signed):
    lane = lax.broadcasted_iota(jnp.int32, (t.shape[0], LANES), 1)
    first = (lane & 31) < 16
    outs = []
    for c in range(t.shape[1] // LANES):
        tc = t[:, c * LANES:(c + 1) * LANES]
        nxt = pltpu.roll(tc, LANES - 16, axis=1)
        prv = pltpu.roll(tc, 16, axis=1)
        outs.append(tc * cos + jnp.where(first, nxt, prv) * sin_signed)
    return outs[0] if len(outs) == 1 else jnp.concatenate(outs, axis=1)


def _in_proj_kernel(even, nx, *refs):
    x_refs = refs[:nx]
    mod_ref, g_ref, w_ref, onesq_ref, onesk_ref, qg_ref, kg_ref, cos_ref, sin_ref = refs[nx:nx + 9]
    rest = refs[nx + 9:]
    if even:
        zc_ref, q_ref, k_ref, v_ref, wbf = rest
        q0 = 3 * CONV_W
    else:
        q_ref, k_ref, v_ref, xd_ref, wbf = rest
        q0 = 0
    k0 = q0 + Q_W
    v0 = k0 + KV_W

    @pl.when(pl.program_id(0) == 0)
    def _():
        _cast_rows(w_ref.at[0], wbf, D_MODEL)

    x = _read_x(pl.program_id(0), x_refs, TM_IN)
    h = _normmod(x, g_ref[...], mod_ref[0, 1:2, :], mod_ref[0, 0:1, :]).astype(BF16)
    cos = cos_ref[...]
    sin = sin_ref[...]

    q = _dot(h, wbf[:, q0:q0 + Q_W])
    q = _rope(_head_rms(q, onesq_ref[...], qg_ref[...]), cos, sin) * (HD ** -0.5)
    q_ref[...] = q.astype(BF16)

    k = _dot(h, wbf[:, k0:k0 + KV_W])
    k = _rope(_head_rms(k, onesk_ref[...], kg_ref[...]), cos, sin)
    k_ref[...] = k.astype(BF16)

    v_ref[...] = _dot(h, wbf[:, v0:v0 + KV_W]).astype(BF16)

    if even:
        zc_ref[...] = _dot(h, wbf[:, 0:3 * CONV_W]).astype(BF16)
    else:
        xd_ref[...] = _dot(h, wbf[:, v0 + KV_W:v0 + KV_W + POOL_W])


def _in_proj(even, layer_i, x, mod_l, g, w, ones_q, ones_k, q_gain, k_gain, cos_tab, sin_tab):
    tm = TM_IN
    n_in = EVEN_IN if even else ODD_IN
    n_ctx = N_CTX_TOK // tm
    per_seq = L_LAT // tm

    def rope_idx(i):
        return (jnp.where(i < n_ctx, 0, per_seq + (i - n_ctx) % per_seq), 0)

    row = lambda i: (i, 0)
    const = lambda i: (0, 0)
    in_specs = _x_specs(tm, len(x) == 2) + [
        pl.BlockSpec((1, 6, D_MODEL), lambda i: (_mod_row(i, tm), 0, 0)),
        pl.BlockSpec((1, D_MODEL), const),
        pl.BlockSpec((1, D_MODEL, n_in), lambda i: (layer_i, 0, 0)),
        pl.BlockSpec((Q_W, Q_W), const),
        pl.BlockSpec((KV_W, KV_W), const),
        pl.BlockSpec((1, Q_W), const),
        pl.BlockSpec((1, KV_W), const),
        pl.BlockSpec((tm, LANES), rope_idx),
        pl.BlockSpec((tm, LANES), rope_idx),
    ]
    qkv_specs = [pl.BlockSpec((tm, Q_W), row), pl.BlockSpec((tm, KV_W), row), pl.BlockSpec((tm, KV_W), row)]
    qkv_shapes = [jax.ShapeDtypeStruct((N_TOK, Q_W), BF16), jax.ShapeDtypeStruct((N_TOK, KV_W), BF16),
                  jax.ShapeDtypeStruct((N_TOK, KV_W), BF16)]
    if even:
        out_specs = [pl.BlockSpec((tm, 3 * CONV_W), row)] + qkv_specs
        out_shape = [jax.ShapeDtypeStruct((N_TOK, 3 * CONV_W), BF16)] + qkv_shapes
    else:
        out_specs = qkv_specs + [pl.BlockSpec((tm, POOL_W), row)]
        out_shape = qkv_shapes + [jax.ShapeDtypeStruct((N_TOK, POOL_W), F32)]
    return pl.pallas_call(
        functools.partial(_in_proj_kernel, even, len(x)),
        grid=(N_TOK // tm,),
        in_specs=in_specs,
        out_specs=out_specs,
        out_shape=out_shape,
        scratch_shapes=[pltpu.VMEM((D_MODEL, n_in), BF16)],
        compiler_params=_cparams(("arbitrary",)),
        name="in_proj_even" if even else "in_proj_odd",
    )(*x, mod_l, g, w, ones_q, ones_k, q_gain, k_gain, cos_tab, sin_tab)


def _dup_heads(t):
    lane = lax.broadcasted_iota(jnp.int32, t.shape, 1)
    swapped = pltpu.roll(t, HD, axis=1)
    low = lane < HD
    return jnp.where(low, t, swapped), jnp.where(low, swapped, t)


def _softmax_pv(scores, values, sink):
    m = scores[0].max(axis=-1, keepdims=True)
    for s in scores[1:]:
        m = jnp.maximum(m, s.max(axis=-1, keepdims=True))
    if sink is not None:
        m = jnp.maximum(m, sink)
    den = None
    acc = None
    for s, v in zip(scores, values):
        e = jnp.exp(s - m)
        d = e.sum(axis=-1, keepdims=True)
        a = _dot(e.astype(BF16), v)
        den = d if den is None else den + d
        acc = a if acc is None else acc + a
    if sink is not None:
        den = den + jnp.exp(sink - m)
    return acc / den


def _group_attention(q_ref, kv, keys, values, sink_ref, mask, stack):
    t = q_ref.shape[0]
    lane = lax.broadcasted_iota(jnp.int32, (t, LANES), 1)
    low = lane < HD
    zero = jnp.zeros((t, LANES), BF16)
    chunks = [q_ref[:, (kv * 2 + c) * LANES:(kv * 2 + c + 1) * LANES] for c in range(2)]
    if not stack:
        outs = []
        for h in range(GROUP):
            qm = jnp.where(low, chunks[h // 2], zero) if h % 2 == 0 else jnp.where(low, zero, chunks[h // 2])
            scores = [_dot_nt(qm, k) for k in keys]
            if mask is not None:
                scores = [s if m is None else jnp.where(m, s, NEG_INF) for s, m in zip(scores, mask)]
            sink = None if sink_ref is None else sink_ref[kv * GROUP + h]
            outs.append(_softmax_pv(scores, values, sink))
        return (jnp.where(low, outs[0], outs[1]).astype(BF16), jnp.where(low, outs[2], outs[3]).astype(BF16))
    qs = jnp.concatenate([jnp.where(low, chunks[0], zero), jnp.where(low, zero, chunks[0]),
                          jnp.where(low, chunks[1], zero), jnp.where(low, zero, chunks[1])], axis=0)
    scores = [_dot_nt(qs, k) for k in keys]
    if mask is not None:
        scores = [s if m is None else jnp.where(m, s, NEG_INF) for s, m in zip(scores, mask)]
    sink = None
    if sink_ref is not None:
        head = lax.broadcasted_iota(jnp.int32, (GROUP * t, 1), 0) // t
        sink = jnp.full((GROUP * t, 1), sink_ref[kv * GROUP], F32)
        for h in range(1, GROUP):
            sink = jnp.where(head == h, sink_ref[kv * GROUP + h], sink)
    out = _softmax_pv(scores, values, sink)
    return (jnp.where(low, out[0:t], out[t:2 * t]).astype(BF16),
            jnp.where(low, out[2 * t:3 * t], out[3 * t:4 * t]).astype(BF16))


def _ctx_attn_kernel(has_sink, first, *refs):
    if has_sink:
        sink_ref, refs = refs[0], refs[1:]
    else:
        sink_ref = None
    q_ref, k_ref, v_ref = refs[:3]
    o_ref, nk_ref, nv_ref = refs[-3:]
    lane = lax.broadcasted_iota(jnp.int32, (L_CTX, LANES), 1)
    low = lane < HD
    for s in range(CTX_SEQS):
        rows = pl.ds(s * L_CTX, L_CTX)
        k = k_ref[rows, :].astype(F32)
        v = v_ref[rows, :].astype(F32)
        k_sw = pltpu.roll(k, HD, axis=1)
        v_sw = pltpu.roll(v, HD, axis=1)
        nk_ref[s, 0, 0] = k[:, 0:HD]
        nk_ref[s, 0, 1] = k_sw[:, 0:HD]
        nv_ref[s, 0, 0] = v[:, 0:HD]
        nv_ref[s, 0, 1] = v_sw[:, 0:HD]
        if first:
            nk_ref[s, 1:] = jnp.zeros((DEPTH - 1, N_KV, L_CTX, HD), F32)
            nv_ref[s, 1:] = jnp.zeros((DEPTH - 1, N_KV, L_CTX, HD), F32)
        k2 = (jnp.where(low, k, k_sw).astype(BF16), jnp.where(low, k_sw, k).astype(BF16))
        v2 = (jnp.where(low, v, v_sw).astype(BF16), jnp.where(low, v_sw, v).astype(BF16))
        for kv in range(N_KV):
            o0, o1 = _group_attention(q_ref.at[rows, :], kv, [k2[kv]], [v2[kv]], sink_ref, None, True)
            o_ref[rows, (2 * kv) * LANES:(2 * kv + 1) * LANES] = o0
            o_ref[rows, (2 * kv + 1) * LANES:(2 * kv + 2) * LANES] = o1


def _ctx_attn(layer, q, k, v, sink, new_kv):
    has_sink = sink is not None
    first = new_kv is None
    row = lambda b: (b, 0)
    rows = CTX_SEQS * L_CTX
    in_specs = [pl.BlockSpec((rows, Q_W), row), pl.BlockSpec((rows, KV_W), row), pl.BlockSpec((rows, KV_W), row)]
    args = [q, k, v]
    if has_sink:
        in_specs = [pl.BlockSpec(memory_space=pltpu.SMEM)] + in_specs
        args = [sink] + args
    aliases = {}
    if first:
        kv_spec = pl.BlockSpec((CTX_SEQS, DEPTH, N_KV, L_CTX, HD), lambda b: (b, 0, 0, 0, 0))
    else:
        kv_spec = pl.BlockSpec((CTX_SEQS, 1, N_KV, L_CTX, HD), lambda b: (b, layer, 0, 0, 0))
        aliases = {len(args): 1, len(args) + 1: 2}
        in_specs = in_specs + [pl.BlockSpec(memory_space=pl.ANY)] * 2
        args = args + list(new_kv)
    kv_shape = jax.ShapeDtypeStruct((N_SEQ_CTX, DEPTH, N_KV, L_CTX, HD), F32)
    o, nk, nv = pl.pallas_call(
        functools.partial(_ctx_attn_kernel, has_sink, first),
        grid=(N_SEQ_CTX // CTX_SEQS,),
        in_specs=in_specs,
        out_specs=[pl.BlockSpec((rows, Q_W), row), kv_spec, kv_spec],
        out_shape=[jax.ShapeDtypeStruct((N_CTX_TOK, Q_W), BF16), kv_shape, kv_shape],
        input_output_aliases=aliases,
        compiler_params=_cparams(("arbitrary",)),
        name="ctx_attn_sink" if has_sink else "ctx_attn",
    )(*args)
    return o, (nk, nv)


def _lat_attn_kernel(windowed, *refs):
    if windowed:
        sink_ref, q_ref, k_ref, v_ref, ck_ref, cv_ref, dup_ref, o_ref, k2s, v2s, ck2s, cv2s = refs
    else:
        q_ref, k_ref, v_ref, ck_ref, cv_ref, dup_ref, o_ref, k2s, v2s, ck2s, cv2s = refs
        sink_ref = None
    j = pl.program_id(1)

    @pl.when(j == 0)
    def _():
        ka, kb = _dup_heads(k_ref[...].astype(F32))
        va, vb = _dup_heads(v_ref[...].astype(F32))
        k2s[0] = ka.astype(BF16)
        k2s[1] = kb.astype(BF16)
        v2s[0] = va.astype(BF16)
        v2s[1] = vb.astype(BF16)
        dup = dup_ref[...]
        for kv in range(N_KV):
            ck2s[kv] = _dot(ck_ref[0, 0, kv].astype(BF16), dup).astype(BF16)
            cv2s[kv] = _dot(cv_ref[0, 0, kv].astype(BF16), dup).astype(BF16)

    mask = None
    if windowed:
        n_loc = TQ + 2 * WINDOW
        start = pl.multiple_of(jnp.clip(j * TQ - WINDOW, 0, L_LAT - n_loc), WINDOW)
        qpos = j * TQ + lax.broadcasted_iota(jnp.int32, (TQ, n_loc), 0)
        kpos = start + lax.broadcasted_iota(jnp.int32, (TQ, n_loc), 1)
        mask = [None, jnp.abs(qpos - kpos) <= WINDOW]
    for kv in range(N_KV):
        if windowed:
            k_own = k2s[kv, pl.ds(start, n_loc), :]
            v_own = v2s[kv, pl.ds(start, n_loc), :]
        else:
            k_own = k2s[kv]
            v_own = v2s[kv]
        o0, o1 = _group_attention(q_ref, kv, [ck2s[kv], k_own], [cv2s[kv], v_own], sink_ref, mask, False)
        o_ref[:, (2 * kv) * LANES:(2 * kv + 1) * LANES] = o0
        o_ref[:, (2 * kv + 1) * LANES:(2 * kv + 2) * LANES] = o1


def _lat_attn(layer, q, k, v, cache_k, cache_v, dup, sink):
    windowed = sink is not None
    n_qt = L_LAT // TQ
    ctx_tiles = N_CTX_TOK // TQ
    ctx_seqs = N_CTX_TOK // L_LAT
    cache_spec = pl.BlockSpec((1, 1, N_KV, PAST, HD), lambda b, j: (b, layer, 0, 0, 0))
    in_specs = [
        pl.BlockSpec((TQ, Q_W), lambda b, j: (ctx_tiles + b * n_qt + j, 0)),
        pl.BlockSpec((L_LAT, KV_W), lambda b, j: (ctx_seqs + b, 0)),
        pl.BlockSpec((L_LAT, KV_W), lambda b, j: (ctx_seqs + b, 0)),
        cache_spec,
        cache_spec,
        pl.BlockSpec((HD, LANES), lambda b, j: (0, 0)),
    ]
    args = [q, k, v, cache_k, cache_v, dup]
    if windowed:
        in_specs = [pl.BlockSpec(memory_space=pltpu.SMEM)] + in_specs
        args = [sink] + args
    return pl.pallas_call(
        functools.partial(_lat_attn_kernel, windowed),
        grid=(N_SEQ_LAT, n_qt),
        in_specs=in_specs,
        out_specs=pl.BlockSpec((TQ, Q_W), lambda b, j: (b * n_qt + j, 0)),
        out_shape=jax.ShapeDtypeStruct((N_LAT_TOK, Q_W), BF16),
        scratch_shapes=[
            pltpu.VMEM((N_KV, L_LAT, LANES), BF16),
            pltpu.VMEM((N_KV, L_LAT, LANES), BF16),
            pltpu.VMEM((N_KV, PAST, LANES), BF16),
            pltpu.VMEM((N_KV, PAST, LANES), BF16),
        ],
        compiler_params=_cparams(("arbitrary", "arbitrary")),
        name="lat_attn_window" if windowed else "lat_attn",
    )(*args)


def _seq_pos(i, width):
    r = lax.broadcasted_iota(jnp.int32, (TM, width), 0)
    is_ctx = i < N_CTX_TOK // TM
    seq_len = jnp.where(is_ctx, L_CTX, L_LAT)
    return r & (seq_len - 1), seq_len


def _shift_rows(t, j, pos, seq_len):
    if j == 0:
        return t
    moved = pltpu.roll(t, (-j) % TM, axis=0)
    ok = (pos + j >= 0) & (pos + j < seq_len)
    return jnp.where(ok, moved, 0.0)


def _conv_mixer(zc_ref, cw_ref, pos, seq_len):
    bg = zc_ref[:, 0:CONV_W].astype(F32)
    u = zc_ref[:, CONV_W:2 * CONV_W].astype(F32) * zc_ref[:, 2 * CONV_W:3 * CONV_W].astype(F32)
    y = (_shift_rows(u, -1, pos, seq_len) * cw_ref[0:1, :] + u * cw_ref[1:2, :]
         + _shift_rows(u, 1, pos, seq_len) * cw_ref[2:3, :])
    return (bg * y).astype(BF16)


def _window_sum(x_seg, w):
    n = x_seg.shape[0] + 2 * POOL_PAD
    z = jnp.zeros((POOL_PAD, x_seg.shape[1]), F32)
    a = jnp.concatenate([z, x_seg, z], axis=0)
    a = a + pltpu.roll(a, 1, axis=0)
    half = 1
    while 2 * half < w:
        a = pltpu.roll(a, half, axis=0) + pltpu.roll(a, n - half, axis=0)
        half *= 2
    return a[POOL_PAD:POOL_PAD + x_seg.shape[0]]


def _pool_mixer(xd_ref, pw_ref, ps_ref, yd_ref, seq_len):
    t = lax.broadcasted_iota(jnp.int32, (seq_len, POOL_G), 0)
    for gi, w in enumerate(POOL_WINDOWS):
        cnt = (jnp.minimum(t + w // 2, seq_len) - jnp.maximum(t - w // 2, 0)).astype(F32)
        wg = pw_ref[gi].astype(BF16)
        lanes = slice(gi * POOL_G, (gi + 1) * POOL_G)
        for s in range(TM // seq_len):
            rows = slice(s * seq_len, (s + 1) * seq_len)
            xg = xd_ref[rows, lanes]
            d = _window_sum(xg, w) / cnt - xg
            yd_ref[rows, lanes] = (_dot(d.astype(BF16), wg) * ps_ref[:, lanes]).astype(BF16)


def _mix_out_kernel(even, nx, *refs):
    x_refs = refs[:nx]
    mod_ref, oc_ref, ol_ref, w_ref = refs[nx:nx + 4]
    rest = refs[nx + 4:]
    if even:
        zc_ref, cw_ref, out_ref, wbf = rest
    else:
        (xd_ref, pw_ref, ps_ref, g2_ref, rw_ref, rb_ref, tri_ref, upper_ref,
         out_ref, xl_ref, meta_ref, cnt_ref, wbf, yd_s) = rest
    i = pl.program_id(0)

    @pl.when(i == 0)
    def _():
        _cast_rows(w_ref.at[0], wbf, D_MODEL)

    o = jnp.where(i < N_CTX_TOK // TM, oc_ref[...], ol_ref[...])
    if even:
        y_o = _dot(o, wbf[CONV_W:, :])
        pos, seq_len = _seq_pos(i, 1)
        ya = _conv_mixer(zc_ref, cw_ref, pos, seq_len)
        y = _dot(ya, wbf[0:CONV_W, :]) + y_o
    else:
        @pl.when(i < N_CTX_TOK // TM)
        def _():
            _pool_mixer(xd_ref, pw_ref, ps_ref, yd_s, L_CTX)

        @pl.when(i >= N_CTX_TOK // TM)
        def _():
            _pool_mixer(xd_ref, pw_ref, ps_ref, yd_s, L_LAT)

        y = _dot(o, wbf[0:Q_W, :]) + _dot(yd_s[...], wbf[Q_W:, :])
    out_ref[...] = _read_x(i, x_refs, TM) + mod_ref[0, 2:3, :] * y
    if not even:
        for part in range(TM // TR):
            rows = pl.ds(part * TR, TR)
            _route_tile(out_ref[rows, :], mod_ref, g2_ref, rw_ref, rb_ref, tri_ref, upper_ref,
                        xl_ref.at[pl.ds(part * LOCAL_ROWS, LOCAL_ROWS), :], meta_ref.at[rows, :],
                        cnt_ref.at[pl.ds(part * 8, 8), :])


def _mix_out(even, layer_i, x, mod_l, o_ctx, o_lat, w_out, *extra):
    n_ctx = N_CTX_TOK // TM
    n_lat = N_LAT_TOK // TM
    row = lambda i: (i, 0)
    const = lambda i: (0, 0)
    in_specs = _x_specs(TM, len(x) == 2) + [
        pl.BlockSpec((1, 6, D_MODEL), lambda i: (_mod_row(i, TM), 0, 0)),
        pl.BlockSpec((TM, Q_W), lambda i: (jnp.minimum(i, n_ctx - 1), 0)),
        pl.BlockSpec((TM, Q_W), lambda i: (jnp.clip(i - n_ctx, 0, n_lat - 1), 0)),
        pl.BlockSpec((1, D_MODEL, D_MODEL), lambda i: (layer_i, 0, 0)),
    ]
    if even:
        in_specs += [pl.BlockSpec((TM, 3 * CONV_W), row), pl.BlockSpec((3, CONV_W), const)]
    else:
        in_specs += [pl.BlockSpec((TM, POOL_W), row),
                     pl.BlockSpec((len(POOL_WINDOWS), POOL_G, POOL_G), lambda i: (0, 0, 0)),
                     pl.BlockSpec((1, POOL_W), const),
                     pl.BlockSpec((1, D_MODEL), const),
                     pl.BlockSpec((D_MODEL, LANES), const),
                     pl.BlockSpec((1, LANES), const),
                     pl.BlockSpec((TR, TR), const),
                     pl.BlockSpec((LANES, LANES), const)]
    out_specs = [pl.BlockSpec((TM, D_MODEL), row)]
    out_shape = [jax.ShapeDtypeStruct((N_TOK, D_MODEL), F32)]
    if not even:
        parts = TM // TR
        out_specs += [pl.BlockSpec((parts * LOCAL_ROWS, D_MODEL), row), pl.BlockSpec((TM, LANES), row),
                      pl.BlockSpec((parts * 8, LANES), row)]
        out_shape += [jax.ShapeDtypeStruct((N_TOK // TR * LOCAL_ROWS, D_MODEL), BF16),
                      jax.ShapeDtypeStruct((N_TOK, LANES), F32),
                      jax.ShapeDtypeStruct((N_TOK // TR * 8, LANES), F32)]
    return pl.pallas_call(
        functools.partial(_mix_out_kernel, even, len(x)),
        grid=(N_TOK // TM,),
        in_specs=in_specs,
        out_specs=out_specs,
        out_shape=out_shape,
        scratch_shapes=[pltpu.VMEM((D_MODEL, D_MODEL), BF16)] + ([] if even else [pltpu.VMEM((TM, POOL_W), BF16)]),
        compiler_params=_cparams(("arbitrary",)),
        name="mix_out_even" if even else "mix_out_odd",
    )(*x, mod_l, o_ctx, o_lat, w_out, *extra)


def _ffn_kernel(layer_i, x_ref, mod_ref, g_ref, w1_hbm, w3_hbm, w2_hbm, out_ref,
                w1s, w3s, w2s, st1, st3, st2, hs, hid, sem):
    i = pl.program_id(0)
    nf = D_FF // TF_FFN

    def chunk_copies(f, slot):
        cols = pl.ds(f * TF_FFN, TF_FFN)
        return (pltpu.make_async_copy(w1_hbm.at[layer_i, :, cols], st1.at[slot], sem.at[0, slot]),
                pltpu.make_async_copy(w3_hbm.at[layer_i, :, cols], st3.at[slot], sem.at[1, slot]),
                pltpu.make_async_copy(w2_hbm.at[layer_i, cols, :], st2.at[slot], sem.at[2, slot]))

    @pl.when(i == 0)
    def _():
        for c in chunk_copies(0, 0):
            c.start()

    hs[...] = _normmod(x_ref[...], g_ref[...], mod_ref[0, 4:5, :], mod_ref[0, 3:4, :]).astype(BF16)
    for f in range(nf):
        slot = f % 2
        lo, hi = f * TF_FFN, (f + 1) * TF_FFN

        @pl.when(i == 0)
        def _(f=f, slot=slot, lo=lo, hi=hi):
            if f + 1 < nf:
                for c in chunk_copies(f + 1, 1 - slot):
                    c.start()
            for c in chunk_copies(f, slot):
                c.wait()
            w1s[:, lo:hi] = st1[slot].astype(BF16)
            w3s[:, lo:hi] = st3[slot].astype(BF16)
            w2s[lo:hi, :] = st2[slot].astype(BF16)

        h = hs[...]
        hid[:, lo:hi] = (_silu(_dot(h, w1s[:, lo:hi])) * _dot(h, w3s[:, lo:hi])).astype(BF16)
    out_ref[...] = x_ref[...] + mod_ref[0, 5:6, :] * _dot(hid[...], w2s[...])


def _ffn(layer_i, x, mod_l, g, w1, w3, w2):
    row = lambda i: (i, 0)
    any_spec = pl.BlockSpec(memory_space=pl.ANY)
    return pl.pallas_call(
        functools.partial(_ffn_kernel, layer_i),
        grid=(N_TOK // TM,),
        in_specs=[
            pl.BlockSpec((TM, D_MODEL), row),
            pl.BlockSpec((1, 6, D_MODEL), lambda i: (_mod_row(i, TM), 0, 0)),
            pl.BlockSpec((1, D_MODEL), lambda i: (0, 0)),
            any_spec, any_spec, any_spec,
        ],
        out_specs=pl.BlockSpec((TM, D_MODEL), row),
        out_shape=jax.ShapeDtypeStruct((N_TOK, D_MODEL), F32),
        scratch_shapes=[
            pltpu.VMEM((D_MODEL, D_FF), BF16),
            pltpu.VMEM((D_MODEL, D_FF), BF16),
            pltpu.VMEM((D_FF, D_MODEL), BF16),
            pltpu.VMEM((2, D_MODEL, TF_FFN), F32),
            pltpu.VMEM((2, D_MODEL, TF_FFN), F32),
            pltpu.VMEM((2, TF_FFN, D_MODEL), F32),
            pltpu.VMEM((TM, D_MODEL), BF16),
            pltpu.VMEM((TM, D_FF), BF16),
            pltpu.SemaphoreType.DMA((3, 2)),
        ],
        compiler_params=pltpu.CompilerParams(dimension_semantics=("arbitrary",), vmem_limit_bytes=FFN_VMEM_LIMIT),
        name="ffn",
    )(x, mod_l, g, w1, w3, w2)


def _split_bf16(t):
    hi = t.astype(BF16)
    return hi, (t - hi.astype(F32)).astype(BF16)


def _lane_values(col_vals, ones_rows):
    q = jnp.floor(col_vals * (1.0 / 32.0))
    r = col_vals - 32.0 * q
    t = 32.0 * _dot_nt(ones_rows, q.astype(BF16)) + _dot_nt(ones_rows, r.astype(BF16))
    return t[0:1, :]


def _route_tile(x, mod_ref, g_ref, rw_ref, rb_ref, tri_ref, upper_ref, xl_ref, meta_ref, cnt_ref):
    h = _normmod(x, g_ref[...], mod_ref[0, 4:5, :], mod_ref[0, 3:4, :])
    h_hi, h_lo = _split_bf16(h)
    w_hi, w_lo = _split_bf16(rw_ref[...])
    logits = _dot(h_hi, w_hi) + _dot(h_hi, w_lo) + _dot(h_lo, w_hi) + rb_ref[...]
    lane = lax.broadcasted_iota(jnp.int32, logits.shape, 1).astype(F32)
    logits = jnp.where(lane < N_EXP, logits, NEG_INF)
    m1 = logits.max(axis=-1, keepdims=True)
    i1 = jnp.where(logits == m1, lane, float(LANES)).min(axis=-1, keepdims=True)
    rest = jnp.where(lane == i1, NEG_INF, logits)
    m2 = rest.max(axis=-1, keepdims=True)
    i2 = jnp.where(rest == m2, lane, float(LANES)).min(axis=-1, keepdims=True)
    e2 = jnp.exp(m2 - m1)
    den = 1.0 + e2
    g1 = 1.0 / den
    g2 = e2 / den

    oh_a = jnp.where(lane == i1, 1.0, 0.0)
    oh_b = jnp.where(lane == i2, 1.0, 0.0)
    tri = tri_ref[...]
    cnt_a = oh_a.sum(axis=0, keepdims=True)
    cnt_b = oh_b.sum(axis=0, keepdims=True)
    run16 = jnp.floor((cnt_a + cnt_b + (RUN_ALIGN - 1)) * (1.0 / RUN_ALIGN))
    run16_rows = jnp.broadcast_to(run16, (8, LANES))
    start = RUN_ALIGN * _dot(run16_rows.astype(BF16), upper_ref[...])[0:1, :]
    row_a = oh_a * (start + _dot(tri, oh_a.astype(BF16)))
    row_b = oh_b * (start + cnt_a + _dot(tri, oh_b.astype(BF16)))

    ones_rows = jnp.ones((8, LANES), BF16)
    tok_a = _lane_values(row_a, ones_rows)
    tok_b = _lane_values(row_b, ones_rows)
    sorted_row = lax.broadcasted_iota(jnp.int32, (LOCAL_ROWS, TR), 0).astype(F32)
    perm = jnp.where((sorted_row == tok_a) | (sorted_row == tok_b), 1.0, 0.0).astype(BF16)
    xl_ref[...] = _dot(perm, h_hi).astype(BF16)

    meta = jnp.zeros_like(logits)
    cols = (row_a.sum(axis=-1, keepdims=True), row_b.sum(axis=-1, keepdims=True), g1, g2)
    for k, col in enumerate(cols):
        meta = jnp.where(lane == k, col, meta)
    meta_ref[...] = meta
    cnt_ref[...] = RUN_ALIGN * run16_rows


def _moe_plan(counts):
    n_tiles = N_TOK // TR
    run = counts.reshape(n_tiles, 8, LANES)[:, 0, :N_EXP].astype(jnp.int32)
    per_expert = jnp.sum(run, axis=0)
    region = (per_expert + (TMS - 1)) // TMS * TMS
    ends = jnp.cumsum(region)
    offs = ends - region
    seg_end = jnp.cumsum(run, axis=0)
    seg_start = seg_end - run
    local_end = jnp.cumsum(run, axis=1)
    local_start = local_end - run
    n_chunks = local_end[:, N_EXP - 1] // RUN_ALIGN
    tile_start = jnp.arange(MOE_TILES, dtype=jnp.int32) * TMS
    tile_expert = jnp.sum((tile_start[:, None] >= ends[None, :]).astype(jnp.int32), axis=1)
    tile_expert = jnp.minimum(tile_expert, N_EXP - 1)
    n_used = (ends[N_EXP - 1] // TMS).reshape(1)
    experts = jnp.arange(N_EXP, dtype=jnp.int32)
    g_row = jnp.arange(MOE_ROWS // RUN_ALIGN, dtype=jnp.int32) * RUN_ALIGN
    g_exp = jnp.repeat(tile_expert, TMS // RUN_ALIGN)
    pick = g_exp[:, None] == experts[None, :]
    rel = g_row - jnp.sum(jnp.where(pick, offs[None, :], 0), axis=1)
    ends_of = jnp.sum(jnp.where(pick[:, None, :], seg_end[None, :, :], 0), axis=2)
    src_tile = jnp.sum((rel[:, None] >= ends_of).astype(jnp.int32), axis=1)
    valid = (src_tile < n_tiles) & (jnp.repeat(jnp.arange(MOE_TILES), TMS // RUN_ALIGN) < n_used[0])
    src_tile = jnp.minimum(src_tile, n_tiles - 1)
    sel = (src_tile[:, None, None] == jnp.arange(n_tiles)[None, :, None]) & pick[:, None, :]
    shift = jnp.sum(jnp.where(sel, (local_start - seg_start)[None, :, :], 0), axis=(1, 2))
    chunk_src = src_tile * LOCAL_ROWS + rel + shift
    n_valid = jnp.sum(valid.reshape(MOE_TILES, TMS // RUN_ALIGN).astype(jnp.int32), axis=1)
    return chunk_src, n_valid, n_chunks, tile_expert, n_used


def _chunk_copy(src_ref, src_row, dst_ref, dst_row, sem):
    return pltpu.make_async_copy(src_ref.at[pl.ds(src_row, RUN_ALIGN), :],
                                 dst_ref.at[pl.ds(dst_row, RUN_ALIGN), :], sem)


CHUNKS = TMS // RUN_ALIGN
LOCAL_CHUNKS = LOCAL_ROWS // RUN_ALIGN


def _expert_kernel(te_ref, nu_ref, src_ref, nv_ref, nc_ref, xl_hbm, w1_ref, w3_ref, w2_ref, yl_hbm,
                   xbuf, ybuf, zbuf, w1s, w3s, w2s, sem_in, sem_out, sem_zero):
    t = pl.program_id(0)
    n_used = nu_ref[0]
    slot = t & 1

    def for_chunks(n, body):
        @pl.when(n == CHUNKS)
        def _():
            for j in range(CHUNKS):
                body(j)

        @pl.when(n < CHUNKS)
        def _():
            def step(j, c):
                body(j)
                return c

            lax.fori_loop(0, n, step, 0)

    def gather(tile, s):
        n = nv_ref[tile]

        def issue(j):
            src = pl.multiple_of(src_ref[tile * CHUNKS + j], RUN_ALIGN)
            _chunk_copy(xl_hbm, src, xbuf.at[s], pl.multiple_of(j * RUN_ALIGN, RUN_ALIGN), sem_in.at[s]).start()

        def pad(j, c):
            xbuf[s, pl.ds(pl.multiple_of(j * RUN_ALIGN, RUN_ALIGN), RUN_ALIGN), :] = jnp.zeros(
                (RUN_ALIGN, D_MODEL), BF16)
            return c

        for_chunks(n, issue)
        lax.fori_loop(n, CHUNKS, pad, 0)

    def scatter(tile, s):
        def issue(j):
            dst = pl.multiple_of(src_ref[tile * CHUNKS + j], RUN_ALIGN)
            _chunk_copy(ybuf.at[s], pl.multiple_of(j * RUN_ALIGN, RUN_ALIGN), yl_hbm, dst, sem_out.at[s]).start()

        for_chunks(nv_ref[tile], issue)

    def drain(count, sem):
        @pl.when(count == CHUNKS)
        def _():
            pltpu.make_async_copy(xl_hbm.at[pl.ds(0, TMS), :], xbuf.at[0], sem).wait()

        @pl.when(count != CHUNKS)
        def _():
            def one(j, c):
                _chunk_copy(xl_hbm, 0, xbuf.at[0], 0, sem).wait()
                return c

            lax.fori_loop(0, count, one, 0)

    @pl.when(t == 0)
    def _():
        zbuf[...] = jnp.zeros_like(zbuf)
        total = jnp.int32(0)
        for tile in range(N_TOK // TR):
            n = nc_ref[tile]

            def clear(j, c, tile=tile):
                row = pl.multiple_of(tile * LOCAL_ROWS + j * RUN_ALIGN, RUN_ALIGN)
                _chunk_copy(zbuf, 0, yl_hbm, row, sem_zero).start()
                return c

            lax.fori_loop(n, LOCAL_CHUNKS, clear, 0)
            total = total + (LOCAL_CHUNKS - n)
        drain(total, sem_zero)
        gather(0, 0)

    @pl.when(t < n_used)
    def _():
        drain(nv_ref[t], sem_in.at[slot])

    @pl.when(t + 1 < n_used)
    def _():
        gather(t + 1, 1 - slot)

    new_expert = (t == 0) | (te_ref[t] != te_ref[jnp.maximum(t - 1, 0)])

    @pl.when(new_expert)
    def _():
        _cast_rows(w1_ref.at[0, 0], w1s, D_MODEL)
        _cast_rows(w3_ref.at[0, 0], w3s, D_MODEL)
        _cast_rows(w2_ref.at[0, 0], w2s, D_FF_E)

    @pl.when((t >= 2) & (t - 2 < n_used))
    def _():
        drain(nv_ref[jnp.maximum(t - 2, 0)], sem_out.at[slot])

    @pl.when(t < n_used)
    def _():
        x = xbuf[slot]
        hid = (_silu(_dot(x, w1s[...])) * _dot(x, w3s[...])).astype(BF16)
        ybuf[slot] = _dot(hid, w2s[...]).astype(BF16)
        scatter(t, slot)

    @pl.when(t == pl.num_programs(0) - 1)
    def _():
        @pl.when((t >= 1) & (t - 1 < n_used))
        def _():
            drain(nv_ref[jnp.maximum(t - 1, 0)], sem_out.at[1 - slot])

        @pl.when(t < n_used)
        def _():
            drain(nv_ref[t], sem_out.at[slot])


def _experts(layer_i, plan, xl, w1, w3, w2):
    chunk_src, n_valid, n_chunks, tile_expert, n_used = plan
    wsel = lambda t, te, nu, src, nv, nc: (layer_i, te[t], 0, 0)
    any_spec = pl.BlockSpec(memory_space=pl.ANY)
    return pl.pallas_call(
        _expert_kernel,
        grid_spec=pltpu.PrefetchScalarGridSpec(
            num_scalar_prefetch=5,
            grid=(MOE_TILES,),
            in_specs=[
                any_spec,
                pl.BlockSpec((1, 1, D_MODEL, D_FF_E), wsel),
                pl.BlockSpec((1, 1, D_MODEL, D_FF_E), wsel),
                pl.BlockSpec((1, 1, D_FF_E, D_MODEL), wsel),
            ],
            out_specs=any_spec,
            scratch_shapes=[
                pltpu.VMEM((2, TMS, D_MODEL), BF16),
                pltpu.VMEM((2, TMS, D_MODEL), BF16),
                pltpu.VMEM((RUN_ALIGN, D_MODEL), BF16),
                pltpu.VMEM((D_MODEL, D_FF_E), BF16),
                pltpu.VMEM((D_MODEL, D_FF_E), BF16),
                pltpu.VMEM((D_FF_E, D_MODEL), BF16),
                pltpu.SemaphoreType.DMA((2,)),
                pltpu.SemaphoreType.DMA((2,)),
                pltpu.SemaphoreType.DMA(()),
            ],
        ),
        out_shape=jax.ShapeDtypeStruct(xl.shape, BF16),
        compiler_params=_cparams(("arbitrary",)),
        name="moe_experts",
    )(tile_expert, n_used, chunk_src, n_valid, n_chunks, xl, w1, w3, w2)


def _combine_kernel(x_ref, mod_ref, meta_ref, yl_ref, *out_refs):
    sorted_row = lax.broadcasted_iota(jnp.int32, (TR, LOCAL_ROWS), 1).astype(F32)
    y = yl_ref[...]
    pick_a = jnp.where(sorted_row == meta_ref[:, 0:1], 1.0, 0.0).astype(BF16)
    pick_b = jnp.where(sorted_row == meta_ref[:, 1:2], 1.0, 0.0).astype(BF16)
    f = meta_ref[:, 2:3] * _dot(pick_a, y) + meta_ref[:, 3:4] * _dot(pick_b, y)
    _write_x(pl.program_id(0), out_refs, TR, x_ref[...] + mod_ref[0, 5:6, :] * f)


def _combine(x, mod_l, meta, yl, split_out):
    row = lambda i: (i, 0)
    return pl.pallas_call(
        _combine_kernel,
        grid=(N_TOK // TR,),
        in_specs=[
            pl.BlockSpec((TR, D_MODEL), row),
            pl.BlockSpec((1, 6, D_MODEL), lambda i: (_mod_row(i, TR), 0, 0)),
            pl.BlockSpec((TR, LANES), row),
            pl.BlockSpec((LOCAL_ROWS, D_MODEL), row),
        ],
        out_specs=_x_specs(TR, split_out),
        out_shape=_x_shapes(split_out),
        compiler_params=_cparams(("arbitrary",)),
        name="moe_combine",
    )(x, mod_l, meta, yl)


def _moe(layer_i, x, routed, mod_l, w1, w3, w2, split_out):
    xl, meta, counts = routed
    yl = _experts(layer_i, _moe_plan(counts), xl, w1, w3, w2)
    return _combine(x, mod_l, meta, yl, split_out)


def _rope_tables():
    n_rows = L_LAT // GRID_W
    rows = np.repeat(np.arange(n_rows, dtype=np.float32), GRID_W)
    cols = np.tile(np.arange(GRID_W, dtype=np.float32), n_rows)
    quarter = HD // 4
    inv = (np.float32(ROPE_THETA) ** (-np.arange(quarter, dtype=np.float32) / np.float32(quarter))).astype(np.float32)
    ang_r = (rows[:, None] * inv).astype(np.float32)
    ang_c = (cols[:, None] * inv).astype(np.float32)
    cos = np.concatenate([np.cos(ang_r)] * 2 + [np.cos(ang_c)] * 2, axis=1)
    sin = np.concatenate([-np.sin(ang_r), np.sin(ang_r), -np.sin(ang_c), np.sin(ang_c)], axis=1)
    cos = np.concatenate([cos, cos], axis=1)
    sin = np.concatenate([sin, sin], axis=1)
    cos = np.concatenate([np.ones((L_LAT, LANES), np.float32), cos], axis=0)
    sin = np.concatenate([np.zeros((L_LAT, LANES), np.float32), sin], axis=0)
    return jnp.asarray(cos, F32), jnp.asarray(sin, F32)


def _block_ones(width):
    r = np.arange(width) // HD
    return jnp.asarray(r[:, None] == r[None, :], BF16)


def kernel(x_prompt, x_sample, cache_k, cache_v, c, c_ctx, norm1, norm2, w_mod, b_mod, ev_w_in, ev_conv, ev_q_norm, ev_k_norm, ev_w_out, od_w_in, od_q_norm, od_k_norm, od_sink, od_pool_w, od_pool_scale, od_w_out, ffn_w1, ffn_w3, ffn_w2, moe_router, moe_router_b, moe_w1, moe_w3, moe_w2):
    x = (x_prompt.reshape(N_CTX_TOK, D_MODEL), x_sample.reshape(N_LAT_TOK, D_MODEL))
    cond = jnp.concatenate([c_ctx[None, :], c, jnp.zeros((MOD_ROWS - 1 - N_SEQ_LAT, D_MODEL), F32)], axis=0)
    mod = _modulation(cond, w_mod, b_mod).reshape(DEPTH, MOD_ROWS, 6, D_MODEL)

    cos_tab, sin_tab = _rope_tables()
    ones_q = _block_ones(Q_W)
    ones_k = _block_ones(KV_W)
    dup = jnp.asarray(np.concatenate([np.eye(HD), np.eye(HD)], axis=1), BF16)
    tri = jnp.asarray(np.arange(TR)[:, None] > np.arange(TR)[None, :], BF16)
    upper = jnp.asarray(np.arange(LANES)[:, None] < np.arange(LANES)[None, :], BF16)

    new_kv = None
    for l in range(DEPTH):
        i = l // 2
        even = l % 2 == 0
        mod_l = mod[l]
        g1 = norm1[l][None, :]
        g2 = norm2[l][None, :]
        if even:
            q_gain, k_gain = ev_q_norm[i], ev_k_norm[i]
            w_in, w_out = ev_w_in, ev_w_out
        else:
            q_gain, k_gain = od_q_norm[i], od_k_norm[i]
            w_in, w_out = od_w_in, od_w_out
        q_gain = jnp.tile(q_gain, N_Q)[None, :]
        k_gain = jnp.tile(k_gain, N_KV)[None, :]
        outs = _in_proj(even, i, x, mod_l, g1, w_in, ones_q, ones_k, q_gain, k_gain, cos_tab, sin_tab)
        if even:
            zc, q, k, v = outs
            sink = None
        else:
            q, k, v, xd = outs
            sink = od_sink[i]
        o_ctx, new_kv = _ctx_attn(l, q, k, v, sink, new_kv)
        o_lat = _lat_attn(l, q, k, v, cache_k, cache_v, dup, sink)
        if even:
            (x1,) = _mix_out(True, i, x, mod_l, o_ctx, o_lat, w_out, zc, ev_conv[i])
            x = (_ffn(i, x1, mod_l, g2, ffn_w1, ffn_w3, ffn_w2),)
        else:
            rw = jnp.pad(moe_router[i], ((0, 0), (0, LANES - N_EXP)))
            rb = jnp.pad(moe_router_b[i], (0, LANES - N_EXP))[None, :]
            x1, *routed = _mix_out(False, i, x, mod_l, o_ctx, o_lat, w_out, xd, od_pool_w[i],
                                   od_pool_scale[i][None, :], g2, rw, rb, tri, upper)
            x = tuple(_moe(i, x1, routed, mod_l, moe_w1, moe_w3, moe_w2, l == DEPTH - 1))

    y_prompt = x[0].reshape(N_SEQ_CTX, L_CTX, D_MODEL)
    y_sample = x[1].reshape(N_SEQ_LAT, L_LAT, D_MODEL)
    return (y_prompt, y_sample, new_kv[0], new_kv[1])
```

```python
import functools

import jax
import jax.numpy as jnp
import numpy as np
from jax import lax
from jax.experimental import pallas as pl
from jax.experimental.pallas import tpu as pltpu

F32 = jnp.float32
BF16 = jnp.bfloat16

D_MODEL = 1024
N_SEQ_CTX = 32
L_CTX = 256
N_SEQ_LAT = 4
L_LAT = 1024
DEPTH = 4
PAST = 512
GRID_W = 64
HD = 64
N_Q = 8
N_KV = 2
GROUP = N_Q // N_KV
Q_W = N_Q * HD
KV_W = N_KV * HD
CONV_W = 512
POOL_W = 512
POOL_WINDOWS = (2, 4, 8, 16)
POOL_G = 128
POOL_PAD = 16
EVEN_IN = 3 * CONV_W + Q_W + 2 * KV_W
ODD_IN = Q_W + 2 * KV_W + POOL_W
WINDOW = 128
D_FF = 2816
N_EXP = 8
TOP_K = 2
D_FF_E = 1024
ROPE_THETA = 10000.0
EPS = 1e-6

N_CTX_TOK = N_SEQ_CTX * L_CTX
N_LAT_TOK = N_SEQ_LAT * L_LAT
N_TOK = N_CTX_TOK + N_LAT_TOK
MOD_ROWS = 16

LANES = 128
VMEM_LIMIT = 56 * 1024 * 1024
FFN_VMEM_LIMIT = 62 * 1024 * 1024

TM_IN = 1024
TM = 1024
TQ = 256
CTX_SEQS = 4
TF_FFN = 256
TR = 512
RUN_ALIGN = 16
LOCAL_ROWS = TOP_K * TR + N_EXP * RUN_ALIGN
TMS = 512
MOE_TILES = -(-(TOP_K * N_TOK + (N_TOK // TR) * N_EXP * (RUN_ALIGN - 1) + N_EXP * (TMS - 1)) // TMS)
MOE_ROWS = MOE_TILES * TMS
NEG_INF = float("-inf")


def _cparams(sem):
    return pltpu.CompilerParams(dimension_semantics=sem, vmem_limit_bytes=VMEM_LIMIT)


def _mod_row(i, tm):
    n_ctx = N_CTX_TOK // tm
    return jnp.where(i < n_ctx, 0, 1 + (i - n_ctx) // (L_LAT // tm))


def _x_specs(tm, split):
    if not split:
        return [pl.BlockSpec((tm, D_MODEL), lambda i, *_: (i, 0))]
    n_ctx = N_CTX_TOK // tm
    n_lat = N_LAT_TOK // tm
    return [pl.BlockSpec((tm, D_MODEL), lambda i, *_: (jnp.minimum(i, n_ctx - 1), 0)),
            pl.BlockSpec((tm, D_MODEL), lambda i, *_: (jnp.clip(i - n_ctx, 0, n_lat - 1), 0))]


def _x_shapes(split):
    if not split:
        return [jax.ShapeDtypeStruct((N_TOK, D_MODEL), F32)]
    return [jax.ShapeDtypeStruct((N_CTX_TOK, D_MODEL), F32), jax.ShapeDtypeStruct((N_LAT_TOK, D_MODEL), F32)]


def _read_x(i, x_refs, tm):
    if len(x_refs) == 1:
        return x_refs[0][...]
    return jnp.where(i < N_CTX_TOK // tm, x_refs[0][...], x_refs[1][...])


def _write_x(i, o_refs, tm, val):
    if len(o_refs) == 1:
        o_refs[0][...] = val
        return

    @pl.when(i < N_CTX_TOK // tm)
    def _():
        o_refs[0][...] = val

    @pl.when(i >= N_CTX_TOK // tm)
    def _():
        o_refs[1][...] = val


def _normmod(x, g, scale, shift):
    ms = jnp.mean(x * x, axis=-1, keepdims=True)
    y = x * lax.rsqrt(ms + EPS) * g
    return y * (1.0 + scale) + shift


def _silu(x):
    return x * jax.nn.sigmoid(x)


def _dot(a, b):
    return jnp.dot(a, b, preferred_element_type=F32)


def _dot_nt(a, b):
    return lax.dot_general(a, b, (((1,), (1,)), ((), ())), preferred_element_type=F32)


def _cast_rows(src_ref, dst_ref, rows, chunk=256):
    for r in range(0, rows, chunk):
        dst_ref[r:r + chunk, :] = src_ref[r:r + chunk, :].astype(dst_ref.dtype)


def _mod_kernel(c_ref, w_ref, b_ref, o_ref):
    s = _silu(c_ref[...]).astype(BF16)
    o_ref[0] = _dot(s, w_ref[0].astype(BF16)) + b_ref[0]


def _modulation(cond, w_mod, b_mod):
    tn = 1536
    return pl.pallas_call(
        _mod_kernel,
        grid=(DEPTH, 6 * D_MODEL // tn),
        in_specs=[
            pl.BlockSpec((MOD_ROWS, D_MODEL), lambda l, j: (0, 0)),
            pl.BlockSpec((1, D_MODEL, tn), lambda l, j: (l, 0, j)),
            pl.BlockSpec((1, 1, tn), lambda l, j: (l, 0, j)),
        ],
        out_specs=pl.BlockSpec((1, MOD_ROWS, tn), lambda l, j: (l, 0, j)),
        out_shape=jax.ShapeDtypeStruct((DEPTH, MOD_ROWS, 6 * D_MODEL), F32),
        compiler_params=_cparams(("arbitrary", "arbitrary")),
        name="modulation",
    )(cond, w_mod, b_mod.reshape(DEPTH, 1, 6 * D_MODEL))


def _head_rms(t, ones_bd, gain):
    ssq = _dot((t * t).astype(BF16), ones_bd)
    return t * lax.rsqrt(ssq * (1.0 / HD) + EPS) * gain


def _rope(t, cos, sin_signed):
    lane = lax.broadcasted_iota(jnp.int32, (t.shape[0], LANES), 1)
    first = (lane & 31) < 16
    outs = []
    for c in range(t.shape[1] // LANES):
        tc = t[:, c * LANES:(c + 1) * LANES]
        nxt = pltpu.roll(tc, LANES - 16, axis=1)
        prv = pltpu.roll(tc, 16, axis=1)
        outs.append(tc * cos + jnp.where(first, nxt, prv) * sin_signed)
    return outs[0] if len(outs) == 1 else jnp.concatenate(outs, axis=1)


def _in_proj_kernel(even, nx, *refs):
    x_refs = refs[:nx]
    mod_ref, g_ref, w_ref, onesq_ref, onesk_ref, qg_ref, kg_ref, cos_ref, sin_ref = refs[nx:nx + 9]
    rest = refs[nx + 9:]
    if even:
        zc_ref, q_ref, k_ref, v_ref, wbf = rest
        q0 = 3 * CONV_W
    else:
        q_ref, k_ref, v_ref, xd_ref, wbf = rest
        q0 = 0
    k0 = q0 + Q_W
    v0 = k0 + KV_W

    @pl.when(pl.program_id(0) == 0)
    def _():
        _cast_rows(w_ref.at[0], wbf, D_MODEL)

    x = _read_x(pl.program_id(0), x_refs, TM_IN)
    h = _normmod(x, g_ref[...], mod_ref[0, 1:2, :], mod_ref[0, 0:1, :]).astype(BF16)
    cos = cos_ref[...]
    sin = sin_ref[...]

    q = _dot(h, wbf[:, q0:q0 + Q_W])
    q = _rope(_head_rms(q, onesq_ref[...], qg_ref[...]), cos, sin) * (HD ** -0.5)
    q_ref[...] = q.astype(BF16)

    kv = _dot(h, wbf[:, k0:k0 + 2 * KV_W])
    k = _rope(_head_rms(kv[:, :KV_W], onesk_ref[...], kg_ref[...]), cos, sin)
    k_ref[...] = k.astype(BF16)
    v_ref[...] = kv[:, KV_W:].astype(BF16)

    if even:
        zc_ref[...] = _dot(h, wbf[:, 0:3 * CONV_W]).astype(BF16)
    else:
        xd_ref[...] = _dot(h, wbf[:, v0 + KV_W:v0 + KV_W + POOL_W])


def _in_proj(even, layer_i, x, mod_l, g, w, ones_q, ones_k, q_gain, k_gain, cos_tab, sin_tab):
    tm = TM_IN
    n_in = EVEN_IN if even else ODD_IN
    n_ctx = N_CTX_TOK // tm
    per_seq = L_LAT // tm

    def rope_idx(i):
        return (jnp.where(i < n_ctx, 0, per_seq + (i - n_ctx) % per_seq), 0)

    row = lambda i: (i, 0)
    const = lambda i: (0, 0)
    in_specs = _x_specs(tm, len(x) == 2) + [
        pl.BlockSpec((1, 6, D_MODEL), lambda i: (_mod_row(i, tm), 0, 0)),
        pl.BlockSpec((1, D_MODEL), const),
        pl.BlockSpec((1, D_MODEL, n_in), lambda i: (layer_i, 0, 0)),
        pl.BlockSpec((Q_W, Q_W), const),
        pl.BlockSpec((KV_W, KV_W), const),
        pl.BlockSpec((1, Q_W), const),
        pl.BlockSpec((1, KV_W), const),
        pl.BlockSpec((tm, LANES), rope_idx),
        pl.BlockSpec((tm, LANES), rope_idx),
    ]
    qkv_specs = [pl.BlockSpec((tm, Q_W), row), pl.BlockSpec((tm, KV_W), row), pl.BlockSpec((tm, KV_W), row)]
    qkv_shapes = [jax.ShapeDtypeStruct((N_TOK, Q_W), BF16), jax.ShapeDtypeStruct((N_TOK, KV_W), BF16),
                  jax.ShapeDtypeStruct((N_TOK, KV_W), BF16)]
    if even:
        out_specs = [pl.BlockSpec((tm, 3 * CONV_W), row)] + qkv_specs
        out_shape = [jax.ShapeDtypeStruct((N_TOK, 3 * CONV_W), BF16)] + qkv_shapes
    else:
        out_specs = qkv_specs + [pl.BlockSpec((tm, POOL_W), row)]
        out_shape = qkv_shapes + [jax.ShapeDtypeStruct((N_TOK, POOL_W), F32)]
    return pl.pallas_call(
        functools.partial(_in_proj_kernel, even, len(x)),
        grid=(N_TOK // tm,),
        in_specs=in_specs,
        out_specs=out_specs,
        out_shape=out_shape,
        scratch_shapes=[pltpu.VMEM((D_MODEL, n_in), BF16)],
        compiler_params=_cparams(("arbitrary",)),
        name="in_proj_even" if even else "in_proj_odd",
    )(*x, mod_l, g, w, ones_q, ones_k, q_gain, k_gain, cos_tab, sin_tab)


def _dup_heads(t):
    lane = lax.broadcasted_iota(jnp.int32, t.shape, 1)
    swapped = pltpu.roll(t, HD, axis=1)
    low = lane < HD
    return jnp.where(low, t, swapped), jnp.where(low, swapped, t)


def _softmax_pv(scores, values, sink):
    m = scores[0].max(axis=-1, keepdims=True)
    for s in scores[1:]:
        m = jnp.maximum(m, s.max(axis=-1, keepdims=True))
    if sink is not None:
        m = jnp.maximum(m, sink)
    den = None
    acc = None
    for s, v in zip(scores, values):
        e = jnp.exp(s - m)
        d = e.sum(axis=-1, keepdims=True)
        a = _dot(e.astype(BF16), v)
        den = d if den is None else den + d
        acc = a if acc is None else acc + a
    if sink is not None:
        den = den + jnp.exp(sink - m)
    return acc / den


def _group_attention(q_ref, kv, keys, values, sink_ref, mask, stack):
    t = q_ref.shape[0]
    lane = lax.broadcasted_iota(jnp.int32, (t, LANES), 1)
    low = lane < HD
    zero = jnp.zeros((t, LANES), BF16)
    chunks = [q_ref[:, (kv * 2 + c) * LANES:(kv * 2 + c + 1) * LANES] for c in range(2)]
    if not stack:
        outs = []
        for h in range(GROUP):
            qm = jnp.where(low, chunks[h // 2], zero) if h % 2 == 0 else jnp.where(low, zero, chunks[h // 2])
            scores = [_dot_nt(qm, k) for k in keys]
            if mask is not None:
                scores = [s if m is None else jnp.where(m, s, NEG_INF) for s, m in zip(scores, mask)]
            sink = None if sink_ref is None else sink_ref[kv * GROUP + h]
            outs.append(_softmax_pv(scores, values, sink))
        return (jnp.where(low, outs[0], outs[1]).astype(BF16), jnp.where(low, outs[2], outs[3]).astype(BF16))
    qs = jnp.concatenate([jnp.where(low, chunks[0], zero), jnp.where(low, zero, chunks[0]),
                          jnp.where(low, chunks[1], zero), jnp.where(low, zero, chunks[1])], axis=0)
    scores = [_dot_nt(qs, k) for k in keys]
    if mask is not None:
        scores = [s if m is None else jnp.where(m, s, NEG_INF) for s, m in zip(scores, mask)]
    sink = None
    if sink_ref is not None:
        head = lax.broadcasted_iota(jnp.int32, (GROUP * t, 1), 0) // t
        sink = jnp.full((GROUP * t, 1), sink_ref[kv * GROUP], F32)
        for h in range(1, GROUP):
            sink = jnp.where(head == h, sink_ref[kv * GROUP + h], sink)
    out = _softmax_pv(scores, values, sink)
    return (jnp.where(low, out[0:t], out[t:2 * t]).astype(BF16),
            jnp.where(low, out[2 * t:3 * t], out[3 * t:4 * t]).astype(BF16))


def _ctx_attn_kernel(has_sink, first, *refs):
    if has_sink:
        sink_ref, refs = refs[0], refs[1:]
    else:
        sink_ref = None
    q_ref, k_ref, v_ref = refs[:3]
    o_ref, nk_ref, nv_ref = refs[-3:]
    lane = lax.broadcasted_iota(jnp.int32, (L_CTX, LANES), 1)
    low = lane < HD
    for s in range(CTX_SEQS):
        rows = pl.ds(s * L_CTX, L_CTX)
        k = k_ref[rows, :].astype(F32)
        v = v_ref[rows, :].astype(F32)
        k_sw = pltpu.roll(k, HD, axis=1)
        v_sw = pltpu.roll(v, HD, axis=1)
        nk_ref[s, 0, 0] = k[:, 0:HD]
        nk_ref[s, 0, 1] = k_sw[:, 0:HD]
        nv_ref[s, 0, 0] = v[:, 0:HD]
        nv_ref[s, 0, 1] = v_sw[:, 0:HD]
        if first:
            nk_ref[s, 1:] = jnp.zeros((DEPTH - 1, N_KV, L_CTX, HD), F32)
            nv_ref[s, 1:] = jnp.zeros((DEPTH - 1, N_KV, L_CTX, HD), F32)
        k2 = (jnp.where(low, k, k_sw).astype(BF16), jnp.where(low, k_sw, k).astype(BF16))
        v2 = (jnp.where(low, v, v_sw).astype(BF16), jnp.where(low, v_sw, v).astype(BF16))
        for kv in range(N_KV):
            o0, o1 = _group_attention(q_ref.at[rows, :], kv, [k2[kv]], [v2[kv]], sink_ref, None, True)
            o_ref[rows, (2 * kv) * LANES:(2 * kv + 1) * LANES] = o0
            o_ref[rows, (2 * kv + 1) * LANES:(2 * kv + 2) * LANES] = o1


def _ctx_attn(layer, q, k, v, sink, new_kv):
    has_sink = sink is not None
    first = new_kv is None
    row = lambda b: (b, 0)
    rows = CTX_SEQS * L_CTX
    in_specs = [pl.BlockSpec((rows, Q_W), row), pl.BlockSpec((rows, KV_W), row), pl.BlockSpec((rows, KV_W), row)]
    args = [q, k, v]
    if has_sink:
        in_specs = [pl.BlockSpec(memory_space=pltpu.SMEM)] + in_specs
        args = [sink] + args
    aliases = {}
    if first:
        kv_spec = pl.BlockSpec((CTX_SEQS, DEPTH, N_KV, L_CTX, HD), lambda b: (b, 0, 0, 0, 0))
    else:
        kv_spec = pl.BlockSpec((CTX_SEQS, 1, N_KV, L_CTX, HD), lambda b: (b, layer, 0, 0, 0))
        aliases = {len(args): 1, len(args) + 1: 2}
        in_specs = in_specs + [pl.BlockSpec(memory_space=pl.ANY)] * 2
        args = args + list(new_kv)
    kv_shape = jax.ShapeDtypeStruct((N_SEQ_CTX, DEPTH, N_KV, L_CTX, HD), F32)
    o, nk, nv = pl.pallas_call(
        functools.partial(_ctx_attn_kernel, has_sink, first),
        grid=(N_SEQ_CTX // CTX_SEQS,),
        in_specs=in_specs,
        out_specs=[pl.BlockSpec((rows, Q_W), row), kv_spec, kv_spec],
        out_shape=[jax.ShapeDtypeStruct((N_CTX_TOK, Q_W), BF16), kv_shape, kv_shape],
        input_output_aliases=aliases,
        compiler_params=_cparams(("arbitrary",)),
        name="ctx_attn_sink" if has_sink else "ctx_attn",
    )(*args)
    return o, (nk, nv)


def _lat_attn_kernel(windowed, *refs):
    if windowed:
        sink_ref, q_ref, k_ref, v_ref, ck_ref, cv_ref, dup_ref, o_ref, k2s, v2s, ck2s, cv2s = refs
    else:
        q_ref, k_ref, v_ref, ck_ref, cv_ref, dup_ref, o_ref, k2s, v2s, ck2s, cv2s = refs
        sink_ref = None
    j = pl.program_id(1)

    @pl.when(j == 0)
    def _():
        ka, kb = _dup_heads(k_ref[...].astype(F32))
        va, vb = _dup_heads(v_ref[...].astype(F32))
        k2s[0] = ka.astype(BF16)
        k2s[1] = kb.astype(BF16)
        v2s[0] = va.astype(BF16)
        v2s[1] = vb.astype(BF16)
        dup = dup_ref[...]
        for kv in range(N_KV):
            ck2s[kv] = _dot(ck_ref[0, 0, kv].astype(BF16), dup).astype(BF16)
            cv2s[kv] = _dot(cv_ref[0, 0, kv].astype(BF16), dup).astype(BF16)

    mask = None
    if windowed:
        n_loc = TQ + 2 * WINDOW
        start = pl.multiple_of(jnp.clip(j * TQ - WINDOW, 0, L_LAT - n_loc), WINDOW)
        qpos = j * TQ + lax.broadcasted_iota(jnp.int32, (TQ, n_loc), 0)
        kpos = start + lax.broadcasted_iota(jnp.int32, (TQ, n_loc), 1)
        mask = [None, jnp.abs(qpos - kpos) <= WINDOW]
    for kv in range(N_KV):
        if windowed:
            k_own = k2s[kv, pl.ds(start, n_loc), :]
            v_own = v2s[kv, pl.ds(start, n_loc), :]
        else:
            k_own = k2s[kv]
            v_own = v2s[kv]
        o0, o1 = _group_attention(q_ref, kv, [ck2s[kv], k_own], [cv2s[kv], v_own], sink_ref, mask, False)
        o_ref[:, (2 * kv) * LANES:(2 * kv + 1) * LANES] = o0
        o_ref[:, (2 * kv + 1) * LANES:(2 * kv + 2) * LANES] = o1


def _lat_attn(layer, q, k, v, cache_k, cache_v, dup, sink):
    windowed = sink is not None
    n_qt = L_LAT // TQ
    ctx_tiles = N_CTX_TOK // TQ
    ctx_seqs = N_CTX_TOK // L_LAT
    cache_spec = pl.BlockSpec((1, 1, N_KV, PAST, HD), lambda b, j: (b, layer, 0, 0, 0))
    in_specs = [
        pl.BlockSpec((TQ, Q_W), lambda b, j: (ctx_tiles + b * n_qt + j, 0)),
        pl.BlockSpec((L_LAT, KV_W), lambda b, j: (ctx_seqs + b, 0)),
        pl.BlockSpec((L_LAT, KV_W), lambda b, j: (ctx_seqs + b, 0)),
        cache_spec,
        cache_spec,
        pl.BlockSpec((HD, LANES), lambda b, j: (0, 0)),
    ]
    args = [q, k, v, cache_k, cache_v, dup]
    if windowed:
        in_specs = [pl.BlockSpec(memory_space=pltpu.SMEM)] + in_specs
        args = [sink] + args
    return pl.pallas_call(
        functools.partial(_lat_attn_kernel, windowed),
        grid=(N_SEQ_LAT, n_qt),
        in_specs=in_specs,
        out_specs=pl.BlockSpec((TQ, Q_W), lambda b, j: (b * n_qt + j, 0)),
        out_shape=jax.ShapeDtypeStruct((N_LAT_TOK, Q_W), BF16),
        scratch_shapes=[
            pltpu.VMEM((N_KV, L_LAT, LANES), BF16),
            pltpu.VMEM((N_KV, L_LAT, LANES), BF16),
            pltpu.VMEM((N_KV, PAST, LANES), BF16),
            pltpu.VMEM((N_KV, PAST, LANES), BF16),
        ],
        compiler_params=_cparams(("arbitrary", "arbitrary")),
        name="lat_attn_window" if windowed else "lat_attn",
    )(*args)


def _seq_pos(i, width):
    r = lax.broadcasted_iota(jnp.int32, (TM, width), 0)
    is_ctx = i < N_CTX_TOK // TM
    seq_len = jnp.where(is_ctx, L_CTX, L_LAT)
    return r & (seq_len - 1), seq_len


def _shift_rows(t, j, pos, seq_len):
    if j == 0:
        return t
    moved = pltpu.roll(t, (-j) % TM, axis=0)
    ok = (pos + j >= 0) & (pos + j < seq_len)
    return jnp.where(ok, moved, 0.0)


def _conv_mixer(zc_ref, cw_ref, pos, seq_len):
    bg = zc_ref[:, 0:CONV_W].astype(F32)
    u = zc_ref[:, CONV_W:2 * CONV_W].astype(F32) * zc_ref[:, 2 * CONV_W:3 * CONV_W].astype(F32)
    y = (_shift_rows(u, -1, pos, seq_len) * cw_ref[0:1, :] + u * cw_ref[1:2, :]
         + _shift_rows(u, 1, pos, seq_len) * cw_ref[2:3, :])
    return (bg * y).astype(BF16)


def _window_sum(x_seg, w):
    n = x_seg.shape[0] + 2 * POOL_PAD
    z = jnp.zeros((POOL_PAD, x_seg.shape[1]), F32)
    a = jnp.concatenate([z, x_seg, z], axis=0)
    a = a + pltpu.roll(a, 1, axis=0)
    half = 1
    while 2 * half < w:
        a = pltpu.roll(a, half, axis=0) + pltpu.roll(a, n - half, axis=0)
        half *= 2
    return a[POOL_PAD:POOL_PAD + x_seg.shape[0]]


def _pool_mixer(xd_ref, pw_ref, ps_ref, yd_ref, seq_len):
    t = lax.broadcasted_iota(jnp.int32, (seq_len, POOL_G), 0)
    for gi, w in enumerate(POOL_WINDOWS):
        cnt = (jnp.minimum(t + w // 2, seq_len) - jnp.maximum(t - w // 2, 0)).astype(F32)
        wg = pw_ref[gi].astype(BF16)
        lanes = slice(gi * POOL_G, (gi + 1) * POOL_G)
        for s in range(TM // seq_len):
            rows = slice(s * seq_len, (s + 1) * seq_len)
            xg = xd_ref[rows, lanes]
            d = _window_sum(xg, w) / cnt - xg
            yd_ref[rows, lanes] = (_dot(d.astype(BF16), wg) * ps_ref[:, lanes]).astype(BF16)


def _mix_out_kernel(even, nx, *refs):
    x_refs = refs[:nx]
    mod_ref, oc_ref, ol_ref, w_ref = refs[nx:nx + 4]
    rest = refs[nx + 4:]
    if even:
        zc_ref, cw_ref, out_ref, wbf = rest
    else:
        (xd_ref, pw_ref, ps_ref, g2_ref, rw_ref, rb_ref, tri_ref, upper_ref,
         out_ref, xl_ref, meta_ref, cnt_ref, wbf, yd_s) = rest
    i = pl.program_id(0)

    @pl.when(i == 0)
    def _():
        _cast_rows(w_ref.at[0], wbf, D_MODEL)

    o = jnp.where(i < N_CTX_TOK // TM, oc_ref[...], ol_ref[...])
    if even:
        y_o = _dot(o, wbf[CONV_W:, :])
        pos, seq_len = _seq_pos(i, 1)
        ya = _conv_mixer(zc_ref, cw_ref, pos, seq_len)
        y = _dot(ya, wbf[0:CONV_W, :]) + y_o
    else:
        @pl.when(i < N_CTX_TOK // TM)
        def _():
            _pool_mixer(xd_ref, pw_ref, ps_ref, yd_s, L_CTX)

        @pl.when(i >= N_CTX_TOK // TM)
        def _():
            _pool_mixer(xd_ref, pw_ref, ps_ref, yd_s, L_LAT)

        y = _dot(o, wbf[0:Q_W, :]) + _dot(yd_s[...], wbf[Q_W:, :])
    out_ref[...] = _read_x(i, x_refs, TM) + mod_ref[0, 2:3, :] * y
    if not even:
        for part in range(TM // TR):
            rows = pl.ds(part * TR, TR)
            _route_tile(out_ref[rows, :], mod_ref, g2_ref, rw_ref, rb_ref, tri_ref, upper_ref,
                        xl_ref.at[pl.ds(part * LOCAL_ROWS, LOCAL_ROWS), :], meta_ref.at[rows, :],
                        cnt_ref.at[pl.ds(part * 8, 8), :])


def _mix_out(even, layer_i, x, mod_l, o_ctx, o_lat, w_out, *extra):
    n_ctx = N_CTX_TOK // TM
    n_lat = N_LAT_TOK // TM
    row = lambda i: (i, 0)
    const = lambda i: (0, 0)
    in_specs = _x_specs(TM, len(x) == 2) + [
        pl.BlockSpec((1, 6, D_MODEL), lambda i: (_mod_row(i, TM), 0, 0)),
        pl.BlockSpec((TM, Q_W), lambda i: (jnp.minimum(i, n_ctx - 1), 0)),
        pl.BlockSpec((TM, Q_W), lambda i: (jnp.clip(i - n_ctx, 0, n_lat - 1), 0)),
        pl.BlockSpec((1, D_MODEL, D_MODEL), lambda i: (layer_i, 0, 0)),
    ]
    if even:
        in_specs += [pl.BlockSpec((TM, 3 * CONV_W), row), pl.BlockSpec((3, CONV_W), const)]
    else:
        in_specs += [pl.BlockSpec((TM, POOL_W), row),
                     pl.BlockSpec((len(POOL_WINDOWS), POOL_G, POOL_G), lambda i: (0, 0, 0)),
                     pl.BlockSpec((1, POOL_W), const),
                     pl.BlockSpec((1, D_MODEL), const),
                     pl.BlockSpec((D_MODEL, LANES), const),
                     pl.BlockSpec((1, LANES), const),
                     pl.BlockSpec((TR, TR), const),
                     pl.BlockSpec((LANES, LANES), const)]
    out_specs = [pl.BlockSpec((TM, D_MODEL), row)]
    out_shape = [jax.ShapeDtypeStruct((N_TOK, D_MODEL), F32)]
    if not even:
        parts = TM // TR
        out_specs += [pl.BlockSpec((parts * LOCAL_ROWS, D_MODEL), row), pl.BlockSpec((TM, LANES), row),
                      pl.BlockSpec((parts * 8, LANES), row)]
        out_shape += [jax.ShapeDtypeStruct((N_TOK // TR * LOCAL_ROWS, D_MODEL), BF16),
                      jax.ShapeDtypeStruct((N_TOK, LANES), F32),
                      jax.ShapeDtypeStruct((N_TOK // TR * 8, LANES), F32)]
    return pl.pallas_call(
        functools.partial(_mix_out_kernel, even, len(x)),
        grid=(N_TOK // TM,),
        in_specs=in_specs,
        out_specs=out_specs,
        out_shape=out_shape,
        scratch_shapes=[pltpu.VMEM((D_MODEL, D_MODEL), BF16)] + ([] if even else [pltpu.VMEM((TM, POOL_W), BF16)]),
        compiler_params=_cparams(("arbitrary",)),
        name="mix_out_even" if even else "mix_out_odd",
    )(*x, mod_l, o_ctx, o_lat, w_out, *extra)


def _ffn_kernel(layer_i, x_ref, mod_ref, g_ref, w1_hbm, w3_hbm, w2_hbm, out_ref,
                w1s, w3s, w2s, st1, st3, st2, hs, hid, sem):
    i = pl.program_id(0)
    nf = D_FF // TF_FFN

    def chunk_copies(f, slot):
        cols = pl.ds(f * TF_FFN, TF_FFN)
        return (pltpu.make_async_copy(w1_hbm.at[layer_i, :, cols], st1.at[slot], sem.at[0, slot]),
                pltpu.make_async_copy(w3_hbm.at[layer_i, :, cols], st3.at[slot], sem.at[1, slot]),
                pltpu.make_async_copy(w2_hbm.at[layer_i, cols, :], st2.at[slot], sem.at[2, slot]))

    @pl.when(i == 0)
    def _():
        for c in chunk_copies(0, 0):
            c.start()

    hs[...] = _normmod(x_ref[...], g_ref[...], mod_ref[0, 4:5, :], mod_ref[0, 3:4, :]).astype(BF16)
    for f in range(nf):
        slot = f % 2
        lo, hi = f * TF_FFN, (f + 1) * TF_FFN

        @pl.when(i == 0)
        def _(f=f, slot=slot, lo=lo, hi=hi):
            if f + 1 < nf:
                for c in chunk_copies(f + 1, 1 - slot):
                    c.start()
            for c in chunk_copies(f, slot):
                c.wait()
            w1s[:, lo:hi] = st1[slot].astype(BF16)
            w3s[:, lo:hi] = st3[slot].astype(BF16)
            w2s[lo:hi, :] = st2[slot].astype(BF16)

        h = hs[...]
        hid[:, lo:hi] = (_silu(_dot(h, w1s[:, lo:hi])) * _dot(h, w3s[:, lo:hi])).astype(BF16)
    out_ref[...] = x_ref[...] + mod_ref[0, 5:6, :] * _dot(hid[...], w2s[...])


def _ffn(layer_i, x, mod_l, g, w1, w3, w2):
    row = lambda i: (i, 0)
    any_spec = pl.BlockSpec(memory_space=pl.ANY)
    return pl.pallas_call(
        functools.partial(_ffn_kernel, layer_i),
        grid=(N_TOK // TM,),
        in_specs=[
            pl.BlockSpec((TM, D_MODEL), row),
            pl.BlockSpec((1, 6, D_MODEL), lambda i: (_mod_row(i, TM), 0, 0)),
            pl.BlockSpec((1, D_MODEL), lambda i: (0, 0)),
            any_spec, any_spec, any_spec,
        ],
        out_specs=pl.BlockSpec((TM, D_MODEL), row),
        out_shape=jax.ShapeDtypeStruct((N_TOK, D_MODEL), F32),
        scratch_shapes=[
            pltpu.VMEM((D_MODEL, D_FF), BF16),
            pltpu.VMEM((D_MODEL, D_FF), BF16),
            pltpu.VMEM((D_FF, D_MODEL), BF16),
            pltpu.VMEM((2, D_MODEL, TF_FFN), F32),
            pltpu.VMEM((2, D_MODEL, TF_FFN), F32),
            pltpu.VMEM((2, TF_FFN, D_MODEL), F32),
            pltpu.VMEM((TM, D_MODEL), BF16),
            pltpu.VMEM((TM, D_FF), BF16),
            pltpu.SemaphoreType.DMA((3, 2)),
        ],
        compiler_params=pltpu.CompilerParams(dimension_semantics=("arbitrary",), vmem_limit_bytes=FFN_VMEM_LIMIT),
        name="ffn",
    )(x, mod_l, g, w1, w3, w2)


def _split_bf16(t):
    hi = t.astype(BF16)
    return hi, (t - hi.astype(F32)).astype(BF16)


def _lane_values(col_vals, ones_rows):
    q = jnp.floor(col_vals * (1.0 / 32.0))
    r = col_vals - 32.0 * q
    t = 32.0 * _dot_nt(ones_rows, q.astype(BF16)) + _dot_nt(ones_rows, r.astype(BF16))
    return t[0:1, :]


def _route_tile(x, mod_ref, g_ref, rw_ref, rb_ref, tri_ref, upper_ref, xl_ref, meta_ref, cnt_ref):
    h = _normmod(x, g_ref[...], mod_ref[0, 4:5, :], mod_ref[0, 3:4, :])
    h_hi, h_lo = _split_bf16(h)
    w_hi, w_lo = _split_bf16(rw_ref[...])
    both = _dot(h_hi, jnp.concatenate([w_hi, w_lo], axis=1))
    logits = both[:, :LANES] + both[:, LANES:] + _dot(h_lo, w_hi) + rb_ref[...]
    lane = lax.broadcasted_iota(jnp.int32, logits.shape, 1).astype(F32)
    logits = jnp.where(lane < N_EXP, logits, NEG_INF)
    m1 = logits.max(axis=-1, keepdims=True)
    i1 = jnp.where(logits == m1, lane, float(LANES)).min(axis=-1, keepdims=True)
    rest = jnp.where(lane == i1, NEG_INF, logits)
    m2 = rest.max(axis=-1, keepdims=True)
    i2 = jnp.where(rest == m2, lane, float(LANES)).min(axis=-1, keepdims=True)
    e2 = jnp.exp(m2 - m1)
    den = 1.0 + e2
    g1 = 1.0 / den
    g2 = e2 / den

    oh_a = jnp.where(lane == i1, 1.0, 0.0)
    oh_b = jnp.where(lane == i2, 1.0, 0.0)
    tri = tri_ref[...]
    cnt_a = oh_a.sum(axis=0, keepdims=True)
    cnt_b = oh_b.sum(axis=0, keepdims=True)
    run16 = jnp.floor((cnt_a + cnt_b + (RUN_ALIGN - 1)) * (1.0 / RUN_ALIGN))
    run16_rows = jnp.broadcast_to(run16, (8, LANES))
    start = RUN_ALIGN * _dot(run16_rows.astype(BF16), upper_ref[...])[0:1, :]
    before = _dot(tri, jnp.concatenate([oh_a, oh_b], axis=1).astype(BF16))
    row_a = oh_a * (start + before[:, :LANES])
    row_b = oh_b * (start + cnt_a + before[:, LANES:])

    ones_rows = jnp.ones((8, LANES), BF16)
    tok_a = _lane_values(row_a, ones_rows)
    tok_b = _lane_values(row_b, ones_rows)
    sorted_row = lax.broadcasted_iota(jnp.int32, (LOCAL_ROWS, TR), 0).astype(F32)
    perm = jnp.where((sorted_row == tok_a) | (sorted_row == tok_b), 1.0, 0.0).astype(BF16)
    xl_ref[...] = _dot(perm, h_hi).astype(BF16)

    meta = jnp.zeros_like(logits)
    cols = (row_a.sum(axis=-1, keepdims=True), row_b.sum(axis=-1, keepdims=True), g1, g2)
    for k, col in enumerate(cols):
        meta = jnp.where(lane == k, col, meta)
    meta_ref[...] = meta
    cnt_ref[...] = RUN_ALIGN * run16_rows


def _moe_plan(counts):
    n_tiles = N_TOK // TR
    run = counts.reshape(n_tiles, 8, LANES)[:, 0, :N_EXP].astype(jnp.int32)
    per_expert = jnp.sum(run, axis=0)
    region = (per_expert + (TMS - 1)) // TMS * TMS
    ends = jnp.cumsum(region)
    offs = ends - region
    seg_end = jnp.cumsum(run, axis=0)
    seg_start = seg_end - run
    local_end = jnp.cumsum(run, axis=1)
    local_start = local_end - run
    n_chunks = local_end[:, N_EXP - 1] // RUN_ALIGN
    tile_start = jnp.arange(MOE_TILES, dtype=jnp.int32) * TMS
    tile_expert = jnp.sum((tile_start[:, None] >= ends[None, :]).astype(jnp.int32), axis=1)
    tile_expert = jnp.minimum(tile_expert, N_EXP - 1)
    n_used = (ends[N_EXP - 1] // TMS).reshape(1)
    experts = jnp.arange(N_EXP, dtype=jnp.int32)
    g_row = jnp.arange(MOE_ROWS // RUN_ALIGN, dtype=jnp.int32) * RUN_ALIGN
    g_exp = jnp.repeat(tile_expert, TMS // RUN_ALIGN)
    pick = g_exp[:, None] == experts[None, :]
    rel = g_row - jnp.sum(jnp.where(pick, offs[None, :], 0), axis=1)
    ends_of = jnp.sum(jnp.where(pick[:, None, :], seg_end[None, :, :], 0), axis=2)
    src_tile = jnp.sum((rel[:, None] >= ends_of).astype(jnp.int32), axis=1)
    valid = (src_tile < n_tiles) & (jnp.repeat(jnp.arange(MOE_TILES), TMS // RUN_ALIGN) < n_used[0])
    src_tile = jnp.minimum(src_tile, n_tiles - 1)
    sel = (src_tile[:, None, None] == jnp.arange(n_tiles)[None, :, None]) & pick[:, None, :]
    shift = jnp.sum(jnp.where(sel, (local_start - seg_start)[None, :, :], 0), axis=(1, 2))
    chunk_src = src_tile * LOCAL_ROWS + rel + shift
    n_valid = jnp.sum(valid.reshape(MOE_TILES, TMS // RUN_ALIGN).astype(jnp.int32), axis=1)
    return chunk_src, n_valid, n_chunks, tile_expert, n_used


def _chunk_copy(src_ref, src_row, dst_ref, dst_row, sem):
    return pltpu.make_async_copy(src_ref.at[pl.ds(src_row, RUN_ALIGN), :],
                                 dst_ref.at[pl.ds(dst_row, RUN_ALIGN), :], sem)


CHUNKS = TMS // RUN_ALIGN
LOCAL_CHUNKS = LOCAL_ROWS // RUN_ALIGN


def _expert_kernel(te_ref, nu_ref, src_ref, nv_ref, nc_ref, xl_hbm, w1_ref, w3_ref, w2_ref, yl_hbm,
                   xbuf, ybuf, zbuf, w1s, w3s, w2s, sem_in, sem_out, sem_zero):
    t = pl.program_id(0)
    n_used = nu_ref[0]
    slot = t & 1

    def for_chunks(n, body):
        @pl.when(n == CHUNKS)
        def _():
            for j in range(CHUNKS):
                body(j)

        @pl.when(n < CHUNKS)
        def _():
            def step(j, c):
                body(j)
                return c

            lax.fori_loop(0, n, step, 0)

    def gather(tile, s):
        n = nv_ref[tile]

        def issue(j):
            src = pl.multiple_of(src_ref[tile * CHUNKS + j], RUN_ALIGN)
            _chunk_copy(xl_hbm, src, xbuf.at[s], pl.multiple_of(j * RUN_ALIGN, RUN_ALIGN), sem_in.at[s]).start()

        def pad(j, c):
            xbuf[s, pl.ds(pl.multiple_of(j * RUN_ALIGN, RUN_ALIGN), RUN_ALIGN), :] = jnp.zeros(
                (RUN_ALIGN, D_MODEL), BF16)
            return c

        for_chunks(n, issue)
        lax.fori_loop(n, CHUNKS, pad, 0)

    def scatter(tile, s):
        def issue(j):
            dst = pl.multiple_of(src_ref[tile * CHUNKS + j], RUN_ALIGN)
            _chunk_copy(ybuf.at[s], pl.multiple_of(j * RUN_ALIGN, RUN_ALIGN), yl_hbm, dst, sem_out.at[s]).start()

        for_chunks(nv_ref[tile], issue)

    def drain(count, sem):
        @pl.when(count == CHUNKS)
        def _():
            pltpu.make_async_copy(xl_hbm.at[pl.ds(0, TMS), :], xbuf.at[0], sem).wait()

        @pl.when(count != CHUNKS)
        def _():
            def one(j, c):
                _chunk_copy(xl_hbm, 0, xbuf.at[0], 0, sem).wait()
                return c

            lax.fori_loop(0, count, one, 0)

    @pl.when(t == 0)
    def _():
        zbuf[...] = jnp.zeros_like(zbuf)
        total = jnp.int32(0)
        for tile in range(N_TOK // TR):
            n = nc_ref[tile]

            def clear(j, c, tile=tile):
                row = pl.multiple_of(tile * LOCAL_ROWS + j * RUN_ALIGN, RUN_ALIGN)
                _chunk_copy(zbuf, 0, yl_hbm, row, sem_zero).start()
                return c

            lax.fori_loop(n, LOCAL_CHUNKS, clear, 0)
            total = total + (LOCAL_CHUNKS - n)
        drain(total, sem_zero)
        gather(0, 0)

    @pl.when(t < n_used)
    def _():
        drain(nv_ref[t], sem_in.at[slot])

    @pl.when(t + 1 < n_used)
    def _():
        gather(t + 1, 1 - slot)

    new_expert = (t == 0) | (te_ref[t] != te_ref[jnp.maximum(t - 1, 0)])

    @pl.when(new_expert)
    def _():
        _cast_rows(w1_ref.at[0, 0], w1s, D_MODEL)
        _cast_rows(w3_ref.at[0, 0], w3s, D_MODEL)
        _cast_rows(w2_ref.at[0, 0], w2s, D_FF_E)

    @pl.when((t >= 2) & (t - 2 < n_used))
    def _():
        drain(nv_ref[jnp.maximum(t - 2, 0)], sem_out.at[slot])

    @pl.when(t < n_used)
    def _():
        x = xbuf[slot]
        hid = (_silu(_dot(x, w1s[...])) * _dot(x, w3s[...])).astype(BF16)
        ybuf[slot] = _dot(hid, w2s[...]).astype(BF16)
        scatter(t, slot)

    @pl.when(t == pl.num_programs(0) - 1)
    def _():
        @pl.when((t >= 1) & (t - 1 < n_used))
        def _():
            drain(nv_ref[jnp.maximum(t - 1, 0)], sem_out.at[1 - slot])

        @pl.when(t < n_used)
        def _():
            drain(nv_ref[t], sem_out.at[slot])


def _experts(layer_i, plan, xl, w1, w3, w2):
    chunk_src, n_valid, n_chunks, tile_expert, n_used = plan
    wsel = lambda t, te, nu, src, nv, nc: (layer_i, te[t], 0, 0)
    any_spec = pl.BlockSpec(memory_space=pl.ANY)
    return pl.pallas_call(
        _expert_kernel,
        grid_spec=pltpu.PrefetchScalarGridSpec(
            num_scalar_prefetch=5,
            grid=(MOE_TILES,),
            in_specs=[
                any_spec,
                pl.BlockSpec((1, 1, D_MODEL, D_FF_E), wsel),
                pl.BlockSpec((1, 1, D_MODEL, D_FF_E), wsel),
                pl.BlockSpec((1, 1, D_FF_E, D_MODEL), wsel),
            ],
            out_specs=any_spec,
            scratch_shapes=[
                pltpu.VMEM((2, TMS, D_MODEL), BF16),
                pltpu.VMEM((2, TMS, D_MODEL), BF16),
                pltpu.VMEM((RUN_ALIGN, D_MODEL), BF16),
                pltpu.VMEM((D_MODEL, D_FF_E), BF16),
                pltpu.VMEM((D_MODEL, D_FF_E), BF16),
                pltpu.VMEM((D_FF_E, D_MODEL), BF16),
                pltpu.SemaphoreType.DMA((2,)),
                pltpu.SemaphoreType.DMA((2,)),
                pltpu.SemaphoreType.DMA(()),
            ],
        ),
        out_shape=jax.ShapeDtypeStruct(xl.shape, BF16),
        compiler_params=_cparams(("arbitrary",)),
        name="moe_experts",
    )(tile_expert, n_used, chunk_src, n_valid, n_chunks, xl, w1, w3, w2)


def _combine_kernel(x_ref, mod_ref, meta_ref, yl_ref, *out_refs):
    sorted_row = lax.broadcasted_iota(jnp.int32, (TR, LOCAL_ROWS), 1).astype(F32)
    y = yl_ref[...]
    pick_a = jnp.where(sorted_row == meta_ref[:, 0:1], 1.0, 0.0).astype(BF16)
    pick_b = jnp.where(sorted_row == meta_ref[:, 1:2], 1.0, 0.0).astype(BF16)
    f = meta_ref[:, 2:3] * _dot(pick_a, y) + meta_ref[:, 3:4] * _dot(pick_b, y)
    _write_x(pl.program_id(0), out_refs, TR, x_ref[...] + mod_ref[0, 5:6, :] * f)


def _combine(x, mod_l, meta, yl, split_out):
    row = lambda i: (i, 0)
    return pl.pallas_call(
        _combine_kernel,
        grid=(N_TOK // TR,),
        in_specs=[
            pl.BlockSpec((TR, D_MODEL), row),
            pl.BlockSpec((1, 6, D_MODEL), lambda i: (_mod_row(i, TR), 0, 0)),
            pl.BlockSpec((TR, LANES), row),
            pl.BlockSpec((LOCAL_ROWS, D_MODEL), row),
        ],
        out_specs=_x_specs(TR, split_out),
        out_shape=_x_shapes(split_out),
        compiler_params=_cparams(("arbitrary",)),
        name="moe_combine",
    )(x, mod_l, meta, yl)


def _moe(layer_i, x, routed, mod_l, w1, w3, w2, split_out):
    xl, meta, counts = routed
    yl = _experts(layer_i, _moe_plan(counts), xl, w1, w3, w2)
    return _combine(x, mod_l, meta, yl, split_out)


def _rope_tables():
    n_rows = L_LAT // GRID_W
    rows = np.repeat(np.arange(n_rows, dtype=np.float32), GRID_W)
    cols = np.tile(np.arange(GRID_W, dtype=np.float32), n_rows)
    quarter = HD // 4
    inv = (np.float32(ROPE_THETA) ** (-np.arange(quarter, dtype=np.float32) / np.float32(quarter))).astype(np.float32)
    ang_r = (rows[:, None] * inv).astype(np.float32)
    ang_c = (cols[:, None] * inv).astype(np.float32)
    cos = np.concatenate([np.cos(ang_r)] * 2 + [np.cos(ang_c)] * 2, axis=1)
    sin = np.concatenate([-np.sin(ang_r), np.sin(ang_r), -np.sin(ang_c), np.sin(ang_c)], axis=1)
    cos = np.concatenate([cos, cos], axis=1)
    sin = np.concatenate([sin, sin], axis=1)
    cos = np.concatenate([np.ones((L_LAT, LANES), np.float32), cos], axis=0)
    sin = np.concatenate([np.zeros((L_LAT, LANES), np.float32), sin], axis=0)
    return jnp.asarray(cos, F32), jnp.asarray(sin, F32)


def _block_ones(width):
    r = np.arange(width) // HD
    return jnp.asarray(r[:, None] == r[None, :], BF16)


def kernel(x_prompt, x_sample, cache_k, cache_v, c, c_ctx, norm1, norm2, w_mod, b_mod, ev_w_in, ev_conv, ev_q_norm, ev_k_norm, ev_w_out, od_w_in, od_q_norm, od_k_norm, od_sink, od_pool_w, od_pool_scale, od_w_out, ffn_w1, ffn_w3, ffn_w2, moe_router, moe_router_b, moe_w1, moe_w3, moe_w2):
    x = (x_prompt.reshape(N_CTX_TOK, D_MODEL), x_sample.reshape(N_LAT_TOK, D_MODEL))
    cond = jnp.concatenate([c_ctx[None, :], c, jnp.zeros((MOD_ROWS - 1 - N_SEQ_LAT, D_MODEL), F32)], axis=0)
    mod = _modulation(cond, w_mod, b_mod).reshape(DEPTH, MOD_ROWS, 6, D_MODEL)

    cos_tab, sin_tab = _rope_tables()
    ones_q = _block_ones(Q_W)
    ones_k = _block_ones(KV_W)
    dup = jnp.asarray(np.concatenate([np.eye(HD), np.eye(HD)], axis=1), BF16)
    tri = jnp.asarray(np.arange(TR)[:, None] > np.arange(TR)[None, :], BF16)
    upper = jnp.asarray(np.arange(LANES)[:, None] < np.arange(LANES)[None, :], BF16)

    new_kv = None
    for l in range(DEPTH):
        i = l // 2
        even = l % 2 == 0
        mod_l = mod[l]
        g1 = norm1[l][None, :]
        g2 = norm2[l][None, :]
        if even:
            q_gain, k_gain = ev_q_norm[i], ev_k_norm[i]
            w_in, w_out = ev_w_in, ev_w_out
        else:
            q_gain, k_gain = od_q_norm[i], od_k_norm[i]
            w_in, w_out = od_w_in, od_w_out
        q_gain = jnp.tile(q_gain, N_Q)[None, :]
        k_gain = jnp.tile(k_gain, N_KV)[None, :]
        outs = _in_proj(even, i, x, mod_l, g1, w_in, ones_q, ones_k, q_gain, k_gain, cos_tab, sin_tab)
        if even:
            zc, q, k, v = outs
            sink = None
        else:
            q, k, v, xd = outs
            sink = od_sink[i]
        o_ctx, new_kv = _ctx_attn(l, q, k, v, sink, new_kv)
        o_lat = _lat_attn(l, q, k, v, cache_k, cache_v, dup, sink)
        if even:
            (x1,) = _mix_out(True, i, x, mod_l, o_ctx, o_lat, w_out, zc, ev_conv[i])
            x = (_ffn(i, x1, mod_l, g2, ffn_w1, ffn_w3, ffn_w2),)
        else:
            rw = jnp.pad(moe_router[i], ((0, 0), (0, LANES - N_EXP)))
            rb = jnp.pad(moe_router_b[i], (0, LANES - N_EXP))[None, :]
            x1, *routed = _mix_out(False, i, x, mod_l, o_ctx, o_lat, w_out, xd, od_pool_w[i],
                                   od_pool_scale[i][None, :], g2, rw, rb, tri, upper)
            x = tuple(_moe(i, x1, routed, mod_l, moe_w1, moe_w3, moe_w2, l == DEPTH - 1))

    y_prompt = x[0].reshape(N_SEQ_CTX, L_CTX, D_MODEL)
    y_sample = x[1].reshape(N_SEQ_LAT, L_LAT, D_MODEL)
    return (y_prompt, y_sample, new_kv[0], new_kv[1])
```

```python
import functools

import jax
import jax.numpy as jnp
import numpy as np
from jax import lax
from jax.experimental import pallas as pl
from jax.experimental.pallas import tpu as pltpu

F32 = jnp.float32
BF16 = jnp.bfloat16

D_MODEL = 1024
N_SEQ_CTX = 32
L_CTX = 256
N_SEQ_LAT = 4
L_LAT = 1024
DEPTH = 4
PAST = 512
GRID_W = 64
HD = 64
N_Q = 8
N_KV = 2
GROUP = N_Q // N_KV
Q_W = N_Q * HD
KV_W = N_KV * HD
CONV_W = 512
POOL_W = 512
POOL_WINDOWS = (2, 4, 8, 16)
POOL_G = 128
POOL_PAD = 16
EVEN_IN = 3 * CONV_W + Q_W + 2 * KV_W
ODD_IN = Q_W + 2 * KV_W + POOL_W
WINDOW = 128
D_FF = 2816
N_EXP = 8
TOP_K = 2
D_FF_E = 1024
ROPE_THETA = 10000.0
EPS = 1e-6

N_CTX_TOK = N_SEQ_CTX * L_CTX
N_LAT_TOK = N_SEQ_LAT * L_LAT
N_TOK = N_CTX_TOK + N_LAT_TOK
MOD_ROWS = 16

LANES = 128
SUBLANES = 8
ROPE_Q = HD // 4
VMEM_LIMIT = 56 * 1024 * 1024
FFN_VMEM_LIMIT = 62 * 1024 * 1024

TM_IN = 1024
TM = 1024
TQ_FULL = 512
TQ_WINDOW = 256
CTX_SEQS = 4
TF_FFN = 256
TR = 512
RUN_ALIGN = 16
LOCAL_ROWS = TOP_K * TR + N_EXP * RUN_ALIGN
TMS = 512
MOE_TILES = -(-(TOP_K * N_TOK + (N_TOK // TR) * N_EXP * (RUN_ALIGN - 1) + N_EXP * (TMS - 1)) // TMS)
MOE_ROWS = MOE_TILES * TMS
NEG_INF = float("-inf")


def _cparams(sem):
    return pltpu.CompilerParams(dimension_semantics=sem, vmem_limit_bytes=VMEM_LIMIT)


def _mod_row(i, tm):
    n_ctx = N_CTX_TOK // tm
    return jnp.where(i < n_ctx, 0, 1 + (i - n_ctx) // (L_LAT // tm))


def _x_specs(tm, split):
    if not split:
        return [pl.BlockSpec((tm, D_MODEL), lambda i, *_: (i, 0))]
    n_ctx = N_CTX_TOK // tm
    n_lat = N_LAT_TOK // tm
    return [pl.BlockSpec((tm, D_MODEL), lambda i, *_: (jnp.minimum(i, n_ctx - 1), 0)),
            pl.BlockSpec((tm, D_MODEL), lambda i, *_: (jnp.clip(i - n_ctx, 0, n_lat - 1), 0))]


def _x_shapes(split):
    if not split:
        return [jax.ShapeDtypeStruct((N_TOK, D_MODEL), F32)]
    return [jax.ShapeDtypeStruct((N_CTX_TOK, D_MODEL), F32), jax.ShapeDtypeStruct((N_LAT_TOK, D_MODEL), F32)]


def _read_x(i, x_refs, tm):
    if len(x_refs) == 1:
        return x_refs[0][...]
    return jnp.where(i < N_CTX_TOK // tm, x_refs[0][...], x_refs[1][...])


def _write_x(i, o_refs, tm, val):
    if len(o_refs) == 1:
        o_refs[0][...] = val
        return

    @pl.when(i < N_CTX_TOK // tm)
    def _():
        o_refs[0][...] = val

    @pl.when(i >= N_CTX_TOK // tm)
    def _():
        o_refs[1][...] = val


def _normmod(x, g, scale, shift):
    ms = jnp.mean(x * x, axis=-1, keepdims=True)
    y = x * lax.rsqrt(ms + EPS) * g
    return y * (1.0 + scale) + shift


def _silu(x):
    return x * jax.nn.sigmoid(x)


def _dot(a, b):
    return jnp.dot(a, b, preferred_element_type=F32)


def _dot_nt(a, b):
    return lax.dot_general(a, b, (((1,), (1,)), ((), ())), preferred_element_type=F32)


def _cast_rows(src_ref, dst_ref, rows, chunk=256):
    for r in range(0, rows, chunk):
        dst_ref[r:r + chunk, :] = src_ref[r:r + chunk, :].astype(dst_ref.dtype)


def _mod_kernel(c_ref, w_ref, b_ref, o_ref):
    s = _silu(c_ref[...]).astype(BF16)
    o_ref[0] = _dot(s, w_ref[0].astype(BF16)) + b_ref[0]


def _modulation(cond, w_mod, b_mod):
    tn = 1536
    return pl.pallas_call(
        _mod_kernel,
        grid=(DEPTH, 6 * D_MODEL // tn),
        in_specs=[
            pl.BlockSpec((MOD_ROWS, D_MODEL), lambda l, j: (0, 0)),
            pl.BlockSpec((1, D_MODEL, tn), lambda l, j: (l, 0, j)),
            pl.BlockSpec((1, 1, tn), lambda l, j: (l, 0, j)),
        ],
        out_specs=pl.BlockSpec((1, MOD_ROWS, tn), lambda l, j: (l, 0, j)),
        out_shape=jax.ShapeDtypeStruct((DEPTH, MOD_ROWS, 6 * D_MODEL), F32),
        compiler_params=_cparams(("arbitrary", "arbitrary")),
        name="modulation",
    )(cond, w_mod, b_mod.reshape(DEPTH, 1, 6 * D_MODEL))


def _head_rms(t, ones_bd, gain):
    ssq = _dot((t * t).astype(BF16), ones_bd)
    return t * lax.rsqrt(ssq * (1.0 / HD) + EPS) * gain


def _rope(t, cos, sin_signed):
    lane = lax.broadcasted_iota(jnp.int32, (t.shape[0], LANES), 1)
    first = (lane & (2 * ROPE_Q - 1)) < ROPE_Q
    outs = []
    for c in range(t.shape[1] // LANES):
        tc = t[:, c * LANES:(c + 1) * LANES]
        nxt = pltpu.roll(tc, LANES - ROPE_Q, axis=1)
        prv = pltpu.roll(tc, ROPE_Q, axis=1)
        outs.append(tc * cos + jnp.where(first, nxt, prv) * sin_signed)
    return outs[0] if len(outs) == 1 else jnp.concatenate(outs, axis=1)


def _in_proj_kernel(even, nx, *refs):
    x_refs = refs[:nx]
    mod_ref, g_ref, w_ref, onesq_ref, onesk_ref, qg_ref, kg_ref, cos_ref, sin_ref = refs[nx:nx + 9]
    rest = refs[nx + 9:]
    if even:
        cw_ref, ya_ref, q_ref, k_ref, v_ref, wbf = rest
        q0 = 3 * CONV_W
    else:
        q_ref, k_ref, v_ref, xd_ref, wbf = rest
        q0 = 0
    k0 = q0 + Q_W
    v0 = k0 + KV_W

    @pl.when(pl.program_id(0) == 0)
    def _():
        _cast_rows(w_ref.at[0], wbf, D_MODEL)

    x = _read_x(pl.program_id(0), x_refs, TM_IN)
    h = _normmod(x, g_ref[...], mod_ref[0, 1:2, :], mod_ref[0, 0:1, :]).astype(BF16)
    cos = cos_ref[...]
    sin = sin_ref[...]

    q = _dot(h, wbf[:, q0:q0 + Q_W])
    q = _rope(_head_rms(q, onesq_ref[...], qg_ref[...]), cos, sin) * (HD ** -0.5)
    q_ref[...] = q.astype(BF16)

    kv = _dot(h, wbf[:, k0:k0 + 2 * KV_W])
    k = _rope(_head_rms(kv[:, :KV_W], onesk_ref[...], kg_ref[...]), cos, sin)
    k_ref[...] = k.astype(BF16)
    v_ref[...] = kv[:, KV_W:].astype(BF16)

    if even:
        pos, seq_len = _seq_pos(pl.program_id(0), 1)
        bg = _dot(h, wbf[:, 0:CONV_W])
        u = _dot(h, wbf[:, CONV_W:2 * CONV_W]) * _dot(h, wbf[:, 2 * CONV_W:3 * CONV_W])
        ya_ref[...] = _conv_mixer(bg, u, cw_ref, pos, seq_len)
    else:
        xd_ref[...] = _dot(h, wbf[:, v0 + KV_W:v0 + KV_W + POOL_W])


def _in_proj(even, layer_i, x, mod_l, g, w, ones_q, ones_k, q_gain, k_gain, cos_tab, sin_tab, conv_w=None):
    assert TM_IN == TM
    tm = TM_IN
    n_in = EVEN_IN if even else ODD_IN
    n_ctx = N_CTX_TOK // tm
    per_seq = L_LAT // tm

    def rope_idx(i):
        return (jnp.where(i < n_ctx, 0, per_seq + (i - n_ctx) % per_seq), 0)

    row = lambda i: (i, 0)
    const = lambda i: (0, 0)
    in_specs = _x_specs(tm, len(x) == 2) + [
        pl.BlockSpec((1, 6, D_MODEL), lambda i: (_mod_row(i, tm), 0, 0)),
        pl.BlockSpec((1, D_MODEL), const),
        pl.BlockSpec((1, D_MODEL, n_in), lambda i: (layer_i, 0, 0)),
        pl.BlockSpec((Q_W, Q_W), const),
        pl.BlockSpec((KV_W, KV_W), const),
        pl.BlockSpec((1, Q_W), const),
        pl.BlockSpec((1, KV_W), const),
        pl.BlockSpec((tm, LANES), rope_idx),
        pl.BlockSpec((tm, LANES), rope_idx),
    ]
    qkv_specs = [pl.BlockSpec((tm, Q_W), row), pl.BlockSpec((tm, KV_W), row), pl.BlockSpec((tm, KV_W), row)]
    qkv_shapes = [jax.ShapeDtypeStruct((N_TOK, Q_W), BF16), jax.ShapeDtypeStruct((N_TOK, KV_W), BF16),
                  jax.ShapeDtypeStruct((N_TOK, KV_W), BF16)]
    args = [*x, mod_l, g, w, ones_q, ones_k, q_gain, k_gain, cos_tab, sin_tab]
    if even:
        in_specs.append(pl.BlockSpec((3, CONV_W), const))
        args.append(conv_w)
        out_specs = [pl.BlockSpec((tm, CONV_W), row)] + qkv_specs
        out_shape = [jax.ShapeDtypeStruct((N_TOK, CONV_W), BF16)] + qkv_shapes
    else:
        out_specs = qkv_specs + [pl.BlockSpec((tm, POOL_W), row)]
        out_shape = qkv_shapes + [jax.ShapeDtypeStruct((N_TOK, POOL_W), F32)]
    return pl.pallas_call(
        functools.partial(_in_proj_kernel, even, len(x)),
        grid=(N_TOK // tm,),
        in_specs=in_specs,
        out_specs=out_specs,
        out_shape=out_shape,
        scratch_shapes=[pltpu.VMEM((D_MODEL, n_in), BF16)],
        compiler_params=_cparams(("arbitrary",)),
        name="in_proj_even" if even else "in_proj_odd",
    )(*args)


def _dup_heads(t):
    lane = lax.broadcasted_iota(jnp.int32, t.shape, 1)
    swapped = pltpu.roll(t, HD, axis=1)
    low = lane < HD
    return jnp.where(low, t, swapped), jnp.where(low, swapped, t)


def _softmax_pv(scores, values, sink):
    m = scores[0].max(axis=-1, keepdims=True)
    for s in scores[1:]:
        m = jnp.maximum(m, s.max(axis=-1, keepdims=True))
    if sink is not None:
        m = jnp.maximum(m, sink)
    den = None
    acc = None
    for s, v in zip(scores, values):
        e = jnp.exp(s - m)
        d = e.sum(axis=-1, keepdims=True)
        a = _dot(e.astype(BF16), v)
        den = d if den is None else den + d
        acc = a if acc is None else acc + a
    if sink is not None:
        den = den + jnp.exp(sink - m)
    return acc / den


def _group_attention(q_ref, kv, keys, values, sink_ref, mask, stack):
    t = q_ref.shape[0]
    lane = lax.broadcasted_iota(jnp.int32, (t, LANES), 1)
    low = lane < HD
    zero = jnp.zeros((t, LANES), BF16)
    chunks = [q_ref[:, (kv * 2 + c) * LANES:(kv * 2 + c + 1) * LANES] for c in range(2)]
    if not stack:
        outs = []
        for h in range(GROUP):
            qm = jnp.where(low, chunks[h // 2], zero) if h % 2 == 0 else jnp.where(low, zero, chunks[h // 2])
            scores = [_dot_nt(qm, k) for k in keys]
            if mask is not None:
                scores = [s if m is None else jnp.where(m, s, NEG_INF) for s, m in zip(scores, mask)]
            sink = None if sink_ref is None else sink_ref[kv * GROUP + h]
            outs.append(_softmax_pv(scores, values, sink))
        return (jnp.where(low, outs[0], outs[1]).astype(BF16), jnp.where(low, outs[2], outs[3]).astype(BF16))
    qs = jnp.concatenate([jnp.where(low, chunks[0], zero), jnp.where(low, zero, chunks[0]),
                          jnp.where(low, chunks[1], zero), jnp.where(low, zero, chunks[1])], axis=0)
    scores = [_dot_nt(qs, k) for k in keys]
    if mask is not None:
        scores = [s if m is None else jnp.where(m, s, NEG_INF) for s, m in zip(scores, mask)]
    sink = None
    if sink_ref is not None:
        head = lax.broadcasted_iota(jnp.int32, (GROUP * t, 1), 0) // t
        sink = jnp.full((GROUP * t, 1), sink_ref[kv * GROUP], F32)
        for h in range(1, GROUP):
            sink = jnp.where(head == h, sink_ref[kv * GROUP + h], sink)
    out = _softmax_pv(scores, values, sink)
    return (jnp.where(low, out[0:t], out[t:2 * t]).astype(BF16),
            jnp.where(low, out[2 * t:3 * t], out[3 * t:4 * t]).astype(BF16))


def _ctx_attn_kernel(has_sink, first, *refs):
    if has_sink:
        sink_ref, refs = refs[0], refs[1:]
    else:
        sink_ref = None
    q_ref, k_ref, v_ref = refs[:3]
    o_ref, nk_ref, nv_ref = refs[-3:]
    lane = lax.broadcasted_iota(jnp.int32, (L_CTX, LANES), 1)
    low = lane < HD
    for s in range(CTX_SEQS):
        rows = pl.ds(s * L_CTX, L_CTX)
        k = k_ref[rows, :].astype(F32)
        v = v_ref[rows, :].astype(F32)
        k_sw = pltpu.roll(k, HD, axis=1)
        v_sw = pltpu.roll(v, HD, axis=1)
        nk_ref[s, 0, 0] = k[:, 0:HD]
        nk_ref[s, 0, 1] = k_sw[:, 0:HD]
        nv_ref[s, 0, 0] = v[:, 0:HD]
        nv_ref[s, 0, 1] = v_sw[:, 0:HD]
        if first:
            nk_ref[s, 1:] = jnp.zeros((DEPTH - 1, N_KV, L_CTX, HD), F32)
            nv_ref[s, 1:] = jnp.zeros((DEPTH - 1, N_KV, L_CTX, HD), F32)
        k2 = (jnp.where(low, k, k_sw).astype(BF16), jnp.where(low, k_sw, k).astype(BF16))
        v2 = (jnp.where(low, v, v_sw).astype(BF16), jnp.where(low, v_sw, v).astype(BF16))
        for kv in range(N_KV):
            o0, o1 = _group_attention(q_ref.at[rows, :], kv, [k2[kv]], [v2[kv]], sink_ref, None, True)
            o_ref[rows, (2 * kv) * LANES:(2 * kv + 1) * LANES] = o0
            o_ref[rows, (2 * kv + 1) * LANES:(2 * kv + 2) * LANES] = o1


def _ctx_attn(layer, q, k, v, sink, new_kv):
    has_sink = sink is not None
    first = new_kv is None
    row = lambda b: (b, 0)
    rows = CTX_SEQS * L_CTX
    in_specs = [pl.BlockSpec((rows, Q_W), row), pl.BlockSpec((rows, KV_W), row), pl.BlockSpec((rows, KV_W), row)]
    args = [q, k, v]
    if has_sink:
        in_specs = [pl.BlockSpec(memory_space=pltpu.SMEM)] + in_specs
        args = [sink] + args
    aliases = {}
    if first:
        kv_spec = pl.BlockSpec((CTX_SEQS, DEPTH, N_KV, L_CTX, HD), lambda b: (b, 0, 0, 0, 0))
    else:
        kv_spec = pl.BlockSpec((CTX_SEQS, 1, N_KV, L_CTX, HD), lambda b: (b, layer, 0, 0, 0))
        aliases = {len(args): 1, len(args) + 1: 2}
        in_specs = in_specs + [pl.BlockSpec(memory_space=pl.ANY)] * 2
        args = args + list(new_kv)
    kv_shape = jax.ShapeDtypeStruct((N_SEQ_CTX, DEPTH, N_KV, L_CTX, HD), F32)
    o, nk, nv = pl.pallas_call(
        functools.partial(_ctx_attn_kernel, has_sink, first),
        grid=(N_SEQ_CTX // CTX_SEQS,),
        in_specs=in_specs,
        out_specs=[pl.BlockSpec((rows, Q_W), row), kv_spec, kv_spec],
        out_shape=[jax.ShapeDtypeStruct((N_CTX_TOK, Q_W), BF16), kv_shape, kv_shape],
        input_output_aliases=aliases,
        compiler_params=_cparams(("arbitrary",)),
        name="ctx_attn_sink" if has_sink else "ctx_attn",
    )(*args)
    return o, (nk, nv)


def _lat_attn_kernel(windowed, *refs):
    if windowed:
        sink_ref, q_ref, k_ref, v_ref, ck_ref, cv_ref, dup_ref, o_ref, k2s, v2s, ck2s, cv2s = refs
    else:
        q_ref, k_ref, v_ref, ck_ref, cv_ref, dup_ref, o_ref, k2s, v2s, ck2s, cv2s = refs
        sink_ref = None
    j = pl.program_id(1)

    @pl.when(j == 0)
    def _():
        ka, kb = _dup_heads(k_ref[...].astype(F32))
        va, vb = _dup_heads(v_ref[...].astype(F32))
        k2s[0] = ka.astype(BF16)
        k2s[1] = kb.astype(BF16)
        v2s[0] = va.astype(BF16)
        v2s[1] = vb.astype(BF16)
        dup = dup_ref[...]
        for kv in range(N_KV):
            ck2s[kv] = _dot(ck_ref[0, 0, kv].astype(BF16), dup).astype(BF16)
            cv2s[kv] = _dot(cv_ref[0, 0, kv].astype(BF16), dup).astype(BF16)

    mask = None
    if windowed:
        tq = q_ref.shape[0]
        n_loc = tq + 2 * WINDOW
        start = pl.multiple_of(jnp.clip(j * tq - WINDOW, 0, L_LAT - n_loc), WINDOW)
        qpos = j * tq + lax.broadcasted_iota(jnp.int32, (tq, n_loc), 0)
        kpos = start + lax.broadcasted_iota(jnp.int32, (tq, n_loc), 1)
        mask = [None, jnp.abs(qpos - kpos) <= WINDOW]
    for kv in range(N_KV):
        if windowed:
            k_own = k2s[kv, pl.ds(start, n_loc), :]
            v_own = v2s[kv, pl.ds(start, n_loc), :]
        else:
            k_own = k2s[kv]
            v_own = v2s[kv]
        o0, o1 = _group_attention(q_ref, kv, [ck2s[kv], k_own], [cv2s[kv], v_own], sink_ref, mask, False)
        o_ref[:, (2 * kv) * LANES:(2 * kv + 1) * LANES] = o0
        o_ref[:, (2 * kv + 1) * LANES:(2 * kv + 2) * LANES] = o1


def _lat_attn(layer, q, k, v, cache_k, cache_v, dup, sink):
    windowed = sink is not None
    tq = TQ_WINDOW if windowed else TQ_FULL
    n_qt = L_LAT // tq
    ctx_tiles = N_CTX_TOK // tq
    ctx_seqs = N_CTX_TOK // L_LAT
    cache_spec = pl.BlockSpec((1, 1, N_KV, PAST, HD), lambda b, j: (b, layer, 0, 0, 0))
    in_specs = [
        pl.BlockSpec((tq, Q_W), lambda b, j: (ctx_tiles + b * n_qt + j, 0)),
        pl.BlockSpec((L_LAT, KV_W), lambda b, j: (ctx_seqs + b, 0)),
        pl.BlockSpec((L_LAT, KV_W), lambda b, j: (ctx_seqs + b, 0)),
        cache_spec,
        cache_spec,
        pl.BlockSpec((HD, LANES), lambda b, j: (0, 0)),
    ]
    args = [q, k, v, cache_k, cache_v, dup]
    if windowed:
        in_specs = [pl.BlockSpec(memory_space=pltpu.SMEM)] + in_specs
        args = [sink] + args
    return pl.pallas_call(
        functools.partial(_lat_attn_kernel, windowed),
        grid=(N_SEQ_LAT, n_qt),
        in_specs=in_specs,
        out_specs=pl.BlockSpec((tq, Q_W), lambda b, j: (b * n_qt + j, 0)),
        out_shape=jax.ShapeDtypeStruct((N_LAT_TOK, Q_W), BF16),
        scratch_shapes=[
            pltpu.VMEM((N_KV, L_LAT, LANES), BF16),
            pltpu.VMEM((N_KV, L_LAT, LANES), BF16),
            pltpu.VMEM((N_KV, PAST, LANES), BF16),
            pltpu.VMEM((N_KV, PAST, LANES), BF16),
        ],
        compiler_params=_cparams(("arbitrary", "arbitrary")),
        name="lat_attn_window" if windowed else "lat_attn",
    )(*args)


def _seq_pos(i, width):
    r = lax.broadcasted_iota(jnp.int32, (TM, width), 0)
    is_ctx = i < N_CTX_TOK // TM
    seq_len = jnp.where(is_ctx, L_CTX, L_LAT)
    return r & (seq_len - 1), seq_len


def _shift_rows(t, j, pos, seq_len):
    if j == 0:
        return t
    moved = pltpu.roll(t, (-j) % TM, axis=0)
    ok = (pos + j >= 0) & (pos + j < seq_len)
    return jnp.where(ok, moved, 0.0)


def _conv_mixer(bg, u, cw_ref, pos, seq_len):
    y = (_shift_rows(u, -1, pos, seq_len) * cw_ref[0:1, :] + u * cw_ref[1:2, :]
         + _shift_rows(u, 1, pos, seq_len) * cw_ref[2:3, :])
    return (bg * y).astype(BF16)


def _window_sum(x_seg, w):
    n = x_seg.shape[0] + 2 * POOL_PAD
    z = jnp.zeros((POOL_PAD, x_seg.shape[1]), F32)
    a = jnp.concatenate([z, x_seg, z], axis=0)
    a = a + pltpu.roll(a, 1, axis=0)
    half = 1
    while 2 * half < w:
        a = pltpu.roll(a, half, axis=0) + pltpu.roll(a, n - half, axis=0)
        half *= 2
    return a[POOL_PAD:POOL_PAD + x_seg.shape[0]]


def _pool_mixer(xd_ref, pw_ref, ps_ref, yd_ref, seq_len):
    t = lax.broadcasted_iota(jnp.int32, (seq_len, POOL_G), 0)
    for gi, w in enumerate(POOL_WINDOWS):
        cnt = (jnp.minimum(t + w // 2, seq_len) - jnp.maximum(t - w // 2, 0)).astype(F32)
        wg = pw_ref[gi].astype(BF16)
        lanes = slice(gi * POOL_G, (gi + 1) * POOL_G)
        for s in range(TM // seq_len):
            rows = slice(s * seq_len, (s + 1) * seq_len)
            xg = xd_ref[rows, lanes]
            d = _window_sum(xg, w) / cnt - xg
            yd_ref[rows, lanes] = (_dot(d.astype(BF16), wg) * ps_ref[:, lanes]).astype(BF16)


def _mix_out_kernel(even, nx, *refs):
    x_refs = refs[:nx]
    mod_ref, oc_ref, ol_ref, w_ref = refs[nx:nx + 4]
    rest = refs[nx + 4:]
    if even:
        ya_ref, out_ref, wbf = rest
    else:
        (xd_ref, pw_ref, ps_ref, g2_ref, rw_ref, rb_ref, tri_ref, upper_ref,
         out_ref, xl_ref, meta_ref, cnt_ref, wbf, yd_s) = rest
    i = pl.program_id(0)

    @pl.when(i == 0)
    def _():
        _cast_rows(w_ref.at[0], wbf, D_MODEL)

    o = jnp.where(i < N_CTX_TOK // TM, oc_ref[...], ol_ref[...])
    if even:
        y = _dot(ya_ref[...], wbf[0:CONV_W, :]) + _dot(o, wbf[CONV_W:, :])
    else:
        @pl.when(i < N_CTX_TOK // TM)
        def _():
            _pool_mixer(xd_ref, pw_ref, ps_ref, yd_s, L_CTX)

        @pl.when(i >= N_CTX_TOK // TM)
        def _():
            _pool_mixer(xd_ref, pw_ref, ps_ref, yd_s, L_LAT)

        y = _dot(o, wbf[0:Q_W, :]) + _dot(yd_s[...], wbf[Q_W:, :])
    out_ref[...] = _read_x(i, x_refs, TM) + mod_ref[0, 2:3, :] * y
    if not even:
        for part in range(TM // TR):
            rows = pl.ds(part * TR, TR)
            _route_tile(out_ref[rows, :], mod_ref, g2_ref, rw_ref, rb_ref, tri_ref, upper_ref,
                        xl_ref.at[pl.ds(part * LOCAL_ROWS, LOCAL_ROWS), :], meta_ref.at[rows, :],
                        cnt_ref.at[pl.ds(part * SUBLANES, SUBLANES), :])


def _mix_out(even, layer_i, x, mod_l, o_ctx, o_lat, w_out, *extra):
    n_ctx = N_CTX_TOK // TM
    n_lat = N_LAT_TOK // TM
    row = lambda i: (i, 0)
    const = lambda i: (0, 0)
    in_specs = _x_specs(TM, len(x) == 2) + [
        pl.BlockSpec((1, 6, D_MODEL), lambda i: (_mod_row(i, TM), 0, 0)),
        pl.BlockSpec((TM, Q_W), lambda i: (jnp.minimum(i, n_ctx - 1), 0)),
        pl.BlockSpec((TM, Q_W), lambda i: (jnp.clip(i - n_ctx, 0, n_lat - 1), 0)),
        pl.BlockSpec((1, D_MODEL, D_MODEL), lambda i: (layer_i, 0, 0)),
    ]
    if even:
        in_specs += [pl.BlockSpec((TM, CONV_W), row)]
    else:
        in_specs += [pl.BlockSpec((TM, POOL_W), row),
                     pl.BlockSpec((len(POOL_WINDOWS), POOL_G, POOL_G), lambda i: (0, 0, 0)),
                     pl.BlockSpec((1, POOL_W), const),
                     pl.BlockSpec((1, D_MODEL), const),
                     pl.BlockSpec((D_MODEL, LANES), const),
                     pl.BlockSpec((1, LANES), const),
                     pl.BlockSpec((TR, TR), const),
                     pl.BlockSpec((LANES, LANES), const)]
    out_specs = [pl.BlockSpec((TM, D_MODEL), row)]
    out_shape = [jax.ShapeDtypeStruct((N_TOK, D_MODEL), F32)]
    if not even:
        parts = TM // TR
        out_specs += [pl.BlockSpec((parts * LOCAL_ROWS, D_MODEL), row), pl.BlockSpec((TM, LANES), row),
                      pl.BlockSpec((parts * SUBLANES, LANES), row)]
        out_shape += [jax.ShapeDtypeStruct((N_TOK // TR * LOCAL_ROWS, D_MODEL), BF16),
                      jax.ShapeDtypeStruct((N_TOK, LANES), F32),
                      jax.ShapeDtypeStruct((N_TOK // TR * SUBLANES, LANES), F32)]
    return pl.pallas_call(
        functools.partial(_mix_out_kernel, even, len(x)),
        grid=(N_TOK // TM,),
        in_specs=in_specs,
        out_specs=out_specs,
        out_shape=out_shape,
        scratch_shapes=[pltpu.VMEM((D_MODEL, D_MODEL), BF16)] + ([] if even else [pltpu.VMEM((TM, POOL_W), BF16)]),
        compiler_params=_cparams(("arbitrary",)),
        name="mix_out_even" if even else "mix_out_odd",
    )(*x, mod_l, o_ctx, o_lat, w_out, *extra)


def _ffn_kernel(layer_i, x_ref, mod_ref, g_ref, w1_hbm, w3_hbm, w2_hbm, out_ref,
                w1s, w3s, w2s, st1, st3, st2, hs, hid, sem):
    i = pl.program_id(0)
    nf = D_FF // TF_FFN

    def chunk_copies(f, slot):
        cols = pl.ds(f * TF_FFN, TF_FFN)
        return (pltpu.make_async_copy(w1_hbm.at[layer_i, :, cols], st1.at[slot], sem.at[0, slot]),
                pltpu.make_async_copy(w3_hbm.at[layer_i, :, cols], st3.at[slot], sem.at[1, slot]),
                pltpu.make_async_copy(w2_hbm.at[layer_i, cols, :], st2.at[slot], sem.at[2, slot]))

    @pl.when(i == 0)
    def _():
        for c in chunk_copies(0, 0):
            c.start()

    hs[...] = _normmod(x_ref[...], g_ref[...], mod_ref[0, 4:5, :], mod_ref[0, 3:4, :]).astype(BF16)
    for f in range(nf):
        slot = f % 2
        lo, hi = f * TF_FFN, (f + 1) * TF_FFN

        @pl.when(i == 0)
        def _(f=f, slot=slot, lo=lo, hi=hi):
            if f + 1 < nf:
                for c in chunk_copies(f + 1, 1 - slot):
                    c.start()
            for c in chunk_copies(f, slot):
                c.wait()
            w1s[:, lo:hi] = st1[slot].astype(BF16)
            w3s[:, lo:hi] = st3[slot].astype(BF16)
            w2s[lo:hi, :] = st2[slot].astype(BF16)

        h = hs[...]
        hid[:, lo:hi] = (_silu(_dot(h, w1s[:, lo:hi])) * _dot(h, w3s[:, lo:hi])).astype(BF16)
    out_ref[...] = x_ref[...] + mod_ref[0, 5:6, :] * _dot(hid[...], w2s[...])


def _ffn(layer_i, x, mod_l, g, w1, w3, w2):
    row = lambda i: (i, 0)
    any_spec = pl.BlockSpec(memory_space=pl.ANY)
    return pl.pallas_call(
        functools.partial(_ffn_kernel, layer_i),
        grid=(N_TOK // TM,),
        in_specs=[
            pl.BlockSpec((TM, D_MODEL), row),
            pl.BlockSpec((1, 6, D_MODEL), lambda i: (_mod_row(i, TM), 0, 0)),
            pl.BlockSpec((1, D_MODEL), lambda i: (0, 0)),
            any_spec, any_spec, any_spec,
        ],
        out_specs=pl.BlockSpec((TM, D_MODEL), row),
        out_shape=jax.ShapeDtypeStruct((N_TOK, D_MODEL), F32),
        scratch_shapes=[
            pltpu.VMEM((D_MODEL, D_FF), BF16),
            pltpu.VMEM((D_MODEL, D_FF), BF16),
            pltpu.VMEM((D_FF, D_MODEL), BF16),
            pltpu.VMEM((2, D_MODEL, TF_FFN), F32),
            pltpu.VMEM((2, D_MODEL, TF_FFN), F32),
            pltpu.VMEM((2, TF_FFN, D_MODEL), F32),
            pltpu.VMEM((TM, D_MODEL), BF16),
            pltpu.VMEM((TM, D_FF), BF16),
            pltpu.SemaphoreType.DMA((3, 2)),
        ],
        compiler_params=pltpu.CompilerParams(dimension_semantics=("arbitrary",), vmem_limit_bytes=FFN_VMEM_LIMIT),
        name="ffn",
    )(x, mod_l, g, w1, w3, w2)


def _split_bf16(t):
    hi = t.astype(BF16)
    return hi, (t - hi.astype(F32)).astype(BF16)


def _lane_values(col_vals, ones_rows):
    base = 32.0
    assert LOCAL_ROWS <= base * 256
    q = jnp.floor(col_vals * (1.0 / base))
    r = col_vals - base * q
    t = base * _dot_nt(ones_rows, q.astype(BF16)) + _dot_nt(ones_rows, r.astype(BF16))
    return t[0:1, :]


def _route_tile(x, mod_ref, g_ref, rw_ref, rb_ref, tri_ref, upper_ref, xl_ref, meta_ref, cnt_ref):
    h = _normmod(x, g_ref[...], mod_ref[0, 4:5, :], mod_ref[0, 3:4, :])
    h_hi, h_lo = _split_bf16(h)
    w_hi, w_lo = _split_bf16(rw_ref[...])
    both = _dot(h_hi, jnp.concatenate([w_hi, w_lo], axis=1))
    logits = both[:, :LANES] + both[:, LANES:] + _dot(h_lo, w_hi) + rb_ref[...]
    lane = lax.broadcasted_iota(jnp.int32, logits.shape, 1).astype(F32)
    logits = jnp.where(lane < N_EXP, logits, NEG_INF)
    m1 = logits.max(axis=-1, keepdims=True)
    i1 = jnp.where(logits == m1, lane, float(LANES)).min(axis=-1, keepdims=True)
    rest = jnp.where(lane == i1, NEG_INF, logits)
    m2 = rest.max(axis=-1, keepdims=True)
    i2 = jnp.where(rest == m2, lane, float(LANES)).min(axis=-1, keepdims=True)
    e2 = jnp.exp(m2 - m1)
    den = 1.0 + e2
    g1 = 1.0 / den
    g2 = e2 / den

    oh_a = jnp.where(lane == i1, 1.0, 0.0)
    oh_b = jnp.where(lane == i2, 1.0, 0.0)
    tri = tri_ref[...]
    cnt_a = oh_a.sum(axis=0, keepdims=True)
    cnt_b = oh_b.sum(axis=0, keepdims=True)
    run16 = jnp.floor((cnt_a + cnt_b + (RUN_ALIGN - 1)) * (1.0 / RUN_ALIGN))
    run16_rows = jnp.broadcast_to(run16, (SUBLANES, LANES))
    start = RUN_ALIGN * _dot(run16_rows.astype(BF16), upper_ref[...])[0:1, :]
    before = _dot(tri, jnp.concatenate([oh_a, oh_b], axis=1).astype(BF16))
    row_a = oh_a * (start + before[:, :LANES])
    row_b = oh_b * (start + cnt_a + before[:, LANES:])

    ones_rows = jnp.ones((SUBLANES, LANES), BF16)
    tok_a = _lane_values(row_a, ones_rows)
    tok_b = _lane_values(row_b, ones_rows)
    sorted_row = lax.broadcasted_iota(jnp.int32, (LOCAL_ROWS, TR), 0).astype(F32)
    perm = jnp.where((sorted_row == tok_a) | (sorted_row == tok_b), 1.0, 0.0).astype(BF16)
    xl_ref[...] = _dot(perm, h_hi).astype(BF16)

    meta = jnp.zeros_like(logits)
    cols = (row_a.sum(axis=-1, keepdims=True), row_b.sum(axis=-1, keepdims=True), g1, g2)
    for k, col in enumerate(cols):
        meta = jnp.where(lane == k, col, meta)
    meta_ref[...] = meta
    cnt_ref[...] = RUN_ALIGN * run16_rows


def _moe_plan(counts):
    n_tiles = N_TOK // TR
    run = counts.reshape(n_tiles, SUBLANES, LANES)[:, 0, :N_EXP].astype(jnp.int32)
    per_expert = jnp.sum(run, axis=0)
    region = (per_expert + (TMS - 1)) // TMS * TMS
    ends = jnp.cumsum(region)
    offs = ends - region
    seg_end = jnp.cumsum(run, axis=0)
    seg_start = seg_end - run
    local_end = jnp.cumsum(run, axis=1)
    local_start = local_end - run
    n_chunks = local_end[:, N_EXP - 1] // RUN_ALIGN
    tile_start = jnp.arange(MOE_TILES, dtype=jnp.int32) * TMS
    tile_expert = jnp.sum((tile_start[:, None] >= ends[None, :]).astype(jnp.int32), axis=1)
    tile_expert = jnp.minimum(tile_expert, N_EXP - 1)
    n_used = (ends[N_EXP - 1] // TMS).reshape(1)
    experts = jnp.arange(N_EXP, dtype=jnp.int32)
    g_row = jnp.arange(MOE_ROWS // RUN_ALIGN, dtype=jnp.int32) * RUN_ALIGN
    g_exp = jnp.repeat(tile_expert, TMS // RUN_ALIGN)
    pick = g_exp[:, None] == experts[None, :]
    rel = g_row - jnp.sum(jnp.where(pick, offs[None, :], 0), axis=1)
    ends_of = jnp.sum(jnp.where(pick[:, None, :], seg_end[None, :, :], 0), axis=2)
    src_tile = jnp.sum((rel[:, None] >= ends_of).astype(jnp.int32), axis=1)
    valid = (src_tile < n_tiles) & (jnp.repeat(jnp.arange(MOE_TILES), TMS // RUN_ALIGN) < n_used[0])
    src_tile = jnp.minimum(src_tile, n_tiles - 1)
    sel = (src_tile[:, None, None] == jnp.arange(n_tiles)[None, :, None]) & pick[:, None, :]
    shift = jnp.sum(jnp.where(sel, (local_start - seg_start)[None, :, :], 0), axis=(1, 2))
    chunk_src = src_tile * LOCAL_ROWS + rel + shift
    n_valid = jnp.sum(valid.reshape(MOE_TILES, TMS // RUN_ALIGN).astype(jnp.int32), axis=1)
    return chunk_src, n_valid, n_chunks, tile_expert, n_used


def _chunk_copy(src_ref, src_row, dst_ref, dst_row, sem):
    return pltpu.make_async_copy(src_ref.at[pl.ds(src_row, RUN_ALIGN), :],
                                 dst_ref.at[pl.ds(dst_row, RUN_ALIGN), :], sem)


CHUNKS = TMS // RUN_ALIGN
LOCAL_CHUNKS = LOCAL_ROWS // RUN_ALIGN


def _expert_kernel(te_ref, nu_ref, src_ref, nv_ref, nc_ref, xl_hbm, w1_ref, w3_ref, w2_ref, yl_hbm,
                   xbuf, ybuf, zbuf, w1s, w3s, w2s, sem_in, sem_out, sem_zero):
    t = pl.program_id(0)
    n_used = nu_ref[0]
    slot = t & 1

    def for_chunks(n, body):
        @pl.when(n == CHUNKS)
        def _():
            for j in range(CHUNKS):
                body(j)

        @pl.when(n < CHUNKS)
        def _():
            def step(j, c):
                body(j)
                return c

            lax.fori_loop(0, n, step, 0)

    def gather(tile, s):
        n = nv_ref[tile]

        def issue(j):
            src = pl.multiple_of(src_ref[tile * CHUNKS + j], RUN_ALIGN)
            _chunk_copy(xl_hbm, src, xbuf.at[s], pl.multiple_of(j * RUN_ALIGN, RUN_ALIGN), sem_in.at[s]).start()

        def pad(j, c):
            xbuf[s, pl.ds(pl.multiple_of(j * RUN_ALIGN, RUN_ALIGN), RUN_ALIGN), :] = jnp.zeros(
                (RUN_ALIGN, D_MODEL), BF16)
            return c

        for_chunks(n, issue)
        lax.fori_loop(n, CHUNKS, pad, 0)

    def scatter(tile, s):
        def issue(j):
            dst = pl.multiple_of(src_ref[tile * CHUNKS + j], RUN_ALIGN)
            _chunk_copy(ybuf.at[s], pl.multiple_of(j * RUN_ALIGN, RUN_ALIGN), yl_hbm, dst, sem_out.at[s]).start()

        for_chunks(nv_ref[tile], issue)

    def drain(count, sem):
        @pl.when(count == CHUNKS)
        def _():
            pltpu.make_async_copy(xl_hbm.at[pl.ds(0, TMS), :], xbuf.at[0], sem).wait()

        @pl.when(count != CHUNKS)
        def _():
            def one(j, c):
                _chunk_copy(xl_hbm, 0, xbuf.at[0], 0, sem).wait()
                return c

            lax.fori_loop(0, count, one, 0)

    @pl.when(t == 0)
    def _():
        zbuf[...] = jnp.zeros_like(zbuf)
        total = jnp.int32(0)
        for tile in range(N_TOK // TR):
            n = nc_ref[tile]

            def clear(j, c, tile=tile):
                row = pl.multiple_of(tile * LOCAL_ROWS + j * RUN_ALIGN, RUN_ALIGN)
                _chunk_copy(zbuf, 0, yl_hbm, row, sem_zero).start()
                return c

            lax.fori_loop(n, LOCAL_CHUNKS, clear, 0)
            total = total + (LOCAL_CHUNKS - n)
        drain(total, sem_zero)
        gather(0, 0)

    @pl.when(t < n_used)
    def _():
        drain(nv_ref[t], sem_in.at[slot])

    @pl.when(t + 1 < n_used)
    def _():
        gather(t + 1, 1 - slot)

    new_expert = (t == 0) | (te_ref[t] != te_ref[jnp.maximum(t - 1, 0)])

    @pl.when(new_expert)
    def _():
        _cast_rows(w1_ref.at[0, 0], w1s, D_MODEL)
        _cast_rows(w3_ref.at[0, 0], w3s, D_MODEL)
        _cast_rows(w2_ref.at[0, 0], w2s, D_FF_E)

    @pl.when((t >= 2) & (t - 2 < n_used))
    def _():
        drain(nv_ref[jnp.maximum(t - 2, 0)], sem_out.at[slot])

    def swiglu_rows(rows):
        x = xbuf[slot, 0:rows, :]
        hid = (_silu(_dot(x, w1s[...])) * _dot(x, w3s[...])).astype(BF16)
        ybuf[slot, 0:rows, :] = _dot(hid, w2s[...]).astype(BF16)

    @pl.when(t < n_used)
    def _():
        half_full = nv_ref[t] <= CHUNKS // 2

        @pl.when(half_full)
        def _():
            swiglu_rows(TMS // 2)

        @pl.when(jnp.logical_not(half_full))
        def _():
            swiglu_rows(TMS)

        scatter(t, slot)

    @pl.when(t == pl.num_programs(0) - 1)
    def _():
        @pl.when((t >= 1) & (t - 1 < n_used))
        def _():
            drain(nv_ref[jnp.maximum(t - 1, 0)], sem_out.at[1 - slot])

        @pl.when(t < n_used)
        def _():
            drain(nv_ref[t], sem_out.at[slot])


def _experts(layer_i, plan, xl, w1, w3, w2):
    chunk_src, n_valid, n_chunks, tile_expert, n_used = plan
    wsel = lambda t, te, nu, src, nv, nc: (layer_i, te[t], 0, 0)
    any_spec = pl.BlockSpec(memory_space=pl.ANY)
    return pl.pallas_call(
        _expert_kernel,
        grid_spec=pltpu.PrefetchScalarGridSpec(
            num_scalar_prefetch=5,
            grid=(MOE_TILES,),
            in_specs=[
                any_spec,
                pl.BlockSpec((1, 1, D_MODEL, D_FF_E), wsel),
                pl.BlockSpec((1, 1, D_MODEL, D_FF_E), wsel),
                pl.BlockSpec((1, 1, D_FF_E, D_MODEL), wsel),
            ],
            out_specs=any_spec,
            scratch_shapes=[
                pltpu.VMEM((2, TMS, D_MODEL), BF16),
                pltpu.VMEM((2, TMS, D_MODEL), BF16),
                pltpu.VMEM((RUN_ALIGN, D_MODEL), BF16),
                pltpu.VMEM((D_MODEL, D_FF_E), BF16),
                pltpu.VMEM((D_MODEL, D_FF_E), BF16),
                pltpu.VMEM((D_FF_E, D_MODEL), BF16),
                pltpu.SemaphoreType.DMA((2,)),
                pltpu.SemaphoreType.DMA((2,)),
                pltpu.SemaphoreType.DMA(()),
            ],
        ),
        out_shape=jax.ShapeDtypeStruct(xl.shape, BF16),
        compiler_params=_cparams(("arbitrary",)),
        name="moe_experts",
    )(tile_expert, n_used, chunk_src, n_valid, n_chunks, xl, w1, w3, w2)


def _combine_kernel(x_ref, mod_ref, meta_ref, yl_ref, *out_refs):
    sorted_row = lax.broadcasted_iota(jnp.int32, (TR, LOCAL_ROWS), 1).astype(F32)
    y = yl_ref[...]
    pick_a = jnp.where(sorted_row == meta_ref[:, 0:1], 1.0, 0.0).astype(BF16)
    pick_b = jnp.where(sorted_row == meta_ref[:, 1:2], 1.0, 0.0).astype(BF16)
    f = meta_ref[:, 2:3] * _dot(pick_a, y) + meta_ref[:, 3:4] * _dot(pick_b, y)
    _write_x(pl.program_id(0), out_refs, TR, x_ref[...] + mod_ref[0, 5:6, :] * f)


def _combine(x, mod_l, meta, yl, split_out):
    row = lambda i: (i, 0)
    return pl.pallas_call(
        _combine_kernel,
        grid=(N_TOK // TR,),
        in_specs=[
            pl.BlockSpec((TR, D_MODEL), row),
            pl.BlockSpec((1, 6, D_MODEL), lambda i: (_mod_row(i, TR), 0, 0)),
            pl.BlockSpec((TR, LANES), row),
            pl.BlockSpec((LOCAL_ROWS, D_MODEL), row),
        ],
        out_specs=_x_specs(TR, split_out),
        out_shape=_x_shapes(split_out),
        compiler_params=_cparams(("arbitrary",)),
        name="moe_combine",
    )(x, mod_l, meta, yl)


def _moe(layer_i, x, routed, mod_l, w1, w3, w2, split_out):
    xl, meta, counts = routed
    yl = _experts(layer_i, _moe_plan(counts), xl, w1, w3, w2)
    return _combine(x, mod_l, meta, yl, split_out)


def _rope_tables():
    n_rows = L_LAT // GRID_W
    rows = np.repeat(np.arange(n_rows, dtype=np.float32), GRID_W)
    cols = np.tile(np.arange(GRID_W, dtype=np.float32), n_rows)
    quarter = HD // 4
    inv = (np.float32(ROPE_THETA) ** (-np.arange(quarter, dtype=np.float32) / np.float32(quarter))).astype(np.float32)
    ang_r = (rows[:, None] * inv).astype(np.float32)
    ang_c = (cols[:, None] * inv).astype(np.float32)
    cos = np.concatenate([np.cos(ang_r)] * 2 + [np.cos(ang_c)] * 2, axis=1)
    sin = np.concatenate([-np.sin(ang_r), np.sin(ang_r), -np.sin(ang_c), np.sin(ang_c)], axis=1)
    cos = np.concatenate([cos, cos], axis=1)
    sin = np.concatenate([sin, sin], axis=1)
    cos = np.concatenate([np.ones((L_LAT, LANES), np.float32), cos], axis=0)
    sin = np.concatenate([np.zeros((L_LAT, LANES), np.float32), sin], axis=0)
    return jnp.asarray(cos, F32), jnp.asarray(sin, F32)


def _block_ones(width):
    r = np.arange(width) // HD
    return jnp.asarray(r[:, None] == r[None, :], BF16)


def kernel(x_prompt, x_sample, cache_k, cache_v, c, c_ctx, norm1, norm2, w_mod, b_mod, ev_w_in, ev_conv, ev_q_norm, ev_k_norm, ev_w_out, od_w_in, od_q_norm, od_k_norm, od_sink, od_pool_w, od_pool_scale, od_w_out, ffn_w1, ffn_w3, ffn_w2, moe_router, moe_router_b, moe_w1, moe_w3, moe_w2):
    x = (x_prompt.reshape(N_CTX_TOK, D_MODEL), x_sample.reshape(N_LAT_TOK, D_MODEL))
    cond = jnp.concatenate([c_ctx[None, :], c, jnp.zeros((MOD_ROWS - 1 - N_SEQ_LAT, D_MODEL), F32)], axis=0)
    mod = _modulation(cond, w_mod, b_mod).reshape(DEPTH, MOD_ROWS, 6, D_MODEL)

    cos_tab, sin_tab = _rope_tables()
    ones_q = _block_ones(Q_W)
    ones_k = _block_ones(KV_W)
    dup = jnp.asarray(np.concatenate([np.eye(HD), np.eye(HD)], axis=1), BF16)
    tri = jnp.asarray(np.arange(TR)[:, None] > np.arange(TR)[None, :], BF16)
    upper = jnp.asarray(np.arange(LANES)[:, None] < np.arange(LANES)[None, :], BF16)

    new_kv = None
    for l in range(DEPTH):
        i = l // 2
        even = l % 2 == 0
        mod_l = mod[l]
        g1 = norm1[l][None, :]
        g2 = norm2[l][None, :]
        if even:
            q_gain, k_gain = ev_q_norm[i], ev_k_norm[i]
            w_in, w_out = ev_w_in, ev_w_out
        else:
            q_gain, k_gain = od_q_norm[i], od_k_norm[i]
            w_in, w_out = od_w_in, od_w_out
        q_gain = jnp.tile(q_gain, N_Q)[None, :]
        k_gain = jnp.tile(k_gain, N_KV)[None, :]
        outs = _in_proj(even, i, x, mod_l, g1, w_in, ones_q, ones_k, q_gain, k_gain, cos_tab, sin_tab,
                        ev_conv[i] if even else None)
        if even:
            ya, q, k, v = outs
            sink = None
        else:
            q, k, v, xd = outs
            sink = od_sink[i]
        o_ctx, new_kv = _ctx_attn(l, q, k, v, sink, new_kv)
        o_lat = _lat_attn(l, q, k, v, cache_k, cache_v, dup, sink)
        if even:
            (x1,) = _mix_out(True, i, x, mod_l, o_ctx, o_lat, w_out, ya)
            x = (_ffn(i, x1, mod_l, g2, ffn_w1, ffn_w3, ffn_w2),)
        else:
            rw = jnp.pad(moe_router[i], ((0, 0), (0, LANES - N_EXP)))
            rb = jnp.pad(moe_router_b[i], (0, LANES - N_EXP))[None, :]
            x1, *routed = _mix_out(False, i, x, mod_l, o_ctx, o_lat, w_out, xd, od_pool_w[i],
                                   od_pool_scale[i][None, :], g2, rw, rb, tri, upper)
            x = tuple(_moe(i, x1, routed, mod_l, moe_w1, moe_w3, moe_w2, l == DEPTH - 1))

    y_prompt = x[0].reshape(N_SEQ_CTX, L_CTX, D_MODEL)
    y_sample = x[1].reshape(N_SEQ_LAT, L_LAT, D_MODEL)
    return (y_prompt, y_sample, new_kv[0], new_kv[1])
```

```python
import functools

import jax
import jax.numpy as jnp
import numpy as np
from jax import lax
from jax.experimental import pallas as pl
from jax.experimental.pallas import tpu as pltpu

F32 = jnp.float32
BF16 = jnp.bfloat16

D_MODEL = 1024
N_SEQ_CTX = 32
L_CTX = 256
N_SEQ_LAT = 4
L_LAT = 1024
DEPTH = 4
PAST = 512
GRID_W = 64
HD = 64
N_Q = 8
N_KV = 2
GROUP = N_Q // N_KV
Q_W = N_Q * HD
KV_W = N_KV * HD
CONV_W = 512
POOL_W = 512
POOL_WINDOWS = (2, 4, 8, 16)
POOL_G = 128
POOL_PAD = 16
EVEN_IN = 3 * CONV_W + Q_W + 2 * KV_W
ODD_IN = Q_W + 2 * KV_W + POOL_W
WINDOW = 128
D_FF = 2816
N_EXP = 8
TOP_K = 2
D_FF_E = 1024
ROPE_THETA = 10000.0
EPS = 1e-6

N_CTX_TOK = N_SEQ_CTX * L_CTX
N_LAT_TOK = N_SEQ_LAT * L_LAT
N_TOK = N_CTX_TOK + N_LAT_TOK
MOD_ROWS = 16

LANES = 128
SUBLANES = 8
ROPE_Q = HD // 4
VMEM_LIMIT = 56 * 1024 * 1024
FFN_VMEM_LIMIT = 62 * 1024 * 1024

TM_IN = 1024
TM = 1024
TQ_FULL = 512
TQ_WINDOW = 256
CTX_SEQS = 8
TF_FFN = 256
TR = 512
RUN_ALIGN = 16
LOCAL_ROWS = TOP_K * TR + N_EXP * RUN_ALIGN
TMS = 512
MOE_TILES = -(-(TOP_K * N_TOK + (N_TOK // TR) * N_EXP * (RUN_ALIGN - 1) + N_EXP * (TMS - 1)) // TMS)
MOE_ROWS = MOE_TILES * TMS
NEG_INF = float("-inf")


def _cparams(sem):
    return pltpu.CompilerParams(dimension_semantics=sem, vmem_limit_bytes=VMEM_LIMIT)


def _mod_row(i, tm):
    n_ctx = N_CTX_TOK // tm
    return jnp.where(i < n_ctx, 0, 1 + (i - n_ctx) // (L_LAT // tm))


def _x_specs(tm, split):
    if not split:
        return [pl.BlockSpec((tm, D_MODEL), lambda i, *_: (i, 0))]
    n_ctx = N_CTX_TOK // tm
    n_lat = N_LAT_TOK // tm
    return [pl.BlockSpec((tm, D_MODEL), lambda i, *_: (jnp.minimum(i, n_ctx - 1), 0)),
            pl.BlockSpec((tm, D_MODEL), lambda i, *_: (jnp.clip(i - n_ctx, 0, n_lat - 1), 0))]


def _x_shapes(split):
    if not split:
        return [jax.ShapeDtypeStruct((N_TOK, D_MODEL), F32)]
    return [jax.ShapeDtypeStruct((N_CTX_TOK, D_MODEL), F32), jax.ShapeDtypeStruct((N_LAT_TOK, D_MODEL), F32)]


def _read_x(i, x_refs, tm):
    if len(x_refs) == 1:
        return x_refs[0][...]
    return jnp.where(i < N_CTX_TOK // tm, x_refs[0][...], x_refs[1][...])


def _write_x(i, o_refs, tm, val):
    if len(o_refs) == 1:
        o_refs[0][...] = val
        return

    @pl.when(i < N_CTX_TOK // tm)
    def _():
        o_refs[0][...] = val

    @pl.when(i >= N_CTX_TOK // tm)
    def _():
        o_refs[1][...] = val


def _normmod(x, g, scale, shift):
    ms = jnp.mean(x * x, axis=-1, keepdims=True)
    y = x * lax.rsqrt(ms + EPS) * g
    return y * (1.0 + scale) + shift


def _silu(x):
    return x * jax.nn.sigmoid(x)


def _dot(a, b):
    return jnp.dot(a, b, preferred_element_type=F32)


def _dot_nt(a, b):
    return lax.dot_general(a, b, (((1,), (1,)), ((), ())), preferred_element_type=F32)


def _cast_rows(src_ref, dst_ref, rows, chunk=256):
    for r in range(0, rows, chunk):
        dst_ref[r:r + chunk, :] = src_ref[r:r + chunk, :].astype(dst_ref.dtype)


def _mod_kernel(c_ref, w_ref, b_ref, o_ref):
    s = _silu(c_ref[...]).astype(BF16)
    o_ref[0] = _dot(s, w_ref[0].astype(BF16)) + b_ref[0]


def _modulation(cond, w_mod, b_mod):
    tn = 3072
    return pl.pallas_call(
        _mod_kernel,
        grid=(DEPTH, 6 * D_MODEL // tn),
        in_specs=[
            pl.BlockSpec((MOD_ROWS, D_MODEL), lambda l, j: (0, 0)),
            pl.BlockSpec((1, D_MODEL, tn), lambda l, j: (l, 0, j)),
            pl.BlockSpec((1, 1, tn), lambda l, j: (l, 0, j)),
        ],
        out_specs=pl.BlockSpec((1, MOD_ROWS, tn), lambda l, j: (l, 0, j)),
        out_shape=jax.ShapeDtypeStruct((DEPTH, MOD_ROWS, 6 * D_MODEL), F32),
        compiler_params=_cparams(("arbitrary", "arbitrary")),
        name="modulation",
    )(cond, w_mod, b_mod.reshape(DEPTH, 1, 6 * D_MODEL))


def _head_rms(t, ones_bd, gain):
    ssq = _dot((t * t).astype(BF16), ones_bd)
    return t * lax.rsqrt(ssq * (1.0 / HD) + EPS) * gain


def _rope(t, cos, sin_signed):
    lane = lax.broadcasted_iota(jnp.int32, (t.shape[0], LANES), 1)
    first = (lane & (2 * ROPE_Q - 1)) < ROPE_Q
    outs = []
    for c in range(t.shape[1] // LANES):
        tc = t[:, c * LANES:(c + 1) * LANES]
        nxt = pltpu.roll(tc, LANES - ROPE_Q, axis=1)
        prv = pltpu.roll(tc, ROPE_Q, axis=1)
        outs.append(tc * cos + jnp.where(first, nxt, prv) * sin_signed)
    return outs[0] if len(outs) == 1 else jnp.concatenate(outs, axis=1)


def _in_proj_kernel(even, nx, *refs):
    x_refs = refs[:nx]
    mod_ref, g_ref, w_ref, onesq_ref, onesk_ref, qg_ref, kg_ref, cos_ref, sin_ref = refs[nx:nx + 9]
    rest = refs[nx + 9:]
    if even:
        cw_ref, ya_ref, q_ref, k_ref, v_ref, wbf = rest
        q0 = 3 * CONV_W
    else:
        q_ref, k_ref, v_ref, xd_ref, wbf = rest
        q0 = 0
    k0 = q0 + Q_W
    v0 = k0 + KV_W

    @pl.when(pl.program_id(0) == 0)
    def _():
        _cast_rows(w_ref.at[0], wbf, D_MODEL)

    x = _read_x(pl.program_id(0), x_refs, TM_IN)
    h = _normmod(x, g_ref[...], mod_ref[0, 1:2, :], mod_ref[0, 0:1, :]).astype(BF16)
    cos = cos_ref[...]
    sin = sin_ref[...]

    q = _dot(h, wbf[:, q0:q0 + Q_W])
    q = _rope(_head_rms(q, onesq_ref[...], qg_ref[...]), cos, sin) * (HD ** -0.5)
    q_ref[...] = q.astype(BF16)

    kv = _dot(h, wbf[:, k0:k0 + 2 * KV_W])
    k = _rope(_head_rms(kv[:, :KV_W], onesk_ref[...], kg_ref[...]), cos, sin)
    k_ref[...] = k.astype(BF16)
    v_ref[...] = kv[:, KV_W:].astype(BF16)

    if even:
        pos, seq_len = _seq_pos(pl.program_id(0), 1)
        bg = _dot(h, wbf[:, 0:CONV_W])
        u = _dot(h, wbf[:, CONV_W:2 * CONV_W]) * _dot(h, wbf[:, 2 * CONV_W:3 * CONV_W])
        ya_ref[...] = _conv_mixer(bg, u, cw_ref, pos, seq_len)
    else:
        xd_ref[...] = _dot(h, wbf[:, v0 + KV_W:v0 + KV_W + POOL_W])


def _in_proj(even, layer_i, x, mod_l, g, w, ones_q, ones_k, q_gain, k_gain, cos_tab, sin_tab, conv_w=None):
    assert TM_IN == TM
    tm = TM_IN
    n_in = EVEN_IN if even else ODD_IN
    n_ctx = N_CTX_TOK // tm
    per_seq = L_LAT // tm

    def rope_idx(i):
        return (jnp.where(i < n_ctx, 0, per_seq + (i - n_ctx) % per_seq), 0)

    row = lambda i: (i, 0)
    const = lambda i: (0, 0)
    in_specs = _x_specs(tm, len(x) == 2) + [
        pl.BlockSpec((1, 6, D_MODEL), lambda i: (_mod_row(i, tm), 0, 0)),
        pl.BlockSpec((1, D_MODEL), const),
        pl.BlockSpec((1, D_MODEL, n_in), lambda i: (layer_i, 0, 0)),
        pl.BlockSpec((Q_W, Q_W), const),
        pl.BlockSpec((KV_W, KV_W), const),
        pl.BlockSpec((1, Q_W), const),
        pl.BlockSpec((1, KV_W), const),
        pl.BlockSpec((tm, LANES), rope_idx),
        pl.BlockSpec((tm, LANES), rope_idx),
    ]
    qkv_specs = [pl.BlockSpec((tm, Q_W), row), pl.BlockSpec((tm, KV_W), row), pl.BlockSpec((tm, KV_W), row)]
    qkv_shapes = [jax.ShapeDtypeStruct((N_TOK, Q_W), BF16), jax.ShapeDtypeStruct((N_TOK, KV_W), BF16),
                  jax.ShapeDtypeStruct((N_TOK, KV_W), BF16)]
    args = [*x, mod_l, g, w, ones_q, ones_k, q_gain, k_gain, cos_tab, sin_tab]
    if even:
        in_specs.append(pl.BlockSpec((3, CONV_W), const))
        args.append(conv_w)
        out_specs = [pl.BlockSpec((tm, CONV_W), row)] + qkv_specs
        out_shape = [jax.ShapeDtypeStruct((N_TOK, CONV_W), BF16)] + qkv_shapes
    else:
        out_specs = qkv_specs + [pl.BlockSpec((tm, POOL_W), row)]
        out_shape = qkv_shapes + [jax.ShapeDtypeStruct((N_TOK, POOL_W), F32)]
    return pl.pallas_call(
        functools.partial(_in_proj_kernel, even, len(x)),
        grid=(N_TOK // tm,),
        in_specs=in_specs,
        out_specs=out_specs,
        out_shape=out_shape,
        scratch_shapes=[pltpu.VMEM((D_MODEL, n_in), BF16)],
        compiler_params=_cparams(("arbitrary",)),
        name="in_proj_even" if even else "in_proj_odd",
    )(*args)


def _dup_heads(t):
    lane = lax.broadcasted_iota(jnp.int32, t.shape, 1)
    swapped = pltpu.roll(t, HD, axis=1)
    low = lane < HD
    return jnp.where(low, t, swapped), jnp.where(low, swapped, t)


def _softmax_pv(scores, values, sink):
    m = scores[0].max(axis=-1, keepdims=True)
    for s in scores[1:]:
        m = jnp.maximum(m, s.max(axis=-1, keepdims=True))
    if sink is not None:
        m = jnp.maximum(m, sink)
    den = None
    acc = None
    for s, v in zip(scores, values):
        e = jnp.exp(s - m)
        d = e.sum(axis=-1, keepdims=True)
        a = _dot(e.astype(BF16), v)
        den = d if den is None else den + d
        acc = a if acc is None else acc + a
    if sink is not None:
        den = den + jnp.exp(sink - m)
    return acc / den


def _group_attention(q_ref, kv, keys, values, sink_ref, mask, stack):
    t = q_ref.shape[0]
    lane = lax.broadcasted_iota(jnp.int32, (t, LANES), 1)
    low = lane < HD
    zero = jnp.zeros((t, LANES), BF16)
    chunks = [q_ref[:, (kv * 2 + c) * LANES:(kv * 2 + c + 1) * LANES] for c in range(2)]
    if not stack:
        outs = []
        for h in range(GROUP):
            qm = jnp.where(low, chunks[h // 2], zero) if h % 2 == 0 else jnp.where(low, zero, chunks[h // 2])
            scores = [_dot_nt(qm, k) for k in keys]
            if mask is not None:
                scores = [s if m is None else jnp.where(m, s, NEG_INF) for s, m in zip(scores, mask)]
            sink = None if sink_ref is None else sink_ref[kv * GROUP + h]
            outs.append(_softmax_pv(scores, values, sink))
        return (jnp.where(low, outs[0], outs[1]).astype(BF16), jnp.where(low, outs[2], outs[3]).astype(BF16))
    qs = jnp.concatenate([jnp.where(low, chunks[0], zero), jnp.where(low, zero, chunks[0]),
                          jnp.where(low, chunks[1], zero), jnp.where(low, zero, chunks[1])], axis=0)
    scores = [_dot_nt(qs, k) for k in keys]
    if mask is not None:
        scores = [s if m is None else jnp.where(m, s, NEG_INF) for s, m in zip(scores, mask)]
    sink = None
    if sink_ref is not None:
        head = lax.broadcasted_iota(jnp.int32, (GROUP * t, 1), 0) // t
        sink = jnp.full((GROUP * t, 1), sink_ref[kv * GROUP], F32)
        for h in range(1, GROUP):
            sink = jnp.where(head == h, sink_ref[kv * GROUP + h], sink)
    out = _softmax_pv(scores, values, sink)
    return (jnp.where(low, out[0:t], out[t:2 * t]).astype(BF16),
            jnp.where(low, out[2 * t:3 * t], out[3 * t:4 * t]).astype(BF16))


def _ctx_attn_kernel(has_sink, first, *refs):
    if has_sink:
        sink_ref, refs = refs[0], refs[1:]
    else:
        sink_ref = None
    q_ref, k_ref, v_ref = refs[:3]
    o_ref, nk_ref, nv_ref = refs[-3:]
    lane = lax.broadcasted_iota(jnp.int32, (L_CTX, LANES), 1)
    low = lane < HD
    for s in range(CTX_SEQS):
        rows = pl.ds(s * L_CTX, L_CTX)
        k = k_ref[rows, :].astype(F32)
        v = v_ref[rows, :].astype(F32)
        k_sw = pltpu.roll(k, HD, axis=1)
        v_sw = pltpu.roll(v, HD, axis=1)
        nk_ref[s, 0, 0] = k[:, 0:HD]
        nk_ref[s, 0, 1] = k_sw[:, 0:HD]
        nv_ref[s, 0, 0] = v[:, 0:HD]
        nv_ref[s, 0, 1] = v_sw[:, 0:HD]
        if first:
            nk_ref[s, 1:] = jnp.zeros((DEPTH - 1, N_KV, L_CTX, HD), F32)
            nv_ref[s, 1:] = jnp.zeros((DEPTH - 1, N_KV, L_CTX, HD), F32)
        k2 = (jnp.where(low, k, k_sw).astype(BF16), jnp.where(low, k_sw, k).astype(BF16))
        v2 = (jnp.where(low, v, v_sw).astype(BF16), jnp.where(low, v_sw, v).astype(BF16))
        for kv in range(N_KV):
            o0, o1 = _group_attention(q_ref.at[rows, :], kv, [k2[kv]], [v2[kv]], sink_ref, None, True)
            o_ref[rows, (2 * kv) * LANES:(2 * kv + 1) * LANES] = o0
            o_ref[rows, (2 * kv + 1) * LANES:(2 * kv + 2) * LANES] = o1


def _ctx_attn(layer, q, k, v, sink, new_kv):
    has_sink = sink is not None
    first = new_kv is None
    row = lambda b: (b, 0)
    rows = CTX_SEQS * L_CTX
    in_specs = [pl.BlockSpec((rows, Q_W), row), pl.BlockSpec((rows, KV_W), row), pl.BlockSpec((rows, KV_W), row)]
    args = [q, k, v]
    if has_sink:
        in_specs = [pl.BlockSpec(memory_space=pltpu.SMEM)] + in_specs
        args = [sink] + args
    aliases = {}
    if first:
        kv_spec = pl.BlockSpec((CTX_SEQS, DEPTH, N_KV, L_CTX, HD), lambda b: (b, 0, 0, 0, 0))
    else:
        kv_spec = pl.BlockSpec((CTX_SEQS, 1, N_KV, L_CTX, HD), lambda b: (b, layer, 0, 0, 0))
        aliases = {len(args): 1, len(args) + 1: 2}
        in_specs = in_specs + [pl.BlockSpec(memory_space=pl.ANY)] * 2
        args = args + list(new_kv)
    kv_shape = jax.ShapeDtypeStruct((N_SEQ_CTX, DEPTH, N_KV, L_CTX, HD), F32)
    o, nk, nv = pl.pallas_call(
        functools.partial(_ctx_attn_kernel, has_sink, first),
        grid=(N_SEQ_CTX // CTX_SEQS,),
        in_specs=in_specs,
        out_specs=[pl.BlockSpec((rows, Q_W), row), kv_spec, kv_spec],
        out_shape=[jax.ShapeDtypeStruct((N_CTX_TOK, Q_W), BF16), kv_shape, kv_shape],
        input_output_aliases=aliases,
        compiler_params=_cparams(("arbitrary",)),
        name="ctx_attn_sink" if has_sink else "ctx_attn",
    )(*args)
    return o, (nk, nv)


def _lat_attn_kernel(windowed, *refs):
    if windowed:
        sink_ref, q_ref, k_ref, v_ref, ck_ref, cv_ref, dup_ref, o_ref, k2s, v2s, ck2s, cv2s = refs
    else:
        q_ref, k_ref, v_ref, ck_ref, cv_ref, dup_ref, o_ref, k2s, v2s, ck2s, cv2s = refs
        sink_ref = None
    j = pl.program_id(1)

    @pl.when(j == 0)
    def _():
        ka, kb = _dup_heads(k_ref[...].astype(F32))
        va, vb = _dup_heads(v_ref[...].astype(F32))
        k2s[0] = ka.astype(BF16)
        k2s[1] = kb.astype(BF16)
        v2s[0] = va.astype(BF16)
        v2s[1] = vb.astype(BF16)
        dup = dup_ref[...]
        for kv in range(N_KV):
            ck2s[kv] = _dot(ck_ref[0, 0, kv].astype(BF16), dup).astype(BF16)
            cv2s[kv] = _dot(cv_ref[0, 0, kv].astype(BF16), dup).astype(BF16)

    mask = None
    if windowed:
        tq = q_ref.shape[0]
        n_loc = tq + 2 * WINDOW
        start = pl.multiple_of(jnp.clip(j * tq - WINDOW, 0, L_LAT - n_loc), WINDOW)
        qpos = j * tq + lax.broadcasted_iota(jnp.int32, (tq, n_loc), 0)
        kpos = start + lax.broadcasted_iota(jnp.int32, (tq, n_loc), 1)
        mask = [None, jnp.abs(qpos - kpos) <= WINDOW]
    for kv in range(N_KV):
        if windowed:
            k_own = k2s[kv, pl.ds(start, n_loc), :]
            v_own = v2s[kv, pl.ds(start, n_loc), :]
        else:
            k_own = k2s[kv]
            v_own = v2s[kv]
        o0, o1 = _group_attention(q_ref, kv, [ck2s[kv], k_own], [cv2s[kv], v_own], sink_ref, mask, False)
        o_ref[:, (2 * kv) * LANES:(2 * kv + 1) * LANES] = o0
        o_ref[:, (2 * kv + 1) * LANES:(2 * kv + 2) * LANES] = o1


def _lat_attn(layer, q, k, v, cache_k, cache_v, dup, sink):
    windowed = sink is not None
    tq = TQ_WINDOW if windowed else TQ_FULL
    n_qt = L_LAT // tq
    ctx_tiles = N_CTX_TOK // tq
    ctx_seqs = N_CTX_TOK // L_LAT
    cache_spec = pl.BlockSpec((1, 1, N_KV, PAST, HD), lambda b, j: (b, layer, 0, 0, 0))
    in_specs = [
        pl.BlockSpec((tq, Q_W), lambda b, j: (ctx_tiles + b * n_qt + j, 0)),
        pl.BlockSpec((L_LAT, KV_W), lambda b, j: (ctx_seqs + b, 0)),
        pl.BlockSpec((L_LAT, KV_W), lambda b, j: (ctx_seqs + b, 0)),
        cache_spec,
        cache_spec,
        pl.BlockSpec((HD, LANES), lambda b, j: (0, 0)),
    ]
    args = [q, k, v, cache_k, cache_v, dup]
    if windowed:
        in_specs = [pl.BlockSpec(memory_space=pltpu.SMEM)] + in_specs
        args = [sink] + args
    return pl.pallas_call(
        functools.partial(_lat_attn_kernel, windowed),
        grid=(N_SEQ_LAT, n_qt),
        in_specs=in_specs,
        out_specs=pl.BlockSpec((tq, Q_W), lambda b, j: (b * n_qt + j, 0)),
        out_shape=jax.ShapeDtypeStruct((N_LAT_TOK, Q_W), BF16),
        scratch_shapes=[
            pltpu.VMEM((N_KV, L_LAT, LANES), BF16),
            pltpu.VMEM((N_KV, L_LAT, LANES), BF16),
            pltpu.VMEM((N_KV, PAST, LANES), BF16),
            pltpu.VMEM((N_KV, PAST, LANES), BF16),
        ],
        compiler_params=_cparams(("arbitrary", "arbitrary")),
        name="lat_attn_window" if windowed else "lat_attn",
    )(*args)


def _seq_pos(i, width):
    r = lax.broadcasted_iota(jnp.int32, (TM, width), 0)
    is_ctx = i < N_CTX_TOK // TM
    seq_len = jnp.where(is_ctx, L_CTX, L_LAT)
    return r & (seq_len - 1), seq_len


def _shift_rows(t, j, pos, seq_len):
    if j == 0:
        return t
    moved = pltpu.roll(t, (-j) % TM, axis=0)
    ok = (pos + j >= 0) & (pos + j < seq_len)
    return jnp.where(ok, moved, 0.0)


def _conv_mixer(bg, u, cw_ref, pos, seq_len):
    y = (_shift_rows(u, -1, pos, seq_len) * cw_ref[0:1, :] + u * cw_ref[1:2, :]
         + _shift_rows(u, 1, pos, seq_len) * cw_ref[2:3, :])
    return (bg * y).astype(BF16)


def _window_sum(x_seg, w):
    n = x_seg.shape[0] + 2 * POOL_PAD
    z = jnp.zeros((POOL_PAD, x_seg.shape[1]), F32)
    a = jnp.concatenate([z, x_seg, z], axis=0)
    a = a + pltpu.roll(a, 1, axis=0)
    half = 1
    while 2 * half < w:
        a = pltpu.roll(a, half, axis=0) + pltpu.roll(a, n - half, axis=0)
        half *= 2
    return a[POOL_PAD:POOL_PAD + x_seg.shape[0]]


def _pool_mixer(xd_ref, pw_ref, ps_ref, yd_ref, seq_len):
    t = lax.broadcasted_iota(jnp.int32, (seq_len, POOL_G), 0)
    for gi, w in enumerate(POOL_WINDOWS):
        cnt = (jnp.minimum(t + w // 2, seq_len) - jnp.maximum(t - w // 2, 0)).astype(F32)
        wg = pw_ref[gi].astype(BF16)
        lanes = slice(gi * POOL_G, (gi + 1) * POOL_G)
        for s in range(TM // seq_len):
            rows = slice(s * seq_len, (s + 1) * seq_len)
            xg = xd_ref[rows, lanes]
            d = _window_sum(xg, w) / cnt - xg
            yd_ref[rows, lanes] = (_dot(d.astype(BF16), wg) * ps_ref[:, lanes]).astype(BF16)


def _mix_out_kernel(even, nx, *refs):
    x_refs = refs[:nx]
    mod_ref, oc_ref, ol_ref, w_ref = refs[nx:nx + 4]
    rest = refs[nx + 4:]
    if even:
        ya_ref, out_ref, wbf = rest
    else:
        (xd_ref, pw_ref, ps_ref, g2_ref, rw_ref, rb_ref, tri_ref, upper_ref,
         out_ref, xl_ref, meta_ref, cnt_ref, wbf, yd_s) = rest
    i = pl.program_id(0)

    @pl.when(i == 0)
    def _():
        _cast_rows(w_ref.at[0], wbf, D_MODEL)

    o = jnp.where(i < N_CTX_TOK // TM, oc_ref[...], ol_ref[...])
    if even:
        y = _dot(ya_ref[...], wbf[0:CONV_W, :]) + _dot(o, wbf[CONV_W:, :])
    else:
        @pl.when(i < N_CTX_TOK // TM)
        def _():
            _pool_mixer(xd_ref, pw_ref, ps_ref, yd_s, L_CTX)

        @pl.when(i >= N_CTX_TOK // TM)
        def _():
            _pool_mixer(xd_ref, pw_ref, ps_ref, yd_s, L_LAT)

        y = _dot(o, wbf[0:Q_W, :]) + _dot(yd_s[...], wbf[Q_W:, :])
    out_ref[...] = _read_x(i, x_refs, TM) + mod_ref[0, 2:3, :] * y
    if not even:
        for part in range(TM // TR):
            rows = pl.ds(part * TR, TR)
            _route_tile(out_ref[rows, :], mod_ref, g2_ref, rw_ref, rb_ref, tri_ref, upper_ref,
                        xl_ref.at[pl.ds(part * LOCAL_ROWS, LOCAL_ROWS), :], meta_ref.at[rows, :],
                        cnt_ref.at[pl.ds(part * SUBLANES, SUBLANES), :])


def _mix_out(even, layer_i, x, mod_l, o_ctx, o_lat, w_out, *extra):
    n_ctx = N_CTX_TOK // TM
    n_lat = N_LAT_TOK // TM
    row = lambda i: (i, 0)
    const = lambda i: (0, 0)
    in_specs = _x_specs(TM, len(x) == 2) + [
        pl.BlockSpec((1, 6, D_MODEL), lambda i: (_mod_row(i, TM), 0, 0)),
        pl.BlockSpec((TM, Q_W), lambda i: (jnp.minimum(i, n_ctx - 1), 0)),
        pl.BlockSpec((TM, Q_W), lambda i: (jnp.clip(i - n_ctx, 0, n_lat - 1), 0)),
        pl.BlockSpec((1, D_MODEL, D_MODEL), lambda i: (layer_i, 0, 0)),
    ]
    if even:
        in_specs += [pl.BlockSpec((TM, CONV_W), row)]
    else:
        in_specs += [pl.BlockSpec((TM, POOL_W), row),
                     pl.BlockSpec((len(POOL_WINDOWS), POOL_G, POOL_G), lambda i: (0, 0, 0)),
                     pl.BlockSpec((1, POOL_W), const),
                     pl.BlockSpec((1, D_MODEL), const),
                     pl.BlockSpec((D_MODEL, LANES), const),
                     pl.BlockSpec((1, LANES), const),
                     pl.BlockSpec((TR, TR), const),
                     pl.BlockSpec((LANES, LANES), const)]
    out_specs = [pl.BlockSpec((TM, D_MODEL), row)]
    out_shape = [jax.ShapeDtypeStruct((N_TOK, D_MODEL), F32)]
    if not even:
        parts = TM // TR
        out_specs += [pl.BlockSpec((parts * LOCAL_ROWS, D_MODEL), row), pl.BlockSpec((TM, LANES), row),
                      pl.BlockSpec((parts * SUBLANES, LANES), row)]
        out_shape += [jax.ShapeDtypeStruct((N_TOK // TR * LOCAL_ROWS, D_MODEL), BF16),
                      jax.ShapeDtypeStruct((N_TOK, LANES), F32),
                      jax.ShapeDtypeStruct((N_TOK // TR * SUBLANES, LANES), F32)]
    return pl.pallas_call(
        functools.partial(_mix_out_kernel, even, len(x)),
        grid=(N_TOK // TM,),
        in_specs=in_specs,
        out_specs=out_specs,
        out_shape=out_shape,
        scratch_shapes=[pltpu.VMEM((D_MODEL, D_MODEL), BF16)] + ([] if even else [pltpu.VMEM((TM, POOL_W), BF16)]),
        compiler_params=_cparams(("arbitrary",)),
        name="mix_out_even" if even else "mix_out_odd",
    )(*x, mod_l, o_ctx, o_lat, w_out, *extra)


def _ffn_kernel(layer_i, x_ref, mod_ref, g_ref, w1_hbm, w3_hbm, w2_hbm, out_ref,
                w1s, w3s, w2s, st1, st3, st2, hs, hid, sem):
    i = pl.program_id(0)
    nf = D_FF // TF_FFN

    def chunk_copies(f, slot):
        cols = pl.ds(f * TF_FFN, TF_FFN)
        return (pltpu.make_async_copy(w1_hbm.at[layer_i, :, cols], st1.at[slot], sem.at[0, slot]),
                pltpu.make_async_copy(w3_hbm.at[layer_i, :, cols], st3.at[slot], sem.at[1, slot]),
                pltpu.make_async_copy(w2_hbm.at[layer_i, cols, :], st2.at[slot], sem.at[2, slot]))

    @pl.when(i == 0)
    def _():
        for c in chunk_copies(0, 0):
            c.start()

    hs[...] = _normmod(x_ref[...], g_ref[...], mod_ref[0, 4:5, :], mod_ref[0, 3:4, :]).astype(BF16)
    for f in range(nf):
        slot = f % 2
        lo, hi = f * TF_FFN, (f + 1) * TF_FFN

        @pl.when(i == 0)
        def _(f=f, slot=slot, lo=lo, hi=hi):
            if f + 1 < nf:
                for c in chunk_copies(f + 1, 1 - slot):
                    c.start()
            for c in chunk_copies(f, slot):
                c.wait()
            w1s[:, lo:hi] = st1[slot].astype(BF16)
            w3s[:, lo:hi] = st3[slot].astype(BF16)
            w2s[lo:hi, :] = st2[slot].astype(BF16)

        h = hs[...]
        hid[:, lo:hi] = (_silu(_dot(h, w1s[:, lo:hi])) * _dot(h, w3s[:, lo:hi])).astype(BF16)
    out_ref[...] = x_ref[...] + mod_ref[0, 5:6, :] * _dot(hid[...], w2s[...])


def _ffn(layer_i, x, mod_l, g, w1, w3, w2):
    row = lambda i: (i, 0)
    any_spec = pl.BlockSpec(memory_space=pl.ANY)
    return pl.pallas_call(
        functools.partial(_ffn_kernel, layer_i),
        grid=(N_TOK // TM,),
        in_specs=[
            pl.BlockSpec((TM, D_MODEL), row),
            pl.BlockSpec((1, 6, D_MODEL), lambda i: (_mod_row(i, TM), 0, 0)),
            pl.BlockSpec((1, D_MODEL), lambda i: (0, 0)),
            any_spec, any_spec, any_spec,
        ],
        out_specs=pl.BlockSpec((TM, D_MODEL), row),
        out_shape=jax.ShapeDtypeStruct((N_TOK, D_MODEL), F32),
        scratch_shapes=[
            pltpu.VMEM((D_MODEL, D_FF), BF16),
            pltpu.VMEM((D_MODEL, D_FF), BF16),
            pltpu.VMEM((D_FF, D_MODEL), BF16),
            pltpu.VMEM((2, D_MODEL, TF_FFN), F32),
            pltpu.VMEM((2, D_MODEL, TF_FFN), F32),
            pltpu.VMEM((2, TF_FFN, D_MODEL), F32),
            pltpu.VMEM((TM, D_MODEL), BF16),
            pltpu.VMEM((TM, D_FF), BF16),
            pltpu.SemaphoreType.DMA((3, 2)),
        ],
        compiler_params=pltpu.CompilerParams(dimension_semantics=("arbitrary",), vmem_limit_bytes=FFN_VMEM_LIMIT),
        name="ffn",
    )(x, mod_l, g, w1, w3, w2)


def _split_bf16(t):
    hi = t.astype(BF16)
    return hi, (t - hi.astype(F32)).astype(BF16)


def _lane_values(col_vals, ones_rows):
    base = 32.0
    assert LOCAL_ROWS <= base * 256
    q = jnp.floor(col_vals * (1.0 / base))
    r = col_vals - base * q
    t = base * _dot_nt(ones_rows, q.astype(BF16)) + _dot_nt(ones_rows, r.astype(BF16))
    return t[0:1, :]


def _route_tile(x, mod_ref, g_ref, rw_ref, rb_ref, tri_ref, upper_ref, xl_ref, meta_ref, cnt_ref):
    h = _normmod(x, g_ref[...], mod_ref[0, 4:5, :], mod_ref[0, 3:4, :])
    h_hi = h.astype(BF16)
    w_hi, w_lo = _split_bf16(rw_ref[...])
    both = _dot(h_hi, jnp.concatenate([w_hi, w_lo], axis=1))
    logits = both[:, :LANES] + both[:, LANES:] + rb_ref[...]
    lane = lax.broadcasted_iota(jnp.int32, logits.shape, 1).astype(F32)
    logits = jnp.where(lane < N_EXP, logits, NEG_INF)
    m1 = logits.max(axis=-1, keepdims=True)
    i1 = jnp.where(logits == m1, lane, float(LANES)).min(axis=-1, keepdims=True)
    rest = jnp.where(lane == i1, NEG_INF, logits)
    m2 = rest.max(axis=-1, keepdims=True)
    i2 = jnp.where(rest == m2, lane, float(LANES)).min(axis=-1, keepdims=True)
    e2 = jnp.exp(m2 - m1)
    den = 1.0 + e2
    g1 = 1.0 / den
    g2 = e2 / den

    oh_a = jnp.where(lane == i1, 1.0, 0.0)
    oh_b = jnp.where(lane == i2, 1.0, 0.0)
    tri = tri_ref[...]
    cnt_a = oh_a.sum(axis=0, keepdims=True)
    cnt_b = oh_b.sum(axis=0, keepdims=True)
    run16 = jnp.floor((cnt_a + cnt_b + (RUN_ALIGN - 1)) * (1.0 / RUN_ALIGN))
    run16_rows = jnp.broadcast_to(run16, (SUBLANES, LANES))
    start = RUN_ALIGN * _dot(run16_rows.astype(BF16), upper_ref[...])[0:1, :]
    before = _dot(tri, jnp.concatenate([oh_a, oh_b], axis=1).astype(BF16))
    row_a = oh_a * (start + before[:, :LANES])
    row_b = oh_b * (start + cnt_a + before[:, LANES:])

    ones_rows = jnp.ones((SUBLANES, LANES), BF16)
    tok_a = _lane_values(row_a, ones_rows)
    tok_b = _lane_values(row_b, ones_rows)
    sorted_row = lax.broadcasted_iota(jnp.int32, (LOCAL_ROWS, TR), 0).astype(F32)
    perm = jnp.where((sorted_row == tok_a) | (sorted_row == tok_b), 1.0, 0.0).astype(BF16)
    xl_ref[...] = _dot(perm, h_hi).astype(BF16)

    meta = jnp.zeros_like(logits)
    cols = (row_a.sum(axis=-1, keepdims=True), row_b.sum(axis=-1, keepdims=True), g1, g2)
    for k, col in enumerate(cols):
        meta = jnp.where(lane == k, col, meta)
    meta_ref[...] = meta
    cnt_ref[...] = RUN_ALIGN * run16_rows


def _moe_plan(counts):
    n_tiles = N_TOK // TR
    run = counts.reshape(n_tiles, SUBLANES, LANES)[:, 0, :N_EXP].astype(jnp.int32)
    per_expert = jnp.sum(run, axis=0)
    region = (per_expert + (TMS - 1)) // TMS * TMS
    ends = jnp.cumsum(region)
    offs = ends - region
    seg_end = jnp.cumsum(run, axis=0)
    seg_start = seg_end - run
    local_end = jnp.cumsum(run, axis=1)
    local_start = local_end - run
    n_chunks = local_end[:, N_EXP - 1] // RUN_ALIGN
    tile_start = jnp.arange(MOE_TILES, dtype=jnp.int32) * TMS
    tile_expert = jnp.sum((tile_start[:, None] >= ends[None, :]).astype(jnp.int32), axis=1)
    tile_expert = jnp.minimum(tile_expert, N_EXP - 1)
    n_used = (ends[N_EXP - 1] // TMS).reshape(1)
    experts = jnp.arange(N_EXP, dtype=jnp.int32)
    g_row = jnp.arange(MOE_ROWS // RUN_ALIGN, dtype=jnp.int32) * RUN_ALIGN
    g_exp = jnp.repeat(tile_expert, TMS // RUN_ALIGN)
    pick = g_exp[:, None] == experts[None, :]
    rel = g_row - jnp.sum(jnp.where(pick, offs[None, :], 0), axis=1)
    ends_of = jnp.sum(jnp.where(pick[:, None, :], seg_end[None, :, :], 0), axis=2)
    src_tile = jnp.sum((rel[:, None] >= ends_of).astype(jnp.int32), axis=1)
    valid = (src_tile < n_tiles) & (jnp.repeat(jnp.arange(MOE_TILES), TMS // RUN_ALIGN) < n_used[0])
    src_tile = jnp.minimum(src_tile, n_tiles - 1)
    sel = (src_tile[:, None, None] == jnp.arange(n_tiles)[None, :, None]) & pick[:, None, :]
    shift = jnp.sum(jnp.where(sel, (local_start - seg_start)[None, :, :], 0), axis=(1, 2))
    chunk_src = src_tile * LOCAL_ROWS + rel + shift
    n_valid = jnp.sum(valid.reshape(MOE_TILES, TMS // RUN_ALIGN).astype(jnp.int32), axis=1)
    return chunk_src, n_valid, n_chunks, tile_expert, n_used


def _chunk_copy(src_ref, src_row, dst_ref, dst_row, sem):
    return pltpu.make_async_copy(src_ref.at[pl.ds(src_row, RUN_ALIGN), :],
                                 dst_ref.at[pl.ds(dst_row, RUN_ALIGN), :], sem)


CHUNKS = TMS // RUN_ALIGN
LOCAL_CHUNKS = LOCAL_ROWS // RUN_ALIGN


def _expert_kernel(te_ref, nu_ref, src_ref, nv_ref, nc_ref, xl_hbm, w1_ref, w3_ref, w2_ref, yl_hbm,
                   xbuf, ybuf, zbuf, w1s, w3s, w2s, sem_in, sem_out, sem_zero):
    t = pl.program_id(0)
    n_used = nu_ref[0]
    slot = t & 1

    def for_chunks(n, body):
        @pl.when(n == CHUNKS)
        def _():
            for j in range(CHUNKS):
                body(j)

        @pl.when(n < CHUNKS)
        def _():
            def step(j, c):
                body(j)
                return c

            lax.fori_loop(0, n, step, 0)

    def gather(tile, s):
        n = nv_ref[tile]

        def issue(j):
            src = pl.multiple_of(src_ref[tile * CHUNKS + j], RUN_ALIGN)
            _chunk_copy(xl_hbm, src, xbuf.at[s], pl.multiple_of(j * RUN_ALIGN, RUN_ALIGN), sem_in.at[s]).start()

        def pad(j, c):
            xbuf[s, pl.ds(pl.multiple_of(j * RUN_ALIGN, RUN_ALIGN), RUN_ALIGN), :] = jnp.zeros(
                (RUN_ALIGN, D_MODEL), BF16)
            return c

        for_chunks(n, issue)
        lax.fori_loop(n, CHUNKS, pad, 0)

    def scatter(tile, s):
        def issue(j):
            dst = pl.multiple_of(src_ref[tile * CHUNKS + j], RUN_ALIGN)
            _chunk_copy(ybuf.at[s], pl.multiple_of(j * RUN_ALIGN, RUN_ALIGN), yl_hbm, dst, sem_out.at[s]).start()

        for_chunks(nv_ref[tile], issue)

    def drain(count, sem):
        @pl.when(count == CHUNKS)
        def _():
            pltpu.make_async_copy(xl_hbm.at[pl.ds(0, TMS), :], xbuf.at[0], sem).wait()

        @pl.when(count != CHUNKS)
        def _():
            def one(j, c):
                _chunk_copy(xl_hbm, 0, xbuf.at[0], 0, sem).wait()
                return c

            lax.fori_loop(0, count, one, 0)

    @pl.when(t == 0)
    def _():
        zbuf[...] = jnp.zeros_like(zbuf)
        total = jnp.int32(0)
        for tile in range(N_TOK // TR):
            n = nc_ref[tile]

            def clear(j, c, tile=tile):
                row = pl.multiple_of(tile * LOCAL_ROWS + j * RUN_ALIGN, RUN_ALIGN)
                _chunk_copy(zbuf, 0, yl_hbm, row, sem_zero).start()
                return c

            lax.fori_loop(n, LOCAL_CHUNKS, clear, 0)
            total = total + (LOCAL_CHUNKS - n)
        drain(total, sem_zero)
        gather(0, 0)

    @pl.when(t < n_used)
    def _():
        drain(nv_ref[t], sem_in.at[slot])

    @pl.when(t + 1 < n_used)
    def _():
        gather(t + 1, 1 - slot)

    new_expert = (t == 0) | (te_ref[t] != te_ref[jnp.maximum(t - 1, 0)])

    @pl.when(new_expert)
    def _():
        _cast_rows(w1_ref.at[0, 0], w1s, D_MODEL)
        _cast_rows(w3_ref.at[0, 0], w3s, D_MODEL)
        _cast_rows(w2_ref.at[0, 0], w2s, D_FF_E)

    @pl.when((t >= 2) & (t - 2 < n_used))
    def _():
        drain(nv_ref[jnp.maximum(t - 2, 0)], sem_out.at[slot])

    def swiglu_rows(rows):
        x = xbuf[slot, 0:rows, :]
        hid = (_silu(_dot(x, w1s[...])) * _dot(x, w3s[...])).astype(BF16)
        ybuf[slot, 0:rows, :] = _dot(hid, w2s[...]).astype(BF16)

    @pl.when(t < n_used)
    def _():
        half_full = nv_ref[t] <= CHUNKS // 2

        @pl.when(half_full)
        def _():
            swiglu_rows(TMS // 2)

        @pl.when(jnp.logical_not(half_full))
        def _():
            swiglu_rows(TMS)

        scatter(t, slot)

    @pl.when(t == pl.num_programs(0) - 1)
    def _():
        @pl.when((t >= 1) & (t - 1 < n_used))
        def _():
            drain(nv_ref[jnp.maximum(t - 1, 0)], sem_out.at[1 - slot])

        @pl.when(t < n_used)
        def _():
            drain(nv_ref[t], sem_out.at[slot])


def _experts(layer_i, plan, xl, w1, w3, w2):
    chunk_src, n_valid, n_chunks, tile_expert, n_used = plan
    wsel = lambda t, te, nu, src, nv, nc: (layer_i, te[t], 0, 0)
    any_spec = pl.BlockSpec(memory_space=pl.ANY)
    return pl.pallas_call(
        _expert_kernel,
        grid_spec=pltpu.PrefetchScalarGridSpec(
            num_scalar_prefetch=5,
            grid=(MOE_TILES,),
            in_specs=[
                any_spec,
                pl.BlockSpec((1, 1, D_MODEL, D_FF_E), wsel),
                pl.BlockSpec((1, 1, D_MODEL, D_FF_E), wsel),
                pl.BlockSpec((1, 1, D_FF_E, D_MODEL), wsel),
            ],
            out_specs=any_spec,
            scratch_shapes=[
                pltpu.VMEM((2, TMS, D_MODEL), BF16),
                pltpu.VMEM((2, TMS, D_MODEL), BF16),
                pltpu.VMEM((RUN_ALIGN, D_MODEL), BF16),
                pltpu.VMEM((D_MODEL, D_FF_E), BF16),
                pltpu.VMEM((D_MODEL, D_FF_E), BF16),
                pltpu.VMEM((D_FF_E, D_MODEL), BF16),
                pltpu.SemaphoreType.DMA((2,)),
                pltpu.SemaphoreType.DMA((2,)),
                pltpu.SemaphoreType.DMA(()),
            ],
        ),
        out_shape=jax.ShapeDtypeStruct(xl.shape, BF16),
        compiler_params=_cparams(("arbitrary",)),
        name="moe_experts",
    )(tile_expert, n_used, chunk_src, n_valid, n_chunks, xl, w1, w3, w2)


def _combine_kernel(x_ref, mod_ref, meta_ref, yl_ref, *out_refs):
    sorted_row = lax.broadcasted_iota(jnp.int32, (TR, LOCAL_ROWS), 1).astype(F32)
    y = yl_ref[...]
    pick_a = jnp.where(sorted_row == meta_ref[:, 0:1], 1.0, 0.0).astype(BF16)
    pick_b = jnp.where(sorted_row == meta_ref[:, 1:2], 1.0, 0.0).astype(BF16)
    f = meta_ref[:, 2:3] * _dot(pick_a, y) + meta_ref[:, 3:4] * _dot(pick_b, y)
    _write_x(pl.program_id(0), out_refs, TR, x_ref[...] + mod_ref[0, 5:6, :] * f)


def _combine(x, mod_l, meta, yl, split_out):
    row = lambda i: (i, 0)
    return pl.pallas_call(
        _combine_kernel,
        grid=(N_TOK // TR,),
        in_specs=[
            pl.BlockSpec((TR, D_MODEL), row),
            pl.BlockSpec((1, 6, D_MODEL), lambda i: (_mod_row(i, TR), 0, 0)),
            pl.BlockSpec((TR, LANES), row),
            pl.BlockSpec((LOCAL_ROWS, D_MODEL), row),
        ],
        out_specs=_x_specs(TR, split_out),
        out_shape=_x_shapes(split_out),
        compiler_params=_cparams(("arbitrary",)),
        name="moe_combine",
    )(x, mod_l, meta, yl)


def _moe(layer_i, x, routed, mod_l, w1, w3, w2, split_out):
    xl, meta, counts = routed
    yl = _experts(layer_i, _moe_plan(counts), xl, w1, w3, w2)
    return _combine(x, mod_l, meta, yl, split_out)


def _rope_tables():
    n_rows = L_LAT // GRID_W
    rows = np.repeat(np.arange(n_rows, dtype=np.float32), GRID_W)
    cols = np.tile(np.arange(GRID_W, dtype=np.float32), n_rows)
    quarter = HD // 4
    inv = (np.float32(ROPE_THETA) ** (-np.arange(quarter, dtype=np.float32) / np.float32(quarter))).astype(np.float32)
    ang_r = (rows[:, None] * inv).astype(np.float32)
    ang_c = (cols[:, None] * inv).astype(np.float32)
    cos = np.concatenate([np.cos(ang_r)] * 2 + [np.cos(ang_c)] * 2, axis=1)
    sin = np.concatenate([-np.sin(ang_r), np.sin(ang_r), -np.sin(ang_c), np.sin(ang_c)], axis=1)
    cos = np.concatenate([cos, cos], axis=1)
    sin = np.concatenate([sin, sin], axis=1)
    cos = np.concatenate([np.ones((L_LAT, LANES), np.float32), cos], axis=0)
    sin = np.concatenate([np.zeros((L_LAT, LANES), np.float32), sin], axis=0)
    return jnp.asarray(cos, F32), jnp.asarray(sin, F32)


def _block_ones(width):
    r = np.arange(width) // HD
    return jnp.asarray(r[:, None] == r[None, :], BF16)


def kernel(x_prompt, x_sample, cache_k, cache_v, c, c_ctx, norm1, norm2, w_mod, b_mod, ev_w_in, ev_conv, ev_q_norm, ev_k_norm, ev_w_out, od_w_in, od_q_norm, od_k_norm, od_sink, od_pool_w, od_pool_scale, od_w_out, ffn_w1, ffn_w3, ffn_w2, moe_router, moe_router_b, moe_w1, moe_w3, moe_w2):
    x = (x_prompt.reshape(N_CTX_TOK, D_MODEL), x_sample.reshape(N_LAT_TOK, D_MODEL))
    cond = jnp.concatenate([c_ctx[None, :], c, jnp.zeros((MOD_ROWS - 1 - N_SEQ_LAT, D_MODEL), F32)], axis=0)
    mod = _modulation(cond, w_mod, b_mod).reshape(DEPTH, MOD_ROWS, 6, D_MODEL)

    cos_tab, sin_tab = _rope_tables()
    ones_q = _block_ones(Q_W)
    ones_k = _block_ones(KV_W)
    dup = jnp.asarray(np.concatenate([np.eye(HD), np.eye(HD)], axis=1), BF16)
    tri = jnp.asarray(np.arange(TR)[:, None] > np.arange(TR)[None, :], BF16)
    upper = jnp.asarray(np.arange(LANES)[:, None] < np.arange(LANES)[None, :], BF16)

    new_kv = None
    for l in range(DEPTH):
        i = l // 2
        even = l % 2 == 0
        mod_l = mod[l]
        g1 = norm1[l][None, :]
        g2 = norm2[l][None, :]
        if even:
            q_gain, k_gain = ev_q_norm[i], ev_k_norm[i]
            w_in, w_out = ev_w_in, ev_w_out
        else:
            q_gain, k_gain = od_q_norm[i], od_k_norm[i]
            w_in, w_out = od_w_in, od_w_out
        q_gain = jnp.tile(q_gain, N_Q)[None, :]
        k_gain = jnp.tile(k_gain, N_KV)[None, :]
        outs = _in_proj(even, i, x, mod_l, g1, w_in, ones_q, ones_k, q_gain, k_gain, cos_tab, sin_tab,
                        ev_conv[i] if even else None)
        if even:
            ya, q, k, v = outs
            sink = None
        else:
            q, k, v, xd = outs
            sink = od_sink[i]
        o_ctx, new_kv = _ctx_attn(l, q, k, v, sink, new_kv)
        o_lat = _lat_attn(l, q, k, v, cache_k, cache_v, dup, sink)
        if even:
            (x1,) = _mix_out(True, i, x, mod_l, o_ctx, o_lat, w_out, ya)
            x = (_ffn(i, x1, mod_l, g2, ffn_w1, ffn_w3, ffn_w2),)
        else:
            rw = jnp.pad(moe_router[i], ((0, 0), (0, LANES - N_EXP)))
            rb = jnp.pad(moe_router_b[i], (0, LANES - N_EXP))[None, :]
            x1, *routed = _mix_out(False, i, x, mod_l, o_ctx, o_lat, w_out, xd, od_pool_w[i],
                                   od_pool_scale[i][None, :], g2, rw, rb, tri, upper)
            x = tuple(_moe(i, x1, routed, mod_l, moe_w1, moe_w3, moe_w2, l == DEPTH - 1))

    y_prompt = x[0].reshape(N_SEQ_CTX, L_CTX, D_MODEL)
    y_sample = x[1].reshape(N_SEQ_LAT, L_LAT, D_MODEL)
    return (y_prompt, y_sample, new_kv[0], new_kv[1])
```

```python
import functools

import jax
import jax.numpy as jnp
import numpy as np
from jax import lax
from jax.experimental import pallas as pl
from jax.experimental.pallas import tpu as pltpu

F32 = jnp.float32
BF16 = jnp.bfloat16

D_MODEL = 1024
N_SEQ_CTX = 32
L_CTX = 256
N_SEQ_LAT = 4
L_LAT = 1024
DEPTH = 4
PAST = 512
GRID_W = 64
HD = 64
N_Q = 8
N_KV = 2
GROUP = N_Q // N_KV
Q_W = N_Q * HD
KV_W = N_KV * HD
CONV_W = 512
POOL_W = 512
POOL_WINDOWS = (2, 4, 8, 16)
POOL_G = 128
POOL_PAD = 16
EVEN_IN = 3 * CONV_W + Q_W + 2 * KV_W
ODD_IN = Q_W + 2 * KV_W + POOL_W
WINDOW = 128
D_FF = 2816
N_EXP = 8
TOP_K = 2
D_FF_E = 1024
ROPE_THETA = 10000.0
EPS = 1e-6

N_CTX_TOK = N_SEQ_CTX * L_CTX
N_LAT_TOK = N_SEQ_LAT * L_LAT
N_TOK = N_CTX_TOK + N_LAT_TOK
MOD_ROWS = 16

LANES = 128
SUBLANES = 8
ROPE_Q = HD // 4
VMEM_LIMIT = 56 * 1024 * 1024
FFN_VMEM_LIMIT = 62 * 1024 * 1024

TM_IN = 1024
TM = 1024
TQ_FULL = 512
TQ_WINDOW = 256
CTX_SEQS = 4
TF_FFN = 256
TR = 512
RUN_ALIGN = 16
LOCAL_ROWS = TOP_K * TR + N_EXP * RUN_ALIGN
TMS = 512
MOE_TILES = -(-(TOP_K * N_TOK + (N_TOK // TR) * N_EXP * (RUN_ALIGN - 1) + N_EXP * (TMS - 1)) // TMS)
MOE_ROWS = MOE_TILES * TMS
NEG_INF = float("-inf")


def _cparams(sem):
    return pltpu.CompilerParams(dimension_semantics=sem, vmem_limit_bytes=VMEM_LIMIT)


def _mod_row(i, tm):
    n_ctx = N_CTX_TOK // tm
    return jnp.where(i < n_ctx, 0, 1 + (i - n_ctx) // (L_LAT // tm))


def _x_specs(tm, split):
    if not split:
        return [pl.BlockSpec((tm, D_MODEL), lambda i, *_: (i, 0))]
    n_ctx = N_CTX_TOK // tm
    n_lat = N_LAT_TOK // tm
    return [pl.BlockSpec((tm, D_MODEL), lambda i, *_: (jnp.minimum(i, n_ctx - 1), 0)),
            pl.BlockSpec((tm, D_MODEL), lambda i, *_: (jnp.clip(i - n_ctx, 0, n_lat - 1), 0))]


def _x_shapes(split):
    if not split:
        return [jax.ShapeDtypeStruct((N_TOK, D_MODEL), F32)]
    return [jax.ShapeDtypeStruct((N_CTX_TOK, D_MODEL), F32), jax.ShapeDtypeStruct((N_LAT_TOK, D_MODEL), F32)]


def _read_x(i, x_refs, tm):
    if len(x_refs) == 1:
        return x_refs[0][...]
    return jnp.where(i < N_CTX_TOK // tm, x_refs[0][...], x_refs[1][...])


def _write_x(i, o_refs, tm, val):
    if len(o_refs) == 1:
        o_refs[0][...] = val
        return

    @pl.when(i < N_CTX_TOK // tm)
    def _():
        o_refs[0][...] = val

    @pl.when(i >= N_CTX_TOK // tm)
    def _():
        o_refs[1][...] = val


def _normmod(x, g, scale, shift):
    ms = jnp.mean(x * x, axis=-1, keepdims=True)
    y = x * lax.rsqrt(ms + EPS) * g
    return y * (1.0 + scale) + shift


def _silu(x):
    return x * jax.nn.sigmoid(x)


def _dot(a, b):
    return jnp.dot(a, b, preferred_element_type=F32)


def _dot_nt(a, b):
    return lax.dot_general(a, b, (((1,), (1,)), ((), ())), preferred_element_type=F32)


def _cast_rows(src_ref, dst_ref, rows, chunk=256):
    for r in range(0, rows, chunk):
        dst_ref[r:r + chunk, :] = src_ref[r:r + chunk, :].astype(dst_ref.dtype)


def _mod_kernel(c_ref, w_ref, b_ref, o_ref):
    s = _silu(c_ref[...]).astype(BF16)
    o_ref[0] = _dot(s, w_ref[0].astype(BF16)) + b_ref[0]


def _modulation(cond, w_mod, b_mod):
    tn = 1536
    return pl.pallas_call(
        _mod_kernel,
        grid=(DEPTH, 6 * D_MODEL // tn),
        in_specs=[
            pl.BlockSpec((MOD_ROWS, D_MODEL), lambda l, j: (0, 0)),
            pl.BlockSpec((1, D_MODEL, tn), lambda l, j: (l, 0, j)),
            pl.BlockSpec((1, 1, tn), lambda l, j: (l, 0, j)),
        ],
        out_specs=pl.BlockSpec((1, MOD_ROWS, tn), lambda l, j: (l, 0, j)),
        out_shape=jax.ShapeDtypeStruct((DEPTH, MOD_ROWS, 6 * D_MODEL), F32),
        compiler_params=_cparams(("arbitrary", "arbitrary")),
        name="modulation",
    )(cond, w_mod, b_mod.reshape(DEPTH, 1, 6 * D_MODEL))


def _head_rms(t, ones_bd, gain):
    ssq = _dot((t * t).astype(BF16), ones_bd)
    return t * lax.rsqrt(ssq * (1.0 / HD) + EPS) * gain


def _rope(t, cos, sin_signed):
    lane = lax.broadcasted_iota(jnp.int32, (t.shape[0], LANES), 1)
    first = (lane & (2 * ROPE_Q - 1)) < ROPE_Q
    outs = []
    for c in range(t.shape[1] // LANES):
        tc = t[:, c * LANES:(c + 1) * LANES]
        nxt = pltpu.roll(tc, LANES - ROPE_Q, axis=1)
        prv = pltpu.roll(tc, ROPE_Q, axis=1)
        outs.append(tc * cos + jnp.where(first, nxt, prv) * sin_signed)
    return outs[0] if len(outs) == 1 else jnp.concatenate(outs, axis=1)


def _in_proj_kernel(even, nx, *refs):
    x_refs = refs[:nx]
    mod_ref, g_ref, w_ref, onesq_ref, onesk_ref, qg_ref, kg_ref, cos_ref, sin_ref = refs[nx:nx + 9]
    rest = refs[nx + 9:]
    if even:
        cw_ref, ya_ref, q_ref, k_ref, v_ref, wbf = rest
        q0 = 3 * CONV_W
    else:
        q_ref, k_ref, v_ref, xd_ref, wbf = rest
        q0 = 0
    k0 = q0 + Q_W
    v0 = k0 + KV_W

    @pl.when(pl.program_id(0) == 0)
    def _():
        _cast_rows(w_ref.at[0], wbf, D_MODEL)

    x = _read_x(pl.program_id(0), x_refs, TM_IN)
    h = _normmod(x, g_ref[...], mod_ref[0, 1:2, :], mod_ref[0, 0:1, :]).astype(BF16)
    cos = cos_ref[...]
    sin = sin_ref[...]

    q = _dot(h, wbf[:, q0:q0 + Q_W])
    q = _rope(_head_rms(q, onesq_ref[...], qg_ref[...]), cos, sin) * (HD ** -0.5)
    q_ref[...] = q.astype(BF16)

    kv = _dot(h, wbf[:, k0:k0 + 2 * KV_W])
    k = _rope(_head_rms(kv[:, :KV_W], onesk_ref[...], kg_ref[...]), cos, sin)
    k_ref[...] = k.astype(BF16)
    v_ref[...] = kv[:, KV_W:].astype(BF16)

    if even:
        pos, seq_len = _seq_pos(pl.program_id(0), 1)
        bg = _dot(h, wbf[:, 0:CONV_W])
        u = _dot(h, wbf[:, CONV_W:2 * CONV_W]) * _dot(h, wbf[:, 2 * CONV_W:3 * CONV_W])
        ya_ref[...] = _conv_mixer(bg, u, cw_ref, pos, seq_len)
    else:
        xd_ref[...] = _dot(h, wbf[:, v0 + KV_W:v0 + KV_W + POOL_W])


def _in_proj(even, layer_i, x, mod_l, g, w, ones_q, ones_k, q_gain, k_gain, cos_tab, sin_tab, conv_w=None):
    assert TM_IN == TM
    tm = TM_IN
    n_in = EVEN_IN if even else ODD_IN
    n_ctx = N_CTX_TOK // tm
    per_seq = L_LAT // tm

    def rope_idx(i):
        return (jnp.where(i < n_ctx, 0, per_seq + (i - n_ctx) % per_seq), 0)

    row = lambda i: (i, 0)
    const = lambda i: (0, 0)
    in_specs = _x_specs(tm, len(x) == 2) + [
        pl.BlockSpec((1, 6, D_MODEL), lambda i: (_mod_row(i, tm), 0, 0)),
        pl.BlockSpec((1, D_MODEL), const),
        pl.BlockSpec((1, D_MODEL, n_in), lambda i: (layer_i, 0, 0)),
        pl.BlockSpec((Q_W, Q_W), const),
        pl.BlockSpec((KV_W, KV_W), const),
        pl.BlockSpec((1, Q_W), const),
        pl.BlockSpec((1, KV_W), const),
        pl.BlockSpec((tm, LANES), rope_idx),
        pl.BlockSpec((tm, LANES), rope_idx),
    ]
    qkv_specs = [pl.BlockSpec((tm, Q_W), row), pl.BlockSpec((tm, KV_W), row), pl.BlockSpec((tm, KV_W), row)]
    qkv_shapes = [jax.ShapeDtypeStruct((N_TOK, Q_W), BF16), jax.ShapeDtypeStruct((N_TOK, KV_W), BF16),
                  jax.ShapeDtypeStruct((N_TOK, KV_W), BF16)]
    args = [*x, mod_l, g, w, ones_q, ones_k, q_gain, k_gain, cos_tab, sin_tab]
    if even:
        in_specs.append(pl.BlockSpec((3, CONV_W), const))
        args.append(conv_w)
        out_specs = [pl.BlockSpec((tm, CONV_W), row)] + qkv_specs
        out_shape = [jax.ShapeDtypeStruct((N_TOK, CONV_W), BF16)] + qkv_shapes
    else:
        out_specs = qkv_specs + [pl.BlockSpec((tm, POOL_W), row)]
        out_shape = qkv_shapes + [jax.ShapeDtypeStruct((N_TOK, POOL_W), F32)]
    return pl.pallas_call(
        functools.partial(_in_proj_kernel, even, len(x)),
        grid=(N_TOK // tm,),
        in_specs=in_specs,
        out_specs=out_specs,
        out_shape=out_shape,
        scratch_shapes=[pltpu.VMEM((D_MODEL, n_in), BF16)],
        compiler_params=_cparams(("arbitrary",)),
        name="in_proj_even" if even else "in_proj_odd",
    )(*args)


def _dup_heads(t):
    lane = lax.broadcasted_iota(jnp.int32, t.shape, 1)
    swapped = pltpu.roll(t, HD, axis=1)
    low = lane < HD
    return jnp.where(low, t, swapped), jnp.where(low, swapped, t)


def _softmax_pv(scores, values, sink):
    m = scores[0].max(axis=-1, keepdims=True)
    for s in scores[1:]:
        m = jnp.maximum(m, s.max(axis=-1, keepdims=True))
    if sink is not None:
        m = jnp.maximum(m, sink)
    den = None
    acc = None
    for s, v in zip(scores, values):
        e = jnp.exp(s - m)
        d = e.sum(axis=-1, keepdims=True)
        a = _dot(e.astype(BF16), v)
        den = d if den is None else den + d
        acc = a if acc is None else acc + a
    if sink is not None:
        den = den + jnp.exp(sink - m)
    return acc / den


def _group_attention(q_ref, kv, keys, values, sink_ref, mask, stack):
    t = q_ref.shape[0]
    lane = lax.broadcasted_iota(jnp.int32, (t, LANES), 1)
    low = lane < HD
    zero = jnp.zeros((t, LANES), BF16)
    chunks = [q_ref[:, (kv * 2 + c) * LANES:(kv * 2 + c + 1) * LANES] for c in range(2)]
    if not stack:
        outs = []
        for h in range(GROUP):
            qm = jnp.where(low, chunks[h // 2], zero) if h % 2 == 0 else jnp.where(low, zero, chunks[h // 2])
            scores = [_dot_nt(qm, k) for k in keys]
            if mask is not None:
                scores = [s if m is None else jnp.where(m, s, NEG_INF) for s, m in zip(scores, mask)]
            sink = None if sink_ref is None else sink_ref[kv * GROUP + h]
            outs.append(_softmax_pv(scores, values, sink))
        return (jnp.where(low, outs[0], outs[1]).astype(BF16), jnp.where(low, outs[2], outs[3]).astype(BF16))
    qs = jnp.concatenate([jnp.where(low, chunks[0], zero), jnp.where(low, zero, chunks[0]),
                          jnp.where(low, chunks[1], zero), jnp.where(low, zero, chunks[1])], axis=0)
    scores = [_dot_nt(qs, k) for k in keys]
    if mask is not None:
        scores = [s if m is None else jnp.where(m, s, NEG_INF) for s, m in zip(scores, mask)]
    sink = None
    if sink_ref is not None:
        head = lax.broadcasted_iota(jnp.int32, (GROUP * t, 1), 0) // t
        sink = jnp.full((GROUP * t, 1), sink_ref[kv * GROUP], F32)
        for h in range(1, GROUP):
            sink = jnp.where(head == h, sink_ref[kv * GROUP + h], sink)
    out = _softmax_pv(scores, values, sink)
    return (jnp.where(low, out[0:t], out[t:2 * t]).astype(BF16),
            jnp.where(low, out[2 * t:3 * t], out[3 * t:4 * t]).astype(BF16))


def _ctx_attn_kernel(has_sink, first, *refs):
    if has_sink:
        sink_ref, refs = refs[0], refs[1:]
    else:
        sink_ref = None
    q_ref, k_ref, v_ref = refs[:3]
    o_ref, nk_ref, nv_ref = refs[-3:]
    lane = lax.broadcasted_iota(jnp.int32, (L_CTX, LANES), 1)
    low = lane < HD
    for s in range(CTX_SEQS):
        rows = pl.ds(s * L_CTX, L_CTX)
        k = k_ref[rows, :].astype(F32)
        v = v_ref[rows, :].astype(F32)
        k_sw = pltpu.roll(k, HD, axis=1)
        v_sw = pltpu.roll(v, HD, axis=1)
        nk_ref[s, 0, 0] = k[:, 0:HD]
        nk_ref[s, 0, 1] = k_sw[:, 0:HD]
        nv_ref[s, 0, 0] = v[:, 0:HD]
        nv_ref[s, 0, 1] = v_sw[:, 0:HD]
        if first:
            nk_ref[s, 1:] = jnp.zeros((DEPTH - 1, N_KV, L_CTX, HD), F32)
            nv_ref[s, 1:] = jnp.zeros((DEPTH - 1, N_KV, L_CTX, HD), F32)
        k2 = (jnp.where(low, k, k_sw).astype(BF16), jnp.where(low, k_sw, k).astype(BF16))
        v2 = (jnp.where(low, v, v_sw).astype(BF16), jnp.where(low, v_sw, v).astype(BF16))
        for kv in range(N_KV):
            o0, o1 = _group_attention(q_ref.at[rows, :], kv, [k2[kv]], [v2[kv]], sink_ref, None, True)
            o_ref[rows, (2 * kv) * LANES:(2 * kv + 1) * LANES] = o0
            o_ref[rows, (2 * kv + 1) * LANES:(2 * kv + 2) * LANES] = o1


def _ctx_attn(layer, q, k, v, sink, new_kv):
    has_sink = sink is not None
    first = new_kv is None
    row = lambda b: (b, 0)
    rows = CTX_SEQS * L_CTX
    in_specs = [pl.BlockSpec((rows, Q_W), row), pl.BlockSpec((rows, KV_W), row), pl.BlockSpec((rows, KV_W), row)]
    args = [q, k, v]
    if has_sink:
        in_specs = [pl.BlockSpec(memory_space=pltpu.SMEM)] + in_specs
        args = [sink] + args
    aliases = {}
    if first:
        kv_spec = pl.BlockSpec((CTX_SEQS, DEPTH, N_KV, L_CTX, HD), lambda b: (b, 0, 0, 0, 0))
    else:
        kv_spec = pl.BlockSpec((CTX_SEQS, 1, N_KV, L_CTX, HD), lambda b: (b, layer, 0, 0, 0))
        aliases = {len(args): 1, len(args) + 1: 2}
        in_specs = in_specs + [pl.BlockSpec(memory_space=pl.ANY)] * 2
        args = args + list(new_kv)
    kv_shape = jax.ShapeDtypeStruct((N_SEQ_CTX, DEPTH, N_KV, L_CTX, HD), F32)
    o, nk, nv = pl.pallas_call(
        functools.partial(_ctx_attn_kernel, has_sink, first),
        grid=(N_SEQ_CTX // CTX_SEQS,),
        in_specs=in_specs,
        out_specs=[pl.BlockSpec((rows, Q_W), row), kv_spec, kv_spec],
        out_shape=[jax.ShapeDtypeStruct((N_CTX_TOK, Q_W), BF16), kv_shape, kv_shape],
        input_output_aliases=aliases,
        compiler_params=_cparams(("arbitrary",)),
        name="ctx_attn_sink" if has_sink else "ctx_attn",
    )(*args)
    return o, (nk, nv)


def _lat_attn_kernel(windowed, *refs):
    if windowed:
        sink_ref, q_ref, k_ref, v_ref, ck_ref, cv_ref, dup_ref, o_ref, k2s, v2s, ck2s, cv2s = refs
    else:
        q_ref, k_ref, v_ref, ck_ref, cv_ref, dup_ref, o_ref, k2s, v2s, ck2s, cv2s = refs
        sink_ref = None
    j = pl.program_id(1)

    @pl.when(j == 0)
    def _():
        ka, kb = _dup_heads(k_ref[...].astype(F32))
        va, vb = _dup_heads(v_ref[...].astype(F32))
        k2s[0] = ka.astype(BF16)
        k2s[1] = kb.astype(BF16)
        v2s[0] = va.astype(BF16)
        v2s[1] = vb.astype(BF16)
        dup = dup_ref[...]
        for kv in range(N_KV):
            ck2s[kv] = _dot(ck_ref[0, 0, kv].astype(BF16), dup).astype(BF16)
            cv2s[kv] = _dot(cv_ref[0, 0, kv].astype(BF16), dup).astype(BF16)

    mask = None
    if windowed:
        tq = q_ref.shape[0]
        n_loc = tq + 2 * WINDOW
        start = pl.multiple_of(jnp.clip(j * tq - WINDOW, 0, L_LAT - n_loc), WINDOW)
        qpos = j * tq + lax.broadcasted_iota(jnp.int32, (tq, n_loc), 0)
        kpos = start + lax.broadcasted_iota(jnp.int32, (tq, n_loc), 1)
        mask = [None, jnp.abs(qpos - kpos) <= WINDOW]
    for kv in range(N_KV):
        if windowed:
            k_own = k2s[kv, pl.ds(start, n_loc), :]
            v_own = v2s[kv, pl.ds(start, n_loc), :]
        else:
            k_own = k2s[kv]
            v_own = v2s[kv]
        o0, o1 = _group_attention(q_ref, kv, [ck2s[kv], k_own], [cv2s[kv], v_own], sink_ref, mask, False)
        o_ref[:, (2 * kv) * LANES:(2 * kv + 1) * LANES] = o0
        o_ref[:, (2 * kv + 1) * LANES:(2 * kv + 2) * LANES] = o1


def _lat_attn(layer, q, k, v, cache_k, cache_v, dup, sink):
    windowed = sink is not None
    tq = TQ_WINDOW if windowed else TQ_FULL
    n_qt = L_LAT // tq
    ctx_tiles = N_CTX_TOK // tq
    ctx_seqs = N_CTX_TOK // L_LAT
    cache_spec = pl.BlockSpec((1, 1, N_KV, PAST, HD), lambda b, j: (b, layer, 0, 0, 0))
    in_specs = [
        pl.BlockSpec((tq, Q_W), lambda b, j: (ctx_tiles + b * n_qt + j, 0)),
        pl.BlockSpec((L_LAT, KV_W), lambda b, j: (ctx_seqs + b, 0)),
        pl.BlockSpec((L_LAT, KV_W), lambda b, j: (ctx_seqs + b, 0)),
        cache_spec,
        cache_spec,
        pl.BlockSpec((HD, LANES), lambda b, j: (0, 0)),
    ]
    args = [q, k, v, cache_k, cache_v, dup]
    if windowed:
        in_specs = [pl.BlockSpec(memory_space=pltpu.SMEM)] + in_specs
        args = [sink] + args
    return pl.pallas_call(
        functools.partial(_lat_attn_kernel, windowed),
        grid=(N_SEQ_LAT, n_qt),
        in_specs=in_specs,
        out_specs=pl.BlockSpec((tq, Q_W), lambda b, j: (b * n_qt + j, 0)),
        out_shape=jax.ShapeDtypeStruct((N_LAT_TOK, Q_W), BF16),
        scratch_shapes=[
            pltpu.VMEM((N_KV, L_LAT, LANES), BF16),
            pltpu.VMEM((N_KV, L_LAT, LANES), BF16),
            pltpu.VMEM((N_KV, PAST, LANES), BF16),
            pltpu.VMEM((N_KV, PAST, LANES), BF16),
        ],
        compiler_params=_cparams(("arbitrary", "arbitrary")),
        name="lat_attn_window" if windowed else "lat_attn",
    )(*args)


def _seq_pos(i, width):
    r = lax.broadcasted_iota(jnp.int32, (TM, width), 0)
    is_ctx = i < N_CTX_TOK // TM
    seq_len = jnp.where(is_ctx, L_CTX, L_LAT)
    return r & (seq_len - 1), seq_len


def _shift_rows(t, j, pos, seq_len):
    if j == 0:
        return t
    moved = pltpu.roll(t, (-j) % TM, axis=0)
    ok = (pos + j >= 0) & (pos + j < seq_len)
    return jnp.where(ok, moved, 0.0)


def _conv_mixer(bg, u, cw_ref, pos, seq_len):
    y = (_shift_rows(u, -1, pos, seq_len) * cw_ref[0:1, :] + u * cw_ref[1:2, :]
         + _shift_rows(u, 1, pos, seq_len) * cw_ref[2:3, :])
    return (bg * y).astype(BF16)


def _window_sum(x_seg, w):
    n = x_seg.shape[0] + 2 * POOL_PAD
    z = jnp.zeros((POOL_PAD, x_seg.shape[1]), F32)
    a = jnp.concatenate([z, x_seg, z], axis=0)
    a = a + pltpu.roll(a, 1, axis=0)
    half = 1
    while 2 * half < w:
        a = pltpu.roll(a, half, axis=0) + pltpu.roll(a, n - half, axis=0)
        half *= 2
    return a[POOL_PAD:POOL_PAD + x_seg.shape[0]]


def _pool_mixer(xd_ref, pw_ref, ps_ref, yd_ref, seq_len):
    t = lax.broadcasted_iota(jnp.int32, (seq_len, POOL_G), 0)
    for gi, w in enumerate(POOL_WINDOWS):
        cnt = (jnp.minimum(t + w // 2, seq_len) - jnp.maximum(t - w // 2, 0)).astype(F32)
        wg = pw_ref[gi].astype(BF16)
        lanes = slice(gi * POOL_G, (gi + 1) * POOL_G)
        for s in range(TM // seq_len):
            rows = slice(s * seq_len, (s + 1) * seq_len)
            xg = xd_ref[rows, lanes]
            d = _window_sum(xg, w) / cnt - xg
            yd_ref[rows, lanes] = (_dot(d.astype(BF16), wg) * ps_ref[:, lanes]).astype(BF16)


def _mix_out_kernel(even, nx, *refs):
    x_refs = refs[:nx]
    mod_ref, oc_ref, ol_ref, w_ref = refs[nx:nx + 4]
    rest = refs[nx + 4:]
    if even:
        ya_ref, out_ref, wbf = rest
    else:
        (xd_ref, pw_ref, ps_ref, g2_ref, rw_ref, rb_ref, tri_ref, upper_ref,
         out_ref, xl_ref, meta_ref, cnt_ref, wbf, yd_s) = rest
    i = pl.program_id(0)

    @pl.when(i == 0)
    def _():
        _cast_rows(w_ref.at[0], wbf, D_MODEL)

    o = jnp.where(i < N_CTX_TOK // TM, oc_ref[...], ol_ref[...])
    if even:
        y = _dot(ya_ref[...], wbf[0:CONV_W, :]) + _dot(o, wbf[CONV_W:, :])
    else:
        @pl.when(i < N_CTX_TOK // TM)
        def _():
            _pool_mixer(xd_ref, pw_ref, ps_ref, yd_s, L_CTX)

        @pl.when(i >= N_CTX_TOK // TM)
        def _():
            _pool_mixer(xd_ref, pw_ref, ps_ref, yd_s, L_LAT)

        y = _dot(o, wbf[0:Q_W, :]) + _dot(yd_s[...], wbf[Q_W:, :])
    out_ref[...] = _read_x(i, x_refs, TM) + mod_ref[0, 2:3, :] * y
    if not even:
        for part in range(TM // TR):
            rows = pl.ds(part * TR, TR)
            _route_tile(out_ref[rows, :], mod_ref, g2_ref, rw_ref, rb_ref, tri_ref, upper_ref,
                        xl_ref.at[pl.ds(part * LOCAL_ROWS, LOCAL_ROWS), :], meta_ref.at[rows, :],
                        cnt_ref.at[pl.ds(part * SUBLANES, SUBLANES), :])


def _mix_out(even, layer_i, x, mod_l, o_ctx, o_lat, w_out, *extra):
    n_ctx = N_CTX_TOK // TM
    n_lat = N_LAT_TOK // TM
    row = lambda i: (i, 0)
    const = lambda i: (0, 0)
    in_specs = _x_specs(TM, len(x) == 2) + [
        pl.BlockSpec((1, 6, D_MODEL), lambda i: (_mod_row(i, TM), 0, 0)),
        pl.BlockSpec((TM, Q_W), lambda i: (jnp.minimum(i, n_ctx - 1), 0)),
        pl.BlockSpec((TM, Q_W), lambda i: (jnp.clip(i - n_ctx, 0, n_lat - 1), 0)),
        pl.BlockSpec((1, D_MODEL, D_MODEL), lambda i: (layer_i, 0, 0)),
    ]
    if even:
        in_specs += [pl.BlockSpec((TM, CONV_W), row)]
    else:
        in_specs += [pl.BlockSpec((TM, POOL_W), row),
                     pl.BlockSpec((len(POOL_WINDOWS), POOL_G, POOL_G), lambda i: (0, 0, 0)),
                     pl.BlockSpec((1, POOL_W), const),
                     pl.BlockSpec((1, D_MODEL), const),
                     pl.BlockSpec((D_MODEL, LANES), const),
                     pl.BlockSpec((1, LANES), const),
                     pl.BlockSpec((TR, TR), const),
                     pl.BlockSpec((LANES, LANES), const)]
    out_specs = [pl.BlockSpec((TM, D_MODEL), row)]
    out_shape = [jax.ShapeDtypeStruct((N_TOK, D_MODEL), F32)]
    if not even:
        parts = TM // TR
        out_specs += [pl.BlockSpec((parts * LOCAL_ROWS, D_MODEL), row), pl.BlockSpec((TM, LANES), row),
                      pl.BlockSpec((parts * SUBLANES, LANES), row)]
        out_shape += [jax.ShapeDtypeStruct((N_TOK // TR * LOCAL_ROWS, D_MODEL), BF16),
                      jax.ShapeDtypeStruct((N_TOK, LANES), F32),
                      jax.ShapeDtypeStruct((N_TOK // TR * SUBLANES, LANES), F32)]
    return pl.pallas_call(
        functools.partial(_mix_out_kernel, even, len(x)),
        grid=(N_TOK // TM,),
        in_specs=in_specs,
        out_specs=out_specs,
        out_shape=out_shape,
        scratch_shapes=[pltpu.VMEM((D_MODEL, D_MODEL), BF16)] + ([] if even else [pltpu.VMEM((TM, POOL_W), BF16)]),
        compiler_params=_cparams(("arbitrary",)),
        name="mix_out_even" if even else "mix_out_odd",
    )(*x, mod_l, o_ctx, o_lat, w_out, *extra)


def _ffn_kernel(layer_i, x_ref, mod_ref, g_ref, w1_hbm, w3_hbm, w2_hbm, out_ref,
                w1s, w3s, w2s, st1, st3, st2, hs, hid, sem):
    i = pl.program_id(0)
    nf = D_FF // TF_FFN

    def chunk_copies(f, slot):
        cols = pl.ds(f * TF_FFN, TF_FFN)
        return (pltpu.make_async_copy(w1_hbm.at[layer_i, :, cols], st1.at[slot], sem.at[0, slot]),
                pltpu.make_async_copy(w3_hbm.at[layer_i, :, cols], st3.at[slot], sem.at[1, slot]),
                pltpu.make_async_copy(w2_hbm.at[layer_i, cols, :], st2.at[slot], sem.at[2, slot]))

    def tile(first):
        if first:
            for c in chunk_copies(0, 0):
                c.start()
        hs[...] = _normmod(x_ref[...], g_ref[...], mod_ref[0, 4:5, :], mod_ref[0, 3:4, :]).astype(BF16)
        for f in range(nf):
            slot = f % 2
            lo, hi = f * TF_FFN, (f + 1) * TF_FFN
            if first:
                if f + 1 < nf:
                    for c in chunk_copies(f + 1, 1 - slot):
                        c.start()
                for c in chunk_copies(f, slot):
                    c.wait()
                w1s[:, lo:hi] = st1[slot].astype(BF16)
                w3s[:, lo:hi] = st3[slot].astype(BF16)
                w2s[lo:hi, :] = st2[slot].astype(BF16)
            h = hs[...]
            hid[:, lo:hi] = (_silu(_dot(h, w1s[:, lo:hi])) * _dot(h, w3s[:, lo:hi])).astype(BF16)
        out_ref[...] = x_ref[...] + mod_ref[0, 5:6, :] * _dot(hid[...], w2s[...])

    @pl.when(i == 0)
    def _():
        tile(True)

    @pl.when(i > 0)
    def _():
        tile(False)


def _ffn(layer_i, x, mod_l, g, w1, w3, w2):
    row = lambda i: (i, 0)
    any_spec = pl.BlockSpec(memory_space=pl.ANY)
    return pl.pallas_call(
        functools.partial(_ffn_kernel, layer_i),
        grid=(N_TOK // TM,),
        in_specs=[
            pl.BlockSpec((TM, D_MODEL), row),
            pl.BlockSpec((1, 6, D_MODEL), lambda i: (_mod_row(i, TM), 0, 0)),
            pl.BlockSpec((1, D_MODEL), lambda i: (0, 0)),
            any_spec, any_spec, any_spec,
        ],
        out_specs=pl.BlockSpec((TM, D_MODEL), row),
        out_shape=jax.ShapeDtypeStruct((N_TOK, D_MODEL), F32),
        scratch_shapes=[
            pltpu.VMEM((D_MODEL, D_FF), BF16),
            pltpu.VMEM((D_MODEL, D_FF), BF16),
            pltpu.VMEM((D_FF, D_MODEL), BF16),
            pltpu.VMEM((2, D_MODEL, TF_FFN), F32),
            pltpu.VMEM((2, D_MODEL, TF_FFN), F32),
            pltpu.VMEM((2, TF_FFN, D_MODEL), F32),
            pltpu.VMEM((TM, D_MODEL), BF16),
            pltpu.VMEM((TM, D_FF), BF16),
            pltpu.SemaphoreType.DMA((3, 2)),
        ],
        compiler_params=pltpu.CompilerParams(dimension_semantics=("arbitrary",), vmem_limit_bytes=FFN_VMEM_LIMIT),
        name="ffn",
    )(x, mod_l, g, w1, w3, w2)


def _split_bf16(t):
    hi = t.astype(BF16)
    return hi, (t - hi.astype(F32)).astype(BF16)


def _lane_values(col_vals, ones_rows):
    base = 32.0
    assert LOCAL_ROWS <= base * 256
    q = jnp.floor(col_vals * (1.0 / base))
    r = col_vals - base * q
    t = base * _dot_nt(ones_rows, q.astype(BF16)) + _dot_nt(ones_rows, r.astype(BF16))
    return t[0:1, :]


def _route_tile(x, mod_ref, g_ref, rw_ref, rb_ref, tri_ref, upper_ref, xl_ref, meta_ref, cnt_ref):
    h = _normmod(x, g_ref[...], mod_ref[0, 4:5, :], mod_ref[0, 3:4, :])
    h_hi, h_lo = _split_bf16(h)
    w_hi, w_lo = _split_bf16(rw_ref[...])
    both = _dot(h_hi, jnp.concatenate([w_hi, w_lo], axis=1))
    logits = both[:, :LANES] + both[:, LANES:] + _dot(h_lo, w_hi) + rb_ref[...]
    lane = lax.broadcasted_iota(jnp.int32, logits.shape, 1).astype(F32)
    logits = jnp.where(lane < N_EXP, logits, NEG_INF)
    m1 = logits.max(axis=-1, keepdims=True)
    i1 = jnp.where(logits == m1, lane, float(LANES)).min(axis=-1, keepdims=True)
    rest = jnp.where(lane == i1, NEG_INF, logits)
    m2 = rest.max(axis=-1, keepdims=True)
    i2 = jnp.where(rest == m2, lane, float(LANES)).min(axis=-1, keepdims=True)
    e2 = jnp.exp(m2 - m1)
    den = 1.0 + e2
    g1 = 1.0 / den
    g2 = e2 / den

    oh_a = jnp.where(lane == i1, 1.0, 0.0)
    oh_b = jnp.where(lane == i2, 1.0, 0.0)
    tri = tri_ref[...]
    cnt_a = oh_a.sum(axis=0, keepdims=True)
    cnt_b = oh_b.sum(axis=0, keepdims=True)
    run16 = jnp.floor((cnt_a + cnt_b + (RUN_ALIGN - 1)) * (1.0 / RUN_ALIGN))
    run16_rows = jnp.broadcast_to(run16, (SUBLANES, LANES))
    start = RUN_ALIGN * _dot(run16_rows.astype(BF16), upper_ref[...])[0:1, :]
    before = _dot(tri, jnp.concatenate([oh_a, oh_b], axis=1).astype(BF16))
    row_a = oh_a * (start + before[:, :LANES])
    row_b = oh_b * (start + cnt_a + before[:, LANES:])

    ones_rows = jnp.ones((SUBLANES, LANES), BF16)
    tok_a = _lane_values(row_a, ones_rows)
    tok_b = _lane_values(row_b, ones_rows)
    sorted_row = lax.broadcasted_iota(jnp.int32, (LOCAL_ROWS, TR), 0).astype(F32)
    perm = jnp.where((sorted_row == tok_a) | (sorted_row == tok_b), 1.0, 0.0).astype(BF16)
    xl_ref[...] = _dot(perm, h_hi).astype(BF16)

    meta = jnp.zeros_like(logits)
    cols = (row_a.sum(axis=-1, keepdims=True), row_b.sum(axis=-1, keepdims=True), g1, g2)
    for k, col in enumerate(cols):
        meta = jnp.where(lane == k, col, meta)
    meta_ref[...] = meta
    cnt_ref[...] = RUN_ALIGN * run16_rows


def _moe_plan(counts):
    n_tiles = N_TOK // TR
    run = counts.reshape(n_tiles, SUBLANES, LANES)[:, 0, :N_EXP].astype(jnp.int32)
    per_expert = jnp.sum(run, axis=0)
    region = (per_expert + (TMS - 1)) // TMS * TMS
    ends = jnp.cumsum(region)
    offs = ends - region
    seg_end = jnp.cumsum(run, axis=0)
    seg_start = seg_end - run
    local_end = jnp.cumsum(run, axis=1)
    local_start = local_end - run
    n_chunks = local_end[:, N_EXP - 1] // RUN_ALIGN
    tile_start = jnp.arange(MOE_TILES, dtype=jnp.int32) * TMS
    tile_expert = jnp.sum((tile_start[:, None] >= ends[None, :]).astype(jnp.int32), axis=1)
    tile_expert = jnp.minimum(tile_expert, N_EXP - 1)
    n_used = (ends[N_EXP - 1] // TMS).reshape(1)
    experts = jnp.arange(N_EXP, dtype=jnp.int32)
    g_row = jnp.arange(MOE_ROWS // RUN_ALIGN, dtype=jnp.int32) * RUN_ALIGN
    g_exp = jnp.repeat(tile_expert, TMS // RUN_ALIGN)
    pick = g_exp[:, None] == experts[None, :]
    rel = g_row - jnp.sum(jnp.where(pick, offs[None, :], 0), axis=1)
    ends_of = jnp.sum(jnp.where(pick[:, None, :], seg_end[None, :, :], 0), axis=2)
    src_tile = jnp.sum((rel[:, None] >= ends_of).astype(jnp.int32), axis=1)
    valid = (src_tile < n_tiles) & (jnp.repeat(jnp.arange(MOE_TILES), TMS // RUN_ALIGN) < n_used[0])
    src_tile = jnp.minimum(src_tile, n_tiles - 1)
    sel = (src_tile[:, None, None] == jnp.arange(n_tiles)[None, :, None]) & pick[:, None, :]
    shift = jnp.sum(jnp.where(sel, (local_start - seg_start)[None, :, :], 0), axis=(1, 2))
    chunk_src = src_tile * LOCAL_ROWS + rel + shift
    n_valid = jnp.sum(valid.reshape(MOE_TILES, TMS // RUN_ALIGN).astype(jnp.int32), axis=1)
    return chunk_src, n_valid, n_chunks, tile_expert, n_used


def _chunk_copy(src_ref, src_row, dst_ref, dst_row, sem):
    return pltpu.make_async_copy(src_ref.at[pl.ds(src_row, RUN_ALIGN), :],
                                 dst_ref.at[pl.ds(dst_row, RUN_ALIGN), :], sem)


CHUNKS = TMS // RUN_ALIGN
LOCAL_CHUNKS = LOCAL_ROWS // RUN_ALIGN


def _expert_kernel(te_ref, nu_ref, src_ref, nv_ref, nc_ref, xl_hbm, w1_ref, w3_ref, w2_ref, yl_hbm,
                   xbuf, ybuf, zbuf, w1s, w3s, w2s, sem_in, sem_out, sem_zero):
    t = pl.program_id(0)
    n_used = nu_ref[0]
    slot = t & 1

    def for_chunks(n, body):
        @pl.when(n == CHUNKS)
        def _():
            for j in range(CHUNKS):
                body(j)

        @pl.when(n < CHUNKS)
        def _():
            def step(j, c):
                body(j)
                return c

            lax.fori_loop(0, n, step, 0)

    def gather(tile, s):
        n = nv_ref[tile]

        def issue(j):
            src = pl.multiple_of(src_ref[tile * CHUNKS + j], RUN_ALIGN)
            _chunk_copy(xl_hbm, src, xbuf.at[s], pl.multiple_of(j * RUN_ALIGN, RUN_ALIGN), sem_in.at[s]).start()

        def pad(j, c):
            xbuf[s, pl.ds(pl.multiple_of(j * RUN_ALIGN, RUN_ALIGN), RUN_ALIGN), :] = jnp.zeros(
                (RUN_ALIGN, D_MODEL), BF16)
            return c

        for_chunks(n, issue)
        lax.fori_loop(n, CHUNKS, pad, 0)

    def scatter(tile, s):
        def issue(j):
            dst = pl.multiple_of(src_ref[tile * CHUNKS + j], RUN_ALIGN)
            _chunk_copy(ybuf.at[s], pl.multiple_of(j * RUN_ALIGN, RUN_ALIGN), yl_hbm, dst, sem_out.at[s]).start()

        for_chunks(nv_ref[tile], issue)

    def drain(count, sem):
        @pl.when(count == CHUNKS)
        def _():
            pltpu.make_async_copy(xl_hbm.at[pl.ds(0, TMS), :], xbuf.at[0], sem).wait()

        @pl.when(count != CHUNKS)
        def _():
            def one(j, c):
                _chunk_copy(xl_hbm, 0, xbuf.at[0], 0, sem).wait()
                return c

            lax.fori_loop(0, count, one, 0)

    @pl.when(t == 0)
    def _():
        zbuf[...] = jnp.zeros_like(zbuf)
        total = jnp.int32(0)
        for tile in range(N_TOK // TR):
            n = nc_ref[tile]

            def clear(j, c, tile=tile):
                row = pl.multiple_of(tile * LOCAL_ROWS + j * RUN_ALIGN, RUN_ALIGN)
                _chunk_copy(zbuf, 0, yl_hbm, row, sem_zero).start()
                return c

            lax.fori_loop(n, LOCAL_CHUNKS, clear, 0)
            total = total + (LOCAL_CHUNKS - n)
        drain(total, sem_zero)
        gather(0, 0)

    @pl.when(t < n_used)
    def _():
        drain(nv_ref[t], sem_in.at[slot])

    @pl.when(t + 1 < n_used)
    def _():
        gather(t + 1, 1 - slot)

    new_expert = (t == 0) | (te_ref[t] != te_ref[jnp.maximum(t - 1, 0)])

    @pl.when(new_expert)
    def _():
        _cast_rows(w1_ref.at[0, 0], w1s, D_MODEL)
        _cast_rows(w3_ref.at[0, 0], w3s, D_MODEL)
        _cast_rows(w2_ref.at[0, 0], w2s, D_FF_E)

    @pl.when((t >= 2) & (t - 2 < n_used))
    def _():
        drain(nv_ref[jnp.maximum(t - 2, 0)], sem_out.at[slot])

    def swiglu_rows(rows):
        x = xbuf[slot, 0:rows, :]
        hid = (_silu(_dot(x, w1s[...])) * _dot(x, w3s[...])).astype(BF16)
        ybuf[slot, 0:rows, :] = _dot(hid, w2s[...]).astype(BF16)

    @pl.when(t < n_used)
    def _():
        half_full = nv_ref[t] <= CHUNKS // 2

        @pl.when(half_full)
        def _():
            swiglu_rows(TMS // 2)

        @pl.when(jnp.logical_not(half_full))
        def _():
            swiglu_rows(TMS)

        scatter(t, slot)

    @pl.when(t == pl.num_programs(0) - 1)
    def _():
        @pl.when((t >= 1) & (t - 1 < n_used))
        def _():
            drain(nv_ref[jnp.maximum(t - 1, 0)], sem_out.at[1 - slot])

        @pl.when(t < n_used)
        def _():
            drain(nv_ref[t], sem_out.at[slot])


def _experts(layer_i, plan, xl, w1, w3, w2):
    chunk_src, n_valid, n_chunks, tile_expert, n_used = plan
    wsel = lambda t, te, nu, src, nv, nc: (layer_i, te[t], 0, 0)
    any_spec = pl.BlockSpec(memory_space=pl.ANY)
    return pl.pallas_call(
        _expert_kernel,
        grid_spec=pltpu.PrefetchScalarGridSpec(
            num_scalar_prefetch=5,
            grid=(MOE_TILES,),
            in_specs=[
                any_spec,
                pl.BlockSpec((1, 1, D_MODEL, D_FF_E), wsel),
                pl.BlockSpec((1, 1, D_MODEL, D_FF_E), wsel),
                pl.BlockSpec((1, 1, D_FF_E, D_MODEL), wsel),
            ],
            out_specs=any_spec,
            scratch_shapes=[
                pltpu.VMEM((2, TMS, D_MODEL), BF16),
                pltpu.VMEM((2, TMS, D_MODEL), BF16),
                pltpu.VMEM((RUN_ALIGN, D_MODEL), BF16),
                pltpu.VMEM((D_MODEL, D_FF_E), BF16),
                pltpu.VMEM((D_MODEL, D_FF_E), BF16),
                pltpu.VMEM((D_FF_E, D_MODEL), BF16),
                pltpu.SemaphoreType.DMA((2,)),
                pltpu.SemaphoreType.DMA((2,)),
                pltpu.SemaphoreType.DMA(()),
            ],
        ),
        out_shape=jax.ShapeDtypeStruct(xl.shape, BF16),
        compiler_params=_cparams(("arbitrary",)),
        name="moe_experts",
    )(tile_expert, n_used, chunk_src, n_valid, n_chunks, xl, w1, w3, w2)


def _combine_kernel(x_ref, mod_ref, meta_ref, yl_ref, *out_refs):
    sorted_row = lax.broadcasted_iota(jnp.int32, (TR, LOCAL_ROWS), 1).astype(F32)
    y = yl_ref[...]
    pick_a = jnp.where(sorted_row == meta_ref[:, 0:1], 1.0, 0.0).astype(BF16)
    pick_b = jnp.where(sorted_row == meta_ref[:, 1:2], 1.0, 0.0).astype(BF16)
    f = meta_ref[:, 2:3] * _dot(pick_a, y) + meta_ref[:, 3:4] * _dot(pick_b, y)
    _write_x(pl.program_id(0), out_refs, TR, x_ref[...] + mod_ref[0, 5:6, :] * f)


def _combine(x, mod_l, meta, yl, split_out):
    row = lambda i: (i, 0)
    return pl.pallas_call(
        _combine_kernel,
        grid=(N_TOK // TR,),
        in_specs=[
            pl.BlockSpec((TR, D_MODEL), row),
            pl.BlockSpec((1, 6, D_MODEL), lambda i: (_mod_row(i, TR), 0, 0)),
            pl.BlockSpec((TR, LANES), row),
            pl.BlockSpec((LOCAL_ROWS, D_MODEL), row),
        ],
        out_specs=_x_specs(TR, split_out),
        out_shape=_x_shapes(split_out),
        compiler_params=_cparams(("arbitrary",)),
        name="moe_combine",
    )(x, mod_l, meta, yl)


def _moe(layer_i, x, routed, mod_l, w1, w3, w2, split_out):
    xl, meta, counts = routed
    yl = _experts(layer_i, _moe_plan(counts), xl, w1, w3, w2)
    return _combine(x, mod_l, meta, yl, split_out)


def _rope_tables():
    n_rows = L_LAT // GRID_W
    rows = np.repeat(np.arange(n_rows, dtype=np.float32), GRID_W)
    cols = np.tile(np.arange(GRID_W, dtype=np.float32), n_rows)
    quarter = HD // 4
    inv = (np.float32(ROPE_THETA) ** (-np.arange(quarter, dtype=np.float32) / np.float32(quarter))).astype(np.float32)
    ang_r = (rows[:, None] * inv).astype(np.float32)
    ang_c = (cols[:, None] * inv).astype(np.float32)
    cos = np.concatenate([np.cos(ang_r)] * 2 + [np.cos(ang_c)] * 2, axis=1)
    sin = np.concatenate([-np.sin(ang_r), np.sin(ang_r), -np.sin(ang_c), np.sin(ang_c)], axis=1)
    cos = np.concatenate([cos, cos], axis=1)
    sin = np.concatenate([sin, sin], axis=1)
    cos = np.concatenate([np.ones((L_LAT, LANES), np.float32), cos], axis=0)
    sin = np.concatenate([np.zeros((L_LAT, LANES), np.float32), sin], axis=0)
    return jnp.asarray(cos, F32), jnp.asarray(sin, F32)


def _block_ones(width):
    r = np.arange(width) // HD
    return jnp.asarray(r[:, None] == r[None, :], BF16)


def kernel(x_prompt, x_sample, cache_k, cache_v, c, c_ctx, norm1, norm2, w_mod, b_mod, ev_w_in, ev_conv, ev_q_norm, ev_k_norm, ev_w_out, od_w_in, od_q_norm, od_k_norm, od_sink, od_pool_w, od_pool_scale, od_w_out, ffn_w1, ffn_w3, ffn_w2, moe_router, moe_router_b, moe_w1, moe_w3, moe_w2):
    x = (x_prompt.reshape(N_CTX_TOK, D_MODEL), x_sample.reshape(N_LAT_TOK, D_MODEL))
    cond = jnp.concatenate([c_ctx[None, :], c, jnp.zeros((MOD_ROWS - 1 - N_SEQ_LAT, D_MODEL), F32)], axis=0)
    mod = _modulation(cond, w_mod, b_mod).reshape(DEPTH, MOD_ROWS, 6, D_MODEL)

    cos_tab, sin_tab = _rope_tables()
    ones_q = _block_ones(Q_W)
    ones_k = _block_ones(KV_W)
    dup = jnp.asarray(np.concatenate([np.eye(HD), np.eye(HD)], axis=1), BF16)
    tri = jnp.asarray(np.arange(TR)[:, None] > np.arange(TR)[None, :], BF16)
    upper = jnp.asarray(np.arange(LANES)[:, None] < np.arange(LANES)[None, :], BF16)

    new_kv = None
    for l in range(DEPTH):
        i = l // 2
        even = l % 2 == 0
        mod_l = mod[l]
        g1 = norm1[l][None, :]
        g2 = norm2[l][None, :]
        if even:
            q_gain, k_gain = ev_q_norm[i], ev_k_norm[i]
            w_in, w_out = ev_w_in, ev_w_out
        else:
            q_gain, k_gain = od_q_norm[i], od_k_norm[i]
            w_in, w_out = od_w_in, od_w_out
        q_gain = jnp.tile(q_gain, N_Q)[None, :]
        k_gain = jnp.tile(k_gain, N_KV)[None, :]
        outs = _in_proj(even, i, x, mod_l, g1, w_in, ones_q, ones_k, q_gain, k_gain, cos_tab, sin_tab,
                        ev_conv[i] if even else None)
        if even:
            ya, q, k, v = outs
            sink = None
        else:
            q, k, v, xd = outs
            sink = od_sink[i]
        o_ctx, new_kv = _ctx_attn(l, q, k, v, sink, new_kv)
        o_lat = _lat_attn(l, q, k, v, cache_k, cache_v, dup, sink)
        if even:
            (x1,) = _mix_out(True, i, x, mod_l, o_ctx, o_lat, w_out, ya)
            x = (_ffn(i, x1, mod_l, g2, ffn_w1, ffn_w3, ffn_w2),)
        else:
            rw = jnp.pad(moe_router[i], ((0, 0), (0, LANES - N_EXP)))
            rb = jnp.pad(moe_router_b[i], (0, LANES - N_EXP))[None, :]
            x1, *routed = _mix_out(False, i, x, mod_l, o_ctx, o_lat, w_out, xd, od_pool_w[i],
                                   od_pool_scale[i][None, :], g2, rw, rb, tri, upper)
            x = tuple(_moe(i, x1, routed, mod_l, moe_w1, moe_w3, moe_w2, l == DEPTH - 1))

    y_prompt = x[0].reshape(N_SEQ_CTX, L_CTX, D_MODEL)
    y_sample = x[1].reshape(N_SEQ_LAT, L_LAT, D_MODEL)
    return (y_prompt, y_sample, new_kv[0], new_kv[1])
```

```python
import functools

import jax
import jax.numpy as jnp
import numpy as np
from jax import lax
from jax.experimental import pallas as pl
from jax.experimental.pallas import tpu as pltpu

F32 = jnp.float32
BF16 = jnp.bfloat16

D_MODEL = 1024
N_SEQ_CTX = 32
L_CTX = 256
N_SEQ_LAT = 4
L_LAT = 1024
DEPTH = 4
PAST = 512
GRID_W = 64
HD = 64
N_Q = 8
N_KV = 2
GROUP = N_Q // N_KV
Q_W = N_Q * HD
KV_W = N_KV * HD
CONV_W = 512
POOL_W = 512
POOL_WINDOWS = (2, 4, 8, 16)
POOL_G = 128
POOL_PAD = 16
EVEN_IN = 3 * CONV_W + Q_W + 2 * KV_W
ODD_IN = Q_W + 2 * KV_W + POOL_W
WINDOW = 128
D_FF = 2816
N_EXP = 8
TOP_K = 2
D_FF_E = 1024
ROPE_THETA = 10000.0
EPS = 1e-6

N_CTX_TOK = N_SEQ_CTX * L_CTX
N_LAT_TOK = N_SEQ_LAT * L_LAT
N_TOK = N_CTX_TOK + N_LAT_TOK
MOD_ROWS = 16

LANES = 128
SUBLANES = 8
ROPE_Q = HD // 4
VMEM_LIMIT = 56 * 1024 * 1024
FFN_VMEM_LIMIT = 62 * 1024 * 1024

TM_IN = 1024
TM = 1024
TQ_FULL = 512
TQ_WINDOW = 256
CTX_SEQS = 4
TF_FFN = 256
TR = 512
RUN_ALIGN = 16
LOCAL_ROWS = TOP_K * TR + N_EXP * RUN_ALIGN
TMS = 512
MOE_TILES = -(-(TOP_K * N_TOK + (N_TOK // TR) * N_EXP * (RUN_ALIGN - 1) + N_EXP * (TMS - 1)) // TMS)
MOE_ROWS = MOE_TILES * TMS
NEG_INF = float("-inf")


def _cparams(sem):
    return pltpu.CompilerParams(dimension_semantics=sem, vmem_limit_bytes=VMEM_LIMIT)


def _mod_row(i, tm):
    n_ctx = N_CTX_TOK // tm
    return jnp.where(i < n_ctx, 0, 1 + (i - n_ctx) // (L_LAT // tm))


def _x_specs(tm, split):
    if not split:
        return [pl.BlockSpec((tm, D_MODEL), lambda i, *_: (i, 0))]
    n_ctx = N_CTX_TOK // tm
    n_lat = N_LAT_TOK // tm
    return [pl.BlockSpec((tm, D_MODEL), lambda i, *_: (jnp.minimum(i, n_ctx - 1), 0)),
            pl.BlockSpec((tm, D_MODEL), lambda i, *_: (jnp.clip(i - n_ctx, 0, n_lat - 1), 0))]


def _x_shapes(split):
    if not split:
        return [jax.ShapeDtypeStruct((N_TOK, D_MODEL), F32)]
    return [jax.ShapeDtypeStruct((N_CTX_TOK, D_MODEL), F32), jax.ShapeDtypeStruct((N_LAT_TOK, D_MODEL), F32)]


def _read_x(i, x_refs, tm):
    if len(x_refs) == 1:
        return x_refs[0][...]
    return jnp.where(i < N_CTX_TOK // tm, x_refs[0][...], x_refs[1][...])


def _write_x(i, o_refs, tm, val):
    if len(o_refs) == 1:
        o_refs[0][...] = val
        return

    @pl.when(i < N_CTX_TOK // tm)
    def _():
        o_refs[0][...] = val

    @pl.when(i >= N_CTX_TOK // tm)
    def _():
        o_refs[1][...] = val


def _normmod(x, g, scale, shift):
    ms = jnp.mean(x * x, axis=-1, keepdims=True)
    y = x * lax.rsqrt(ms + EPS) * g
    return y * (1.0 + scale) + shift


def _silu(x):
    return x * jax.nn.sigmoid(x)


def _dot(a, b):
    return jnp.dot(a, b, preferred_element_type=F32)


def _dot_nt(a, b):
    return lax.dot_general(a, b, (((1,), (1,)), ((), ())), preferred_element_type=F32)


def _cast_rows(src_ref, dst_ref, rows, chunk=256):
    for r in range(0, rows, chunk):
        dst_ref[r:r + chunk, :] = src_ref[r:r + chunk, :].astype(dst_ref.dtype)


def _mod_kernel(c_ref, w_ref, b_ref, o_ref):
    s = _silu(c_ref[...]).astype(BF16)
    o_ref[0] = _dot(s, w_ref[0].astype(BF16)) + b_ref[0]


def _modulation(cond, w_mod, b_mod):
    tn = 1536
    return pl.pallas_call(
        _mod_kernel,
        grid=(DEPTH, 6 * D_MODEL // tn),
        in_specs=[
            pl.BlockSpec((MOD_ROWS, D_MODEL), lambda l, j: (0, 0)),
            pl.BlockSpec((1, D_MODEL, tn), lambda l, j: (l, 0, j)),
            pl.BlockSpec((1, 1, tn), lambda l, j: (l, 0, j)),
        ],
        out_specs=pl.BlockSpec((1, MOD_ROWS, tn), lambda l, j: (l, 0, j)),
        out_shape=jax.ShapeDtypeStruct((DEPTH, MOD_ROWS, 6 * D_MODEL), F32),
        compiler_params=_cparams(("arbitrary", "arbitrary")),
        name="modulation",
    )(cond, w_mod, b_mod.reshape(DEPTH, 1, 6 * D_MODEL))


def _head_rms(t, ones_bd, gain):
    ssq = _dot((t * t).astype(BF16), ones_bd)
    return t * lax.rsqrt(ssq * (1.0 / HD) + EPS) * gain


def _rope(t, cos, sin_signed):
    lane = lax.broadcasted_iota(jnp.int32, (t.shape[0], LANES), 1)
    first = (lane & (2 * ROPE_Q - 1)) < ROPE_Q
    outs = []
    for c in range(t.shape[1] // LANES):
        tc = t[:, c * LANES:(c + 1) * LANES]
        nxt = pltpu.roll(tc, LANES - ROPE_Q, axis=1)
        prv = pltpu.roll(tc, ROPE_Q, axis=1)
        outs.append(tc * cos + jnp.where(first, nxt, prv) * sin_signed)
    return outs[0] if len(outs) == 1 else jnp.concatenate(outs, axis=1)


def _in_proj_kernel(even, nx, *refs):
    x_refs = refs[:nx]
    mod_ref, g_ref, w_ref, onesq_ref, onesk_ref, qg_ref, kg_ref, cos_ref, sin_ref = refs[nx:nx + 9]
    rest = refs[nx + 9:]
    if even:
        cw_ref, ya_ref, q_ref, k_ref, v_ref, wbf = rest
        q0 = 3 * CONV_W
    else:
        q_ref, k_ref, v_ref, xd_ref, wbf = rest
        q0 = 0
    k0 = q0 + Q_W
    v0 = k0 + KV_W

    @pl.when(pl.program_id(0) == 0)
    def _():
        _cast_rows(w_ref.at[0], wbf, D_MODEL)

    x = _read_x(pl.program_id(0), x_refs, TM_IN)
    h = _normmod(x, g_ref[...], mod_ref[0, 1:2, :], mod_ref[0, 0:1, :]).astype(BF16)
    cos = cos_ref[...]
    sin = sin_ref[...]

    q = _dot(h, wbf[:, q0:q0 + Q_W])
    q = _rope(_head_rms(q, onesq_ref[...], qg_ref[...]), cos, sin) * (HD ** -0.5)
    q_ref[...] = q.astype(BF16)

    kv = _dot(h, wbf[:, k0:k0 + 2 * KV_W])
    k = _rope(_head_rms(kv[:, :KV_W], onesk_ref[...], kg_ref[...]), cos, sin)
    k_ref[...] = k.astype(BF16)
    v_ref[...] = kv[:, KV_W:].astype(BF16)

    if even:
        pos, seq_len = _seq_pos(pl.program_id(0), 1)
        bg = _dot(h, wbf[:, 0:CONV_W])
        u = _dot(h, wbf[:, CONV_W:2 * CONV_W]) * _dot(h, wbf[:, 2 * CONV_W:3 * CONV_W])
        ya_ref[...] = _conv_mixer(bg, u, cw_ref, pos, seq_len)
    else:
        xd_ref[...] = _dot(h, wbf[:, v0 + KV_W:v0 + KV_W + POOL_W])


def _in_proj(even, layer_i, x, mod_l, g, w, ones_q, ones_k, q_gain, k_gain, cos_tab, sin_tab, conv_w=None):
    assert TM_IN == TM
    tm = TM_IN
    n_in = EVEN_IN if even else ODD_IN
    n_ctx = N_CTX_TOK // tm
    per_seq = L_LAT // tm

    def rope_idx(i):
        return (jnp.where(i < n_ctx, 0, per_seq + (i - n_ctx) % per_seq), 0)

    row = lambda i: (i, 0)
    const = lambda i: (0, 0)
    in_specs = _x_specs(tm, len(x) == 2) + [
        pl.BlockSpec((1, 6, D_MODEL), lambda i: (_mod_row(i, tm), 0, 0)),
        pl.BlockSpec((1, D_MODEL), const),
        pl.BlockSpec((1, D_MODEL, n_in), lambda i: (layer_i, 0, 0)),
        pl.BlockSpec((Q_W, Q_W), const),
        pl.BlockSpec((KV_W, KV_W), const),
        pl.BlockSpec((1, Q_W), const),
        pl.BlockSpec((1, KV_W), const),
        pl.BlockSpec((tm, LANES), rope_idx),
        pl.BlockSpec((tm, LANES), rope_idx),
    ]
    qkv_specs = [pl.BlockSpec((tm, Q_W), row), pl.BlockSpec((tm, KV_W), row), pl.BlockSpec((tm, KV_W), row)]
    qkv_shapes = [jax.ShapeDtypeStruct((N_TOK, Q_W), BF16), jax.ShapeDtypeStruct((N_TOK, KV_W), BF16),
                  jax.ShapeDtypeStruct((N_TOK, KV_W), BF16)]
    args = [*x, mod_l, g, w, ones_q, ones_k, q_gain, k_gain, cos_tab, sin_tab]
    if even:
        in_specs.append(pl.BlockSpec((3, CONV_W), const))
        args.append(conv_w)
        out_specs = [pl.BlockSpec((tm, CONV_W), row)] + qkv_specs
        out_shape = [jax.ShapeDtypeStruct((N_TOK, CONV_W), BF16)] + qkv_shapes
    else:
        out_specs = qkv_specs + [pl.BlockSpec((tm, POOL_W), row)]
        out_shape = qkv_shapes + [jax.ShapeDtypeStruct((N_TOK, POOL_W), F32)]
    return pl.pallas_call(
        functools.partial(_in_proj_kernel, even, len(x)),
        grid=(N_TOK // tm,),
        in_specs=in_specs,
        out_specs=out_specs,
        out_shape=out_shape,
        scratch_shapes=[pltpu.VMEM((D_MODEL, n_in), BF16)],
        compiler_params=_cparams(("arbitrary",)),
        name="in_proj_even" if even else "in_proj_odd",
    )(*args)


def _dup_heads(t):
    lane = lax.broadcasted_iota(jnp.int32, t.shape, 1)
    swapped = pltpu.roll(t, HD, axis=1)
    low = lane < HD
    return jnp.where(low, t, swapped), jnp.where(low, swapped, t)


def _softmax_pv(scores, values, sink):
    m = scores[0].max(axis=-1, keepdims=True)
    for s in scores[1:]:
        m = jnp.maximum(m, s.max(axis=-1, keepdims=True))
    if sink is not None:
        m = jnp.maximum(m, sink)
    den = None
    acc = None
    for s, v in zip(scores, values):
        e = jnp.exp(s - m)
        d = e.sum(axis=-1, keepdims=True)
        a = _dot(e.astype(BF16), v)
        den = d if den is None else den + d
        acc = a if acc is None else acc + a
    if sink is not None:
        den = den + jnp.exp(sink - m)
    return acc / den


def _group_attention(q_ref, kv, keys, values, sink_ref, mask, stack):
    t = q_ref.shape[0]
    lane = lax.broadcasted_iota(jnp.int32, (t, LANES), 1)
    low = lane < HD
    zero = jnp.zeros((t, LANES), BF16)
    chunks = [q_ref[:, (kv * 2 + c) * LANES:(kv * 2 + c + 1) * LANES] for c in range(2)]
    if not stack:
        outs = []
        for h in range(GROUP):
            qm = jnp.where(low, chunks[h // 2], zero) if h % 2 == 0 else jnp.where(low, zero, chunks[h // 2])
            scores = [_dot_nt(qm, k) for k in keys]
            if mask is not None:
                scores = [s if m is None else jnp.where(m, s, NEG_INF) for s, m in zip(scores, mask)]
            sink = None if sink_ref is None else sink_ref[kv * GROUP + h]
            outs.append(_softmax_pv(scores, values, sink))
        return (jnp.where(low, outs[0], outs[1]).astype(BF16), jnp.where(low, outs[2], outs[3]).astype(BF16))
    qs = jnp.concatenate([jnp.where(low, chunks[0], zero), jnp.where(low, zero, chunks[0]),
                          jnp.where(low, chunks[1], zero), jnp.where(low, zero, chunks[1])], axis=0)
    scores = [_dot_nt(qs, k) for k in keys]
    if mask is not None:
        scores = [s if m is None else jnp.where(m, s, NEG_INF) for s, m in zip(scores, mask)]
    sink = None
    if sink_ref is not None:
        head = lax.broadcasted_iota(jnp.int32, (GROUP * t, 1), 0) // t
        sink = jnp.full((GROUP * t, 1), sink_ref[kv * GROUP], F32)
        for h in range(1, GROUP):
            sink = jnp.where(head == h, sink_ref[kv * GROUP + h], sink)
    out = _softmax_pv(scores, values, sink)
    return (jnp.where(low, out[0:t], out[t:2 * t]).astype(BF16),
            jnp.where(low, out[2 * t:3 * t], out[3 * t:4 * t]).astype(BF16))


def _ctx_attn_kernel(has_sink, first, *refs):
    if has_sink:
        sink_ref, refs = refs[0], refs[1:]
    else:
        sink_ref = None
    q_ref, k_ref, v_ref = refs[:3]
    o_ref, nk_ref, nv_ref = refs[-3:]
    lane = lax.broadcasted_iota(jnp.int32, (L_CTX, LANES), 1)
    low = lane < HD
    for s in range(CTX_SEQS):
        rows = pl.ds(s * L_CTX, L_CTX)
        k = k_ref[rows, :].astype(F32)
        v = v_ref[rows, :].astype(F32)
        k_sw = pltpu.roll(k, HD, axis=1)
        v_sw = pltpu.roll(v, HD, axis=1)
        nk_ref[s, 0, 0] = k[:, 0:HD]
        nk_ref[s, 0, 1] = k_sw[:, 0:HD]
        nv_ref[s, 0, 0] = v[:, 0:HD]
        nv_ref[s, 0, 1] = v_sw[:, 0:HD]
        if first:
            nk_ref[s, 1:] = jnp.zeros((DEPTH - 1, N_KV, L_CTX, HD), F32)
            nv_ref[s, 1:] = jnp.zeros((DEPTH - 1, N_KV, L_CTX, HD), F32)
        k2 = (jnp.where(low, k, k_sw).astype(BF16), jnp.where(low, k_sw, k).astype(BF16))
        v2 = (jnp.where(low, v, v_sw).astype(BF16), jnp.where(low, v_sw, v).astype(BF16))
        for kv in range(N_KV):
            o0, o1 = _group_attention(q_ref.at[rows, :], kv, [k2[kv]], [v2[kv]], sink_ref, None, True)
            o_ref[rows, (2 * kv) * LANES:(2 * kv + 1) * LANES] = o0
            o_ref[rows, (2 * kv + 1) * LANES:(2 * kv + 2) * LANES] = o1


def _ctx_attn(layer, q, k, v, sink, new_kv):
    has_sink = sink is not None
    first = new_kv is None
    row = lambda b: (b, 0)
    rows = CTX_SEQS * L_CTX
    in_specs = [pl.BlockSpec((rows, Q_W), row), pl.BlockSpec((rows, KV_W), row), pl.BlockSpec((rows, KV_W), row)]
    args = [q, k, v]
    if has_sink:
        in_specs = [pl.BlockSpec(memory_space=pltpu.SMEM)] + in_specs
        args = [sink] + args
    aliases = {}
    if first:
        kv_spec = pl.BlockSpec((CTX_SEQS, DEPTH, N_KV, L_CTX, HD), lambda b: (b, 0, 0, 0, 0))
    else:
        kv_spec = pl.BlockSpec((CTX_SEQS, 1, N_KV, L_CTX, HD), lambda b: (b, layer, 0, 0, 0))
        aliases = {len(args): 1, len(args) + 1: 2}
        in_specs = in_specs + [pl.BlockSpec(memory_space=pl.ANY)] * 2
        args = args + list(new_kv)
    kv_shape = jax.ShapeDtypeStruct((N_SEQ_CTX, DEPTH, N_KV, L_CTX, HD), F32)
    o, nk, nv = pl.pallas_call(
        functools.partial(_ctx_attn_kernel, has_sink, first),
        grid=(N_SEQ_CTX // CTX_SEQS,),
        in_specs=in_specs,
        out_specs=[pl.BlockSpec((rows, Q_W), row), kv_spec, kv_spec],
        out_shape=[jax.ShapeDtypeStruct((N_CTX_TOK, Q_W), BF16), kv_shape, kv_shape],
        input_output_aliases=aliases,
        compiler_params=_cparams(("arbitrary",)),
        name="ctx_attn_sink" if has_sink else "ctx_attn",
    )(*args)
    return o, (nk, nv)


def _lat_attn_kernel(windowed, *refs):
    if windowed:
        sink_ref, q_ref, k_ref, v_ref, ck_ref, cv_ref, dup_ref, o_ref, k2s, v2s, ck2s, cv2s = refs
    else:
        q_ref, k_ref, v_ref, ck_ref, cv_ref, dup_ref, o_ref, k2s, v2s, ck2s, cv2s = refs
        sink_ref = None
    j = pl.program_id(1)

    @pl.when(j == 0)
    def _():
        ka, kb = _dup_heads(k_ref[...].astype(F32))
        va, vb = _dup_heads(v_ref[...].astype(F32))
        k2s[0] = ka.astype(BF16)
        k2s[1] = kb.astype(BF16)
        v2s[0] = va.astype(BF16)
        v2s[1] = vb.astype(BF16)
        dup = dup_ref[...]
        for kv in range(N_KV):
            ck2s[kv] = _dot(ck_ref[0, 0, kv].astype(BF16), dup).astype(BF16)
            cv2s[kv] = _dot(cv_ref[0, 0, kv].astype(BF16), dup).astype(BF16)

    mask = None
    if windowed:
        tq = q_ref.shape[0]
        n_loc = tq + 2 * WINDOW
        start = pl.multiple_of(jnp.clip(j * tq - WINDOW, 0, L_LAT - n_loc), WINDOW)
        qpos = j * tq + lax.broadcasted_iota(jnp.int32, (tq, n_loc), 0)
        kpos = start + lax.broadcasted_iota(jnp.int32, (tq, n_loc), 1)
        mask = [None, jnp.abs(qpos - kpos) <= WINDOW]
    for kv in range(N_KV):
        if windowed:
            k_own = k2s[kv, pl.ds(start, n_loc), :]
            v_own = v2s[kv, pl.ds(start, n_loc), :]
        else:
            k_own = k2s[kv]
            v_own = v2s[kv]
        o0, o1 = _group_attention(q_ref, kv, [ck2s[kv], k_own], [cv2s[kv], v_own], sink_ref, mask, False)
        o_ref[:, (2 * kv) * LANES:(2 * kv + 1) * LANES] = o0
        o_ref[:, (2 * kv + 1) * LANES:(2 * kv + 2) * LANES] = o1


def _lat_attn(layer, q, k, v, cache_k, cache_v, dup, sink):
    windowed = sink is not None
    tq = TQ_WINDOW if windowed else TQ_FULL
    n_qt = L_LAT // tq
    ctx_tiles = N_CTX_TOK // tq
    ctx_seqs = N_CTX_TOK // L_LAT
    cache_spec = pl.BlockSpec((1, 1, N_KV, PAST, HD), lambda b, j: (b, layer, 0, 0, 0))
    in_specs = [
        pl.BlockSpec((tq, Q_W), lambda b, j: (ctx_tiles + b * n_qt + j, 0)),
        pl.BlockSpec((L_LAT, KV_W), lambda b, j: (ctx_seqs + b, 0)),
        pl.BlockSpec((L_LAT, KV_W), lambda b, j: (ctx_seqs + b, 0)),
        cache_spec,
        cache_spec,
        pl.BlockSpec((HD, LANES), lambda b, j: (0, 0)),
    ]
    args = [q, k, v, cache_k, cache_v, dup]
    if windowed:
        in_specs = [pl.BlockSpec(memory_space=pltpu.SMEM)] + in_specs
        args = [sink] + args
    return pl.pallas_call(
        functools.partial(_lat_attn_kernel, windowed),
        grid=(N_SEQ_LAT, n_qt),
        in_specs=in_specs,
        out_specs=pl.BlockSpec((tq, Q_W), lambda b, j: (b * n_qt + j, 0)),
        out_shape=jax.ShapeDtypeStruct((N_LAT_TOK, Q_W), BF16),
        scratch_shapes=[
            pltpu.VMEM((N_KV, L_LAT, LANES), BF16),
            pltpu.VMEM((N_KV, L_LAT, LANES), BF16),
            pltpu.VMEM((N_KV, PAST, LANES), BF16),
            pltpu.VMEM((N_KV, PAST, LANES), BF16),
        ],
        compiler_params=_cparams(("arbitrary", "arbitrary")),
        name="lat_attn_window" if windowed else "lat_attn",
    )(*args)


def _seq_pos(i, width):
    r = lax.broadcasted_iota(jnp.int32, (TM, width), 0)
    is_ctx = i < N_CTX_TOK // TM
    seq_len = jnp.where(is_ctx, L_CTX, L_LAT)
    return r & (seq_len - 1), seq_len


def _shift_rows(t, j, pos, seq_len):
    if j == 0:
        return t
    moved = pltpu.roll(t, (-j) % TM, axis=0)
    ok = (pos + j >= 0) & (pos + j < seq_len)
    return jnp.where(ok, moved, 0.0)


def _conv_mixer(bg, u, cw_ref, pos, seq_len):
    y = (_shift_rows(u, -1, pos, seq_len) * cw_ref[0:1, :] + u * cw_ref[1:2, :]
         + _shift_rows(u, 1, pos, seq_len) * cw_ref[2:3, :])
    return (bg * y).astype(BF16)


def _window_sum(x_seg, w):
    n = x_seg.shape[0] + 2 * POOL_PAD
    z = jnp.zeros((POOL_PAD, x_seg.shape[1]), F32)
    a = jnp.concatenate([z, x_seg, z], axis=0)
    a = a + pltpu.roll(a, 1, axis=0)
    half = 1
    while 2 * half < w:
        a = pltpu.roll(a, half, axis=0) + pltpu.roll(a, n - half, axis=0)
        half *= 2
    return a[POOL_PAD:POOL_PAD + x_seg.shape[0]]


def _pool_mixer(xd_ref, pw_ref, ps_ref, yd_ref, seq_len):
    t = lax.broadcasted_iota(jnp.int32, (seq_len, POOL_G), 0)
    for gi, w in enumerate(POOL_WINDOWS):
        cnt = (jnp.minimum(t + w // 2, seq_len) - jnp.maximum(t - w // 2, 0)).astype(F32)
        wg = pw_ref[gi].astype(BF16)
        lanes = slice(gi * POOL_G, (gi + 1) * POOL_G)
        for s in range(TM // seq_len):
            rows = slice(s * seq_len, (s + 1) * seq_len)
            xg = xd_ref[rows, lanes]
            d = _window_sum(xg, w) / cnt - xg
            yd_ref[rows, lanes] = (_dot(d.astype(BF16), wg) * ps_ref[:, lanes]).astype(BF16)


def _mix_out_kernel(even, nx, *refs):
    x_refs = refs[:nx]
    mod_ref, oc_ref, ol_ref, w_ref = refs[nx:nx + 4]
    rest = refs[nx + 4:]
    if even:
        ya_ref, out_ref, wbf = rest
    else:
        (xd_ref, pw_ref, ps_ref, g2_ref, rw_ref, rb_ref, tri_ref, upper_ref,
         out_ref, xl_ref, meta_ref, cnt_ref, wbf, yd_s) = rest
    i = pl.program_id(0)

    @pl.when(i == 0)
    def _():
        _cast_rows(w_ref.at[0], wbf, D_MODEL)

    o = jnp.where(i < N_CTX_TOK // TM, oc_ref[...], ol_ref[...])
    if even:
        y = _dot(ya_ref[...], wbf[0:CONV_W, :]) + _dot(o, wbf[CONV_W:, :])
    else:
        @pl.when(i < N_CTX_TOK // TM)
        def _():
            _pool_mixer(xd_ref, pw_ref, ps_ref, yd_s, L_CTX)

        @pl.when(i >= N_CTX_TOK // TM)
        def _():
            _pool_mixer(xd_ref, pw_ref, ps_ref, yd_s, L_LAT)

        y = _dot(o, wbf[0:Q_W, :]) + _dot(yd_s[...], wbf[Q_W:, :])
    out_ref[...] = _read_x(i, x_refs, TM) + mod_ref[0, 2:3, :] * y
    if not even:
        for part in range(TM // TR):
            rows = pl.ds(part * TR, TR)
            _route_tile(out_ref[rows, :], mod_ref, g2_ref, rw_ref, rb_ref, tri_ref, upper_ref,
                        xl_ref.at[pl.ds(part * LOCAL_ROWS, LOCAL_ROWS), :], meta_ref.at[rows, :],
                        cnt_ref.at[pl.ds(part * SUBLANES, SUBLANES), :])


def _mix_out(even, layer_i, x, mod_l, o_ctx, o_lat, w_out, *extra):
    n_ctx = N_CTX_TOK // TM
    n_lat = N_LAT_TOK // TM
    row = lambda i: (i, 0)
    const = lambda i: (0, 0)
    in_specs = _x_specs(TM, len(x) == 2) + [
        pl.BlockSpec((1, 6, D_MODEL), lambda i: (_mod_row(i, TM), 0, 0)),
        pl.BlockSpec((TM, Q_W), lambda i: (jnp.minimum(i, n_ctx - 1), 0)),
        pl.BlockSpec((TM, Q_W), lambda i: (jnp.clip(i - n_ctx, 0, n_lat - 1), 0)),
        pl.BlockSpec((1, D_MODEL, D_MODEL), lambda i: (layer_i, 0, 0)),
    ]
    if even:
        in_specs += [pl.BlockSpec((TM, CONV_W), row)]
    else:
        in_specs += [pl.BlockSpec((TM, POOL_W), row),
                     pl.BlockSpec((len(POOL_WINDOWS), POOL_G, POOL_G), lambda i: (0, 0, 0)),
                     pl.BlockSpec((1, POOL_W), const),
                     pl.BlockSpec((1, D_MODEL), const),
                     pl.BlockSpec((D_MODEL, LANES), const),
                     pl.BlockSpec((1, LANES), const),
                     pl.BlockSpec((TR, TR), const),
                     pl.BlockSpec((LANES, LANES), const)]
    out_specs = [pl.BlockSpec((TM, D_MODEL), row)]
    out_shape = [jax.ShapeDtypeStruct((N_TOK, D_MODEL), F32)]
    if not even:
        parts = TM // TR
        out_specs += [pl.BlockSpec((parts * LOCAL_ROWS, D_MODEL), row), pl.BlockSpec((TM, LANES), row),
                      pl.BlockSpec((parts * SUBLANES, LANES), row)]
        out_shape += [jax.ShapeDtypeStruct((N_TOK // TR * LOCAL_ROWS, D_MODEL), BF16),
                      jax.ShapeDtypeStruct((N_TOK, LANES), F32),
                      jax.ShapeDtypeStruct((N_TOK // TR * SUBLANES, LANES), F32)]
    return pl.pallas_call(
        functools.partial(_mix_out_kernel, even, len(x)),
        grid=(N_TOK // TM,),
        in_specs=in_specs,
        out_specs=out_specs,
        out_shape=out_shape,
        scratch_shapes=[pltpu.VMEM((D_MODEL, D_MODEL), BF16)] + ([] if even else [pltpu.VMEM((TM, POOL_W), BF16)]),
        compiler_params=_cparams(("arbitrary",)),
        name="mix_out_even" if even else "mix_out_odd",
    )(*x, mod_l, o_ctx, o_lat, w_out, *extra)


def _ffn_kernel(layer_i, x_ref, mod_ref, g_ref, w1_hbm, w3_hbm, w2_hbm, out_ref,
                w1s, w3s, w2s, st1, st3, st2, hs, hid, sem):
    i = pl.program_id(0)
    nf = D_FF // TF_FFN

    def chunk_copies(f, slot):
        cols = pl.ds(f * TF_FFN, TF_FFN)
        return (pltpu.make_async_copy(w1_hbm.at[layer_i, :, cols], st1.at[slot], sem.at[0, slot]),
                pltpu.make_async_copy(w3_hbm.at[layer_i, :, cols], st3.at[slot], sem.at[1, slot]),
                pltpu.make_async_copy(w2_hbm.at[layer_i, cols, :], st2.at[slot], sem.at[2, slot]))

    def tile(first):
        if first:
            for c in chunk_copies(0, 0):
                c.start()
        hs[...] = _normmod(x_ref[...], g_ref[...], mod_ref[0, 4:5, :], mod_ref[0, 3:4, :]).astype(BF16)
        for f in range(nf):
            slot = f % 2
            lo, hi = f * TF_FFN, (f + 1) * TF_FFN
            if first:
                if f + 1 < nf:
                    for c in chunk_copies(f + 1, 1 - slot):
                        c.start()
                for c in chunk_copies(f, slot):
                    c.wait()
                w1s[:, lo:hi] = st1[slot].astype(BF16)
                w3s[:, lo:hi] = st3[slot].astype(BF16)
                w2s[lo:hi, :] = st2[slot].astype(BF16)
            h = hs[...]
            hid[:, lo:hi] = (_silu(_dot(h, w1s[:, lo:hi])) * _dot(h, w3s[:, lo:hi])).astype(BF16)
        out_ref[...] = x_ref[...] + mod_ref[0, 5:6, :] * _dot(hid[...], w2s[...])

    @pl.when(i == 0)
    def _():
        tile(True)

    @pl.when(i > 0)
    def _():
        tile(False)


def _ffn(layer_i, x, mod_l, g, w1, w3, w2):
    row = lambda i: (i, 0)
    any_spec = pl.BlockSpec(memory_space=pl.ANY)
    return pl.pallas_call(
        functools.partial(_ffn_kernel, layer_i),
        grid=(N_TOK // TM,),
        in_specs=[
            pl.BlockSpec((TM, D_MODEL), row),
            pl.BlockSpec((1, 6, D_MODEL), lambda i: (_mod_row(i, TM), 0, 0)),
            pl.BlockSpec((1, D_MODEL), lambda i: (0, 0)),
            any_spec, any_spec, any_spec,
        ],
        out_specs=pl.BlockSpec((TM, D_MODEL), row),
        out_shape=jax.ShapeDtypeStruct((N_TOK, D_MODEL), F32),
        scratch_shapes=[
            pltpu.VMEM((D_MODEL, D_FF), BF16),
            pltpu.VMEM((D_MODEL, D_FF), BF16),
            pltpu.VMEM((D_FF, D_MODEL), BF16),
            pltpu.VMEM((2, D_MODEL, TF_FFN), F32),
            pltpu.VMEM((2, D_MODEL, TF_FFN), F32),
            pltpu.VMEM((2, TF_FFN, D_MODEL), F32),
            pltpu.VMEM((TM, D_MODEL), BF16),
            pltpu.VMEM((TM, D_FF), BF16),
            pltpu.SemaphoreType.DMA((3, 2)),
        ],
        compiler_params=pltpu.CompilerParams(dimension_semantics=("arbitrary",), vmem_limit_bytes=FFN_VMEM_LIMIT),
        name="ffn",
    )(x, mod_l, g, w1, w3, w2)


def _split_bf16(t):
    hi = t.astype(BF16)
    return hi, (t - hi.astype(F32)).astype(BF16)


def _lane_values(col_vals, ones_rows):
    base = 32.0
    assert LOCAL_ROWS <= base * 256
    q = jnp.floor(col_vals * (1.0 / base))
    r = col_vals - base * q
    t = base * _dot_nt(ones_rows, q.astype(BF16)) + _dot_nt(ones_rows, r.astype(BF16))
    return t[0:1, :]


def _route_tile(x, mod_ref, g_ref, rw_ref, rb_ref, tri_ref, upper_ref, xl_ref, meta_ref, cnt_ref):
    h = _normmod(x, g_ref[...], mod_ref[0, 4:5, :], mod_ref[0, 3:4, :])
    h_hi, h_lo = _split_bf16(h)
    w_hi, w_lo = _split_bf16(rw_ref[...])
    both = _dot(h_hi, jnp.concatenate([w_hi, w_lo], axis=1))
    logits = both[:, :LANES] + both[:, LANES:] + _dot(h_lo, w_hi) + rb_ref[...]
    lane = lax.broadcasted_iota(jnp.int32, logits.shape, 1).astype(F32)
    logits = jnp.where(lane < N_EXP, logits, NEG_INF)
    m1 = logits.max(axis=-1, keepdims=True)
    i1 = jnp.where(logits == m1, lane, float(LANES)).min(axis=-1, keepdims=True)
    rest = jnp.where(lane == i1, NEG_INF, logits)
    m2 = rest.max(axis=-1, keepdims=True)
    i2 = jnp.where(rest == m2, lane, float(LANES)).min(axis=-1, keepdims=True)
    e2 = jnp.exp(m2 - m1)
    den = 1.0 + e2
    g1 = 1.0 / den
    g2 = e2 / den

    oh_a = jnp.where(lane == i1, 1.0, 0.0)
    oh_b = jnp.where(lane == i2, 1.0, 0.0)
    tri = tri_ref[...]
    cnt_a = oh_a.sum(axis=0, keepdims=True)
    cnt_b = oh_b.sum(axis=0, keepdims=True)
    run16 = jnp.floor((cnt_a + cnt_b + (RUN_ALIGN - 1)) * (1.0 / RUN_ALIGN))
    run16_rows = jnp.broadcast_to(run16, (SUBLANES, LANES))
    start = RUN_ALIGN * _dot(run16_rows.astype(BF16), upper_ref[...])[0:1, :]
    before = _dot(tri, jnp.concatenate([oh_a, oh_b], axis=1).astype(BF16))
    row_a = oh_a * (start + before[:, :LANES])
    row_b = oh_b * (start + cnt_a + before[:, LANES:])

    ones_rows = jnp.ones((SUBLANES, LANES), BF16)
    tok_a = _lane_values(row_a, ones_rows)
    tok_b = _lane_values(row_b, ones_rows)
    sorted_row = lax.broadcasted_iota(jnp.int32, (LOCAL_ROWS, TR), 0).astype(F32)
    perm = jnp.where((sorted_row == tok_a) | (sorted_row == tok_b), 1.0, 0.0).astype(BF16)
    xl_ref[...] = _dot(perm, h_hi).astype(BF16)

    meta = jnp.zeros_like(logits)
    cols = (row_a.sum(axis=-1, keepdims=True), row_b.sum(axis=-1, keepdims=True), g1, g2)
    for k, col in enumerate(cols):
        meta = jnp.where(lane == k, col, meta)
    meta_ref[...] = meta
    cnt_ref[...] = RUN_ALIGN * run16_rows


def _moe_plan(counts):
    n_tiles = N_TOK // TR
    run = counts.reshape(n_tiles, SUBLANES, LANES)[:, 0, :N_EXP].astype(jnp.int32)
    per_expert = jnp.sum(run, axis=0)
    region = (per_expert + (TMS - 1)) // TMS * TMS
    ends = jnp.cumsum(region)
    offs = ends - region
    seg_end = jnp.cumsum(run, axis=0)
    seg_start = seg_end - run
    local_end = jnp.cumsum(run, axis=1)
    local_start = local_end - run
    n_chunks = local_end[:, N_EXP - 1] // RUN_ALIGN
    tile_start = jnp.arange(MOE_TILES, dtype=jnp.int32) * TMS
    tile_expert = jnp.sum((tile_start[:, None] >= ends[None, :]).astype(jnp.int32), axis=1)
    tile_expert = jnp.minimum(tile_expert, N_EXP - 1)
    n_used = (ends[N_EXP - 1] // TMS).reshape(1)
    experts = jnp.arange(N_EXP, dtype=jnp.int32)
    g_row = jnp.arange(MOE_ROWS // RUN_ALIGN, dtype=jnp.int32) * RUN_ALIGN
    g_exp = jnp.repeat(tile_expert, TMS // RUN_ALIGN)
    pick = g_exp[:, None] == experts[None, :]
    rel = g_row - jnp.sum(jnp.where(pick, offs[None, :], 0), axis=1)
    ends_of = jnp.sum(jnp.where(pick[:, None, :], seg_end[None, :, :], 0), axis=2)
    src_tile = jnp.sum((rel[:, None] >= ends_of).astype(jnp.int32), axis=1)
    valid = (src_tile < n_tiles) & (jnp.repeat(jnp.arange(MOE_TILES), TMS // RUN_ALIGN) < n_used[0])
    src_tile = jnp.minimum(src_tile, n_tiles - 1)
    sel = (src_tile[:, None, None] == jnp.arange(n_tiles)[None, :, None]) & pick[:, None, :]
    shift = jnp.sum(jnp.where(sel, (local_start - seg_start)[None, :, :], 0), axis=(1, 2))
    chunk_src = src_tile * LOCAL_ROWS + rel + shift
    n_valid = jnp.sum(valid.reshape(MOE_TILES, TMS // RUN_ALIGN).astype(jnp.int32), axis=1)
    return chunk_src, n_valid, n_chunks, tile_expert, n_used


def _chunk_copy(src_ref, src_row, dst_ref, dst_row, sem):
    return pltpu.make_async_copy(src_ref.at[pl.ds(src_row, RUN_ALIGN), :],
                                 dst_ref.at[pl.ds(dst_row, RUN_ALIGN), :], sem)


CHUNKS = TMS // RUN_ALIGN
LOCAL_CHUNKS = LOCAL_ROWS // RUN_ALIGN


def _expert_kernel(te_ref, nu_ref, src_ref, nv_ref, nc_ref, xl_hbm, w1_ref, w3_ref, w2_ref, yl_hbm,
                   xbuf, ybuf, zbuf, w1s, w3s, w2s, sem_in, sem_out, sem_zero):
    t = pl.program_id(0)
    n_used = nu_ref[0]
    slot = t & 1

    def for_chunks(n, body):
        @pl.when(n == CHUNKS)
        def _():
            for j in range(CHUNKS):
                body(j)

        @pl.when(n < CHUNKS)
        def _():
            def step(j, c):
                body(j)
                return c

            lax.fori_loop(0, n, step, 0)

    def gather(tile, s):
        n = nv_ref[tile]

        def issue(j):
            src = pl.multiple_of(src_ref[tile * CHUNKS + j], RUN_ALIGN)
            _chunk_copy(xl_hbm, src, xbuf.at[s], pl.multiple_of(j * RUN_ALIGN, RUN_ALIGN), sem_in.at[s]).start()

        def pad(j, c):
            xbuf[s, pl.ds(pl.multiple_of(j * RUN_ALIGN, RUN_ALIGN), RUN_ALIGN), :] = jnp.zeros(
                (RUN_ALIGN, D_MODEL), BF16)
            return c

        for_chunks(n, issue)
        lax.fori_loop(n, CHUNKS, pad, 0)

    def scatter(tile, s):
        def issue(j):
            dst = pl.multiple_of(src_ref[tile * CHUNKS + j], RUN_ALIGN)
            _chunk_copy(ybuf.at[s], pl.multiple_of(j * RUN_ALIGN, RUN_ALIGN), yl_hbm, dst, sem_out.at[s]).start()

        for_chunks(nv_ref[tile], issue)

    def drain(count, sem):
        @pl.when(count == CHUNKS)
        def _():
            pltpu.make_async_copy(xl_hbm.at[pl.ds(0, TMS), :], xbuf.at[0], sem).wait()

        @pl.when(count != CHUNKS)
        def _():
            def one(j, c):
                _chunk_copy(xl_hbm, 0, xbuf.at[0], 0, sem).wait()
                return c

            lax.fori_loop(0, count, one, 0)

    @pl.when(t == 0)
    def _():
        zbuf[...] = jnp.zeros_like(zbuf)
        total = jnp.int32(0)
        for tile in range(N_TOK // TR):
            n = nc_ref[tile]

            def clear(j, c, tile=tile):
                row = pl.multiple_of(tile * LOCAL_ROWS + j * RUN_ALIGN, RUN_ALIGN)
                _chunk_copy(zbuf, 0, yl_hbm, row, sem_zero).start()
                return c

            lax.fori_loop(n, LOCAL_CHUNKS, clear, 0)
            total = total + (LOCAL_CHUNKS - n)
        drain(total, sem_zero)
        gather(0, 0)

    @pl.when(t < n_used)
    def _():
        drain(nv_ref[t], sem_in.at[slot])

    @pl.when(t + 1 < n_used)
    def _():
        gather(t + 1, 1 - slot)

    new_expert = (t == 0) | (te_ref[t] != te_ref[jnp.maximum(t - 1, 0)])

    @pl.when(new_expert)
    def _():
        _cast_rows(w1_ref.at[0, 0], w1s, D_MODEL)
        _cast_rows(w3_ref.at[0, 0], w3s, D_MODEL)
        _cast_rows(w2_ref.at[0, 0], w2s, D_FF_E)

    @pl.when((t >= 2) & (t - 2 < n_used))
    def _():
        drain(nv_ref[jnp.maximum(t - 2, 0)], sem_out.at[slot])

    @pl.when(t < n_used)
    def _():
        x = xbuf[slot]
        hid = (_silu(_dot(x, w1s[...])) * _dot(x, w3s[...])).astype(BF16)
        ybuf[slot] = _dot(hid, w2s[...]).astype(BF16)
        scatter(t, slot)

    @pl.when(t == pl.num_programs(0) - 1)
    def _():
        @pl.when((t >= 1) & (t - 1 < n_used))
        def _():
            drain(nv_ref[jnp.maximum(t - 1, 0)], sem_out.at[1 - slot])

        @pl.when(t < n_used)
        def _():
            drain(nv_ref[t], sem_out.at[slot])


def _experts(layer_i, plan, xl, w1, w3, w2):
    chunk_src, n_valid, n_chunks, tile_expert, n_used = plan
    wsel = lambda t, te, nu, src, nv, nc: (layer_i, te[t], 0, 0)
    any_spec = pl.BlockSpec(memory_space=pl.ANY)
    return pl.pallas_call(
        _expert_kernel,
        grid_spec=pltpu.PrefetchScalarGridSpec(
            num_scalar_prefetch=5,
            grid=(MOE_TILES,),
            in_specs=[
                any_spec,
                pl.BlockSpec((1, 1, D_MODEL, D_FF_E), wsel),
                pl.BlockSpec((1, 1, D_MODEL, D_FF_E), wsel),
                pl.BlockSpec((1, 1, D_FF_E, D_MODEL), wsel),
            ],
            out_specs=any_spec,
            scratch_shapes=[
                pltpu.VMEM((2, TMS, D_MODEL), BF16),
                pltpu.VMEM((2, TMS, D_MODEL), BF16),
                pltpu.VMEM((RUN_ALIGN, D_MODEL), BF16),
                pltpu.VMEM((D_MODEL, D_FF_E), BF16),
                pltpu.VMEM((D_MODEL, D_FF_E), BF16),
                pltpu.VMEM((D_FF_E, D_MODEL), BF16),
                pltpu.SemaphoreType.DMA((2,)),
                pltpu.SemaphoreType.DMA((2,)),
                pltpu.SemaphoreType.DMA(()),
            ],
        ),
        out_shape=jax.ShapeDtypeStruct(xl.shape, BF16),
        compiler_params=_cparams(("arbitrary",)),
        name="moe_experts",
    )(tile_expert, n_used, chunk_src, n_valid, n_chunks, xl, w1, w3, w2)


def _combine_kernel(x_ref, mod_ref, meta_ref, yl_ref, *out_refs):
    sorted_row = lax.broadcasted_iota(jnp.int32, (TR, LOCAL_ROWS), 1).astype(F32)
    y = yl_ref[...]
    pick_a = jnp.where(sorted_row == meta_ref[:, 0:1], 1.0, 0.0).astype(BF16)
    pick_b = jnp.where(sorted_row == meta_ref[:, 1:2], 1.0, 0.0).astype(BF16)
    f = meta_ref[:, 2:3] * _dot(pick_a, y) + meta_ref[:, 3:4] * _dot(pick_b, y)
    _write_x(pl.program_id(0), out_refs, TR, x_ref[...] + mod_ref[0, 5:6, :] * f)


def _combine(x, mod_l, meta, yl, split_out):
    row = lambda i: (i, 0)
    return pl.pallas_call(
        _combine_kernel,
        grid=(N_TOK // TR,),
        in_specs=[
            pl.BlockSpec((TR, D_MODEL), row),
            pl.BlockSpec((1, 6, D_MODEL), lambda i: (_mod_row(i, TR), 0, 0)),
            pl.BlockSpec((TR, LANES), row),
            pl.BlockSpec((LOCAL_ROWS, D_MODEL), row),
        ],
        out_specs=_x_specs(TR, split_out),
        out_shape=_x_shapes(split_out),
        compiler_params=_cparams(("arbitrary",)),
        name="moe_combine",
    )(x, mod_l, meta, yl)


def _moe(layer_i, x, routed, mod_l, w1, w3, w2, split_out):
    xl, meta, counts = routed
    yl = _experts(layer_i, _moe_plan(counts), xl, w1, w3, w2)
    return _combine(x, mod_l, meta, yl, split_out)


def _rope_tables():
    n_rows = L_LAT // GRID_W
    rows = np.repeat(np.arange(n_rows, dtype=np.float32), GRID_W)
    cols = np.tile(np.arange(GRID_W, dtype=np.float32), n_rows)
    quarter = HD // 4
    inv = (np.float32(ROPE_THETA) ** (-np.arange(quarter, dtype=np.float32) / np.float32(quarter))).astype(np.float32)
    ang_r = (rows[:, None] * inv).astype(np.float32)
    ang_c = (cols[:, None] * inv).astype(np.float32)
    cos = np.concatenate([np.cos(ang_r)] * 2 + [np.cos(ang_c)] * 2, axis=1)
    sin = np.concatenate([-np.sin(ang_r), np.sin(ang_r), -np.sin(ang_c), np.sin(ang_c)], axis=1)
    cos = np.concatenate([cos, cos], axis=1)
    sin = np.concatenate([sin, sin], axis=1)
    cos = np.concatenate([np.ones((L_LAT, LANES), np.float32), cos], axis=0)
    sin = np.concatenate([np.zeros((L_LAT, LANES), np.float32), sin], axis=0)
    return jnp.asarray(cos, F32), jnp.asarray(sin, F32)


def _block_ones(width):
    r = np.arange(width) // HD
    return jnp.asarray(r[:, None] == r[None, :], BF16)


def kernel(x_prompt, x_sample, cache_k, cache_v, c, c_ctx, norm1, norm2, w_mod, b_mod, ev_w_in, ev_conv, ev_q_norm, ev_k_norm, ev_w_out, od_w_in, od_q_norm, od_k_norm, od_sink, od_pool_w, od_pool_scale, od_w_out, ffn_w1, ffn_w3, ffn_w2, moe_router, moe_router_b, moe_w1, moe_w3, moe_w2):
    x = (x_prompt.reshape(N_CTX_TOK, D_MODEL), x_sample.reshape(N_LAT_TOK, D_MODEL))
    cond = jnp.concatenate([c_ctx[None, :], c, jnp.zeros((MOD_ROWS - 1 - N_SEQ_LAT, D_MODEL), F32)], axis=0)
    mod = _modulation(cond, w_mod, b_mod).reshape(DEPTH, MOD_ROWS, 6, D_MODEL)

    cos_tab, sin_tab = _rope_tables()
    ones_q = _block_ones(Q_W)
    ones_k = _block_ones(KV_W)
    dup = jnp.asarray(np.concatenate([np.eye(HD), np.eye(HD)], axis=1), BF16)
    tri = jnp.asarray(np.arange(TR)[:, None] > np.arange(TR)[None, :], BF16)
    upper = jnp.asarray(np.arange(LANES)[:, None] < np.arange(LANES)[None, :], BF16)

    new_kv = None
    for l in range(DEPTH):
        i = l // 2
        even = l % 2 == 0
        mod_l = mod[l]
        g1 = norm1[l][None, :]
        g2 = norm2[l][None, :]
        if even:
            q_gain, k_gain = ev_q_norm[i], ev_k_norm[i]
            w_in, w_out = ev_w_in, ev_w_out
        else:
            q_gain, k_gain = od_q_norm[i], od_k_norm[i]
            w_in, w_out = od_w_in, od_w_out
        q_gain = jnp.tile(q_gain, N_Q)[None, :]
        k_gain = jnp.tile(k_gain, N_KV)[None, :]
        outs = _in_proj(even, i, x, mod_l, g1, w_in, ones_q, ones_k, q_gain, k_gain, cos_tab, sin_tab,
                        ev_conv[i] if even else None)
        if even:
            ya, q, k, v = outs
            sink = None
        else:
            q, k, v, xd = outs
            sink = od_sink[i]
        o_ctx, new_kv = _ctx_attn(l, q, k, v, sink, new_kv)
        o_lat = _lat_attn(l, q, k, v, cache_k, cache_v, dup, sink)
        if even:
            (x1,) = _mix_out(True, i, x, mod_l, o_ctx, o_lat, w_out, ya)
            x = (_ffn(i, x1, mod_l, g2, ffn_w1, ffn_w3, ffn_w2),)
        else:
            rw = jnp.pad(moe_router[i], ((0, 0), (0, LANES - N_EXP)))
            rb = jnp.pad(moe_router_b[i], (0, LANES - N_EXP))[None, :]
            x1, *routed = _mix_out(False, i, x, mod_l, o_ctx, o_lat, w_out, xd, od_pool_w[i],
                                   od_pool_scale[i][None, :], g2, rw, rb, tri, upper)
            x = tuple(_moe(i, x1, routed, mod_l, moe_w1, moe_w3, moe_w2, l == DEPTH - 1))

    y_prompt = x[0].reshape(N_SEQ_CTX, L_CTX, D_MODEL)
    y_sample = x[1].reshape(N_SEQ_LAT, L_LAT, D_MODEL)
    return (y_prompt, y_sample, new_kv[0], new_kv[1])
```

```python
import functools

import jax
import jax.numpy as jnp
import numpy as np
from jax import lax
from jax.experimental import pallas as pl
from jax.experimental.pallas import tpu as pltpu

F32 = jnp.float32
BF16 = jnp.bfloat16

D_MODEL = 1024
N_SEQ_CTX = 32
L_CTX = 256
N_SEQ_LAT = 4
L_LAT = 1024
DEPTH = 4
PAST = 512
GRID_W = 64
HD = 64
N_Q = 8
N_KV = 2
GROUP = N_Q // N_KV
Q_W = N_Q * HD
KV_W = N_KV * HD
CONV_W = 512
POOL_W = 512
POOL_WINDOWS = (2, 4, 8, 16)
POOL_G = 128
POOL_PAD = 16
EVEN_IN = 3 * CONV_W + Q_W + 2 * KV_W
ODD_IN = Q_W + 2 * KV_W + POOL_W
WINDOW = 128
D_FF = 2816
N_EXP = 8
TOP_K = 2
D_FF_E = 1024
ROPE_THETA = 10000.0
EPS = 1e-6

N_CTX_TOK = N_SEQ_CTX * L_CTX
N_LAT_TOK = N_SEQ_LAT * L_LAT
N_TOK = N_CTX_TOK + N_LAT_TOK
MOD_ROWS = 16

LANES = 128
SUBLANES = 8
ROPE_Q = HD // 4
VMEM_LIMIT = 56 * 1024 * 1024
FFN_VMEM_LIMIT = 62 * 1024 * 1024

TM_IN = 1024
TM = 1024
TQ_FULL = 512
TQ_WINDOW = 256
CTX_SEQS = 4
TF_FFN = 256
TR = 512
RUN_ALIGN = 16
LOCAL_ROWS = TOP_K * TR + N_EXP * RUN_ALIGN
TMS = 512
MOE_TILES = -(-(TOP_K * N_TOK + (N_TOK // TR) * N_EXP * (RUN_ALIGN - 1) + N_EXP * (TMS - 1)) // TMS)
MOE_ROWS = MOE_TILES * TMS
NEG_INF = float("-inf")


def _cparams(sem):
    return pltpu.CompilerParams(dimension_semantics=sem, vmem_limit_bytes=VMEM_LIMIT)


def _mod_row(i, tm):
    n_ctx = N_CTX_TOK // tm
    return jnp.where(i < n_ctx, 0, 1 + (i - n_ctx) // (L_LAT // tm))


def _x_specs(tm, split):
    if not split:
        return [pl.BlockSpec((tm, D_MODEL), lambda i, *_: (i, 0))]
    n_ctx = N_CTX_TOK // tm
    n_lat = N_LAT_TOK // tm
    return [pl.BlockSpec((tm, D_MODEL), lambda i, *_: (jnp.minimum(i, n_ctx - 1), 0)),
            pl.BlockSpec((tm, D_MODEL), lambda i, *_: (jnp.clip(i - n_ctx, 0, n_lat - 1), 0))]


def _x_shapes(split):
    if not split:
        return [jax.ShapeDtypeStruct((N_TOK, D_MODEL), F32)]
    return [jax.ShapeDtypeStruct((N_CTX_TOK, D_MODEL), F32), jax.ShapeDtypeStruct((N_LAT_TOK, D_MODEL), F32)]


def _read_x(i, x_refs, tm):
    if len(x_refs) == 1:
        return x_refs[0][...]
    return jnp.where(i < N_CTX_TOK // tm, x_refs[0][...], x_refs[1][...])


def _write_x(i, o_refs, tm, val):
    if len(o_refs) == 1:
        o_refs[0][...] = val
        return

    @pl.when(i < N_CTX_TOK // tm)
    def _():
        o_refs[0][...] = val

    @pl.when(i >= N_CTX_TOK // tm)
    def _():
        o_refs[1][...] = val


def _normmod(x, g, scale, shift):
    ms = jnp.mean(x * x, axis=-1, keepdims=True)
    y = x * lax.rsqrt(ms + EPS) * g
    return y * (1.0 + scale) + shift


def _silu(x):
    return x * jax.nn.sigmoid(x)


def _dot(a, b):
    return jnp.dot(a, b, preferred_element_type=F32)


def _dot_nt(a, b):
    return lax.dot_general(a, b, (((1,), (1,)), ((), ())), preferred_element_type=F32)


def _cast_rows(src_ref, dst_ref, rows, chunk=256):
    for r in range(0, rows, chunk):
        dst_ref[r:r + chunk, :] = src_ref[r:r + chunk, :].astype(dst_ref.dtype)


def _mod_kernel(c_ref, w_ref, b_ref, o_ref):
    s = _silu(c_ref[...]).astype(BF16)
    o_ref[0] = _dot(s, w_ref[0].astype(BF16)) + b_ref[0]


def _modulation(cond, w_mod, b_mod):
    tn = 1536
    return pl.pallas_call(
        _mod_kernel,
        grid=(DEPTH, 6 * D_MODEL // tn),
        in_specs=[
            pl.BlockSpec((MOD_ROWS, D_MODEL), lambda l, j: (0, 0)),
            pl.BlockSpec((1, D_MODEL, tn), lambda l, j: (l, 0, j)),
            pl.BlockSpec((1, 1, tn), lambda l, j: (l, 0, j)),
        ],
        out_specs=pl.BlockSpec((1, MOD_ROWS, tn), lambda l, j: (l, 0, j)),
        out_shape=jax.ShapeDtypeStruct((DEPTH, MOD_ROWS, 6 * D_MODEL), F32),
        compiler_params=_cparams(("arbitrary", "arbitrary")),
        name="modulation",
    )(cond, w_mod, b_mod.reshape(DEPTH, 1, 6 * D_MODEL))


def _head_rms(t, ones_bd, gain):
    ssq = _dot((t * t).astype(BF16), ones_bd)
    return t * lax.rsqrt(ssq * (1.0 / HD) + EPS) * gain


def _rope(t, cos, sin_signed):
    lane = lax.broadcasted_iota(jnp.int32, (t.shape[0], LANES), 1)
    first = (lane & (2 * ROPE_Q - 1)) < ROPE_Q
    outs = []
    for c in range(t.shape[1] // LANES):
        tc = t[:, c * LANES:(c + 1) * LANES]
        nxt = pltpu.roll(tc, LANES - ROPE_Q, axis=1)
        prv = pltpu.roll(tc, ROPE_Q, axis=1)
        outs.append(tc * cos + jnp.where(first, nxt, prv) * sin_signed)
    return outs[0] if len(outs) == 1 else jnp.concatenate(outs, axis=1)


def _in_proj_kernel(even, nx, *refs):
    x_refs = refs[:nx]
    mod_ref, g_ref, w_ref, onesq_ref, onesk_ref, qg_ref, kg_ref, cos_ref, sin_ref = refs[nx:nx + 9]
    rest = refs[nx + 9:]
    if even:
        cw_ref, ya_ref, q_ref, k_ref, v_ref, wbf = rest
        q0 = 3 * CONV_W
    else:
        q_ref, k_ref, v_ref, xd_ref, wbf = rest
        q0 = 0
    k0 = q0 + Q_W
    v0 = k0 + KV_W

    @pl.when(pl.program_id(0) == 0)
    def _():
        _cast_rows(w_ref.at[0], wbf, D_MODEL)

    x = _read_x(pl.program_id(0), x_refs, TM_IN)
    h = _normmod(x, g_ref[...], mod_ref[0, 1:2, :], mod_ref[0, 0:1, :]).astype(BF16)
    cos = cos_ref[...]
    sin = sin_ref[...]

    q = _dot(h, wbf[:, q0:q0 + Q_W])
    q = _rope(_head_rms(q, onesq_ref[...], qg_ref[...]), cos, sin) * (HD ** -0.5)
    q_ref[...] = q.astype(BF16)

    kv = _dot(h, wbf[:, k0:k0 + 2 * KV_W])
    k = _rope(_head_rms(kv[:, :KV_W], onesk_ref[...], kg_ref[...]), cos, sin)
    k_ref[...] = k.astype(BF16)
    v_ref[...] = kv[:, KV_W:].astype(BF16)

    if even:
        pos, seq_len = _seq_pos(pl.program_id(0), 1)
        bg = _dot(h, wbf[:, 0:CONV_W])
        u = _dot(h, wbf[:, CONV_W:2 * CONV_W]) * _dot(h, wbf[:, 2 * CONV_W:3 * CONV_W])
        ya_ref[...] = _conv_mixer(bg, u, cw_ref, pos, seq_len)
    else:
        xd_ref[...] = _dot(h, wbf[:, v0 + KV_W:v0 + KV_W + POOL_W])


def _in_proj(even, layer_i, x, mod_l, g, w, ones_q, ones_k, q_gain, k_gain, cos_tab, sin_tab, conv_w=None):
    assert TM_IN == TM
    tm = TM_IN
    n_in = EVEN_IN if even else ODD_IN
    n_ctx = N_CTX_TOK // tm
    per_seq = L_LAT // tm

    def rope_idx(i):
        return (jnp.where(i < n_ctx, 0, per_seq + (i - n_ctx) % per_seq), 0)

    row = lambda i: (i, 0)
    const = lambda i: (0, 0)
    in_specs = _x_specs(tm, len(x) == 2) + [
        pl.BlockSpec((1, 6, D_MODEL), lambda i: (_mod_row(i, tm), 0, 0)),
        pl.BlockSpec((1, D_MODEL), const),
        pl.BlockSpec((1, D_MODEL, n_in), lambda i: (layer_i, 0, 0)),
        pl.BlockSpec((Q_W, Q_W), const),
        pl.BlockSpec((KV_W, KV_W), const),
        pl.BlockSpec((1, Q_W), const),
        pl.BlockSpec((1, KV_W), const),
        pl.BlockSpec((tm, LANES), rope_idx),
        pl.BlockSpec((tm, LANES), rope_idx),
    ]
    qkv_specs = [pl.BlockSpec((tm, Q_W), row), pl.BlockSpec((tm, KV_W), row), pl.BlockSpec((tm, KV_W), row)]
    qkv_shapes = [jax.ShapeDtypeStruct((N_TOK, Q_W), BF16), jax.ShapeDtypeStruct((N_TOK, KV_W), BF16),
                  jax.ShapeDtypeStruct((N_TOK, KV_W), BF16)]
    args = [*x, mod_l, g, w, ones_q, ones_k, q_gain, k_gain, cos_tab, sin_tab]
    if even:
        in_specs.append(pl.BlockSpec((3, CONV_W), const))
        args.append(conv_w)
        out_specs = [pl.BlockSpec((tm, CONV_W), row)] + qkv_specs
        out_shape = [jax.ShapeDtypeStruct((N_TOK, CONV_W), BF16)] + qkv_shapes
    else:
        out_specs = qkv_specs + [pl.BlockSpec((tm, POOL_W), row)]
        out_shape = qkv_shapes + [jax.ShapeDtypeStruct((N_TOK, POOL_W), F32)]
    return pl.pallas_call(
        functools.partial(_in_proj_kernel, even, len(x)),
        grid=(N_TOK // tm,),
        in_specs=in_specs,
        out_specs=out_specs,
        out_shape=out_shape,
        scratch_shapes=[pltpu.VMEM((D_MODEL, n_in), BF16)],
        compiler_params=_cparams(("arbitrary",)),
        name="in_proj_even" if even else "in_proj_odd",
    )(*args)


def _dup_heads(t):
    lane = lax.broadcasted_iota(jnp.int32, t.shape, 1)
    swapped = pltpu.roll(t, HD, axis=1)
    low = lane < HD
    return jnp.where(low, t, swapped), jnp.where(low, swapped, t)


def _softmax_pv(scores, values, sink):
    m = scores[0].max(axis=-1, keepdims=True)
    for s in scores[1:]:
        m = jnp.maximum(m, s.max(axis=-1, keepdims=True))
    if sink is not None:
        m = jnp.maximum(m, sink)
    den = None
    acc = None
    for s, v in zip(scores, values):
        e = jnp.exp(s - m)
        d = e.sum(axis=-1, keepdims=True)
        a = _dot(e.astype(BF16), v)
        den = d if den is None else den + d
        acc = a if acc is None else acc + a
    if sink is not None:
        den = den + jnp.exp(sink - m)
    return acc / den


def _group_attention(q_ref, kv, keys, values, sink_ref, mask, stack):
    t = q_ref.shape[0]
    lane = lax.broadcasted_iota(jnp.int32, (t, LANES), 1)
    low = lane < HD
    zero = jnp.zeros((t, LANES), BF16)
    chunks = [q_ref[:, (kv * 2 + c) * LANES:(kv * 2 + c + 1) * LANES] for c in range(2)]
    if not stack:
        outs = []
        for h in range(GROUP):
            qm = jnp.where(low, chunks[h // 2], zero) if h % 2 == 0 else jnp.where(low, zero, chunks[h // 2])
            scores = [_dot_nt(qm, k) for k in keys]
            if mask is not None:
                scores = [s if m is None else jnp.where(m, s, NEG_INF) for s, m in zip(scores, mask)]
            sink = None if sink_ref is None else sink_ref[kv * GROUP + h]
            outs.append(_softmax_pv(scores, values, sink))
        return (jnp.where(low, outs[0], outs[1]).astype(BF16), jnp.where(low, outs[2], outs[3]).astype(BF16))
    qs = jnp.concatenate([jnp.where(low, chunks[0], zero), jnp.where(low, zero, chunks[0]),
                          jnp.where(low, chunks[1], zero), jnp.where(low, zero, chunks[1])], axis=0)
    scores = [_dot_nt(qs, k) for k in keys]
    if mask is not None:
        scores = [s if m is None else jnp.where(m, s, NEG_INF) for s, m in zip(scores, mask)]
    sink = None
    if sink_ref is not None:
        head = lax.broadcasted_iota(jnp.int32, (GROUP * t, 1), 0) // t
        sink = jnp.full((GROUP * t, 1), sink_ref[kv * GROUP], F32)
        for h in range(1, GROUP):
            sink = jnp.where(head == h, sink_ref[kv * GROUP + h], sink)
    out = _softmax_pv(scores, values, sink)
    return (jnp.where(low, out[0:t], out[t:2 * t]).astype(BF16),
            jnp.where(low, out[2 * t:3 * t], out[3 * t:4 * t]).astype(BF16))


def _ctx_attn_kernel(has_sink, first, *refs):
    if has_sink:
        sink_ref, refs = refs[0], refs[1:]
    else:
        sink_ref = None
    q_ref, k_ref, v_ref = refs[:3]
    o_ref, nk_ref, nv_ref = refs[-3:]
    lane = lax.broadcasted_iota(jnp.int32, (L_CTX, LANES), 1)
    low = lane < HD
    for s in range(CTX_SEQS):
        rows = pl.ds(s * L_CTX, L_CTX)
        k = k_ref[rows, :].astype(F32)
        v = v_ref[rows, :].astype(F32)
        k_sw = pltpu.roll(k, HD, axis=1)
        v_sw = pltpu.roll(v, HD, axis=1)
        nk_ref[s, 0, 0] = k[:, 0:HD]
        nk_ref[s, 0, 1] = k_sw[:, 0:HD]
        nv_ref[s, 0, 0] = v[:, 0:HD]
        nv_ref[s, 0, 1] = v_sw[:, 0:HD]
        if first:
            nk_ref[s, 1:] = jnp.zeros((DEPTH - 1, N_KV, L_CTX, HD), F32)
            nv_ref[s, 1:] = jnp.zeros((DEPTH - 1, N_KV, L_CTX, HD), F32)
        k2 = (jnp.where(low, k, k_sw).astype(BF16), jnp.where(low, k_sw, k).astype(BF16))
        v2 = (jnp.where(low, v, v_sw).astype(BF16), jnp.where(low, v_sw, v).astype(BF16))
        for kv in range(N_KV):
            o0, o1 = _group_attention(q_ref.at[rows, :], kv, [k2[kv]], [v2[kv]], sink_ref, None, True)
            o_ref[rows, (2 * kv) * LANES:(2 * kv + 1) * LANES] = o0
            o_ref[rows, (2 * kv + 1) * LANES:(2 * kv + 2) * LANES] = o1


def _ctx_attn(layer, q, k, v, sink, new_kv):
    has_sink = sink is not None
    first = new_kv is None
    row = lambda b: (b, 0)
    rows = CTX_SEQS * L_CTX
    in_specs = [pl.BlockSpec((rows, Q_W), row), pl.BlockSpec((rows, KV_W), row), pl.BlockSpec((rows, KV_W), row)]
    args = [q, k, v]
    if has_sink:
        in_specs = [pl.BlockSpec(memory_space=pltpu.SMEM)] + in_specs
        args = [sink] + args
    aliases = {}
    if first:
        kv_spec = pl.BlockSpec((CTX_SEQS, DEPTH, N_KV, L_CTX, HD), lambda b: (b, 0, 0, 0, 0))
    else:
        kv_spec = pl.BlockSpec((CTX_SEQS, 1, N_KV, L_CTX, HD), lambda b: (b, layer, 0, 0, 0))
        aliases = {len(args): 1, len(args) + 1: 2}
        in_specs = in_specs + [pl.BlockSpec(memory_space=pl.ANY)] * 2
        args = args + list(new_kv)
    kv_shape = jax.ShapeDtypeStruct((N_SEQ_CTX, DEPTH, N_KV, L_CTX, HD), F32)
    o, nk, nv = pl.pallas_call(
        functools.partial(_ctx_attn_kernel, has_sink, first),
        grid=(N_SEQ_CTX // CTX_SEQS,),
        in_specs=in_specs,
        out_specs=[pl.BlockSpec((rows, Q_W), row), kv_spec, kv_spec],
        out_shape=[jax.ShapeDtypeStruct((N_CTX_TOK, Q_W), BF16), kv_shape, kv_shape],
        input_output_aliases=aliases,
        compiler_params=_cparams(("arbitrary",)),
        name="ctx_attn_sink" if has_sink else "ctx_attn",
    )(*args)
    return o, (nk, nv)


def _lat_attn_kernel(windowed, *refs):
    if windowed:
        sink_ref, q_ref, k_ref, v_ref, ck_ref, cv_ref, dup_ref, o_ref, k2s, v2s, ck2s, cv2s = refs
    else:
        q_ref, k_ref, v_ref, ck_ref, cv_ref, dup_ref, o_ref, k2s, v2s, ck2s, cv2s = refs
        sink_ref = None
    j = pl.program_id(1)

    @pl.when(j == 0)
    def _():
        ka, kb = _dup_heads(k_ref[...].astype(F32))
        va, vb = _dup_heads(v_ref[...].astype(F32))
        k2s[0] = ka.astype(BF16)
        k2s[1] = kb.astype(BF16)
        v2s[0] = va.astype(BF16)
        v2s[1] = vb.astype(BF16)
        dup = dup_ref[...]
        for kv in range(N_KV):
            ck2s[kv] = _dot(ck_ref[0, 0, kv].astype(BF16), dup).astype(BF16)
            cv2s[kv] = _dot(cv_ref[0, 0, kv].astype(BF16), dup).astype(BF16)

    mask = None
    if windowed:
        tq = q_ref.shape[0]
        n_loc = tq + 2 * WINDOW
        start = pl.multiple_of(jnp.clip(j * tq - WINDOW, 0, L_LAT - n_loc), WINDOW)
        qpos = j * tq + lax.broadcasted_iota(jnp.int32, (tq, n_loc), 0)
        kpos = start + lax.broadcasted_iota(jnp.int32, (tq, n_loc), 1)
        mask = [None, jnp.abs(qpos - kpos) <= WINDOW]
    for kv in range(N_KV):
        if windowed:
            k_own = k2s[kv, pl.ds(start, n_loc), :]
            v_own = v2s[kv, pl.ds(start, n_loc), :]
        else:
            k_own = k2s[kv]
            v_own = v2s[kv]
        o0, o1 = _group_attention(q_ref, kv, [ck2s[kv], k_own], [cv2s[kv], v_own], sink_ref, mask, False)
        o_ref[:, (2 * kv) * LANES:(2 * kv + 1) * LANES] = o0
        o_ref[:, (2 * kv + 1) * LANES:(2 * kv + 2) * LANES] = o1


def _lat_attn(layer, q, k, v, cache_k, cache_v, dup, sink):
    windowed = sink is not None
    tq = TQ_WINDOW if windowed else TQ_FULL
    n_qt = L_LAT // tq
    ctx_tiles = N_CTX_TOK // tq
    ctx_seqs = N_CTX_TOK // L_LAT
    cache_spec = pl.BlockSpec((1, 1, N_KV, PAST, HD), lambda b, j: (b, layer, 0, 0, 0))
    in_specs = [
        pl.BlockSpec((tq, Q_W), lambda b, j: (ctx_tiles + b * n_qt + j, 0)),
        pl.BlockSpec((L_LAT, KV_W), lambda b, j: (ctx_seqs + b, 0)),
        pl.BlockSpec((L_LAT, KV_W), lambda b, j: (ctx_seqs + b, 0)),
        cache_spec,
        cache_spec,
        pl.BlockSpec((HD, LANES), lambda b, j: (0, 0)),
    ]
    args = [q, k, v, cache_k, cache_v, dup]
    if windowed:
        in_specs = [pl.BlockSpec(memory_space=pltpu.SMEM)] + in_specs
        args = [sink] + args
    return pl.pallas_call(
        functools.partial(_lat_attn_kernel, windowed),
        grid=(N_SEQ_LAT, n_qt),
        in_specs=in_specs,
        out_specs=pl.BlockSpec((tq, Q_W), lambda b, j: (b * n_qt + j, 0)),
        out_shape=jax.ShapeDtypeStruct((N_LAT_TOK, Q_W), BF16),
        scratch_shapes=[
            pltpu.VMEM((N_KV, L_LAT, LANES), BF16),
            pltpu.VMEM((N_KV, L_LAT, LANES), BF16),
            pltpu.VMEM((N_KV, PAST, LANES), BF16),
            pltpu.VMEM((N_KV, PAST, LANES), BF16),
        ],
        compiler_params=_cparams(("arbitrary", "arbitrary")),
        name="lat_attn_window" if windowed else "lat_attn",
    )(*args)


def _seq_pos(i, width):
    r = lax.broadcasted_iota(jnp.int32, (TM, width), 0)
    is_ctx = i < N_CTX_TOK // TM
    seq_len = jnp.where(is_ctx, L_CTX, L_LAT)
    return r & (seq_len - 1), seq_len


def _shift_rows(t, j, pos, seq_len):
    if j == 0:
        return t
    moved = pltpu.roll(t, (-j) % TM, axis=0)
    ok = (pos + j >= 0) & (pos + j < seq_len)
    return jnp.where(ok, moved, 0.0)


def _conv_mixer(bg, u, cw_ref, pos, seq_len):
    y = (_shift_rows(u, -1, pos, seq_len) * cw_ref[0:1, :] + u * cw_ref[1:2, :]
         + _shift_rows(u, 1, pos, seq_len) * cw_ref[2:3, :])
    return (bg * y).astype(BF16)


def _window_sum(x_seg, w):
    n = x_seg.shape[0] + 2 * POOL_PAD
    z = jnp.zeros((POOL_PAD, x_seg.shape[1]), F32)
    a = jnp.concatenate([z, x_seg, z], axis=0)
    a = a + pltpu.roll(a, 1, axis=0)
    half = 1
    while 2 * half < w:
        a = pltpu.roll(a, half, axis=0) + pltpu.roll(a, n - half, axis=0)
        half *= 2
    return a[POOL_PAD:POOL_PAD + x_seg.shape[0]]


def _pool_mixer(xd_ref, pw_ref, ps_ref, yd_ref, seq_len):
    t = lax.broadcasted_iota(jnp.int32, (seq_len, POOL_G), 0)
    for gi, w in enumerate(POOL_WINDOWS):
        cnt = (jnp.minimum(t + w // 2, seq_len) - jnp.maximum(t - w // 2, 0)).astype(F32)
        wg = pw_ref[gi].astype(BF16)
        lanes = slice(gi * POOL_G, (gi + 1) * POOL_G)
        for s in range(TM // seq_len):
            rows = slice(s * seq_len, (s + 1) * seq_len)
            xg = xd_ref[rows, lanes]
            d = _window_sum(xg, w) / cnt - xg
            yd_ref[rows, lanes] = (_dot(d.astype(BF16), wg) * ps_ref[:, lanes]).astype(BF16)


def _mix_out_kernel(even, nx, *refs):
    x_refs = refs[:nx]
    mod_ref, oc_ref, ol_ref, w_ref = refs[nx:nx + 4]
    rest = refs[nx + 4:]
    if even:
        ya_ref, out_ref, wbf = rest
    else:
        (xd_ref, pw_ref, ps_ref, g2_ref, rw_ref, rb_ref, tri_ref, upper_ref,
         out_ref, xl_ref, meta_ref, cnt_ref, wbf, yd_s) = rest
    i = pl.program_id(0)

    @pl.when(i == 0)
    def _():
        _cast_rows(w_ref.at[0], wbf, D_MODEL)

    o = jnp.where(i < N_CTX_TOK // TM, oc_ref[...], ol_ref[...])
    if even:
        y = _dot(ya_ref[...], wbf[0:CONV_W, :]) + _dot(o, wbf[CONV_W:, :])
    else:
        @pl.when(i < N_CTX_TOK // TM)
        def _():
            _pool_mixer(xd_ref, pw_ref, ps_ref, yd_s, L_CTX)

        @pl.when(i >= N_CTX_TOK // TM)
        def _():
            _pool_mixer(xd_ref, pw_ref, ps_ref, yd_s, L_LAT)

        y = _dot(o, wbf[0:Q_W, :]) + _dot(yd_s[...], wbf[Q_W:, :])
    out_ref[...] = _read_x(i, x_refs, TM) + mod_ref[0, 2:3, :] * y
    if not even:
        for part in range(TM // TR):
            rows = pl.ds(part * TR, TR)
            _route_tile(out_ref[rows, :], mod_ref, g2_ref, rw_ref, rb_ref, tri_ref, upper_ref,
                        xl_ref.at[pl.ds(part * LOCAL_ROWS, LOCAL_ROWS), :], meta_ref.at[rows, :],
                        cnt_ref.at[pl.ds(part * SUBLANES, SUBLANES), :])


def _mix_out(even, layer_i, x, mod_l, o_ctx, o_lat, w_out, *extra):
    n_ctx = N_CTX_TOK // TM
    n_lat = N_LAT_TOK // TM
    row = lambda i: (i, 0)
    const = lambda i: (0, 0)
    in_specs = _x_specs(TM, len(x) == 2) + [
        pl.BlockSpec((1, 6, D_MODEL), lambda i: (_mod_row(i, TM), 0, 0)),
        pl.BlockSpec((TM, Q_W), lambda i: (jnp.minimum(i, n_ctx - 1), 0)),
        pl.BlockSpec((TM, Q_W), lambda i: (jnp.clip(i - n_ctx, 0, n_lat - 1), 0)),
        pl.BlockSpec((1, D_MODEL, D_MODEL), lambda i: (layer_i, 0, 0)),
    ]
    if even:
        in_specs += [pl.BlockSpec((TM, CONV_W), row)]
    else:
        in_specs += [pl.BlockSpec((TM, POOL_W), row),
                     pl.BlockSpec((len(POOL_WINDOWS), POOL_G, POOL_G), lambda i: (0, 0, 0)),
                     pl.BlockSpec((1, POOL_W), const),
                     pl.BlockSpec((1, D_MODEL), const),
                     pl.BlockSpec((D_MODEL, LANES), const),
                     pl.BlockSpec((1, LANES), const),
                     pl.BlockSpec((TR, TR), const),
                     pl.BlockSpec((LANES, LANES), const)]
    out_specs = [pl.BlockSpec((TM, D_MODEL), row)]
    out_shape = [jax.ShapeDtypeStruct((N_TOK, D_MODEL), F32)]
    if not even:
        parts = TM // TR
        out_specs += [pl.BlockSpec((parts * LOCAL_ROWS, D_MODEL), row), pl.BlockSpec((TM, LANES), row),
                      pl.BlockSpec((parts * SUBLANES, LANES), row)]
        out_shape += [jax.ShapeDtypeStruct((N_TOK // TR * LOCAL_ROWS, D_MODEL), BF16),
                      jax.ShapeDtypeStruct((N_TOK, LANES), F32),
                      jax.ShapeDtypeStruct((N_TOK // TR * SUBLANES, LANES), F32)]
    return pl.pallas_call(
        functools.partial(_mix_out_kernel, even, len(x)),
        grid=(N_TOK // TM,),
        in_specs=in_specs,
        out_specs=out_specs,
        out_shape=out_shape,
        scratch_shapes=[pltpu.VMEM((D_MODEL, D_MODEL), BF16)] + ([] if even else [pltpu.VMEM((TM, POOL_W), BF16)]),
        compiler_params=_cparams(("arbitrary",)),
        name="mix_out_even" if even else "mix_out_odd",
    )(*x, mod_l, o_ctx, o_lat, w_out, *extra)


def _ffn_kernel(layer_i, x_ref, mod_ref, g_ref, w1_hbm, w3_hbm, w2_hbm, out_ref,
                w1s, w3s, w2s, st1, st3, st2, hs, hid, sem):
    i = pl.program_id(0)
    nf = D_FF // TF_FFN

    def chunk_copies(f, slot):
        cols = pl.ds(f * TF_FFN, TF_FFN)
        return (pltpu.make_async_copy(w1_hbm.at[layer_i, :, cols], st1.at[slot], sem.at[0, slot]),
                pltpu.make_async_copy(w3_hbm.at[layer_i, :, cols], st3.at[slot], sem.at[1, slot]),
                pltpu.make_async_copy(w2_hbm.at[layer_i, cols, :], st2.at[slot], sem.at[2, slot]))

    def tile(first):
        if first:
            for c in chunk_copies(0, 0):
                c.start()
        hs[...] = _normmod(x_ref[...], g_ref[...], mod_ref[0, 4:5, :], mod_ref[0, 3:4, :]).astype(BF16)
        for f in range(nf):
            slot = f % 2
            lo, hi = f * TF_FFN, (f + 1) * TF_FFN
            if first:
                if f + 1 < nf:
                    for c in chunk_copies(f + 1, 1 - slot):
                        c.start()
                for c in chunk_copies(f, slot):
                    c.wait()
                w1s[:, lo:hi] = st1[slot].astype(BF16)
                w3s[:, lo:hi] = st3[slot].astype(BF16)
                w2s[lo:hi, :] = st2[slot].astype(BF16)
            h = hs[...]
            hid[:, lo:hi] = (_silu(_dot(h, w1s[:, lo:hi])) * _dot(h, w3s[:, lo:hi])).astype(BF16)
        out_ref[...] = x_ref[...] + mod_ref[0, 5:6, :] * _dot(hid[...], w2s[...])

    @pl.when(i == 0)
    def _():
        tile(True)

    @pl.when(i > 0)
    def _():
        tile(False)


def _ffn(layer_i, x, mod_l, g, w1, w3, w2):
    row = lambda i: (i, 0)
    any_spec = pl.BlockSpec(memory_space=pl.ANY)
    return pl.pallas_call(
        functools.partial(_ffn_kernel, layer_i),
        grid=(N_TOK // TM,),
        in_specs=[
            pl.BlockSpec((TM, D_MODEL), row),
            pl.BlockSpec((1, 6, D_MODEL), lambda i: (_mod_row(i, TM), 0, 0)),
            pl.BlockSpec((1, D_MODEL), lambda i: (0, 0)),
            any_spec, any_spec, any_spec,
        ],
        out_specs=pl.BlockSpec((TM, D_MODEL), row),
        out_shape=jax.ShapeDtypeStruct((N_TOK, D_MODEL), F32),
        scratch_shapes=[
            pltpu.VMEM((D_MODEL, D_FF), BF16),
            pltpu.VMEM((D_MODEL, D_FF), BF16),
            pltpu.VMEM((D_FF, D_MODEL), BF16),
            pltpu.VMEM((2, D_MODEL, TF_FFN), F32),
            pltpu.VMEM((2, D_MODEL, TF_FFN), F32),
            pltpu.VMEM((2, TF_FFN, D_MODEL), F32),
            pltpu.VMEM((TM, D_MODEL), BF16),
            pltpu.VMEM((TM, D_FF), BF16),
            pltpu.SemaphoreType.DMA((3, 2)),
        ],
        compiler_params=pltpu.CompilerParams(dimension_semantics=("arbitrary",), vmem_limit_bytes=FFN_VMEM_LIMIT),
        name="ffn",
    )(x, mod_l, g, w1, w3, w2)


def _split_bf16(t):
    hi = t.astype(BF16)
    return hi, (t - hi.astype(F32)).astype(BF16)


def _lane_values(col_vals, ones_rows):
    base = 32.0
    assert LOCAL_ROWS <= base * 256
    q = jnp.floor(col_vals * (1.0 / base))
    r = col_vals - base * q
    t = base * _dot_nt(ones_rows, q.astype(BF16)) + _dot_nt(ones_rows, r.astype(BF16))
    return t[0:1, :]


def _route_tile(x, mod_ref, g_ref, rw_ref, rb_ref, tri_ref, upper_ref, xl_ref, meta_ref, cnt_ref):
    h = _normmod(x, g_ref[...], mod_ref[0, 4:5, :], mod_ref[0, 3:4, :])
    h_hi, h_lo = _split_bf16(h)
    w_hi, w_lo = _split_bf16(rw_ref[...])
    both = _dot(h_hi, jnp.concatenate([w_hi, w_lo], axis=1))
    logits = both[:, :LANES] + both[:, LANES:] + _dot(h_lo, w_hi) + rb_ref[...]
    lane = lax.broadcasted_iota(jnp.int32, logits.shape, 1).astype(F32)
    logits = jnp.where(lane < N_EXP, logits, NEG_INF)
    m1 = logits.max(axis=-1, keepdims=True)
    i1 = jnp.where(logits == m1, lane, float(LANES)).min(axis=-1, keepdims=True)
    rest = jnp.where(lane == i1, NEG_INF, logits)
    m2 = rest.max(axis=-1, keepdims=True)
    i2 = jnp.where(rest == m2, lane, float(LANES)).min(axis=-1, keepdims=True)
    e2 = jnp.exp(m2 - m1)
    den = 1.0 + e2
    g1 = 1.0 / den
    g2 = e2 / den

    oh_a = jnp.where(lane == i1, 1.0, 0.0)
    oh_b = jnp.where(lane == i2, 1.0, 0.0)
    tri = tri_ref[...]
    cnt_a = oh_a.sum(axis=0, keepdims=True)
    cnt_b = oh_b.sum(axis=0, keepdims=True)
    run16 = jnp.floor((cnt_a + cnt_b + (RUN_ALIGN - 1)) * (1.0 / RUN_ALIGN))
    run16_rows = jnp.broadcast_to(run16, (SUBLANES, LANES))
    start = RUN_ALIGN * _dot(run16_rows.astype(BF16), upper_ref[...])[0:1, :]
    before = _dot(tri, jnp.concatenate([oh_a, oh_b], axis=1).astype(BF16))
    row_a = oh_a * (start + before[:, :LANES])
    row_b = oh_b * (start + cnt_a + before[:, LANES:])

    ones_rows = jnp.ones((SUBLANES, LANES), BF16)
    tok_a = _lane_values(row_a, ones_rows)
    tok_b = _lane_values(row_b, ones_rows)
    sorted_row = lax.broadcasted_iota(jnp.int32, (LOCAL_ROWS, TR), 0).astype(F32)
    perm = jnp.where((sorted_row == tok_a) | (sorted_row == tok_b), 1.0, 0.0).astype(BF16)
    xl_ref[...] = _dot(perm, h_hi).astype(BF16)

    meta = jnp.zeros_like(logits)
    cols = (row_a.sum(axis=-1, keepdims=True), row_b.sum(axis=-1, keepdims=True), g1, g2)
    for k, col in enumerate(cols):
        meta = jnp.where(lane == k, col, meta)
    meta_ref[...] = meta
    cnt_ref[...] = RUN_ALIGN * run16_rows


def _moe_plan(counts):
    n_tiles = N_TOK // TR
    run = counts.reshape(n_tiles, SUBLANES, LANES)[:, 0, :N_EXP].astype(jnp.int32)
    per_expert = jnp.sum(run, axis=0)
    region = (per_expert + (TMS - 1)) // TMS * TMS
    ends = jnp.cumsum(region)
    offs = ends - region
    seg_end = jnp.cumsum(run, axis=0)
    seg_start = seg_end - run
    local_end = jnp.cumsum(run, axis=1)
    local_start = local_end - run
    n_chunks = local_end[:, N_EXP - 1] // RUN_ALIGN
    tile_start = jnp.arange(MOE_TILES, dtype=jnp.int32) * TMS
    tile_expert = jnp.sum((tile_start[:, None] >= ends[None, :]).astype(jnp.int32), axis=1)
    tile_expert = jnp.minimum(tile_expert, N_EXP - 1)
    n_used = (ends[N_EXP - 1] // TMS).reshape(1)
    experts = jnp.arange(N_EXP, dtype=jnp.int32)
    g_row = jnp.arange(MOE_ROWS // RUN_ALIGN, dtype=jnp.int32) * RUN_ALIGN
    g_exp = jnp.repeat(tile_expert, TMS // RUN_ALIGN)
    pick = g_exp[:, None] == experts[None, :]
    rel = g_row - jnp.sum(jnp.where(pick, offs[None, :], 0), axis=1)
    ends_of = jnp.sum(jnp.where(pick[:, None, :], seg_end[None, :, :], 0), axis=2)
    src_tile = jnp.sum((rel[:, None] >= ends_of).astype(jnp.int32), axis=1)
    valid = (src_tile < n_tiles) & (jnp.repeat(jnp.arange(MOE_TILES), TMS // RUN_ALIGN) < n_used[0])
    src_tile = jnp.minimum(src_tile, n_tiles - 1)
    sel = (src_tile[:, None, None] == jnp.arange(n_tiles)[None, :, None]) & pick[:, None, :]
    shift = jnp.sum(jnp.where(sel, (local_start - seg_start)[None, :, :], 0), axis=(1, 2))
    chunk_src = src_tile * LOCAL_ROWS + rel + shift
    n_valid = jnp.sum(valid.reshape(MOE_TILES, TMS // RUN_ALIGN).astype(jnp.int32), axis=1)
    return chunk_src, n_valid, n_chunks, tile_expert, n_used


def _dma_priority(j):
    return j % 2 if isinstance(j, int) else 0


def _chunk_copy(src_ref, src_row, dst_ref, dst_row, sem):
    return pltpu.make_async_copy(src_ref.at[pl.ds(src_row, RUN_ALIGN), :],
                                 dst_ref.at[pl.ds(dst_row, RUN_ALIGN), :], sem)


CHUNKS = TMS // RUN_ALIGN
LOCAL_CHUNKS = LOCAL_ROWS // RUN_ALIGN


def _expert_kernel(te_ref, nu_ref, src_ref, nv_ref, nc_ref, xl_hbm, w1_ref, w3_ref, w2_ref, yl_hbm,
                   xbuf, ybuf, zbuf, w1s, w3s, w2s, sem_in, sem_out, sem_zero):
    t = pl.program_id(0)
    n_used = nu_ref[0]
    slot = t & 1

    def for_chunks(n, body):
        @pl.when(n == CHUNKS)
        def _():
            for j in range(CHUNKS):
                body(j)

        @pl.when(n < CHUNKS)
        def _():
            def step(j, c):
                body(j)
                return c

            lax.fori_loop(0, n, step, 0)

    def gather(tile, s):
        n = nv_ref[tile]

        def issue(j):
            src = pl.multiple_of(src_ref[tile * CHUNKS + j], RUN_ALIGN)
            _chunk_copy(xl_hbm, src, xbuf.at[s], pl.multiple_of(j * RUN_ALIGN, RUN_ALIGN),
                        sem_in.at[s]).start(priority=_dma_priority(j))

        def pad(j, c):
            xbuf[s, pl.ds(pl.multiple_of(j * RUN_ALIGN, RUN_ALIGN), RUN_ALIGN), :] = jnp.zeros(
                (RUN_ALIGN, D_MODEL), BF16)
            return c

        for_chunks(n, issue)
        lax.fori_loop(n, CHUNKS, pad, 0)

    def scatter(tile, s):
        def issue(j):
            dst = pl.multiple_of(src_ref[tile * CHUNKS + j], RUN_ALIGN)
            _chunk_copy(ybuf.at[s], pl.multiple_of(j * RUN_ALIGN, RUN_ALIGN), yl_hbm, dst,
                        sem_out.at[s]).start(priority=_dma_priority(j))

        for_chunks(nv_ref[tile], issue)

    def drain(count, sem):
        @pl.when(count == CHUNKS)
        def _():
            pltpu.make_async_copy(xl_hbm.at[pl.ds(0, TMS), :], xbuf.at[0], sem).wait()

        @pl.when(count != CHUNKS)
        def _():
            def one(j, c):
                _chunk_copy(xl_hbm, 0, xbuf.at[0], 0, sem).wait()
                return c

            lax.fori_loop(0, count, one, 0)

    @pl.when(t == 0)
    def _():
        zbuf[...] = jnp.zeros_like(zbuf)
        total = jnp.int32(0)
        for tile in range(N_TOK // TR):
            n = nc_ref[tile]

            def clear(j, c, tile=tile):
                row = pl.multiple_of(tile * LOCAL_ROWS + j * RUN_ALIGN, RUN_ALIGN)
                _chunk_copy(zbuf, 0, yl_hbm, row, sem_zero).start()
                return c

            lax.fori_loop(n, LOCAL_CHUNKS, clear, 0)
            total = total + (LOCAL_CHUNKS - n)
        drain(total, sem_zero)
        gather(0, 0)

    @pl.when(t < n_used)
    def _():
        drain(nv_ref[t], sem_in.at[slot])

    @pl.when(t + 1 < n_used)
    def _():
        gather(t + 1, 1 - slot)

    new_expert = (t == 0) | (te_ref[t] != te_ref[jnp.maximum(t - 1, 0)])

    @pl.when(new_expert)
    def _():
        _cast_rows(w1_ref.at[0, 0], w1s, D_MODEL)
        _cast_rows(w3_ref.at[0, 0], w3s, D_MODEL)
        _cast_rows(w2_ref.at[0, 0], w2s, D_FF_E)

    @pl.when((t >= 2) & (t - 2 < n_used))
    def _():
        drain(nv_ref[jnp.maximum(t - 2, 0)], sem_out.at[slot])

    @pl.when(t < n_used)
    def _():
        x = xbuf[slot]
        hid = (_silu(_dot(x, w1s[...])) * _dot(x, w3s[...])).astype(BF16)
        ybuf[slot] = _dot(hid, w2s[...]).astype(BF16)
        scatter(t, slot)

    @pl.when(t == pl.num_programs(0) - 1)
    def _():
        @pl.when((t >= 1) & (t - 1 < n_used))
        def _():
            drain(nv_ref[jnp.maximum(t - 1, 0)], sem_out.at[1 - slot])

        @pl.when(t < n_used)
        def _():
            drain(nv_ref[t], sem_out.at[slot])


def _experts(layer_i, plan, xl, w1, w3, w2):
    chunk_src, n_valid, n_chunks, tile_expert, n_used = plan
    wsel = lambda t, te, nu, src, nv, nc: (layer_i, te[t], 0, 0)
    any_spec = pl.BlockSpec(memory_space=pl.ANY)
    return pl.pallas_call(
        _expert_kernel,
        grid_spec=pltpu.PrefetchScalarGridSpec(
            num_scalar_prefetch=5,
            grid=(MOE_TILES,),
            in_specs=[
                any_spec,
                pl.BlockSpec((1, 1, D_MODEL, D_FF_E), wsel),
                pl.BlockSpec((1, 1, D_MODEL, D_FF_E), wsel),
                pl.BlockSpec((1, 1, D_FF_E, D_MODEL), wsel),
            ],
            out_specs=any_spec,
            scratch_shapes=[
                pltpu.VMEM((2, TMS, D_MODEL), BF16),
                pltpu.VMEM((2, TMS, D_MODEL), BF16),
                pltpu.VMEM((RUN_ALIGN, D_MODEL), BF16),
                pltpu.VMEM((D_MODEL, D_FF_E), BF16),
                pltpu.VMEM((D_MODEL, D_FF_E), BF16),
                pltpu.VMEM((D_FF_E, D_MODEL), BF16),
                pltpu.SemaphoreType.DMA((2,)),
                pltpu.SemaphoreType.DMA((2,)),
                pltpu.SemaphoreType.DMA(()),
            ],
        ),
        out_shape=jax.ShapeDtypeStruct(xl.shape, BF16),
        compiler_params=_cparams(("arbitrary",)),
        name="moe_experts",
    )(tile_expert, n_used, chunk_src, n_valid, n_chunks, xl, w1, w3, w2)


def _combine_kernel(x_ref, mod_ref, meta_ref, yl_ref, *out_refs):
    sorted_row = lax.broadcasted_iota(jnp.int32, (TR, LOCAL_ROWS), 1).astype(F32)
    y = yl_ref[...]
    pick_a = jnp.where(sorted_row == meta_ref[:, 0:1], 1.0, 0.0).astype(BF16)
    pick_b = jnp.where(sorted_row == meta_ref[:, 1:2], 1.0, 0.0).astype(BF16)
    f = meta_ref[:, 2:3] * _dot(pick_a, y) + meta_ref[:, 3:4] * _dot(pick_b, y)
    _write_x(pl.program_id(0), out_refs, TR, x_ref[...] + mod_ref[0, 5:6, :] * f)


def _combine(x, mod_l, meta, yl, split_out):
    row = lambda i: (i, 0)
    return pl.pallas_call(
        _combine_kernel,
        grid=(N_TOK // TR,),
        in_specs=[
            pl.BlockSpec((TR, D_MODEL), row),
            pl.BlockSpec((1, 6, D_MODEL), lambda i: (_mod_row(i, TR), 0, 0)),
            pl.BlockSpec((TR, LANES), row),
            pl.BlockSpec((LOCAL_ROWS, D_MODEL), row),
        ],
        out_specs=_x_specs(TR, split_out),
        out_shape=_x_shapes(split_out),
        compiler_params=_cparams(("arbitrary",)),
        name="moe_combine",
    )(x, mod_l, meta, yl)


def _moe(layer_i, x, routed, mod_l, w1, w3, w2, split_out):
    xl, meta, counts = routed
    yl = _experts(layer_i, _moe_plan(counts), xl, w1, w3, w2)
    return _combine(x, mod_l, meta, yl, split_out)


def _rope_tables():
    n_rows = L_LAT // GRID_W
    rows = np.repeat(np.arange(n_rows, dtype=np.float32), GRID_W)
    cols = np.tile(np.arange(GRID_W, dtype=np.float32), n_rows)
    quarter = HD // 4
    inv = (np.float32(ROPE_THETA) ** (-np.arange(quarter, dtype=np.float32) / np.float32(quarter))).astype(np.float32)
    ang_r = (rows[:, None] * inv).astype(np.float32)
    ang_c = (cols[:, None] * inv).astype(np.float32)
    cos = np.concatenate([np.cos(ang_r)] * 2 + [np.cos(ang_c)] * 2, axis=1)
    sin = np.concatenate([-np.sin(ang_r), np.sin(ang_r), -np.sin(ang_c), np.sin(ang_c)], axis=1)
    cos = np.concatenate([cos, cos], axis=1)
    sin = np.concatenate([sin, sin], axis=1)
    cos = np.concatenate([np.ones((L_LAT, LANES), np.float32), cos], axis=0)
    sin = np.concatenate([np.zeros((L_LAT, LANES), np.float32), sin], axis=0)
    return jnp.asarray(cos, F32), jnp.asarray(sin, F32)


def _block_ones(width):
    r = np.arange(width) // HD
    return jnp.asarray(r[:, None] == r[None, :], BF16)


def kernel(x_prompt, x_sample, cache_k, cache_v, c, c_ctx, norm1, norm2, w_mod, b_mod, ev_w_in, ev_conv, ev_q_norm, ev_k_norm, ev_w_out, od_w_in, od_q_norm, od_k_norm, od_sink, od_pool_w, od_pool_scale, od_w_out, ffn_w1, ffn_w3, ffn_w2, moe_router, moe_router_b, moe_w1, moe_w3, moe_w2):
    x = (x_prompt.reshape(N_CTX_TOK, D_MODEL), x_sample.reshape(N_LAT_TOK, D_MODEL))
    cond = jnp.concatenate([c_ctx[None, :], c, jnp.zeros((MOD_ROWS - 1 - N_SEQ_LAT, D_MODEL), F32)], axis=0)
    mod = _modulation(cond, w_mod, b_mod).reshape(DEPTH, MOD_ROWS, 6, D_MODEL)

    cos_tab, sin_tab = _rope_tables()
    ones_q = _block_ones(Q_W)
    ones_k = _block_ones(KV_W)
    dup = jnp.asarray(np.concatenate([np.eye(HD), np.eye(HD)], axis=1), BF16)
    tri = jnp.asarray(np.arange(TR)[:, None] > np.arange(TR)[None, :], BF16)
    upper = jnp.asarray(np.arange(LANES)[:, None] < np.arange(LANES)[None, :], BF16)

    new_kv = None
    for l in range(DEPTH):
        i = l // 2
        even = l % 2 == 0
        mod_l = mod[l]
        g1 = norm1[l][None, :]
        g2 = norm2[l][None, :]
        if even:
            q_gain, k_gain = ev_q_norm[i], ev_k_norm[i]
            w_in, w_out = ev_w_in, ev_w_out
        else:
            q_gain, k_gain = od_q_norm[i], od_k_norm[i]
            w_in, w_out = od_w_in, od_w_out
        q_gain = jnp.tile(q_gain, N_Q)[None, :]
        k_gain = jnp.tile(k_gain, N_KV)[None, :]
        outs = _in_proj(even, i, x, mod_l, g1, w_in, ones_q, ones_k, q_gain, k_gain, cos_tab, sin_tab,
                        ev_conv[i] if even else None)
        if even:
            ya, q, k, v = outs
            sink = None
        else:
            q, k, v, xd = outs
            sink = od_sink[i]
        o_ctx, new_kv = _ctx_attn(l, q, k, v, sink, new_kv)
        o_lat = _lat_attn(l, q, k, v, cache_k, cache_v, dup, sink)
        if even:
            (x1,) = _mix_out(True, i, x, mod_l, o_ctx, o_lat, w_out, ya)
            x = (_ffn(i, x1, mod_l, g2, ffn_w1, ffn_w3, ffn_w2),)
        else:
            rw = jnp.pad(moe_router[i], ((0, 0), (0, LANES - N_EXP)))
            rb = jnp.pad(moe_router_b[i], (0, LANES - N_EXP))[None, :]
            x1, *routed = _mix_out(False, i, x, mod_l, o_ctx, o_lat, w_out, xd, od_pool_w[i],
                                   od_pool_scale[i][None, :], g2, rw, rb, tri, upper)
            x = tuple(_moe(i, x1, routed, mod_l, moe_w1, moe_w3, moe_w2, l == DEPTH - 1))

    y_prompt = x[0].reshape(N_SEQ_CTX, L_CTX, D_MODEL)
    y_sample = x[1].reshape(N_SEQ_LAT, L_LAT, D_MODEL)
    return (y_prompt, y_sample, new_kv[0], new_kv[1])
```
